```python
import math
import jax, jax.numpy as jnp
from jax import lax
import numpy as np

D_MODEL = 2048
BATCH = 4
SEQ = 2048
DEPTH = 1
DEC_BATCH = 32
DEC_SEQ = 4
PAST_LEN = 16384
PAGE_SIZE = 128

D_MIX = D_MODEL
D_CONV = D_MIX // 2
CONV_GROUPS = 8
CONV_WIDTH = 31
D_ATTN = D_MIX - D_CONV
HEAD_DIM = 128
N_HEADS = D_ATTN // HEAD_DIM
N_KV_HEADS = 2
GQA_GROUP = N_HEADS // N_KV_HEADS
D_KV = N_KV_HEADS * HEAD_DIM
WINDOW = 128
BLOCK_Q = 128
SCALE = 1.0 / math.sqrt(HEAD_DIM)
N_EXPERT_GROUPS = 4
EXPERTS_PER_GROUP = 8
N_EXPERTS = N_EXPERT_GROUPS * EXPERTS_PER_GROUP
TOP_K = 2
D_EXPERT = D_MODEL // 4
MOE_BLOCK = 128
D_IN = 2 * D_CONV + D_ATTN + 2 * D_KV
EPS = 1e-6

kernel_name = 'hymba_conformer_swa_hiermoe_step'


def rms_norm(x, g):
    xf = x.astype(jnp.float32)
    y = xf * lax.rsqrt(jnp.mean(xf * xf, axis=-1, keepdims=True) + EPS)
    return (y * g.astype(jnp.float32)).astype(x.dtype)


def layer_norm(x, g, b):
    xf = x.astype(jnp.float32)
    xc = xf - jnp.mean(xf, axis=-1, keepdims=True)
    var = jnp.mean(xc * xc, axis=-1, keepdims=True)
    y = xc * lax.rsqrt(var + EPS) * g.astype(jnp.float32) + b.astype(jnp.float32)
    return y.astype(x.dtype)


def alibi_slopes():
    h = jnp.arange(1, N_HEADS + 1, dtype=jnp.float32)
    return jnp.exp2(-8.0 * h / N_HEADS).reshape(N_KV_HEADS, GQA_GROUP)


def sink_softmax(logits, valid, sink):
    logits = jnp.where(valid, logits, -jnp.inf)
    m = jnp.maximum(jnp.max(logits, axis=-1, keepdims=True), sink)
    p = jnp.exp(logits - m)
    return p / (jnp.sum(p, axis=-1, keepdims=True) + jnp.exp(sink - m))


def window_attention_prompt(q, k, v, sinks):
    B, T = q.shape[:2]
    nb = T // BLOCK_Q
    qb = q.reshape(B, nb, BLOCK_Q, N_KV_HEADS, GQA_GROUP, HEAD_DIM)
    kb = k.reshape(B, nb, BLOCK_Q, N_KV_HEADS, HEAD_DIM)
    vb = v.reshape(B, nb, BLOCK_Q, N_KV_HEADS, HEAD_DIM)
    pad = ((0, 0), (1, 0), (0, 0), (0, 0), (0, 0))
    kk = jnp.concatenate([jnp.pad(kb, pad)[:, :-1], kb], axis=2)
    vv = jnp.concatenate([jnp.pad(vb, pad)[:, :-1], vb], axis=2)
    q_rel = jnp.arange(BLOCK_Q) + BLOCK_Q
    k_rel = jnp.arange(2 * BLOCK_Q)
    dist = q_rel[:, None] - k_rel[None, :]
    k_abs = (jnp.arange(nb) * BLOCK_Q - BLOCK_Q)[:, None] + k_rel[None, :]
    valid = ((dist >= 0) & (dist < WINDOW))[None] & (k_abs >= 0)[:, None, :]
    s = jnp.einsum('bnqkgd,bnskd->bnkgqs', qb, kk).astype(jnp.float32) * SCALE
    s = s - alibi_slopes()[:, :, None, None] * dist.astype(jnp.float32)
    sink = sinks.astype(jnp.float32).reshape(N_KV_HEADS, GQA_GROUP, 1, 1)
    p = sink_softmax(s, valid[None, :, None, None], sink)
    o = jnp.einsum('bnkgqs,bnskd->bnqkgd', p.astype(v.dtype), vv)
    return o.reshape(B, T, D_ATTN), k[:, T - WINDOW:], v[:, T - WINDOW:]


def window_attention_sample(q, k, v, k_past, v_past, sinks):
    B, T = q.shape[:2]
    W = k_past.shape[1]
    kk = jnp.concatenate([k_past, k], axis=1)
    vv = jnp.concatenate([v_past, v], axis=1)
    q_abs = PAST_LEN + jnp.arange(T)
    k_abs = PAST_LEN - W + jnp.arange(W + T)
    dist = q_abs[:, None] - k_abs[None, :]
    valid = (dist >= 0) & (dist < WINDOW)
    qg = q.reshape(B, T, N_KV_HEADS, GQA_GROUP, HEAD_DIM)
    s = jnp.einsum('bqkgd,bskd->bkgqs', qg, kk).astype(jnp.float32) * SCALE
    s = s - alibi_slopes()[:, :, None, None] * dist.astype(jnp.float32)
    sink = sinks.astype(jnp.float32).reshape(N_KV_HEADS, GQA_GROUP, 1, 1)
    p = sink_softmax(s, valid, sink)
    o = jnp.einsum('bkgqs,bskd->bqkgd', p.astype(v.dtype), vv)
    return o.reshape(B, T, D_ATTN), kk[:, T:], vv[:, T:]


def causal_depthwise_conv(u_ext, conv_w, conv_b):
    out = lax.conv_general_dilated(u_ext, conv_w[:, None, :], window_strides=(1,), padding='VALID',
                                   dimension_numbers=('NWC', 'WIO', 'NWC'),
                                   feature_group_count=D_CONV)
    return out + conv_b


def hier_moe(x2, w_rg, b_rg, w_re, b_re, w1, w3, w2):
    N = x2.shape[0]
    g_prob = jax.nn.softmax((x2 @ w_rg).astype(jnp.float32) + b_rg.astype(jnp.float32), axis=-1)
    g_top, g_idx = lax.top_k(g_prob, 1)
    e_logits = ((x2 @ w_re).astype(jnp.float32) + b_re.astype(jnp.float32)).reshape(
        N, N_EXPERT_GROUPS, EXPERTS_PER_GROUP)
    e_in = jnp.take_along_axis(e_logits, g_idx[:, :, None], axis=1)[:, 0]
    e_top, e_loc = lax.top_k(jax.nn.softmax(e_in, axis=-1), TOP_K)
    gate = (g_top * e_top / jnp.sum(e_top, axis=-1, keepdims=True)).reshape(-1)
    eid = (g_idx * EXPERTS_PER_GROUP + e_loc).reshape(-1).astype(jnp.int32)
    tok = jnp.repeat(jnp.arange(N, dtype=jnp.int32), TOP_K)
    A = N * TOP_K
    order = jnp.argsort(eid)
    se, stok, sg = eid[order], tok[order], gate[order]
    counts = jnp.bincount(eid, length=N_EXPERTS).astype(jnp.int32)
    start = jnp.cumsum(counts) - counts
    pcounts = (counts + MOE_BLOCK - 1) // MOE_BLOCK * MOE_BLOCK
    pend = jnp.cumsum(pcounts)
    pstart = pend - pcounts
    dest = pstart[se] + (jnp.arange(A, dtype=jnp.int32) - start[se])
    nb = -(-A // MOE_BLOCK) + N_EXPERTS
    P = nb * MOE_BLOCK
    buf_tok = jnp.full((P,), N, dtype=jnp.int32).at[dest].set(stok)
    buf_gate = jnp.zeros((P,), jnp.float32).at[dest].set(sg)
    blk_start = jnp.arange(nb, dtype=jnp.int32) * MOE_BLOCK
    blk_exp = jnp.minimum(jnp.searchsorted(pend, blk_start, side='right'), N_EXPERTS - 1)
    x_pad = jnp.concatenate([x2, jnp.zeros((1, x2.shape[1]), x2.dtype)], axis=0)

    def run_block(args):
        rows, e = args
        xb = x_pad[rows]
        hdn = jax.nn.silu(xb @ w1[e]) * (xb @ w3[e])
        return hdn @ w2[e]

    out = lax.map(run_block, (buf_tok.reshape(nb, MOE_BLOCK), blk_exp)).reshape(P, -1)
    out = out * buf_gate[:, None].astype(out.dtype)
    return jax.ops.segment_sum(out, buf_tok, num_segments=N + 1)[:N]


def hybrid_layer(x, conv_past, k_past, v_past, norm1_g, w_in, conv_w, conv_b, conv_norm_g,
                 conv_norm_b, q_norm_g, k_norm_g, attn_sinks, w_out, norm2_g, w_rg, b_rg,
                 w_re, b_re, w1, w3, w2):
    B, T, _ = x.shape
    n = rms_norm(x, norm1_g)
    a, gt, q, k, v = jnp.split(n @ w_in, [D_CONV, 2 * D_CONV, 2 * D_CONV + D_ATTN,
                                          2 * D_CONV + D_ATTN + D_KV], axis=-1)
    u = a * jax.nn.sigmoid(gt)
    if conv_past is None:
        conv_past = jnp.zeros((B, CONV_WIDTH - 1, D_CONV), u.dtype)
    u_ext = jnp.concatenate([conv_past, u], axis=1)
    c = causal_depthwise_conv(u_ext, conv_w, conv_b)
    c = jax.nn.silu(layer_norm(c, conv_norm_g, conv_norm_b))
    new_conv = u_ext[:, -(CONV_WIDTH - 1):]
    q = rms_norm(q.reshape(B, T, N_HEADS, HEAD_DIM), q_norm_g)
    k = rms_norm(k.reshape(B, T, N_KV_HEADS, HEAD_DIM), k_norm_g)
    v = v.reshape(B, T, N_KV_HEADS, HEAD_DIM)
    if k_past is None:
        o, new_k, new_v = window_attention_prompt(q, k, v, attn_sinks)
    else:
        o, new_k, new_v = window_attention_sample(q, k, v, k_past, v_past, attn_sinks)
    x = x + jnp.concatenate([c, o], axis=-1) @ w_out
    h2 = rms_norm(x, norm2_g).reshape(B * T, D_MODEL)
    y = x + hier_moe(h2, w_rg, b_rg, w_re, b_re, w1, w3, w2).reshape(B, T, D_MODEL)
    return y, new_conv, new_k, new_v


def setup_inputs(seed: int = 0) -> dict:
    key = jax.random.key(seed)
    ks = jax.random.split(key, 24)
    f32 = jnp.float32

    def nrm(k, shape, scale):
        return jax.random.normal(k, shape, f32) * scale

    wbuf = min(WINDOW, PAST_LEN)
    return {
        'x_prompt': nrm(ks[0], (BATCH, SEQ, D_MODEL), 1.0),
        'x_sample': nrm(ks[1], (DEC_BATCH, DEC_SEQ, D_MODEL), 1.0),
        'state_conv': nrm(ks[2], (DEPTH, DEC_BATCH, CONV_WIDTH - 1, D_CONV), 0.5),
        'cache_k': nrm(ks[3], (DEPTH, DEC_BATCH, wbuf, N_KV_HEADS, HEAD_DIM), 1.0),
        'cache_v': nrm(ks[4], (DEPTH, DEC_BATCH, wbuf, N_KV_HEADS, HEAD_DIM), 1.0),
        'norm1_g': 1.0 + nrm(ks[5], (DEPTH, D_MODEL), 0.02),
        'w_in': nrm(ks[6], (DEPTH, D_MODEL, D_IN), D_MODEL ** -0.5),
        'conv_w': nrm(ks[7], (DEPTH, CONV_WIDTH, D_CONV), CONV_WIDTH ** -0.5),
        'conv_b': nrm(ks[8], (DEPTH, D_CONV), 0.02),
        'conv_norm_g': 1.0 + nrm(ks[9], (DEPTH, D_CONV), 0.02),
        'conv_norm_b': nrm(ks[10], (DEPTH, D_CONV), 0.02),
        'q_norm_g': 1.0 + nrm(ks[11], (DEPTH, HEAD_DIM), 0.02),
        'k_norm_g': 1.0 + nrm(ks[12], (DEPTH, HEAD_DIM), 0.02),
        'attn_sinks': nrm(ks[13], (DEPTH, N_HEADS), 1.0),
        'w_out': nrm(ks[14], (DEPTH, D_MIX, D_MODEL), D_MIX ** -0.5),
        'norm2_g': 1.0 + nrm(ks[15], (DEPTH, D_MODEL), 0.02),
        'w_router_group': nrm(ks[16], (DEPTH, D_MODEL, N_EXPERT_GROUPS), D_MODEL ** -0.5),
        'b_router_group': nrm(ks[17], (DEPTH, N_EXPERT_GROUPS), 0.01),
        'w_router_expert': nrm(ks[18], (DEPTH, D_MODEL, N_EXPERTS), D_MODEL ** -0.5),
        'b_router_expert': nrm(ks[19], (DEPTH, N_EXPERTS), 0.01),
        'w1': nrm(ks[20], (DEPTH, N_EXPERTS, D_MODEL, D_EXPERT), D_MODEL ** -0.5),
        'w3': nrm(ks[21], (DEPTH, N_EXPERTS, D_MODEL, D_EXPERT), D_MODEL ** -0.5),
        'w2': nrm(ks[22], (DEPTH, N_EXPERTS, D_EXPERT, D_MODEL), D_EXPERT ** -0.5),
    }


def reference(x_prompt, x_sample, state_conv, cache_k, cache_v, norm1_g, w_in, conv_w, conv_b,
              conv_norm_g, conv_norm_b, q_norm_g, k_norm_g, attn_sinks, w_out, norm2_g,
              w_router_group, b_router_group, w_router_expert, b_router_expert, w1, w3, w2):
    yp, ys = x_prompt, x_sample
    conv_p, k_p, v_p, conv_s, k_s, v_s = [], [], [], [], [], []
    for l in range(DEPTH):
        params = (norm1_g[l], w_in[l], conv_w[l], conv_b[l], conv_norm_g[l], conv_norm_b[l],
                  q_norm_g[l], k_norm_g[l], attn_sinks[l], w_out[l], norm2_g[l],
                  w_router_group[l], b_router_group[l], w_router_expert[l], b_router_expert[l],
                  w1[l], w3[l], w2[l])
        yp, c, kn, vn = hybrid_layer(yp, None, None, None, *params)
        conv_p.append(c)
        k_p.append(kn)
        v_p.append(vn)
        ys, c, kn, vn = hybrid_layer(ys, state_conv[l], cache_k[l], cache_v[l], *params)
        conv_s.append(c)
        k_s.append(kn)
        v_s.append(vn)
    return (yp, ys, jnp.stack(conv_p), jnp.stack(k_p), jnp.stack(v_p),
            jnp.stack(conv_s), jnp.stack(k_s), jnp.stack(v_s))
```

```python
import functools
import math

import jax
import jax.numpy as jnp
from jax import lax
from jax.experimental import pallas as pl
from jax.experimental.pallas import tpu as pltpu

F32 = jnp.float32
BF16 = jnp.bfloat16
I32 = jnp.int32

D_MODEL = 2048
D_CONV = 1024
CONV_WIDTH = 31
CONV_HIST = CONV_WIDTH - 1
D_ATTN = 1024
HEAD_DIM = 128
N_HEADS = 8
N_KV_HEADS = 2
GQA_GROUP = N_HEADS // N_KV_HEADS
D_KV = N_KV_HEADS * HEAD_DIM
WINDOW = 128
BLOCK_Q = 128
SCALE = 1.0 / math.sqrt(HEAD_DIM)
N_EXPERT_GROUPS = 4
EXPERTS_PER_GROUP = 8
N_EXPERTS = N_EXPERT_GROUPS * EXPERTS_PER_GROUP
D_EXPERT = 512
MOE_BLOCK = 128
D_IN = 2 * D_CONV + D_ATTN + 2 * D_KV
EPS = 1e-6
PAST_LEN = 16384

LANES = 128
SUBLANES = 8
MXU_COLS = 256
VMEM_LIMIT_BYTES = 56 * 1024 * 1024
NEG_INF = float("-inf")
EXPERT_LANE0 = N_EXPERT_GROUPS


def _params(n_axes):
    return pltpu.CompilerParams(dimension_semantics=("arbitrary",) * n_axes,
                                vmem_limit_bytes=VMEM_LIMIT_BYTES)


def _resident(shape):
    nd = len(shape)
    return pl.BlockSpec(shape, lambda *_: (0,) * nd, pipeline_mode=pl.Buffered(1))


def _dot(a, b):
    return jnp.dot(a, b, preferred_element_type=F32)


def _in_proj_body(x_ref, g1_ref, w_ref, qg_ref, kg_ref, u_ref, q_ref, k_ref, v_ref, n_ref):
    x = x_ref[...]
    ms = jnp.mean(x * x, axis=-1, keepdims=True)
    n_ref[...] = (x * lax.rsqrt(ms + EPS) * g1_ref[...]).astype(BF16)

    def head_norm(h, g):
        return h * lax.rsqrt(jnp.mean(h * h, axis=-1, keepdims=True) + EPS) * g

    ch = MXU_COLS
    for c in range(D_CONV // ch):
        a = _dot(n_ref[...], w_ref[:, c * ch:(c + 1) * ch])
        g = _dot(n_ref[...], w_ref[:, D_CONV + c * ch:D_CONV + (c + 1) * ch])
        u_ref[:, c * ch:(c + 1) * ch] = a * jax.nn.sigmoid(g)
    q_off = 2 * D_CONV
    for c in range(D_ATTN // ch):
        qq = _dot(n_ref[...], w_ref[:, q_off + c * ch:q_off + (c + 1) * ch])
        for j in range(ch // HEAD_DIM):
            qh = head_norm(qq[:, j * HEAD_DIM:(j + 1) * HEAD_DIM], qg_ref[...])
            q_ref[:, c * ch + j * HEAD_DIM:c * ch + (j + 1) * HEAD_DIM] = qh.astype(q_ref.dtype)
    k_off = q_off + D_ATTN
    kk = _dot(n_ref[...], w_ref[:, k_off:k_off + D_KV])
    for j in range(N_KV_HEADS):
        k_ref[:, j * HEAD_DIM:(j + 1) * HEAD_DIM] = head_norm(kk[:, j * HEAD_DIM:(j + 1) * HEAD_DIM], kg_ref[...])
    v_ref[...] = _dot(n_ref[...], w_ref[:, k_off + D_KV:k_off + 2 * D_KV])


def _in_proj(x2, g1, w_in_bf, qg, kg, tm, q_dtype):
    n = x2.shape[0]
    row = lambda w: pl.BlockSpec((tm, w), lambda i: (i, 0))
    return pl.pallas_call(
        _in_proj_body,
        grid=(n // tm,),
        in_specs=[row(D_MODEL), _resident((1, D_MODEL)), _resident((D_MODEL, D_IN)),
                  _resident((1, HEAD_DIM)), _resident((1, HEAD_DIM))],
        out_specs=[row(D_CONV), row(D_ATTN), row(D_KV), row(D_KV)],
        out_shape=[jax.ShapeDtypeStruct((n, D_CONV), F32), jax.ShapeDtypeStruct((n, D_ATTN), q_dtype),
                   jax.ShapeDtypeStruct((n, D_KV), F32), jax.ShapeDtypeStruct((n, D_KV), F32)],
        scratch_shapes=[pltpu.VMEM((tm, D_MODEL), BF16)],
        compiler_params=_params(1),
        name="in_proj",
    )(x2, g1, w_in_bf, qg, kg)


N_CCHUNK = D_CONV // LANES
CONV_ROWS = 64


def _conv_ln_swish(ue_ref, cw_ref, cb_ref, lg_ref, lb_ref, conv_ref, cat_ref, rows, first_row):
    base = first_row - CONV_HIST
    rb = min(CONV_ROWS, rows)

    def chunk(c, carry):
        wc = cw_ref[c]
        for r0 in range(0, rows, rb):
            acc = jnp.broadcast_to(cb_ref[c], (rb, LANES))
            for tap in range(CONV_WIDTH):
                acc = acc + wc[tap:tap + 1, :] * ue_ref[c, base + r0 + tap:base + r0 + tap + rb, :]
            conv_ref[c, r0:r0 + rb, :] = acc
        return carry

    lax.fori_loop(0, N_CCHUNK, chunk, 0)

    tot = jnp.zeros((rows, 1), F32)
    for c in range(N_CCHUNK):
        tot = tot + jnp.sum(conv_ref[c], axis=-1, keepdims=True)
    mean = tot / D_CONV
    var = jnp.zeros((rows, 1), F32)
    for c in range(N_CCHUNK):
        xc = conv_ref[c] - mean
        var = var + jnp.sum(xc * xc, axis=-1, keepdims=True)
    rstd = lax.rsqrt(var / D_CONV + EPS)
    for c in range(N_CCHUNK):
        y = (conv_ref[c] - mean) * rstd * lg_ref[c] + lb_ref[c]
        cat_ref[:, c * LANES:(c + 1) * LANES] = (y * jax.nn.sigmoid(y)).astype(cat_ref.dtype)


def _sink_softmax_rows(s, sink):
    m = jnp.maximum(jnp.max(s, axis=-1, keepdims=True), sink)
    p = jnp.exp(s - m)
    return p / (jnp.sum(p, axis=-1, keepdims=True) + jnp.exp(sink - m))


def _alibi_slope(head):
    return 2.0 ** (-8.0 * (head + 1) / N_HEADS)


def _mixer_prompt_body(sink_ref, u_ref, uh_ref, q_ref, k_ref, kh_ref, v_ref, vh_ref, cw_ref, cb_ref, lg_ref,
                       lb_ref, cat_ref, ue_ref, conv_ref, *, tm, halo):
    j = pl.program_id(1)
    has_prev = j > 0
    for c in range(N_CCHUNK):
        cs = slice(c * LANES, (c + 1) * LANES)
        ue_ref[c, 0:halo, :] = jnp.where(has_prev, uh_ref[:, cs], 0.0)
        ue_ref[c, halo:halo + tm, :] = u_ref[:, cs]
    _conv_ln_swish(ue_ref, cw_ref, cb_ref, lg_ref, lb_ref, conv_ref, cat_ref, tm, halo)

    qi = lax.broadcasted_iota(I32, (BLOCK_Q, 2 * BLOCK_Q), 0)
    kj = lax.broadcasted_iota(I32, (BLOCK_Q, 2 * BLOCK_Q), 1)
    dist = qi + BLOCK_Q - kj
    distf = dist.astype(F32)
    band = jnp.where(dist >= 0, jnp.where(dist < WINDOW, 0.0, NEG_INF), NEG_INF)
    band_first = jnp.where(kj >= BLOCK_Q, band, jnp.where(has_prev, band, NEG_INF))

    for qb in range(tm // BLOCK_Q):
        rows = slice(qb * BLOCK_Q, (qb + 1) * BLOCK_Q)
        prev = slice((qb - 1) * BLOCK_Q, qb * BLOCK_Q)
        mask = band_first if qb == 0 else band
        for kv in range(N_KV_HEADS):
            hs = slice(kv * HEAD_DIM, (kv + 1) * HEAD_DIM)
            k_prev = kh_ref[:, hs] if qb == 0 else k_ref[prev, hs]
            v_prev = vh_ref[:, hs] if qb == 0 else v_ref[prev, hs]
            kk = jnp.concatenate([k_prev, k_ref[rows, hs]], axis=0).astype(BF16)
            vv = jnp.concatenate([v_prev, v_ref[rows, hs]], axis=0).astype(BF16)
            heads = [kv * GQA_GROUP + g for g in range(GQA_GROUP)]
            qs = jnp.concatenate([q_ref[rows, h * HEAD_DIM:(h + 1) * HEAD_DIM] for h in heads], axis=0)
            s = lax.dot_general(qs, kk, (((1,), (1,)), ((), ())), preferred_element_type=F32)
            ps = []
            for g, h in enumerate(heads):
                sg = s[g * BLOCK_Q:(g + 1) * BLOCK_Q] * SCALE - _alibi_slope(h) * distf + mask
                ps.append(_sink_softmax_rows(sg, sink_ref[h]).astype(BF16))
            o = _dot(jnp.concatenate(ps, axis=0), vv)
            for g, h in enumerate(heads):
                cat_ref[rows, D_CONV + h * HEAD_DIM:D_CONV + (h + 1) * HEAD_DIM] = (
                    o[g * BLOCK_Q:(g + 1) * BLOCK_Q].astype(cat_ref.dtype))


def _mixer_prompt(sinks, u, q, k, v, cw, cb, lg, lb, tm):
    b, t, _ = u.shape
    halo = 32
    hpb = tm // halo
    kpb = tm // BLOCK_Q
    main = lambda w: pl.BlockSpec((None, tm, w), lambda bi, j: (bi, j, 0))
    body = functools.partial(_mixer_prompt_body, tm=tm, halo=halo)
    return pl.pallas_call(
        body,
        grid=(b, t // tm),
        in_specs=[pl.BlockSpec(memory_space=pltpu.SMEM),
                  main(D_CONV),
                  pl.BlockSpec((None, halo, D_CONV), lambda bi, j: (bi, jnp.maximum(j * hpb - 1, 0), 0)),
                  main(D_ATTN),
                  main(D_KV),
                  pl.BlockSpec((None, BLOCK_Q, D_KV), lambda bi, j: (bi, jnp.maximum(j * kpb - 1, 0), 0)),
                  main(D_KV),
                  pl.BlockSpec((None, BLOCK_Q, D_KV), lambda bi, j: (bi, jnp.maximum(j * kpb - 1, 0), 0)),
                  _resident((N_CCHUNK, CONV_WIDTH, LANES)), _resident((N_CCHUNK, 1, LANES)),
                  _resident((N_CCHUNK, 1, LANES)), _resident((N_CCHUNK, 1, LANES))],
        out_specs=main(D_MODEL),
        out_shape=jax.ShapeDtypeStruct((b, t, D_MODEL), BF16),
        scratch_shapes=[pltpu.VMEM((N_CCHUNK, halo + tm, LANES), F32),
                        pltpu.VMEM((N_CCHUNK, tm, LANES), F32)],
        compiler_params=_params(2),
        name="mixer_prompt",
    )(sinks, u, u, q, k, k, v, v, cw, cb, lg, lb)


KEY_PAD = 8


def _mixer_sample_body(sink_ref, u_ref, st_ref, q_ref, k_ref, v_ref, ck_ref, cv_ref, cw_ref, cb_ref, lg_ref,
                       lb_ref, cat_ref, nst_ref, nk_ref, nv_ref, ue_ref, conv_ref, kk_ref, vv_ref, *, t_new):
    hist = st_ref.shape[0]
    w_past = ck_ref.shape[0]
    for c in range(N_CCHUNK):
        cs = slice(c * LANES, (c + 1) * LANES)
        ue_ref[c, 0:hist, :] = st_ref[:, cs]
        ue_ref[c, hist:hist + t_new, :] = u_ref[:, cs]
    _conv_ln_swish(ue_ref, cw_ref, cb_ref, lg_ref, lb_ref, conv_ref, cat_ref, t_new, hist)
    nst_ref[0:hist - t_new, :] = st_ref[t_new:hist, :]
    nst_ref[hist - t_new:hist, :] = u_ref[...]

    nk_ref[0:w_past - t_new, :] = ck_ref[t_new:w_past, :]
    nk_ref[w_past - t_new:w_past, :] = k_ref[...]
    nv_ref[0:w_past - t_new, :] = cv_ref[t_new:w_past, :]
    nv_ref[w_past - t_new:w_past, :] = v_ref[...]

    n_keys = w_past + KEY_PAD
    kk_ref[0:w_past, :] = ck_ref[...]
    kk_ref[w_past:n_keys, :] = jnp.zeros((KEY_PAD, D_KV), F32)
    kk_ref[w_past:w_past + t_new, :] = k_ref[...]
    vv_ref[0:w_past, :] = cv_ref[...]
    vv_ref[w_past:n_keys, :] = jnp.zeros((KEY_PAD, D_KV), F32)
    vv_ref[w_past:w_past + t_new, :] = v_ref[...]

    qi = lax.broadcasted_iota(I32, (t_new, n_keys), 0)
    kj = lax.broadcasted_iota(I32, (t_new, n_keys), 1)
    dist = qi + w_past - kj
    distf = dist.astype(F32)
    in_window = jnp.where(dist >= 0, jnp.where(dist < WINDOW, 0.0, NEG_INF), NEG_INF)
    mask = jnp.where(kj < w_past + t_new, in_window, NEG_INF)
    for kv in range(N_KV_HEADS):
        hs = slice(kv * HEAD_DIM, (kv + 1) * HEAD_DIM)
        kk = kk_ref[:, hs].astype(BF16)
        vv = vv_ref[:, hs].astype(BF16)
        for g in range(GQA_GROUP):
            h = kv * GQA_GROUP + g
            qh = q_ref[:, h * HEAD_DIM:(h + 1) * HEAD_DIM].astype(BF16)
            s = lax.dot_general(qh, kk, (((1,), (1,)), ((), ())), preferred_element_type=F32)
            sg = s * SCALE - _alibi_slope(h) * distf + mask
            p = _sink_softmax_rows(sg, sink_ref[h]).astype(BF16)
            cat_ref[:, D_CONV + h * HEAD_DIM:D_CONV + (h + 1) * HEAD_DIM] = _dot(p, vv).astype(cat_ref.dtype)


def _mixer_sample(sinks, u, state, q, k, v, ck, cv, cw, cb, lg, lb):
    b, t_new, _ = u.shape
    hist = state.shape[1]
    w_past = ck.shape[1]
    per = lambda r, w: pl.BlockSpec((None, r, w), lambda bi: (bi, 0, 0))
    body = functools.partial(_mixer_sample_body, t_new=t_new)
    return pl.pallas_call(
        body,
        grid=(b,),
        in_specs=[pl.BlockSpec(memory_space=pltpu.SMEM),
                  per(t_new, D_CONV), per(hist, D_CONV), per(t_new, D_ATTN), per(t_new, D_KV), per(t_new, D_KV),
                  per(w_past, D_KV), per(w_past, D_KV),
                  _resident((N_CCHUNK, CONV_WIDTH, LANES)), _resident((N_CCHUNK, 1, LANES)),
                  _resident((N_CCHUNK, 1, LANES)), _resident((N_CCHUNK, 1, LANES))],
        out_specs=[per(t_new, D_MODEL), per(hist, D_CONV), per(w_past, D_KV), per(w_past, D_KV)],
        out_shape=[jax.ShapeDtypeStruct((b, t_new, D_MODEL), F32),
                   jax.ShapeDtypeStruct((b, hist, D_CONV), F32),
                   jax.ShapeDtypeStruct((b, w_past, D_KV), F32),
                   jax.ShapeDtypeStruct((b, w_past, D_KV), F32)],
        scratch_shapes=[pltpu.VMEM((N_CCHUNK, hist + SUBLANES, LANES), F32),
                        pltpu.VMEM((N_CCHUNK, t_new, LANES), F32),
                        pltpu.VMEM((w_past + KEY_PAD, D_KV), F32),
                        pltpu.VMEM((w_past + KEY_PAD, D_KV), F32)],
        compiler_params=_params(1),
        name="mixer_sample",
    )(sinks, u, state, q, k, v, ck, cv, cw, cb, lg, lb)


OUT_CHUNK = 512


def _out_proj_body(catp_ref, cats_ref, xp_ref, xs_ref, wo_ref, g2_ref, wr_ref, x1_ref, h2_ref, lg_ref, *,
                   n_prompt_tiles):
    is_prompt = pl.program_id(0) < n_prompt_tiles
    tm = catp_ref.shape[0]
    cat = jnp.where(is_prompt, catp_ref[...], cats_ref[...])
    ss = jnp.zeros((tm, 1), F32)
    for c in range(D_MODEL // OUT_CHUNK):
        cs = slice(c * OUT_CHUNK, (c + 1) * OUT_CHUNK)
        y = jnp.where(is_prompt, xp_ref[:, cs], xs_ref[:, cs]) + _dot(cat, wo_ref[:, cs])
        x1_ref[:, cs] = y
        ss = ss + jnp.sum(y * y, axis=-1, keepdims=True)
    r = lax.rsqrt(ss / D_MODEL + EPS)
    for c in range(D_MODEL // OUT_CHUNK):
        cs = slice(c * OUT_CHUNK, (c + 1) * OUT_CHUNK)
        h2_ref[:, cs] = x1_ref[:, cs] * r * g2_ref[:, cs]
    lg_ref[...] = _dot(h2_ref[...].astype(BF16), wr_ref[...])


def _out_proj(cat_p, cat_s, xp2, xs2, wo_bf, g2, wr_bf, tm):
    n_prompt_tiles = cat_p.shape[0] // tm
    n_rows = (n_prompt_tiles + 1) * tm
    prompt = lambda w: pl.BlockSpec((tm, w), lambda i: (jnp.minimum(i, n_prompt_tiles - 1), 0))
    out_row = lambda w: pl.BlockSpec((tm, w), lambda i: (i, 0))
    body = functools.partial(_out_proj_body, n_prompt_tiles=n_prompt_tiles)
    return pl.pallas_call(
        body,
        grid=(n_prompt_tiles + 1,),
        in_specs=[prompt(D_MODEL), _resident((tm, D_MODEL)), prompt(D_MODEL), _resident((tm, D_MODEL)),
                  _resident((D_MODEL, D_MODEL)), _resident((1, D_MODEL)), _resident((D_MODEL, LANES))],
        out_specs=[out_row(D_MODEL), out_row(D_MODEL), out_row(LANES)],
        out_shape=[jax.ShapeDtypeStruct((n_rows, D_MODEL), F32), jax.ShapeDtypeStruct((n_rows, D_MODEL), F32),
                   jax.ShapeDtypeStruct((n_rows, LANES), F32)],
        compiler_params=_params(1),
        name="out_proj",
    )(cat_p, cat_s, xp2, xs2, wo_bf, g2, wr_bf)


ROUTE_CHUNK = 128


def _route_body(lg_ref, bias_ref, slot_ref, gate_ref, blk_ref, cum_ref, sel_ref, *, n_tok, n_blk_rows):
    n_chunks = n_tok // ROUTE_CHUNK
    lane = lax.broadcasted_iota(I32, (ROUTE_CHUNK, LANES), 1).astype(F32)
    ri = lax.broadcasted_iota(I32, (ROUTE_CHUNK, ROUTE_CHUNK), 0)
    ci = lax.broadcasted_iota(I32, (ROUTE_CHUNK, ROUTE_CHUNK), 1)
    lower = jnp.where(ci < ri, 1.0, 0.0).astype(BF16)
    upper = jnp.where(ri < ci, 1.0, 0.0).astype(BF16)
    is_group = lane < N_EXPERT_GROUPS

    def first_max(vals):
        m = jnp.max(vals, axis=-1, keepdims=True)
        idx = jnp.min(jnp.where(vals == m, lane, float(LANES)), axis=-1, keepdims=True)
        return m, idx

    def assign(i, carry):
        rows = pl.ds(pl.multiple_of(i * ROUTE_CHUNK, ROUTE_CHUNK), ROUTE_CHUNK)
        l = lg_ref[rows, :] + bias_ref[...]
        gl = jnp.where(is_group, l, NEG_INF)
        g_max, g_idx = first_max(gl)
        g_top = 1.0 / jnp.sum(jnp.exp(gl - g_max), axis=-1, keepdims=True)
        lo = EXPERT_LANE0 + g_idx * EXPERTS_PER_GROUP
        el = jnp.where(lane >= lo, jnp.where(lane < lo + EXPERTS_PER_GROUP, l, NEG_INF), NEG_INF)
        m1, i1 = first_max(el)
        p = jnp.exp(el - m1)
        probs = p / jnp.sum(p, axis=-1, keepdims=True)
        e1 = jnp.sum(jnp.where(lane == i1, probs, 0.0), axis=-1, keepdims=True)
        _, i2 = first_max(jnp.where(lane == i1, NEG_INF, el))
        e2 = jnp.sum(jnp.where(lane == i2, probs, 0.0), axis=-1, keepdims=True)
        gate1 = g_top * e1 / (e1 + e2)
        gate2 = g_top * e2 / (e1 + e2)
        gate_ref[rows, :] = jnp.where(lane == 0, gate1, jnp.where(lane == 1, gate2, 0.0))
        sel_ref[rows, :] = jnp.where(lane == 0, i1, jnp.where(lane == 1, i2, 0.0))
        onehot = jnp.where(lane == i1, 1.0, jnp.where(lane == i2, 1.0, 0.0))
        before = _dot(lower, onehot.astype(BF16)) + carry
        cum_ref[rows, :] = before
        return carry + jnp.sum(onehot, axis=0, keepdims=True)

    counts = lax.fori_loop(0, n_chunks, assign, jnp.zeros((1, LANES), F32))
    n_blocks = jnp.floor((counts + (MOE_BLOCK - 1)) / MOE_BLOCK)
    nb8 = jnp.broadcast_to(n_blocks, (SUBLANES, LANES)).astype(BF16)
    blk_start = _dot(nb8, upper)[0:1, :]
    row_start = blk_start * MOE_BLOCK

    def place(i, carry):
        rows = pl.ds(pl.multiple_of(i * ROUTE_CHUNK, ROUTE_CHUNK), ROUTE_CHUNK)
        pos = cum_ref[rows, :] + row_start
        sel = sel_ref[rows, :]
        i1 = sel[:, 0:1]
        i2 = sel[:, 1:2]
        s1 = jnp.sum(jnp.where(lane == i1, pos, 0.0), axis=-1, keepdims=True)
        s2 = jnp.sum(jnp.where(lane == i2, pos, 0.0), axis=-1, keepdims=True)
        slot_ref[rows, :] = jnp.where(lane == 0, s1, jnp.where(lane == 1, s2, 0.0)).astype(I32)
        return carry

    lax.fori_loop(0, n_chunks, place, 0)

    blk_end = blk_start + n_blocks
    bi = lax.broadcasted_iota(I32, (n_blk_rows, LANES), 0).astype(F32)
    lane_b = lax.broadcasted_iota(I32, (n_blk_rows, LANES), 1)
    expert_lane = jnp.where(lane_b >= EXPERT_LANE0, jnp.where(lane_b < EXPERT_LANE0 + N_EXPERTS, 1.0, 0.0), 0.0)
    done = expert_lane * jnp.where(blk_end <= bi, 1.0, 0.0)
    blk_exp = jnp.minimum(jnp.sum(done, axis=-1, keepdims=True), N_EXPERTS - 1.0)
    n_used = jnp.max(expert_lane * blk_end, axis=-1, keepdims=True)
    blk_ref[...] = jnp.where(lane_b == 0, blk_exp, jnp.where(lane_b == 1, n_used, 0.0)).astype(I32)


def _route(logits, bias, n_tok, n_blk_rows):
    body = functools.partial(_route_body, n_tok=n_tok, n_blk_rows=n_blk_rows)
    whole = lambda r: pl.BlockSpec((r, LANES), lambda i: (0, 0))
    return pl.pallas_call(
        body,
        grid=(1,),
        in_specs=[whole(n_tok), whole(1)],
        out_specs=[whole(n_tok), whole(n_tok), whole(n_blk_rows)],
        out_shape=[jax.ShapeDtypeStruct((n_tok, LANES), I32), jax.ShapeDtypeStruct((n_tok, LANES), F32),
                   jax.ShapeDtypeStruct((n_blk_rows, LANES), I32)],
        scratch_shapes=[pltpu.VMEM((n_tok, LANES), F32), pltpu.VMEM((n_tok, LANES), F32)],
        compiler_params=_params(1),
        name="route",
    )(logits, bias)


def _invert_body(s0_ref, s1_ref, tok_ref):
    n_slots = tok_ref.shape[0]
    n_tok = s0_ref.shape[0]

    def clear(i, c):
        tok_ref[i] = 0
        return c

    lax.fori_loop(0, n_slots, clear, 0)

    def put(t, c):
        tok_ref[s0_ref[t]] = t
        tok_ref[s1_ref[t]] = t
        return c

    lax.fori_loop(0, n_tok, put, 0)


def _invert(slot0, slot1, n_slots):
    smem = pl.BlockSpec(memory_space=pltpu.SMEM)
    return pl.pallas_call(
        _invert_body,
        in_specs=[smem, smem],
        out_specs=smem,
        out_shape=jax.ShapeDtypeStruct((n_slots,), I32),
        name="invert",
    )(slot0, slot1)


def _row_gather_start(idx_ref, base, src_hbm, dst, sem, n_rows):
    for r in range(n_rows):
        tok = idx_ref[base + r]
        pltpu.make_async_copy(src_hbm.at[pl.ds(tok, 1), :], dst.at[pl.ds(r, 1), :], sem).start()


def _experts_body(blk_ref, used_ref, tok_ref, h2_hbm, w1_ref, w3_ref, w2_ref, out_ref, xbuf, w1b, w3b, w2b, sem):
    i = pl.program_id(0)
    n_used = used_ref[0]
    slot = lax.rem(i, 2)

    def start(block, s):
        _row_gather_start(tok_ref, block * MOE_BLOCK, h2_hbm, xbuf.at[s], sem.at[s], MOE_BLOCK)

    @pl.when(i == 0)
    def _():
        start(0, 0)

    @pl.when(i + 1 < n_used)
    def _():
        start(i + 1, 1 - slot)

    @pl.when(i < n_used)
    def _():
        pltpu.make_async_copy(xbuf.at[slot], xbuf.at[slot], sem.at[slot]).wait()
        prev = blk_ref[jnp.maximum(i - 1, 0)]
        changed = jnp.logical_or(i == 0, blk_ref[i] != prev)

        @pl.when(changed)
        def _():
            w1b[...] = w1_ref[...].astype(BF16)
            w3b[...] = w3_ref[...].astype(BF16)
            w2b[...] = w2_ref[...].astype(BF16)

        x = xbuf[slot].astype(BF16)
        a = _dot(x, w1b[...])
        b = _dot(x, w3b[...])
        hdn = (a * jax.nn.sigmoid(a) * b).astype(BF16)
        out_ref[...] = _dot(hdn, w2b[...])

    @pl.when(i >= n_used)
    def _():
        out_ref[...] = jnp.zeros_like(out_ref)


def _experts(blk_exp, n_used, tok, h2, w1, w3, w2, n_blocks):
    def wspec(r, c):
        return pl.BlockSpec((None, r, c), lambda i, blk_ref, used_ref, tok_ref: (blk_ref[i], 0, 0))

    grid_spec = pltpu.PrefetchScalarGridSpec(
        num_scalar_prefetch=3,
        grid=(n_blocks,),
        in_specs=[pl.BlockSpec(memory_space=pl.ANY),
                  wspec(D_MODEL, D_EXPERT), wspec(D_MODEL, D_EXPERT), wspec(D_EXPERT, D_MODEL)],
        out_specs=pl.BlockSpec((MOE_BLOCK, D_MODEL), lambda i, blk_ref, used_ref, tok_ref: (i, 0)),
        scratch_shapes=[pltpu.VMEM((2, MOE_BLOCK, D_MODEL), F32),
                        pltpu.VMEM((D_MODEL, D_EXPERT), BF16), pltpu.VMEM((D_MODEL, D_EXPERT), BF16),
                        pltpu.VMEM((D_EXPERT, D_MODEL), BF16),
                        pltpu.SemaphoreType.DMA((2,))],
    )
    return pl.pallas_call(
        _experts_body,
        grid_spec=grid_spec,
        out_shape=jax.ShapeDtypeStruct((n_blocks * MOE_BLOCK, D_MODEL), F32),
        compiler_params=_params(1),
        name="experts",
    )(blk_exp, n_used, tok, h2, w1, w3, w2)


def _combine_body(s0_ref, s1_ref, x1_ref, gate_ref, eo_hbm, yp_ref, ys_ref, buf, sem, *, n_tiles, n_prompt_tiles):
    i = pl.program_id(0)
    slot = lax.rem(i, 2)
    tm = x1_ref.shape[0]

    def start(tile, s):
        _row_gather_start(s0_ref, tile * tm, eo_hbm, buf.at[s, 0], sem.at[s], tm)
        _row_gather_start(s1_ref, tile * tm, eo_hbm, buf.at[s, 1], sem.at[s], tm)

    @pl.when(i == 0)
    def _():
        start(0, 0)

    @pl.when(i + 1 < n_tiles)
    def _():
        start(i + 1, 1 - slot)

    pltpu.make_async_copy(buf.at[slot], buf.at[slot], sem.at[slot]).wait()
    g = gate_ref[...]
    y = x1_ref[...] + g[:, 0:1] * buf[slot, 0] + g[:, 1:2] * buf[slot, 1]

    @pl.when(i < n_prompt_tiles)
    def _():
        yp_ref[...] = y

    @pl.when(i >= n_prompt_tiles)
    def _():
        ys_ref[...] = y


def _combine(slot0, slot1, x1, gates, eo, n_prompt, n_tok, tm):
    n_tiles = n_tok // tm
    n_prompt_tiles = n_prompt // tm
    body = functools.partial(_combine_body, n_tiles=n_tiles, n_prompt_tiles=n_prompt_tiles)
    grid_spec = pltpu.PrefetchScalarGridSpec(
        num_scalar_prefetch=2,
        grid=(n_tiles,),
        in_specs=[pl.BlockSpec((tm, D_MODEL), lambda i, a, b: (i, 0)),
                  pl.BlockSpec((tm, LANES), lambda i, a, b: (i, 0)),
                  pl.BlockSpec(memory_space=pl.ANY)],
        out_specs=[pl.BlockSpec((tm, D_MODEL), lambda i, a, b: (jnp.minimum(i, n_prompt_tiles - 1), 0)),
                   pl.BlockSpec((tm, D_MODEL), lambda i, a, b: (jnp.maximum(i - n_prompt_tiles, 0), 0))],
        scratch_shapes=[pltpu.VMEM((2, 2, tm, D_MODEL), F32), pltpu.SemaphoreType.DMA((2,))],
    )
    return pl.pallas_call(
        body,
        grid_spec=grid_spec,
        out_shape=[jax.ShapeDtypeStruct((n_prompt, D_MODEL), F32),
                   jax.ShapeDtypeStruct((n_tok - n_prompt, D_MODEL), F32)],
        compiler_params=_params(1),
        name="combine",
    )(slot0, slot1, x1, gates, eo)


def _layer(x_prompt, x_sample, state_conv, cache_k, cache_v, norm1_g, w_in, conv_w, conv_b, conv_norm_g,
           conv_norm_b, q_norm_g, k_norm_g, attn_sinks, w_out, norm2_g, w_rg, b_rg, w_re, b_re, w1, w3, w2):
    b, t, _ = x_prompt.shape
    sb, st, _ = x_sample.shape
    n_p, n_s = b * t, sb * st
    n_tok = n_p + n_s
    w_past = cache_k.shape[1]

    w_in_bf = w_in.astype(BF16)
    w_out_bf = w_out.astype(BF16)
    g1 = norm1_g.reshape(1, D_MODEL)
    g2 = norm2_g.reshape(1, D_MODEL)
    qg = q_norm_g.reshape(1, HEAD_DIM)
    kg = k_norm_g.reshape(1, HEAD_DIM)
    chunked = lambda a: a.reshape(-1, N_CCHUNK, LANES).transpose(1, 0, 2)
    cw, cb, lg, lb = chunked(conv_w), chunked(conv_b), chunked(conv_norm_g), chunked(conv_norm_b)
    w_router = jnp.zeros((D_MODEL, LANES), F32)
    w_router = w_router.at[:, :N_EXPERT_GROUPS].set(w_rg).at[:, EXPERT_LANE0:EXPERT_LANE0 + N_EXPERTS].set(w_re)
    b_router = jnp.zeros((1, LANES), F32)
    b_router = b_router.at[0, :N_EXPERT_GROUPS].set(b_rg).at[0, EXPERT_LANE0:EXPERT_LANE0 + N_EXPERTS].set(b_re)
    w_router_bf = w_router.astype(BF16)

    xp2 = x_prompt.reshape(n_p, D_MODEL)
    xs2 = x_sample.reshape(n_s, D_MODEL)

    u_p, q_p, k_p, v_p = _in_proj(xp2, g1, w_in_bf, qg, kg, 512, BF16)
    u_s, q_s, k_s, v_s = _in_proj(xs2, g1, w_in_bf, qg, kg, n_s, F32)

    r3 = lambda a, bb: a.reshape(bb, -1, a.shape[-1])
    cat_p = _mixer_prompt(attn_sinks, r3(u_p, b), r3(q_p, b), r3(k_p, b), r3(v_p, b), cw, cb, lg, lb, 256)
    cat_s, conv_s, knew_s, vnew_s = _mixer_sample(
        attn_sinks, r3(u_s, sb), state_conv, r3(q_s, sb), r3(k_s, sb), r3(v_s, sb),
        cache_k.reshape(sb, w_past, D_KV), cache_v.reshape(sb, w_past, D_KV), cw, cb, lg, lb)

    tm_o = 256
    pad_rows = lambda a: jnp.pad(a, ((0, tm_o - n_s), (0, 0)))
    x1, h2, logits = _out_proj(cat_p.reshape(n_p, D_MODEL), pad_rows(cat_s.reshape(n_s, D_MODEL).astype(BF16)),
                               xp2, pad_rows(xs2), w_out_bf, g2, w_router_bf, tm_o)

    n_blocks = -(-(n_tok * 2) // MOE_BLOCK) + N_EXPERTS
    n_blk_rows = -(-n_blocks // SUBLANES) * SUBLANES
    slots, gates, blk = _route(logits, b_router, n_tok, n_blk_rows)
    slot0, slot1 = slots[:, 0], slots[:, 1]
    tok = _invert(slot0, slot1, n_blocks * MOE_BLOCK)
    eo = _experts(blk[:n_blocks, 0], blk[0:1, 1], tok, h2, w1, w3, w2, n_blocks)
    y_p, y_s = _combine(slot0, slot1, x1, gates, eo, n_p, n_tok, MOE_BLOCK)

    conv_p = r3(u_p, b)[:, t - CONV_HIST:, :]
    knew_p = r3(k_p, b)[:, t - WINDOW:, :].reshape(b, WINDOW, N_KV_HEADS, HEAD_DIM)
    vnew_p = r3(v_p, b)[:, t - WINDOW:, :].reshape(b, WINDOW, N_KV_HEADS, HEAD_DIM)
    return (y_p.reshape(b, t, D_MODEL), y_s.reshape(sb, st, D_MODEL), conv_p, knew_p, vnew_p, conv_s,
            knew_s.reshape(sb, w_past, N_KV_HEADS, HEAD_DIM), vnew_s.reshape(sb, w_past, N_KV_HEADS, HEAD_DIM))


def kernel(x_prompt, x_sample, state_conv, cache_k, cache_v, norm1_g, w_in, conv_w, conv_b, conv_norm_g, conv_norm_b, q_norm_g, k_norm_g, attn_sinks, w_out, norm2_g, w_router_group, b_router_group, w_router_expert, b_router_expert, w1, w3, w2):
    depth = w_in.shape[0]
    assert depth == 1, "single-layer step"
    outs = _layer(x_prompt, x_sample, state_conv[0], cache_k[0], cache_v[0], norm1_g[0], w_in[0], conv_w[0],
                  conv_b[0], conv_norm_g[0], conv_norm_b[0], q_norm_g[0], k_norm_g[0], attn_sinks[0], w_out[0],
                  norm2_g[0], w_router_group[0], b_router_group[0], w_router_expert[0], b_router_expert[0],
                  w1[0], w3[0], w2[0])
    y_p, y_s = outs[0], outs[1]
    return (y_p, y_s) + tuple(o[None] for o in outs[2:])
```

```python
import functools
import math

import jax
import jax.numpy as jnp
from jax import lax
from jax.experimental import pallas as pl
from jax.experimental.pallas import tpu as pltpu

F32 = jnp.float32
BF16 = jnp.bfloat16
I32 = jnp.int32

D_MODEL = 2048
D_CONV = 1024
CONV_WIDTH = 31
CONV_HIST = CONV_WIDTH - 1
D_ATTN = 1024
HEAD_DIM = 128
N_HEADS = 8
N_KV_HEADS = 2
GQA_GROUP = N_HEADS // N_KV_HEADS
D_KV = N_KV_HEADS * HEAD_DIM
WINDOW = 128
BLOCK_Q = 128
SCALE = 1.0 / math.sqrt(HEAD_DIM)
N_EXPERT_GROUPS = 4
EXPERTS_PER_GROUP = 8
N_EXPERTS = N_EXPERT_GROUPS * EXPERTS_PER_GROUP
D_EXPERT = 512
MOE_BLOCK = 128
D_IN = 2 * D_CONV + D_ATTN + 2 * D_KV
EPS = 1e-6
PAST_LEN = 16384

LANES = 128
SUBLANES = 8
MXU_COLS = 256
VMEM_LIMIT_BYTES = 56 * 1024 * 1024
NEG_INF = float("-inf")
EXPERT_LANE0 = N_EXPERT_GROUPS


def _params(n_axes):
    return pltpu.CompilerParams(dimension_semantics=("arbitrary",) * n_axes,
                                vmem_limit_bytes=VMEM_LIMIT_BYTES)


def _resident(shape):
    nd = len(shape)
    return pl.BlockSpec(shape, lambda *_: (0,) * nd, pipeline_mode=pl.Buffered(1))


def _dot(a, b):
    return jnp.dot(a, b, preferred_element_type=F32)


def _in_proj_body(x_ref, g1_ref, w_ref, qg_ref, kg_ref, u_ref, q_ref, k_ref, v_ref, n_ref):
    x = x_ref[...]
    ms = jnp.mean(x * x, axis=-1, keepdims=True)
    n_ref[...] = (x * lax.rsqrt(ms + EPS) * g1_ref[...]).astype(BF16)

    def head_norm(h, g):
        return h * lax.rsqrt(jnp.mean(h * h, axis=-1, keepdims=True) + EPS) * g

    ch = MXU_COLS
    for c in range(D_CONV // ch):
        a = _dot(n_ref[...], w_ref[:, c * ch:(c + 1) * ch])
        g = _dot(n_ref[...], w_ref[:, D_CONV + c * ch:D_CONV + (c + 1) * ch])
        u_ref[:, c * ch:(c + 1) * ch] = a * jax.nn.sigmoid(g)
    q_off = 2 * D_CONV
    for c in range(D_ATTN // ch):
        qq = _dot(n_ref[...], w_ref[:, q_off + c * ch:q_off + (c + 1) * ch])
        for j in range(ch // HEAD_DIM):
            qh = head_norm(qq[:, j * HEAD_DIM:(j + 1) * HEAD_DIM], qg_ref[...])
            q_ref[:, c * ch + j * HEAD_DIM:c * ch + (j + 1) * HEAD_DIM] = qh.astype(q_ref.dtype)
    k_off = q_off + D_ATTN
    kk = _dot(n_ref[...], w_ref[:, k_off:k_off + D_KV])
    for j in range(N_KV_HEADS):
        k_ref[:, j * HEAD_DIM:(j + 1) * HEAD_DIM] = head_norm(kk[:, j * HEAD_DIM:(j + 1) * HEAD_DIM], kg_ref[...])
    v_ref[...] = _dot(n_ref[...], w_ref[:, k_off + D_KV:k_off + 2 * D_KV])


def _in_proj(x2, g1, w_in_bf, qg, kg, tm, q_dtype):
    n = x2.shape[0]
    row = lambda w: pl.BlockSpec((tm, w), lambda i: (i, 0))
    return pl.pallas_call(
        _in_proj_body,
        grid=(n // tm,),
        in_specs=[row(D_MODEL), _resident((1, D_MODEL)), _resident((D_MODEL, D_IN)),
                  _resident((1, HEAD_DIM)), _resident((1, HEAD_DIM))],
        out_specs=[row(D_CONV), row(D_ATTN), row(D_KV), row(D_KV)],
        out_shape=[jax.ShapeDtypeStruct((n, D_CONV), F32), jax.ShapeDtypeStruct((n, D_ATTN), q_dtype),
                   jax.ShapeDtypeStruct((n, D_KV), F32), jax.ShapeDtypeStruct((n, D_KV), F32)],
        scratch_shapes=[pltpu.VMEM((tm, D_MODEL), BF16)],
        compiler_params=_params(1),
        name="in_proj",
    )(x2, g1, w_in_bf, qg, kg)


N_CCHUNK = D_CONV // LANES
CONV_ROWS = 64


def _conv_ln_swish(ue_ref, cw_ref, cb_ref, lg_ref, lb_ref, conv_ref, cat_ref, rows, first_row):
    base = first_row - CONV_HIST
    rb = min(CONV_ROWS, rows)

    def chunk(c, carry):
        wc = cw_ref[c]
        for r0 in range(0, rows, rb):
            acc = jnp.broadcast_to(cb_ref[c], (rb, LANES))
            for tap in range(CONV_WIDTH):
                acc = acc + wc[tap:tap + 1, :] * ue_ref[c, base + r0 + tap:base + r0 + tap + rb, :]
            conv_ref[c, r0:r0 + rb, :] = acc
        return carry

    lax.fori_loop(0, N_CCHUNK, chunk, 0)

    tot = jnp.zeros((rows, 1), F32)
    for c in range(N_CCHUNK):
        tot = tot + jnp.sum(conv_ref[c], axis=-1, keepdims=True)
    mean = tot / D_CONV
    var = jnp.zeros((rows, 1), F32)
    for c in range(N_CCHUNK):
        xc = conv_ref[c] - mean
        var = var + jnp.sum(xc * xc, axis=-1, keepdims=True)
    rstd = lax.rsqrt(var / D_CONV + EPS)
    for c in range(N_CCHUNK):
        y = (conv_ref[c] - mean) * rstd * lg_ref[c] + lb_ref[c]
        cat_ref[:, c * LANES:(c + 1) * LANES] = (y * jax.nn.sigmoid(y)).astype(cat_ref.dtype)


def _sink_softmax_rows(s, sink):
    m = jnp.maximum(jnp.max(s, axis=-1, keepdims=True), sink)
    p = jnp.exp(s - m)
    return p / (jnp.sum(p, axis=-1, keepdims=True) + jnp.exp(sink - m))


def _alibi_slope(head):
    return 2.0 ** (-8.0 * (head + 1) / N_HEADS)


def _mixer_prompt_body(sink_ref, u_ref, uh_ref, q_ref, k_ref, kh_ref, v_ref, vh_ref, cw_ref, cb_ref, lg_ref,
                       lb_ref, cat_ref, ue_ref, conv_ref, *, tm, halo):
    j = pl.program_id(1)
    has_prev = j > 0
    for c in range(N_CCHUNK):
        cs = slice(c * LANES, (c + 1) * LANES)
        ue_ref[c, 0:halo, :] = jnp.where(has_prev, uh_ref[:, cs], 0.0)
        ue_ref[c, halo:halo + tm, :] = u_ref[:, cs]
    _conv_ln_swish(ue_ref, cw_ref, cb_ref, lg_ref, lb_ref, conv_ref, cat_ref, tm, halo)

    qi = lax.broadcasted_iota(I32, (BLOCK_Q, 2 * BLOCK_Q), 0)
    kj = lax.broadcasted_iota(I32, (BLOCK_Q, 2 * BLOCK_Q), 1)
    dist = qi + BLOCK_Q - kj
    distf = dist.astype(F32)
    band = jnp.where(dist >= 0, jnp.where(dist < WINDOW, 0.0, NEG_INF), NEG_INF)
    band_first = jnp.where(kj >= BLOCK_Q, band, jnp.where(has_prev, band, NEG_INF))

    for qb in range(tm // BLOCK_Q):
        rows = slice(qb * BLOCK_Q, (qb + 1) * BLOCK_Q)
        prev = slice((qb - 1) * BLOCK_Q, qb * BLOCK_Q)
        mask = band_first if qb == 0 else band
        for kv in range(N_KV_HEADS):
            hs = slice(kv * HEAD_DIM, (kv + 1) * HEAD_DIM)
            k_prev = kh_ref[:, hs] if qb == 0 else k_ref[prev, hs]
            v_prev = vh_ref[:, hs] if qb == 0 else v_ref[prev, hs]
            kk = jnp.concatenate([k_prev, k_ref[rows, hs]], axis=0).astype(BF16)
            vv = jnp.concatenate([v_prev, v_ref[rows, hs]], axis=0).astype(BF16)
            heads = [kv * GQA_GROUP + g for g in range(GQA_GROUP)]
            qs = jnp.concatenate([q_ref[rows, h * HEAD_DIM:(h + 1) * HEAD_DIM] for h in heads], axis=0)
            s = lax.dot_general(qs, kk, (((1,), (1,)), ((), ())), preferred_element_type=F32)
            ps = []
            for g, h in enumerate(heads):
                sg = s[g * BLOCK_Q:(g + 1) * BLOCK_Q] * SCALE - _alibi_slope(h) * distf + mask
                ps.append(_sink_softmax_rows(sg, sink_ref[h]).astype(BF16))
            o = _dot(jnp.concatenate(ps, axis=0), vv)
            for g, h in enumerate(heads):
                cat_ref[rows, D_CONV + h * HEAD_DIM:D_CONV + (h + 1) * HEAD_DIM] = (
                    o[g * BLOCK_Q:(g + 1) * BLOCK_Q].astype(cat_ref.dtype))


def _mixer_prompt(sinks, u, q, k, v, cw, cb, lg, lb, tm):
    b, t, _ = u.shape
    halo = 32
    hpb = tm // halo
    kpb = tm // BLOCK_Q
    main = lambda w: pl.BlockSpec((None, tm, w), lambda bi, j: (bi, j, 0))
    body = functools.partial(_mixer_prompt_body, tm=tm, halo=halo)
    return pl.pallas_call(
        body,
        grid=(b, t // tm),
        in_specs=[pl.BlockSpec(memory_space=pltpu.SMEM),
                  main(D_CONV),
                  pl.BlockSpec((None, halo, D_CONV), lambda bi, j: (bi, jnp.maximum(j * hpb - 1, 0), 0)),
                  main(D_ATTN),
                  main(D_KV),
                  pl.BlockSpec((None, BLOCK_Q, D_KV), lambda bi, j: (bi, jnp.maximum(j * kpb - 1, 0), 0)),
                  main(D_KV),
                  pl.BlockSpec((None, BLOCK_Q, D_KV), lambda bi, j: (bi, jnp.maximum(j * kpb - 1, 0), 0)),
                  _resident((N_CCHUNK, CONV_WIDTH, LANES)), _resident((N_CCHUNK, 1, LANES)),
                  _resident((N_CCHUNK, 1, LANES)), _resident((N_CCHUNK, 1, LANES))],
        out_specs=main(D_MODEL),
        out_shape=jax.ShapeDtypeStruct((b, t, D_MODEL), BF16),
        scratch_shapes=[pltpu.VMEM((N_CCHUNK, halo + tm, LANES), F32),
                        pltpu.VMEM((N_CCHUNK, tm, LANES), F32)],
        compiler_params=_params(2),
        name="mixer_prompt",
    )(sinks, u, u, q, k, k, v, v, cw, cb, lg, lb)


KEY_PAD = 8


def _mixer_sample_body(sink_ref, u_ref, st_ref, q_ref, k_ref, v_ref, ck_ref, cv_ref, cw_ref, cb_ref, lg_ref,
                       lb_ref, cat_ref, nst_ref, nk_ref, nv_ref, ue_ref, conv_ref, kk_ref, vv_ref, *, t_new):
    hist = st_ref.shape[0]
    w_past = ck_ref.shape[0]
    for c in range(N_CCHUNK):
        cs = slice(c * LANES, (c + 1) * LANES)
        ue_ref[c, 0:hist, :] = st_ref[:, cs]
        ue_ref[c, hist:hist + t_new, :] = u_ref[:, cs]
    _conv_ln_swish(ue_ref, cw_ref, cb_ref, lg_ref, lb_ref, conv_ref, cat_ref, t_new, hist)
    nst_ref[0:hist - t_new, :] = st_ref[t_new:hist, :]
    nst_ref[hist - t_new:hist, :] = u_ref[...]

    nk_ref[0:w_past - t_new, :] = ck_ref[t_new:w_past, :]
    nk_ref[w_past - t_new:w_past, :] = k_ref[...]
    nv_ref[0:w_past - t_new, :] = cv_ref[t_new:w_past, :]
    nv_ref[w_past - t_new:w_past, :] = v_ref[...]

    n_keys = w_past + KEY_PAD
    kk_ref[0:w_past, :] = ck_ref[...]
    kk_ref[w_past:n_keys, :] = jnp.zeros((KEY_PAD, D_KV), F32)
    kk_ref[w_past:w_past + t_new, :] = k_ref[...]
    vv_ref[0:w_past, :] = cv_ref[...]
    vv_ref[w_past:n_keys, :] = jnp.zeros((KEY_PAD, D_KV), F32)
    vv_ref[w_past:w_past + t_new, :] = v_ref[...]

    qi = lax.broadcasted_iota(I32, (t_new, n_keys), 0)
    kj = lax.broadcasted_iota(I32, (t_new, n_keys), 1)
    dist = qi + w_past - kj
    distf = dist.astype(F32)
    in_window = jnp.where(dist >= 0, jnp.where(dist < WINDOW, 0.0, NEG_INF), NEG_INF)
    mask = jnp.where(kj < w_past + t_new, in_window, NEG_INF)
    for kv in range(N_KV_HEADS):
        hs = slice(kv * HEAD_DIM, (kv + 1) * HEAD_DIM)
        kk = kk_ref[:, hs].astype(BF16)
        vv = vv_ref[:, hs].astype(BF16)
        for g in range(GQA_GROUP):
            h = kv * GQA_GROUP + g
            qh = q_ref[:, h * HEAD_DIM:(h + 1) * HEAD_DIM].astype(BF16)
            s = lax.dot_general(qh, kk, (((1,), (1,)), ((), ())), preferred_element_type=F32)
            sg = s * SCALE - _alibi_slope(h) * distf + mask
            p = _sink_softmax_rows(sg, sink_ref[h]).astype(BF16)
            cat_ref[:, D_CONV + h * HEAD_DIM:D_CONV + (h + 1) * HEAD_DIM] = _dot(p, vv).astype(cat_ref.dtype)


def _mixer_sample(sinks, u, state, q, k, v, ck, cv, cw, cb, lg, lb):
    b, t_new, _ = u.shape
    hist = state.shape[1]
    w_past = ck.shape[1]
    per = lambda r, w: pl.BlockSpec((None, r, w), lambda bi: (bi, 0, 0))
    body = functools.partial(_mixer_sample_body, t_new=t_new)
    return pl.pallas_call(
        body,
        grid=(b,),
        in_specs=[pl.BlockSpec(memory_space=pltpu.SMEM),
                  per(t_new, D_CONV), per(hist, D_CONV), per(t_new, D_ATTN), per(t_new, D_KV), per(t_new, D_KV),
                  per(w_past, D_KV), per(w_past, D_KV),
                  _resident((N_CCHUNK, CONV_WIDTH, LANES)), _resident((N_CCHUNK, 1, LANES)),
                  _resident((N_CCHUNK, 1, LANES)), _resident((N_CCHUNK, 1, LANES))],
        out_specs=[per(t_new, D_MODEL), per(hist, D_CONV), per(w_past, D_KV), per(w_past, D_KV)],
        out_shape=[jax.ShapeDtypeStruct((b, t_new, D_MODEL), F32),
                   jax.ShapeDtypeStruct((b, hist, D_CONV), F32),
                   jax.ShapeDtypeStruct((b, w_past, D_KV), F32),
                   jax.ShapeDtypeStruct((b, w_past, D_KV), F32)],
        scratch_shapes=[pltpu.VMEM((N_CCHUNK, hist + SUBLANES, LANES), F32),
                        pltpu.VMEM((N_CCHUNK, t_new, LANES), F32),
                        pltpu.VMEM((w_past + KEY_PAD, D_KV), F32),
                        pltpu.VMEM((w_past + KEY_PAD, D_KV), F32)],
        compiler_params=_params(1),
        name="mixer_sample",
    )(sinks, u, state, q, k, v, ck, cv, cw, cb, lg, lb)


OUT_CHUNK = 512
D_HALF = D_MODEL // 2
U32 = jnp.uint32


def _pack_bf16_pairs(hi, lo):
    hi_bits = lax.bitcast_convert_type(hi.astype(BF16).astype(F32), U32)
    lo_bits = lax.bitcast_convert_type(lo.astype(BF16).astype(F32), U32)
    return hi_bits | (lo_bits >> 16)


def _unpack_bf16_pairs(words):
    hi = lax.bitcast_convert_type(words & U32(0xFFFF0000), F32)
    lo = lax.bitcast_convert_type(words << 16, F32)
    return hi, lo


def _out_proj_body(catp_ref, cats_ref, xp_ref, xs_ref, wo_ref, g2_ref, wr_ref, x1_ref, h2p_ref, lg_ref, h_ref, *,
                   n_prompt_tiles):
    is_prompt = pl.program_id(0) < n_prompt_tiles
    tm = catp_ref.shape[0]
    cat = jnp.where(is_prompt, catp_ref[...], cats_ref[...])
    ss = jnp.zeros((tm, 1), F32)
    for c in range(D_MODEL // OUT_CHUNK):
        cs = slice(c * OUT_CHUNK, (c + 1) * OUT_CHUNK)
        y = jnp.where(is_prompt, xp_ref[:, cs], xs_ref[:, cs]) + _dot(cat, wo_ref[:, cs])
        x1_ref[:, cs] = y
        ss = ss + jnp.sum(y * y, axis=-1, keepdims=True)
    r = lax.rsqrt(ss / D_MODEL + EPS)
    for c in range(D_MODEL // OUT_CHUNK):
        cs = slice(c * OUT_CHUNK, (c + 1) * OUT_CHUNK)
        h_ref[:, cs] = x1_ref[:, cs] * r * g2_ref[:, cs]
    for c in range(D_HALF // OUT_CHUNK):
        cs = slice(c * OUT_CHUNK, (c + 1) * OUT_CHUNK)
        cs_lo = slice(D_HALF + c * OUT_CHUNK, D_HALF + (c + 1) * OUT_CHUNK)
        h2p_ref[:, cs] = _pack_bf16_pairs(h_ref[:, cs], h_ref[:, cs_lo])
    lg_ref[...] = _dot(h_ref[...].astype(BF16), wr_ref[...])


def _out_proj(cat_p, cat_s, xp2, xs2, wo_bf, g2, wr_bf, tm):
    n_prompt_tiles = cat_p.shape[0] // tm
    n_rows = (n_prompt_tiles + 1) * tm
    prompt = lambda w: pl.BlockSpec((tm, w), lambda i: (jnp.minimum(i, n_prompt_tiles - 1), 0))
    out_row = lambda w: pl.BlockSpec((tm, w), lambda i: (i, 0))
    body = functools.partial(_out_proj_body, n_prompt_tiles=n_prompt_tiles)
    return pl.pallas_call(
        body,
        grid=(n_prompt_tiles + 1,),
        in_specs=[prompt(D_MODEL), _resident((tm, D_MODEL)), prompt(D_MODEL), _resident((tm, D_MODEL)),
                  _resident((D_MODEL, D_MODEL)), _resident((1, D_MODEL)), _resident((D_MODEL, LANES))],
        out_specs=[out_row(D_MODEL), out_row(D_HALF), out_row(LANES)],
        out_shape=[jax.ShapeDtypeStruct((n_rows, D_MODEL), F32), jax.ShapeDtypeStruct((n_rows, D_HALF), U32),
                   jax.ShapeDtypeStruct((n_rows, LANES), F32)],
        scratch_shapes=[pltpu.VMEM((tm, D_MODEL), F32)],
        compiler_params=_params(1),
        name="out_proj",
    )(cat_p, cat_s, xp2, xs2, wo_bf, g2, wr_bf)


ROUTE_CHUNK = 128


def _route_body(lg_ref, bias_ref, slot_ref, gate_ref, blk_ref, cum_ref, sel_ref, *, n_tok):
    n_chunks = n_tok // ROUTE_CHUNK
    lane = lax.broadcasted_iota(I32, (ROUTE_CHUNK, LANES), 1).astype(F32)
    ri = lax.broadcasted_iota(I32, (ROUTE_CHUNK, ROUTE_CHUNK), 0)
    ci = lax.broadcasted_iota(I32, (ROUTE_CHUNK, ROUTE_CHUNK), 1)
    lower = jnp.where(ci < ri, 1.0, 0.0).astype(BF16)
    upper = jnp.where(ri < ci, 1.0, 0.0).astype(BF16)
    is_group = lane < N_EXPERT_GROUPS

    def first_max(vals):
        m = jnp.max(vals, axis=-1, keepdims=True)
        idx = jnp.min(jnp.where(vals == m, lane, float(LANES)), axis=-1, keepdims=True)
        return m, idx

    def assign(i, carry):
        rows = pl.ds(pl.multiple_of(i * ROUTE_CHUNK, ROUTE_CHUNK), ROUTE_CHUNK)
        l = lg_ref[rows, :] + bias_ref[...]
        gl = jnp.where(is_group, l, NEG_INF)
        g_max, g_idx = first_max(gl)
        g_top = 1.0 / jnp.sum(jnp.exp(gl - g_max), axis=-1, keepdims=True)
        lo = EXPERT_LANE0 + g_idx * EXPERTS_PER_GROUP
        el = jnp.where(lane >= lo, jnp.where(lane < lo + EXPERTS_PER_GROUP, l, NEG_INF), NEG_INF)
        m1, i1 = first_max(el)
        p = jnp.exp(el - m1)
        probs = p / jnp.sum(p, axis=-1, keepdims=True)
        e1 = jnp.sum(jnp.where(lane == i1, probs, 0.0), axis=-1, keepdims=True)
        _, i2 = first_max(jnp.where(lane == i1, NEG_INF, el))
        e2 = jnp.sum(jnp.where(lane == i2, probs, 0.0), axis=-1, keepdims=True)
        gate1 = g_top * e1 / (e1 + e2)
        gate2 = g_top * e2 / (e1 + e2)
        gate_ref[rows, :] = jnp.where(lane == 0, gate1, jnp.where(lane == 1, gate2, 0.0))
        sel_ref[rows, :] = jnp.where(lane == 0, i1, jnp.where(lane == 1, i2, 0.0))
        onehot = jnp.where(lane == i1, 1.0, jnp.where(lane == i2, 1.0, 0.0))
        before = _dot(lower, onehot.astype(BF16)) + carry
        cum_ref[rows, :] = before
        return carry + jnp.sum(onehot, axis=0, keepdims=True)

    counts = lax.fori_loop(0, n_chunks, assign, jnp.zeros((1, LANES), F32))
    n_blocks = jnp.floor((counts + (MOE_BLOCK - 1)) / MOE_BLOCK)
    nb8 = jnp.broadcast_to(n_blocks, (SUBLANES, LANES)).astype(BF16)
    blk_start = _dot(nb8, upper)[0:1, :]
    row_start = blk_start * MOE_BLOCK

    def place(i, carry):
        rows = pl.ds(pl.multiple_of(i * ROUTE_CHUNK, ROUTE_CHUNK), ROUTE_CHUNK)
        pos = cum_ref[rows, :] + row_start
        sel = sel_ref[rows, :]
        i1 = sel[:, 0:1]
        i2 = sel[:, 1:2]
        s1 = jnp.sum(jnp.where(lane == i1, pos, 0.0), axis=-1, keepdims=True)
        s2 = jnp.sum(jnp.where(lane == i2, pos, 0.0), axis=-1, keepdims=True)
        slot_ref[rows, :] = jnp.where(lane == 0, s1, jnp.where(lane == 1, s2, 0.0)).astype(I32)
        return carry

    lax.fori_loop(0, n_chunks, place, 0)

    blk_ref[...] = jnp.broadcast_to(blk_start, (SUBLANES, LANES)).astype(I32)


def _route(logits, bias, n_tok):
    body = functools.partial(_route_body, n_tok=n_tok)
    whole = lambda r: pl.BlockSpec((r, LANES), lambda i: (0, 0))
    return pl.pallas_call(
        body,
        grid=(1,),
        in_specs=[whole(n_tok), whole(1)],
        out_specs=[whole(n_tok), whole(n_tok), whole(SUBLANES)],
        out_shape=[jax.ShapeDtypeStruct((n_tok, LANES), I32), jax.ShapeDtypeStruct((n_tok, LANES), F32),
                   jax.ShapeDtypeStruct((SUBLANES, LANES), I32)],
        scratch_shapes=[pltpu.VMEM((n_tok, LANES), F32), pltpu.VMEM((n_tok, LANES), F32)],
        compiler_params=_params(1),
        name="route",
    )(logits, bias)


INVERT_UNROLL = 16


def _invert_body(s0_ref, s1_ref, tok_ref):
    n_slots = tok_ref.shape[0]
    n_tok = s0_ref.shape[0]

    def clear(i, c):
        tok_ref[i] = 0
        return c

    lax.fori_loop(0, n_slots, clear, 0, unroll=INVERT_UNROLL)

    def put(t, c):
        tok_ref[s0_ref[t]] = t
        tok_ref[s1_ref[t]] = t
        return c

    lax.fori_loop(0, n_tok, put, 0, unroll=INVERT_UNROLL)


def _invert(slot0, slot1, n_slots):
    smem = pl.BlockSpec(memory_space=pltpu.SMEM)
    return pl.pallas_call(
        _invert_body,
        in_specs=[smem, smem],
        out_specs=smem,
        out_shape=jax.ShapeDtypeStruct((n_slots,), I32),
        name="invert",
    )(slot0, slot1)


def _row_gather_start(idx_ref, base, src_hbm, dst, sem, n_rows):
    for r in range(n_rows):
        tok = idx_ref[base + r]
        pltpu.make_async_copy(src_hbm.at[pl.ds(tok, 1), :], dst.at[pl.ds(r, 1), :], sem).start()


def _experts_body(bstart_ref, tok_ref, h2p_hbm, w1_ref, w3_ref, w2_ref, eo_hbm, xbuf, obuf, w1b, w3b, w2b, gsem,
                  osem, *, n_blocks):
    e = pl.program_id(0)
    n_exp = pl.num_programs(0)
    b0 = bstart_ref[e]
    nb = bstart_ref[e + 1] - b0
    n_used = bstart_ref[n_exp]

    def gather(block, s):
        _row_gather_start(tok_ref, block * MOE_BLOCK, h2p_hbm, xbuf.at[s], gsem.at[s], MOE_BLOCK)

    def out_copy(block, s):
        rows = pl.ds(pl.multiple_of(block * MOE_BLOCK, MOE_BLOCK), MOE_BLOCK)
        return pltpu.make_async_copy(obuf.at[s], eo_hbm.at[rows, :], osem.at[s])

    @pl.when(e == 0)
    def _():
        gather(0, 0)

    @pl.when(nb > 0)
    def _():
        w1b[...] = w1_ref[...].astype(BF16)
        w3b[...] = w3_ref[...].astype(BF16)
        w2b[...] = w2_ref[...].astype(BF16)

    def block(j, carry):
        b = b0 + j
        s = lax.rem(b, 2)

        @pl.when(b + 1 < n_used)
        def _():
            gather(b + 1, 1 - s)

        pltpu.make_async_copy(xbuf.at[s], xbuf.at[s], gsem.at[s]).wait()
        hi, lo = _unpack_bf16_pairs(xbuf[s])
        x = jnp.concatenate([hi.astype(BF16), lo.astype(BF16)], axis=1)
        a = _dot(x, w1b[...])
        g = _dot(x, w3b[...])
        hdn = (a * jax.nn.sigmoid(a) * g).astype(BF16)
        o = _dot(hdn, w2b[...])

        @pl.when(b >= 2)
        def _():
            out_copy(b - 2, s).wait()

        obuf[s] = _pack_bf16_pairs(o[:, :D_HALF], o[:, D_HALF:])
        out_copy(b, s).start()
        return carry

    lax.fori_loop(0, nb, block, 0)

    @pl.when(e == n_exp - 1)
    def _():
        @pl.when(n_used >= 2)
        def _():
            out_copy(n_used - 2, lax.rem(n_used, 2)).wait()

        out_copy(n_used - 1, lax.rem(n_used - 1, 2)).wait()
        obuf[0] = jnp.zeros((MOE_BLOCK, D_HALF), U32)

        def fill(tb, carry):
            out_copy(tb, 0).start()
            return carry

        lax.fori_loop(n_used, n_blocks, fill, 0)

        def drain(tb, carry):
            out_copy(tb, 0).wait()
            return carry

        lax.fori_loop(n_used, n_blocks, drain, 0)


def _experts(bstart, tok, h2p, w1, w3, w2, n_blocks):
    def wspec(r, c):
        return pl.BlockSpec((None, r, c), lambda e, bstart_ref, tok_ref: (e, 0, 0))

    grid_spec = pltpu.PrefetchScalarGridSpec(
        num_scalar_prefetch=2,
        grid=(N_EXPERTS,),
        in_specs=[pl.BlockSpec(memory_space=pl.ANY),
                  wspec(D_MODEL, D_EXPERT), wspec(D_MODEL, D_EXPERT), wspec(D_EXPERT, D_MODEL)],
        out_specs=pl.BlockSpec(memory_space=pl.ANY),
        scratch_shapes=[pltpu.VMEM((2, MOE_BLOCK, D_HALF), U32), pltpu.VMEM((2, MOE_BLOCK, D_HALF), U32),
                        pltpu.VMEM((D_MODEL, D_EXPERT), BF16), pltpu.VMEM((D_MODEL, D_EXPERT), BF16),
                        pltpu.VMEM((D_EXPERT, D_MODEL), BF16),
                        pltpu.SemaphoreType.DMA((2,)), pltpu.SemaphoreType.DMA((2,))],
    )
    return pl.pallas_call(
        functools.partial(_experts_body, n_blocks=n_blocks),
        grid_spec=grid_spec,
        out_shape=jax.ShapeDtypeStruct((n_blocks * MOE_BLOCK, D_HALF), U32),
        compiler_params=_params(1),
        name="experts",
    )(bstart, tok, h2p, w1, w3, w2)


def _combine_body(s0_ref, s1_ref, x1_ref, gate_ref, eo_hbm, yp_ref, ys_ref, buf, sem, *, n_tiles, n_prompt_tiles):
    i = pl.program_id(0)
    slot = lax.rem(i, 2)
    tm = x1_ref.shape[0]

    def start(tile, s):
        _row_gather_start(s0_ref, tile * tm, eo_hbm, buf.at[s, 0], sem.at[s], tm)
        _row_gather_start(s1_ref, tile * tm, eo_hbm, buf.at[s, 1], sem.at[s], tm)

    @pl.when(i == 0)
    def _():
        start(0, 0)

    @pl.when(i + 1 < n_tiles)
    def _():
        start(i + 1, 1 - slot)

    pltpu.make_async_copy(buf.at[slot], buf.at[slot], sem.at[slot]).wait()
    g = gate_ref[...]
    hi0, lo0 = _unpack_bf16_pairs(buf[slot, 0])
    hi1, lo1 = _unpack_bf16_pairs(buf[slot, 1])
    y_hi = x1_ref[:, :D_HALF] + g[:, 0:1] * hi0 + g[:, 1:2] * hi1
    y_lo = x1_ref[:, D_HALF:] + g[:, 0:1] * lo0 + g[:, 1:2] * lo1

    @pl.when(i < n_prompt_tiles)
    def _():
        yp_ref[:, :D_HALF] = y_hi
        yp_ref[:, D_HALF:] = y_lo

    @pl.when(i >= n_prompt_tiles)
    def _():
        ys_ref[:, :D_HALF] = y_hi
        ys_ref[:, D_HALF:] = y_lo


def _combine(slot0, slot1, x1, gates, eo, n_prompt, n_tok, tm):
    n_tiles = n_tok // tm
    n_prompt_tiles = n_prompt // tm
    body = functools.partial(_combine_body, n_tiles=n_tiles, n_prompt_tiles=n_prompt_tiles)
    grid_spec = pltpu.PrefetchScalarGridSpec(
        num_scalar_prefetch=2,
        grid=(n_tiles,),
        in_specs=[pl.BlockSpec((tm, D_MODEL), lambda i, a, b: (i, 0)),
                  pl.BlockSpec((tm, LANES), lambda i, a, b: (i, 0)),
                  pl.BlockSpec(memory_space=pl.ANY)],
        out_specs=[pl.BlockSpec((tm, D_MODEL), lambda i, a, b: (jnp.minimum(i, n_prompt_tiles - 1), 0)),
                   pl.BlockSpec((tm, D_MODEL), lambda i, a, b: (jnp.maximum(i - n_prompt_tiles, 0), 0))],
        scratch_shapes=[pltpu.VMEM((2, 2, tm, D_HALF), U32), pltpu.SemaphoreType.DMA((2,))],
    )
    return pl.pallas_call(
        body,
        grid_spec=grid_spec,
        out_shape=[jax.ShapeDtypeStruct((n_prompt, D_MODEL), F32),
                   jax.ShapeDtypeStruct((n_tok - n_prompt, D_MODEL), F32)],
        compiler_params=_params(1),
        name="combine",
    )(slot0, slot1, x1, gates, eo)


def _layer(x_prompt, x_sample, state_conv, cache_k, cache_v, norm1_g, w_in, conv_w, conv_b, conv_norm_g,
           conv_norm_b, q_norm_g, k_norm_g, attn_sinks, w_out, norm2_g, w_rg, b_rg, w_re, b_re, w1, w3, w2):
    b, t, _ = x_prompt.shape
    sb, st, _ = x_sample.shape
    n_p, n_s = b * t, sb * st
    n_tok = n_p + n_s
    w_past = cache_k.shape[1]

    w_in_bf = w_in.astype(BF16)
    w_out_bf = w_out.astype(BF16)
    g1 = norm1_g.reshape(1, D_MODEL)
    g2 = norm2_g.reshape(1, D_MODEL)
    qg = q_norm_g.reshape(1, HEAD_DIM)
    kg = k_norm_g.reshape(1, HEAD_DIM)
    chunked = lambda a: a.reshape(-1, N_CCHUNK, LANES).transpose(1, 0, 2)
    cw, cb, lg, lb = chunked(conv_w), chunked(conv_b), chunked(conv_norm_g), chunked(conv_norm_b)
    w_router = jnp.zeros((D_MODEL, LANES), F32)
    w_router = w_router.at[:, :N_EXPERT_GROUPS].set(w_rg).at[:, EXPERT_LANE0:EXPERT_LANE0 + N_EXPERTS].set(w_re)
    b_router = jnp.zeros((1, LANES), F32)
    b_router = b_router.at[0, :N_EXPERT_GROUPS].set(b_rg).at[0, EXPERT_LANE0:EXPERT_LANE0 + N_EXPERTS].set(b_re)
    w_router_bf = w_router.astype(BF16)

    xp2 = x_prompt.reshape(n_p, D_MODEL)
    xs2 = x_sample.reshape(n_s, D_MODEL)

    u_p, q_p, k_p, v_p = _in_proj(xp2, g1, w_in_bf, qg, kg, 512, BF16)
    u_s, q_s, k_s, v_s = _in_proj(xs2, g1, w_in_bf, qg, kg, n_s, F32)

    r3 = lambda a, bb: a.reshape(bb, -1, a.shape[-1])
    cat_p = _mixer_prompt(attn_sinks, r3(u_p, b), r3(q_p, b), r3(k_p, b), r3(v_p, b), cw, cb, lg, lb, 256)
    cat_s, conv_s, knew_s, vnew_s = _mixer_sample(
        attn_sinks, r3(u_s, sb), state_conv, r3(q_s, sb), r3(k_s, sb), r3(v_s, sb),
        cache_k.reshape(sb, w_past, D_KV), cache_v.reshape(sb, w_past, D_KV), cw, cb, lg, lb)

    tm_o = 256
    pad_rows = lambda a: jnp.pad(a, ((0, tm_o - n_s), (0, 0)))
    x1, h2p, logits = _out_proj(cat_p.reshape(n_p, D_MODEL), pad_rows(cat_s.reshape(n_s, D_MODEL).astype(BF16)),
                               xp2, pad_rows(xs2), w_out_bf, g2, w_router_bf, tm_o)

    n_blocks = -(-(n_tok * 2) // MOE_BLOCK) + N_EXPERTS
    slots, gates, blk = _route(logits, b_router, n_tok)
    slot0, slot1 = slots[:, 0], slots[:, 1]
    tok = _invert(slot0, slot1, n_blocks * MOE_BLOCK)
    bstart = blk[0, EXPERT_LANE0:EXPERT_LANE0 + N_EXPERTS + 1]
    eo = _experts(bstart, tok, h2p, w1, w3, w2, n_blocks)
    y_p, y_s = _combine(slot0, slot1, x1, gates, eo, n_p, n_tok, MOE_BLOCK)

    conv_p = r3(u_p, b)[:, t - CONV_HIST:, :]
    knew_p = r3(k_p, b)[:, t - WINDOW:, :].reshape(b, WINDOW, N_KV_HEADS, HEAD_DIM)
    vnew_p = r3(v_p, b)[:, t - WINDOW:, :].reshape(b, WINDOW, N_KV_HEADS, HEAD_DIM)
    return (y_p.reshape(b, t, D_MODEL), y_s.reshape(sb, st, D_MODEL), conv_p, knew_p, vnew_p, conv_s,
            knew_s.reshape(sb, w_past, N_KV_HEADS, HEAD_DIM), vnew_s.reshape(sb, w_past, N_KV_HEADS, HEAD_DIM))


def kernel(x_prompt, x_sample, state_conv, cache_k, cache_v, norm1_g, w_in, conv_w, conv_b, conv_norm_g, conv_norm_b, q_norm_g, k_norm_g, attn_sinks, w_out, norm2_g, w_router_group, b_router_group, w_router_expert, b_router_expert, w1, w3, w2):
    depth = w_in.shape[0]
    assert depth == 1, "single-layer step"
    outs = _layer(x_prompt, x_sample, state_conv[0], cache_k[0], cache_v[0], norm1_g[0], w_in[0], conv_w[0],
                  conv_b[0], conv_norm_g[0], conv_norm_b[0], q_norm_g[0], k_norm_g[0], attn_sinks[0], w_out[0],
                  norm2_g[0], w_router_group[0], b_router_group[0], w_router_expert[0], b_router_expert[0],
                  w1[0], w3[0], w2[0])
    y_p, y_s = outs[0], outs[1]
    return (y_p, y_s) + tuple(o[None] for o in outs[2:])
```

```python
import functools
import math

import jax
import jax.numpy as jnp
from jax import lax
from jax.experimental import pallas as pl
from jax.experimental.pallas import tpu as pltpu

F32 = jnp.float32
BF16 = jnp.bfloat16
I32 = jnp.int32

D_MODEL = 2048
D_CONV = 1024
CONV_WIDTH = 31
CONV_HIST = CONV_WIDTH - 1
D_ATTN = 1024
HEAD_DIM = 128
N_HEADS = 8
N_KV_HEADS = 2
GQA_GROUP = N_HEADS // N_KV_HEADS
D_KV = N_KV_HEADS * HEAD_DIM
WINDOW = 128
BLOCK_Q = 128
SCALE = 1.0 / math.sqrt(HEAD_DIM)
N_EXPERT_GROUPS = 4
EXPERTS_PER_GROUP = 8
N_EXPERTS = N_EXPERT_GROUPS * EXPERTS_PER_GROUP
D_EXPERT = 512
MOE_BLOCK = 128
D_IN = 2 * D_CONV + D_ATTN + 2 * D_KV
EPS = 1e-6
PAST_LEN = 16384

LANES = 128
SUBLANES = 8
MXU_COLS = 256
VMEM_LIMIT_BYTES = 56 * 1024 * 1024
NEG_INF = float("-inf")
EXPERT_LANE0 = N_EXPERT_GROUPS


def _params(n_axes):
    return pltpu.CompilerParams(dimension_semantics=("arbitrary",) * n_axes,
                                vmem_limit_bytes=VMEM_LIMIT_BYTES)


def _resident(shape):
    nd = len(shape)
    return pl.BlockSpec(shape, lambda *_: (0,) * nd, pipeline_mode=pl.Buffered(1))


def _dot(a, b):
    return jnp.dot(a, b, preferred_element_type=F32)


def _in_proj_body(x_ref, g1_ref, w_ref, qg_ref, kg_ref, u_ref, q_ref, k_ref, v_ref, n_ref):
    x = x_ref[...]
    ms = jnp.mean(x * x, axis=-1, keepdims=True)
    n_ref[...] = (x * lax.rsqrt(ms + EPS) * g1_ref[...]).astype(BF16)

    def head_norm(h, g):
        return h * lax.rsqrt(jnp.mean(h * h, axis=-1, keepdims=True) + EPS) * g

    ch = MXU_COLS
    for c in range(D_CONV // ch):
        a = _dot(n_ref[...], w_ref[:, c * ch:(c + 1) * ch])
        g = _dot(n_ref[...], w_ref[:, D_CONV + c * ch:D_CONV + (c + 1) * ch])
        u_ref[:, c * ch:(c + 1) * ch] = a * jax.nn.sigmoid(g)
    q_off = 2 * D_CONV
    for c in range(D_ATTN // ch):
        qq = _dot(n_ref[...], w_ref[:, q_off + c * ch:q_off + (c + 1) * ch])
        for j in range(ch // HEAD_DIM):
            qh = head_norm(qq[:, j * HEAD_DIM:(j + 1) * HEAD_DIM], qg_ref[...])
            q_ref[:, c * ch + j * HEAD_DIM:c * ch + (j + 1) * HEAD_DIM] = qh.astype(q_ref.dtype)
    k_off = q_off + D_ATTN
    kk = _dot(n_ref[...], w_ref[:, k_off:k_off + D_KV])
    for j in range(N_KV_HEADS):
        k_ref[:, j * HEAD_DIM:(j + 1) * HEAD_DIM] = head_norm(kk[:, j * HEAD_DIM:(j + 1) * HEAD_DIM], kg_ref[...])
    v_ref[...] = _dot(n_ref[...], w_ref[:, k_off + D_KV:k_off + 2 * D_KV])


def _in_proj(x2, g1, w_in_bf, qg, kg, tm, q_dtype):
    n = x2.shape[0]
    row = lambda w: pl.BlockSpec((tm, w), lambda i: (i, 0))
    return pl.pallas_call(
        _in_proj_body,
        grid=(n // tm,),
        in_specs=[row(D_MODEL), _resident((1, D_MODEL)), _resident((D_MODEL, D_IN)),
                  _resident((1, HEAD_DIM)), _resident((1, HEAD_DIM))],
        out_specs=[row(D_CONV), row(D_ATTN), row(D_KV), row(D_KV)],
        out_shape=[jax.ShapeDtypeStruct((n, D_CONV), F32), jax.ShapeDtypeStruct((n, D_ATTN), q_dtype),
                   jax.ShapeDtypeStruct((n, D_KV), F32), jax.ShapeDtypeStruct((n, D_KV), F32)],
        scratch_shapes=[pltpu.VMEM((tm, D_MODEL), BF16)],
        compiler_params=_params(1),
        name="in_proj",
    )(x2, g1, w_in_bf, qg, kg)


N_CCHUNK = D_CONV // LANES
CONV_ROWS = 64


def _conv_ln_swish(ue_ref, cw_ref, cb_ref, lg_ref, lb_ref, conv_ref, cat_ref, rows, first_row):
    base = first_row - CONV_HIST
    rb = min(CONV_ROWS, rows)

    def chunk(c, carry):
        wc = cw_ref[c]
        for r0 in range(0, rows, rb):
            acc = jnp.broadcast_to(cb_ref[c], (rb, LANES))
            for tap in range(CONV_WIDTH):
                acc = acc + wc[tap:tap + 1, :] * ue_ref[c, base + r0 + tap:base + r0 + tap + rb, :]
            conv_ref[c, r0:r0 + rb, :] = acc
        return carry

    lax.fori_loop(0, N_CCHUNK, chunk, 0)

    tot = jnp.zeros((rows, 1), F32)
    for c in range(N_CCHUNK):
        tot = tot + jnp.sum(conv_ref[c], axis=-1, keepdims=True)
    mean = tot / D_CONV
    var = jnp.zeros((rows, 1), F32)
    for c in range(N_CCHUNK):
        xc = conv_ref[c] - mean
        var = var + jnp.sum(xc * xc, axis=-1, keepdims=True)
    rstd = lax.rsqrt(var / D_CONV + EPS)
    for c in range(N_CCHUNK):
        y = (conv_ref[c] - mean) * rstd * lg_ref[c] + lb_ref[c]
        cat_ref[:, c * LANES:(c + 1) * LANES] = (y * jax.nn.sigmoid(y)).astype(cat_ref.dtype)


def _sink_softmax_rows(s, sink):
    m = jnp.maximum(jnp.max(s, axis=-1, keepdims=True), sink)
    p = jnp.exp(s - m)
    return p / (jnp.sum(p, axis=-1, keepdims=True) + jnp.exp(sink - m))


def _alibi_slope(head):
    return 2.0 ** (-8.0 * (head + 1) / N_HEADS)


def _mixer_prompt_body(sink_ref, u_ref, uh_ref, q_ref, k_ref, kh_ref, v_ref, vh_ref, cw_ref, cb_ref, lg_ref,
                       lb_ref, cat_ref, ue_ref, conv_ref, *, tm, halo):
    j = pl.program_id(1)
    has_prev = j > 0
    for c in range(N_CCHUNK):
        cs = slice(c * LANES, (c + 1) * LANES)
        ue_ref[c, 0:halo, :] = jnp.where(has_prev, uh_ref[:, cs], 0.0)
        ue_ref[c, halo:halo + tm, :] = u_ref[:, cs]
    _conv_ln_swish(ue_ref, cw_ref, cb_ref, lg_ref, lb_ref, conv_ref, cat_ref, tm, halo)

    qi = lax.broadcasted_iota(I32, (BLOCK_Q, 2 * BLOCK_Q), 0)
    kj = lax.broadcasted_iota(I32, (BLOCK_Q, 2 * BLOCK_Q), 1)
    dist = qi + BLOCK_Q - kj
    distf = dist.astype(F32)
    band = jnp.where(dist >= 0, jnp.where(dist < WINDOW, 0.0, NEG_INF), NEG_INF)
    band_first = jnp.where(kj >= BLOCK_Q, band, jnp.where(has_prev, band, NEG_INF))

    for qb in range(tm // BLOCK_Q):
        rows = slice(qb * BLOCK_Q, (qb + 1) * BLOCK_Q)
        prev = slice((qb - 1) * BLOCK_Q, qb * BLOCK_Q)
        mask = band_first if qb == 0 else band
        for kv in range(N_KV_HEADS):
            hs = slice(kv * HEAD_DIM, (kv + 1) * HEAD_DIM)
            k_prev = kh_ref[:, hs] if qb == 0 else k_ref[prev, hs]
            v_prev = vh_ref[:, hs] if qb == 0 else v_ref[prev, hs]
            kk = jnp.concatenate([k_prev, k_ref[rows, hs]], axis=0).astype(BF16)
            vv = jnp.concatenate([v_prev, v_ref[rows, hs]], axis=0).astype(BF16)
            heads = [kv * GQA_GROUP + g for g in range(GQA_GROUP)]
            qs = jnp.concatenate([q_ref[rows, h * HEAD_DIM:(h + 1) * HEAD_DIM] for h in heads], axis=0)
            s = lax.dot_general(qs, kk, (((1,), (1,)), ((), ())), preferred_element_type=F32)
            ps = []
            for g, h in enumerate(heads):
                sg = s[g * BLOCK_Q:(g + 1) * BLOCK_Q] * SCALE - _alibi_slope(h) * distf + mask
                ps.append(_sink_softmax_rows(sg, sink_ref[h]).astype(BF16))
            o = _dot(jnp.concatenate(ps, axis=0), vv)
            for g, h in enumerate(heads):
                cat_ref[rows, D_CONV + h * HEAD_DIM:D_CONV + (h + 1) * HEAD_DIM] = (
                    o[g * BLOCK_Q:(g + 1) * BLOCK_Q].astype(cat_ref.dtype))


def _mixer_prompt(sinks, u, q, k, v, cw, cb, lg, lb, tm):
    b, t, _ = u.shape
    halo = 32
    hpb = tm // halo
    kpb = tm // BLOCK_Q
    main = lambda w: pl.BlockSpec((None, tm, w), lambda bi, j: (bi, j, 0))
    body = functools.partial(_mixer_prompt_body, tm=tm, halo=halo)
    return pl.pallas_call(
        body,
        grid=(b, t // tm),
        in_specs=[pl.BlockSpec(memory_space=pltpu.SMEM),
                  main(D_CONV),
                  pl.BlockSpec((None, halo, D_CONV), lambda bi, j: (bi, jnp.maximum(j * hpb - 1, 0), 0)),
                  main(D_ATTN),
                  main(D_KV),
                  pl.BlockSpec((None, BLOCK_Q, D_KV), lambda bi, j: (bi, jnp.maximum(j * kpb - 1, 0), 0)),
                  main(D_KV),
                  pl.BlockSpec((None, BLOCK_Q, D_KV), lambda bi, j: (bi, jnp.maximum(j * kpb - 1, 0), 0)),
                  _resident((N_CCHUNK, CONV_WIDTH, LANES)), _resident((N_CCHUNK, 1, LANES)),
                  _resident((N_CCHUNK, 1, LANES)), _resident((N_CCHUNK, 1, LANES))],
        out_specs=main(D_MODEL),
        out_shape=jax.ShapeDtypeStruct((b, t, D_MODEL), BF16),
        scratch_shapes=[pltpu.VMEM((N_CCHUNK, halo + tm, LANES), F32),
                        pltpu.VMEM((N_CCHUNK, tm, LANES), F32)],
        compiler_params=_params(2),
        name="mixer_prompt",
    )(sinks, u, u, q, k, k, v, v, cw, cb, lg, lb)


KEY_PAD = 8


def _mixer_sample_body(sink_ref, u_ref, st_ref, q_ref, k_ref, v_ref, ck_ref, cv_ref, cw_ref, cb_ref, lg_ref,
                       lb_ref, cat_ref, nst_ref, nk_ref, nv_ref, ue_ref, conv_ref, kk_ref, vv_ref, *, t_new):
    hist = st_ref.shape[0]
    w_past = ck_ref.shape[0]
    for c in range(N_CCHUNK):
        cs = slice(c * LANES, (c + 1) * LANES)
        ue_ref[c, 0:hist, :] = st_ref[:, cs]
        ue_ref[c, hist:hist + t_new, :] = u_ref[:, cs]
    _conv_ln_swish(ue_ref, cw_ref, cb_ref, lg_ref, lb_ref, conv_ref, cat_ref, t_new, hist)
    nst_ref[0:hist - t_new, :] = st_ref[t_new:hist, :]
    nst_ref[hist - t_new:hist, :] = u_ref[...]

    nk_ref[0:w_past - t_new, :] = ck_ref[t_new:w_past, :]
    nk_ref[w_past - t_new:w_past, :] = k_ref[...]
    nv_ref[0:w_past - t_new, :] = cv_ref[t_new:w_past, :]
    nv_ref[w_past - t_new:w_past, :] = v_ref[...]

    n_keys = w_past + KEY_PAD
    kk_ref[0:w_past, :] = ck_ref[...]
    kk_ref[w_past:n_keys, :] = jnp.zeros((KEY_PAD, D_KV), F32)
    kk_ref[w_past:w_past + t_new, :] = k_ref[...]
    vv_ref[0:w_past, :] = cv_ref[...]
    vv_ref[w_past:n_keys, :] = jnp.zeros((KEY_PAD, D_KV), F32)
    vv_ref[w_past:w_past + t_new, :] = v_ref[...]

    qi = lax.broadcasted_iota(I32, (t_new, n_keys), 0)
    kj = lax.broadcasted_iota(I32, (t_new, n_keys), 1)
    dist = qi + w_past - kj
    distf = dist.astype(F32)
    in_window = jnp.where(dist >= 0, jnp.where(dist < WINDOW, 0.0, NEG_INF), NEG_INF)
    mask = jnp.where(kj < w_past + t_new, in_window, NEG_INF)
    for kv in range(N_KV_HEADS):
        hs = slice(kv * HEAD_DIM, (kv + 1) * HEAD_DIM)
        kk = kk_ref[:, hs].astype(BF16)
        vv = vv_ref[:, hs].astype(BF16)
        for g in range(GQA_GROUP):
            h = kv * GQA_GROUP + g
            qh = q_ref[:, h * HEAD_DIM:(h + 1) * HEAD_DIM].astype(BF16)
            s = lax.dot_general(qh, kk, (((1,), (1,)), ((), ())), preferred_element_type=F32)
            sg = s * SCALE - _alibi_slope(h) * distf + mask
            p = _sink_softmax_rows(sg, sink_ref[h]).astype(BF16)
            cat_ref[:, D_CONV + h * HEAD_DIM:D_CONV + (h + 1) * HEAD_DIM] = _dot(p, vv).astype(cat_ref.dtype)


def _mixer_sample(sinks, u, state, q, k, v, ck, cv, cw, cb, lg, lb):
    b, t_new, _ = u.shape
    hist = state.shape[1]
    w_past = ck.shape[1]
    per = lambda r, w: pl.BlockSpec((None, r, w), lambda bi: (bi, 0, 0))
    body = functools.partial(_mixer_sample_body, t_new=t_new)
    return pl.pallas_call(
        body,
        grid=(b,),
        in_specs=[pl.BlockSpec(memory_space=pltpu.SMEM),
                  per(t_new, D_CONV), per(hist, D_CONV), per(t_new, D_ATTN), per(t_new, D_KV), per(t_new, D_KV),
                  per(w_past, D_KV), per(w_past, D_KV),
                  _resident((N_CCHUNK, CONV_WIDTH, LANES)), _resident((N_CCHUNK, 1, LANES)),
                  _resident((N_CCHUNK, 1, LANES)), _resident((N_CCHUNK, 1, LANES))],
        out_specs=[per(t_new, D_MODEL), per(hist, D_CONV), per(w_past, D_KV), per(w_past, D_KV)],
        out_shape=[jax.ShapeDtypeStruct((b, t_new, D_MODEL), F32),
                   jax.ShapeDtypeStruct((b, hist, D_CONV), F32),
                   jax.ShapeDtypeStruct((b, w_past, D_KV), F32),
                   jax.ShapeDtypeStruct((b, w_past, D_KV), F32)],
        scratch_shapes=[pltpu.VMEM((N_CCHUNK, hist + SUBLANES, LANES), F32),
                        pltpu.VMEM((N_CCHUNK, t_new, LANES), F32),
                        pltpu.VMEM((w_past + KEY_PAD, D_KV), F32),
                        pltpu.VMEM((w_past + KEY_PAD, D_KV), F32)],
        compiler_params=_params(1),
        name="mixer_sample",
    )(sinks, u, state, q, k, v, ck, cv, cw, cb, lg, lb)


OUT_CHUNK = 512
D_HALF = D_MODEL // 2
U32 = jnp.uint32


def _pack_bf16_pairs(hi, lo):
    hi_bits = lax.bitcast_convert_type(hi.astype(BF16).astype(F32), U32)
    lo_bits = lax.bitcast_convert_type(lo.astype(BF16).astype(F32), U32)
    return hi_bits | (lo_bits >> 16)


def _unpack_bf16_pairs(words):
    hi = lax.bitcast_convert_type(words & U32(0xFFFF0000), F32)
    lo = lax.bitcast_convert_type(words << 16, F32)
    return hi, lo


def _out_proj_body(catp_ref, cats_ref, xp_ref, xs_ref, wo_ref, g2_ref, wr_ref, x1_ref, h2p_ref, lg_ref, h_ref, *,
                   n_prompt_tiles):
    is_prompt = pl.program_id(0) < n_prompt_tiles
    tm = catp_ref.shape[0]
    cat = jnp.where(is_prompt, catp_ref[...], cats_ref[...])
    ss = jnp.zeros((tm, 1), F32)
    for c in range(D_MODEL // OUT_CHUNK):
        cs = slice(c * OUT_CHUNK, (c + 1) * OUT_CHUNK)
        y = jnp.where(is_prompt, xp_ref[:, cs], xs_ref[:, cs]) + _dot(cat, wo_ref[:, cs])
        x1_ref[:, cs] = y
        ss = ss + jnp.sum(y * y, axis=-1, keepdims=True)
    r = lax.rsqrt(ss / D_MODEL + EPS)
    for c in range(D_MODEL // OUT_CHUNK):
        cs = slice(c * OUT_CHUNK, (c + 1) * OUT_CHUNK)
        h_ref[:, cs] = x1_ref[:, cs] * r * g2_ref[:, cs]
    for c in range(D_HALF // OUT_CHUNK):
        cs = slice(c * OUT_CHUNK, (c + 1) * OUT_CHUNK)
        cs_lo = slice(D_HALF + c * OUT_CHUNK, D_HALF + (c + 1) * OUT_CHUNK)
        h2p_ref[:, cs] = _pack_bf16_pairs(h_ref[:, cs], h_ref[:, cs_lo])
    lg_ref[...] = _dot(h_ref[...].astype(BF16), wr_ref[...])


def _out_proj(cat_p, cat_s, xp2, xs2, wo_bf, g2, wr_bf, tm):
    n_prompt_tiles = cat_p.shape[0] // tm
    n_rows = (n_prompt_tiles + 1) * tm
    prompt = lambda w: pl.BlockSpec((tm, w), lambda i: (jnp.minimum(i, n_prompt_tiles - 1), 0))
    out_row = lambda w: pl.BlockSpec((tm, w), lambda i: (i, 0))
    body = functools.partial(_out_proj_body, n_prompt_tiles=n_prompt_tiles)
    return pl.pallas_call(
        body,
        grid=(n_prompt_tiles + 1,),
        in_specs=[prompt(D_MODEL), _resident((tm, D_MODEL)), prompt(D_MODEL), _resident((tm, D_MODEL)),
                  _resident((D_MODEL, D_MODEL)), _resident((1, D_MODEL)), _resident((D_MODEL, LANES))],
        out_specs=[out_row(D_MODEL), out_row(D_HALF), out_row(LANES)],
        out_shape=[jax.ShapeDtypeStruct((n_rows, D_MODEL), F32), jax.ShapeDtypeStruct((n_rows, D_HALF), U32),
                   jax.ShapeDtypeStruct((n_rows, LANES), F32)],
        scratch_shapes=[pltpu.VMEM((tm, D_MODEL), F32)],
        compiler_params=_params(1),
        name="out_proj",
    )(cat_p, cat_s, xp2, xs2, wo_bf, g2, wr_bf)


ROUTE_CHUNK = 128


def _route_body(lg_ref, bias_ref, slot_ref, gate_ref, blk_ref, cum_ref, sel_ref, *, n_tok):
    n_chunks = n_tok // ROUTE_CHUNK
    lane = lax.broadcasted_iota(I32, (ROUTE_CHUNK, LANES), 1).astype(F32)
    ri = lax.broadcasted_iota(I32, (ROUTE_CHUNK, ROUTE_CHUNK), 0)
    ci = lax.broadcasted_iota(I32, (ROUTE_CHUNK, ROUTE_CHUNK), 1)
    lower = jnp.where(ci < ri, 1.0, 0.0).astype(BF16)
    upper = jnp.where(ri < ci, 1.0, 0.0).astype(BF16)
    is_group = lane < N_EXPERT_GROUPS

    def first_max(vals):
        m = jnp.max(vals, axis=-1, keepdims=True)
        idx = jnp.min(jnp.where(vals == m, lane, float(LANES)), axis=-1, keepdims=True)
        return m, idx

    def assign(i, carry):
        rows = pl.ds(pl.multiple_of(i * ROUTE_CHUNK, ROUTE_CHUNK), ROUTE_CHUNK)
        l = lg_ref[rows, :] + bias_ref[...]
        gl = jnp.where(is_group, l, NEG_INF)
        g_max, g_idx = first_max(gl)
        g_top = 1.0 / jnp.sum(jnp.exp(gl - g_max), axis=-1, keepdims=True)
        lo = EXPERT_LANE0 + g_idx * EXPERTS_PER_GROUP
        el = jnp.where(lane >= lo, jnp.where(lane < lo + EXPERTS_PER_GROUP, l, NEG_INF), NEG_INF)
        m1, i1 = first_max(el)
        p = jnp.exp(el - m1)
        probs = p / jnp.sum(p, axis=-1, keepdims=True)
        e1 = jnp.sum(jnp.where(lane == i1, probs, 0.0), axis=-1, keepdims=True)
        _, i2 = first_max(jnp.where(lane == i1, NEG_INF, el))
        e2 = jnp.sum(jnp.where(lane == i2, probs, 0.0), axis=-1, keepdims=True)
        gate1 = g_top * e1 / (e1 + e2)
        gate2 = g_top * e2 / (e1 + e2)
        gate_ref[rows, :] = jnp.where(lane == 0, gate1, jnp.where(lane == 1, gate2, 0.0))
        sel_ref[rows, :] = jnp.where(lane == 0, i1, jnp.where(lane == 1, i2, 0.0))
        onehot = jnp.where(lane == i1, 1.0, jnp.where(lane == i2, 1.0, 0.0))
        before = _dot(lower, onehot.astype(BF16)) + carry
        cum_ref[rows, :] = before
        return carry + jnp.sum(onehot, axis=0, keepdims=True)

    counts = lax.fori_loop(0, n_chunks, assign, jnp.zeros((1, LANES), F32))
    n_blocks = jnp.floor((counts + (MOE_BLOCK - 1)) / MOE_BLOCK)
    nb8 = jnp.broadcast_to(n_blocks, (SUBLANES, LANES)).astype(BF16)
    blk_start = _dot(nb8, upper)[0:1, :]
    row_start = blk_start * MOE_BLOCK

    def place(i, carry):
        rows = pl.ds(pl.multiple_of(i * ROUTE_CHUNK, ROUTE_CHUNK), ROUTE_CHUNK)
        pos = cum_ref[rows, :] + row_start
        sel = sel_ref[rows, :]
        i1 = sel[:, 0:1]
        i2 = sel[:, 1:2]
        s1 = jnp.sum(jnp.where(lane == i1, pos, 0.0), axis=-1, keepdims=True)
        s2 = jnp.sum(jnp.where(lane == i2, pos, 0.0), axis=-1, keepdims=True)
        slot_ref[rows, :] = jnp.where(lane == 0, s1, jnp.where(lane == 1, s2, 0.0)).astype(I32)
        return carry

    lax.fori_loop(0, n_chunks, place, 0)

    blk_ref[...] = jnp.broadcast_to(blk_start, (SUBLANES, LANES)).astype(I32)


def _route(logits, bias, n_tok):
    body = functools.partial(_route_body, n_tok=n_tok)
    whole = lambda r: pl.BlockSpec((r, LANES), lambda i: (0, 0))
    return pl.pallas_call(
        body,
        grid=(1,),
        in_specs=[whole(n_tok), whole(1)],
        out_specs=[whole(n_tok), whole(n_tok), whole(SUBLANES)],
        out_shape=[jax.ShapeDtypeStruct((n_tok, LANES), I32), jax.ShapeDtypeStruct((n_tok, LANES), F32),
                   jax.ShapeDtypeStruct((SUBLANES, LANES), I32)],
        scratch_shapes=[pltpu.VMEM((n_tok, LANES), F32), pltpu.VMEM((n_tok, LANES), F32)],
        compiler_params=_params(1),
        name="route",
    )(logits, bias)


INVERT_UNROLL = 16


def _invert_body(s0_ref, s1_ref, tok_ref):
    n_slots = tok_ref.shape[0]
    n_tok = s0_ref.shape[0]

    def clear(i, c):
        tok_ref[i] = 0
        return c

    lax.fori_loop(0, n_slots, clear, 0, unroll=INVERT_UNROLL)

    def put(t, c):
        tok_ref[s0_ref[t]] = t
        tok_ref[s1_ref[t]] = t
        return c

    lax.fori_loop(0, n_tok, put, 0, unroll=INVERT_UNROLL)


def _invert(slot0, slot1, n_slots):
    smem = pl.BlockSpec(memory_space=pltpu.SMEM)
    return pl.pallas_call(
        _invert_body,
        in_specs=[smem, smem],
        out_specs=smem,
        out_shape=jax.ShapeDtypeStruct((n_slots,), I32),
        name="invert",
    )(slot0, slot1)


def _row_gather_start(idx_ref, base, src_hbm, dst, sem, n_rows, priorities):
    for r in range(n_rows):
        tok = idx_ref[base + r]
        pltpu.make_async_copy(src_hbm.at[pl.ds(tok, 1), :], dst.at[pl.ds(r, 1), :], sem).start(
            priority=priorities[r % len(priorities)])


def _experts_body(bstart_ref, tok_ref, h2p_hbm, w1_ref, w3_ref, w2_ref, eo_hbm, xbuf, obuf, w1b, w3b, w2b, gsem,
                  osem, *, n_blocks):
    e = pl.program_id(0)
    n_exp = pl.num_programs(0)
    b0 = bstart_ref[e]
    nb = bstart_ref[e + 1] - b0
    n_used = bstart_ref[n_exp]

    def gather(block, s):
        _row_gather_start(tok_ref, block * MOE_BLOCK, h2p_hbm, xbuf.at[s], gsem.at[s], MOE_BLOCK, (1,))

    def out_copy(block, s):
        rows = pl.ds(pl.multiple_of(block * MOE_BLOCK, MOE_BLOCK), MOE_BLOCK)
        return pltpu.make_async_copy(obuf.at[s], eo_hbm.at[rows, :], osem.at[s])

    @pl.when(e == 0)
    def _():
        gather(0, 0)

    @pl.when(nb > 0)
    def _():
        w1b[...] = w1_ref[...].astype(BF16)
        w3b[...] = w3_ref[...].astype(BF16)
        w2b[...] = w2_ref[...].astype(BF16)

    def block(j, carry):
        b = b0 + j
        s = lax.rem(b, 2)

        @pl.when(b + 1 < n_used)
        def _():
            gather(b + 1, 1 - s)

        pltpu.make_async_copy(xbuf.at[s], xbuf.at[s], gsem.at[s]).wait()
        hi, lo = _unpack_bf16_pairs(xbuf[s])
        x = jnp.concatenate([hi.astype(BF16), lo.astype(BF16)], axis=1)
        a = _dot(x, w1b[...])
        g = _dot(x, w3b[...])
        hdn = (a * jax.nn.sigmoid(a) * g).astype(BF16)
        o = _dot(hdn, w2b[...])

        @pl.when(b >= 2)
        def _():
            out_copy(b - 2, s).wait()

        obuf[s] = _pack_bf16_pairs(o[:, :D_HALF], o[:, D_HALF:])
        out_copy(b, s).start()
        return carry

    lax.fori_loop(0, nb, block, 0)

    @pl.when(e == n_exp - 1)
    def _():
        @pl.when(n_used >= 2)
        def _():
            out_copy(n_used - 2, lax.rem(n_used, 2)).wait()

        out_copy(n_used - 1, lax.rem(n_used - 1, 2)).wait()
        obuf[0] = jnp.zeros((MOE_BLOCK, D_HALF), U32)

        def fill(tb, carry):
            out_copy(tb, 0).start()
            return carry

        lax.fori_loop(n_used, n_blocks, fill, 0)

        def drain(tb, carry):
            out_copy(tb, 0).wait()
            return carry

        lax.fori_loop(n_used, n_blocks, drain, 0)


def _experts(bstart, tok, h2p, w1, w3, w2, n_blocks):
    def wspec(r, c):
        return pl.BlockSpec((None, r, c), lambda e, bstart_ref, tok_ref: (e, 0, 0))

    grid_spec = pltpu.PrefetchScalarGridSpec(
        num_scalar_prefetch=2,
        grid=(N_EXPERTS,),
        in_specs=[pl.BlockSpec(memory_space=pl.ANY),
                  wspec(D_MODEL, D_EXPERT), wspec(D_MODEL, D_EXPERT), wspec(D_EXPERT, D_MODEL)],
        out_specs=pl.BlockSpec(memory_space=pl.ANY),
        scratch_shapes=[pltpu.VMEM((2, MOE_BLOCK, D_HALF), U32), pltpu.VMEM((2, MOE_BLOCK, D_HALF), U32),
                        pltpu.VMEM((D_MODEL, D_EXPERT), BF16), pltpu.VMEM((D_MODEL, D_EXPERT), BF16),
                        pltpu.VMEM((D_EXPERT, D_MODEL), BF16),
                        pltpu.SemaphoreType.DMA((2,)), pltpu.SemaphoreType.DMA((2,))],
    )
    return pl.pallas_call(
        functools.partial(_experts_body, n_blocks=n_blocks),
        grid_spec=grid_spec,
        out_shape=jax.ShapeDtypeStruct((n_blocks * MOE_BLOCK, D_HALF), U32),
        compiler_params=_params(1),
        name="experts",
    )(bstart, tok, h2p, w1, w3, w2)


def _combine_body(s0_ref, s1_ref, x1_ref, gate_ref, eo_hbm, yp_ref, ys_ref, buf, sem, *, n_tiles, n_prompt_tiles):
    i = pl.program_id(0)
    slot = lax.rem(i, 2)
    tm = x1_ref.shape[0]

    def start(tile, s):
        _row_gather_start(s0_ref, tile * tm, eo_hbm, buf.at[s, 0], sem.at[s], tm, (0, 1))
        _row_gather_start(s1_ref, tile * tm, eo_hbm, buf.at[s, 1], sem.at[s], tm, (0, 1))

    @pl.when(i == 0)
    def _():
        start(0, 0)

    @pl.when(i + 1 < n_tiles)
    def _():
        start(i + 1, 1 - slot)

    pltpu.make_async_copy(buf.at[slot], buf.at[slot], sem.at[slot]).wait()
    g = gate_ref[...]
    hi0, lo0 = _unpack_bf16_pairs(buf[slot, 0])
    hi1, lo1 = _unpack_bf16_pairs(buf[slot, 1])
    y_hi = x1_ref[:, :D_HALF] + g[:, 0:1] * hi0 + g[:, 1:2] * hi1
    y_lo = x1_ref[:, D_HALF:] + g[:, 0:1] * lo0 + g[:, 1:2] * lo1

    @pl.when(i < n_prompt_tiles)
    def _():
        yp_ref[:, :D_HALF] = y_hi
        yp_ref[:, D_HALF:] = y_lo

    @pl.when(i >= n_prompt_tiles)
    def _():
        ys_ref[:, :D_HALF] = y_hi
        ys_ref[:, D_HALF:] = y_lo


def _combine(slot0, slot1, x1, gates, eo, n_prompt, n_tok, tm):
    n_tiles = n_tok // tm
    n_prompt_tiles = n_prompt // tm
    body = functools.partial(_combine_body, n_tiles=n_tiles, n_prompt_tiles=n_prompt_tiles)
    grid_spec = pltpu.PrefetchScalarGridSpec(
        num_scalar_prefetch=2,
        grid=(n_tiles,),
        in_specs=[pl.BlockSpec((tm, D_MODEL), lambda i, a, b: (i, 0)),
                  pl.BlockSpec((tm, LANES), lambda i, a, b: (i, 0)),
                  pl.BlockSpec(memory_space=pl.ANY)],
        out_specs=[pl.BlockSpec((tm, D_MODEL), lambda i, a, b: (jnp.minimum(i, n_prompt_tiles - 1), 0)),
                   pl.BlockSpec((tm, D_MODEL), lambda i, a, b: (jnp.maximum(i - n_prompt_tiles, 0), 0))],
        scratch_shapes=[pltpu.VMEM((2, 2, tm, D_HALF), U32), pltpu.SemaphoreType.DMA((2,))],
    )
    return pl.pallas_call(
        body,
        grid_spec=grid_spec,
        out_shape=[jax.ShapeDtypeStruct((n_prompt, D_MODEL), F32),
                   jax.ShapeDtypeStruct((n_tok - n_prompt, D_MODEL), F32)],
        compiler_params=_params(1),
        name="combine",
    )(slot0, slot1, x1, gates, eo)


def _layer(x_prompt, x_sample, state_conv, cache_k, cache_v, norm1_g, w_in, conv_w, conv_b, conv_norm_g,
           conv_norm_b, q_norm_g, k_norm_g, attn_sinks, w_out, norm2_g, w_rg, b_rg, w_re, b_re, w1, w3, w2):
    b, t, _ = x_prompt.shape
    sb, st, _ = x_sample.shape
    n_p, n_s = b * t, sb * st
    n_tok = n_p + n_s
    w_past = cache_k.shape[1]

    w_in_bf = w_in.astype(BF16)
    w_out_bf = w_out.astype(BF16)
    g1 = norm1_g.reshape(1, D_MODEL)
    g2 = norm2_g.reshape(1, D_MODEL)
    qg = q_norm_g.reshape(1, HEAD_DIM)
    kg = k_norm_g.reshape(1, HEAD_DIM)
    chunked = lambda a: a.reshape(-1, N_CCHUNK, LANES).transpose(1, 0, 2)
    cw, cb, lg, lb = chunked(conv_w), chunked(conv_b), chunked(conv_norm_g), chunked(conv_norm_b)
    w_router = jnp.zeros((D_MODEL, LANES), F32)
    w_router = w_router.at[:, :N_EXPERT_GROUPS].set(w_rg).at[:, EXPERT_LANE0:EXPERT_LANE0 + N_EXPERTS].set(w_re)
    b_router = jnp.zeros((1, LANES), F32)
    b_router = b_router.at[0, :N_EXPERT_GROUPS].set(b_rg).at[0, EXPERT_LANE0:EXPERT_LANE0 + N_EXPERTS].set(b_re)
    w_router_bf = w_router.astype(BF16)

    xp2 = x_prompt.reshape(n_p, D_MODEL)
    xs2 = x_sample.reshape(n_s, D_MODEL)

    u_p, q_p, k_p, v_p = _in_proj(xp2, g1, w_in_bf, qg, kg, 512, BF16)
    u_s, q_s, k_s, v_s = _in_proj(xs2, g1, w_in_bf, qg, kg, n_s, F32)

    r3 = lambda a, bb: a.reshape(bb, -1, a.shape[-1])
    cat_p = _mixer_prompt(attn_sinks, r3(u_p, b), r3(q_p, b), r3(k_p, b), r3(v_p, b), cw, cb, lg, lb, 256)
    cat_s, conv_s, knew_s, vnew_s = _mixer_sample(
        attn_sinks, r3(u_s, sb), state_conv, r3(q_s, sb), r3(k_s, sb), r3(v_s, sb),
        cache_k.reshape(sb, w_past, D_KV), cache_v.reshape(sb, w_past, D_KV), cw, cb, lg, lb)

    tm_o = 256
    pad_rows = lambda a: jnp.pad(a, ((0, tm_o - n_s), (0, 0)))
    x1, h2p, logits = _out_proj(cat_p.reshape(n_p, D_MODEL), pad_rows(cat_s.reshape(n_s, D_MODEL).astype(BF16)),
                               xp2, pad_rows(xs2), w_out_bf, g2, w_router_bf, tm_o)

    n_blocks = -(-(n_tok * 2) // MOE_BLOCK) + N_EXPERTS
    slots, gates, blk = _route(logits, b_router, n_tok)
    slot0, slot1 = slots[:, 0], slots[:, 1]
    tok = _invert(slot0, slot1, n_blocks * MOE_BLOCK)
    bstart = blk[0, EXPERT_LANE0:EXPERT_LANE0 + N_EXPERTS + 1]
    eo = _experts(bstart, tok, h2p, w1, w3, w2, n_blocks)
    y_p, y_s = _combine(slot0, slot1, x1, gates, eo, n_p, n_tok, MOE_BLOCK)

    conv_p = r3(u_p, b)[:, t - CONV_HIST:, :]
    knew_p = r3(k_p, b)[:, t - WINDOW:, :].reshape(b, WINDOW, N_KV_HEADS, HEAD_DIM)
    vnew_p = r3(v_p, b)[:, t - WINDOW:, :].reshape(b, WINDOW, N_KV_HEADS, HEAD_DIM)
    return (y_p.reshape(b, t, D_MODEL), y_s.reshape(sb, st, D_MODEL), conv_p, knew_p, vnew_p, conv_s,
            knew_s.reshape(sb, w_past, N_KV_HEADS, HEAD_DIM), vnew_s.reshape(sb, w_past, N_KV_HEADS, HEAD_DIM))


def kernel(x_prompt, x_sample, state_conv, cache_k, cache_v, norm1_g, w_in, conv_w, conv_b, conv_norm_g, conv_norm_b, q_norm_g, k_norm_g, attn_sinks, w_out, norm2_g, w_router_group, b_router_group, w_router_expert, b_router_expert, w1, w3, w2):
    depth = w_in.shape[0]
    assert depth == 1, "single-layer step"
    outs = _layer(x_prompt, x_sample, state_conv[0], cache_k[0], cache_v[0], norm1_g[0], w_in[0], conv_w[0],
                  conv_b[0], conv_norm_g[0], conv_norm_b[0], q_norm_g[0], k_norm_g[0], attn_sinks[0], w_out[0],
                  norm2_g[0], w_router_group[0], b_router_group[0], w_router_expert[0], b_router_expert[0],
                  w1[0], w3[0], w2[0])
    y_p, y_s = outs[0], outs[1]
    return (y_p, y_s) + tuple(o[None] for o in outs[2:])
```

```python
import functools
import math

import jax
import jax.numpy as jnp
from jax import lax
from jax.experimental import pallas as pl
from jax.experimental.pallas import tpu as pltpu

F32 = jnp.float32
BF16 = jnp.bfloat16
I32 = jnp.int32

D_MODEL = 2048
D_CONV = 1024
CONV_WIDTH = 31
CONV_HIST = CONV_WIDTH - 1
D_ATTN = 1024
HEAD_DIM = 128
N_HEADS = 8
N_KV_HEADS = 2
GQA_GROUP = N_HEADS // N_KV_HEADS
D_KV = N_KV_HEADS * HEAD_DIM
WINDOW = 128
BLOCK_Q = 128
SCALE = 1.0 / math.sqrt(HEAD_DIM)
N_EXPERT_GROUPS = 4
EXPERTS_PER_GROUP = 8
N_EXPERTS = N_EXPERT_GROUPS * EXPERTS_PER_GROUP
D_EXPERT = 512
MOE_BLOCK = 256
COMBINE_TILE = 128
D_IN = 2 * D_CONV + D_ATTN + 2 * D_KV
EPS = 1e-6
PAST_LEN = 16384

LANES = 128
SUBLANES = 8
MXU_COLS = 256
VMEM_LIMIT_BYTES = 56 * 1024 * 1024
NEG_INF = float("-inf")
EXPERT_LANE0 = N_EXPERT_GROUPS


def _params(n_axes):
    return pltpu.CompilerParams(dimension_semantics=("arbitrary",) * n_axes,
                                vmem_limit_bytes=VMEM_LIMIT_BYTES)


def _resident(shape):
    nd = len(shape)
    return pl.BlockSpec(shape, lambda *_: (0,) * nd, pipeline_mode=pl.Buffered(1))


def _dot(a, b):
    return jnp.dot(a, b, preferred_element_type=F32)


def _in_proj_body(x_ref, g1_ref, w_ref, qg_ref, kg_ref, u_ref, q_ref, k_ref, v_ref, n_ref):
    x = x_ref[...]
    ms = jnp.mean(x * x, axis=-1, keepdims=True)
    n_ref[...] = (x * lax.rsqrt(ms + EPS) * g1_ref[...]).astype(BF16)

    def head_norm(h, g):
        return h * lax.rsqrt(jnp.mean(h * h, axis=-1, keepdims=True) + EPS) * g

    ch = MXU_COLS
    for c in range(D_CONV // ch):
        a = _dot(n_ref[...], w_ref[:, c * ch:(c + 1) * ch])
        g = _dot(n_ref[...], w_ref[:, D_CONV + c * ch:D_CONV + (c + 1) * ch])
        u_ref[:, c * ch:(c + 1) * ch] = a * jax.nn.sigmoid(g)
    q_off = 2 * D_CONV
    for c in range(D_ATTN // ch):
        qq = _dot(n_ref[...], w_ref[:, q_off + c * ch:q_off + (c + 1) * ch])
        for j in range(ch // HEAD_DIM):
            qh = head_norm(qq[:, j * HEAD_DIM:(j + 1) * HEAD_DIM], qg_ref[...])
            q_ref[:, c * ch + j * HEAD_DIM:c * ch + (j + 1) * HEAD_DIM] = qh.astype(q_ref.dtype)
    k_off = q_off + D_ATTN
    kk = _dot(n_ref[...], w_ref[:, k_off:k_off + D_KV])
    for j in range(N_KV_HEADS):
        k_ref[:, j * HEAD_DIM:(j + 1) * HEAD_DIM] = head_norm(kk[:, j * HEAD_DIM:(j + 1) * HEAD_DIM], kg_ref[...])
    v_ref[...] = _dot(n_ref[...], w_ref[:, k_off + D_KV:k_off + 2 * D_KV])


def _in_proj(x2, g1, w_in_bf, qg, kg, tm, q_dtype):
    n = x2.shape[0]
    row = lambda w: pl.BlockSpec((tm, w), lambda i: (i, 0))
    return pl.pallas_call(
        _in_proj_body,
        grid=(n // tm,),
        in_specs=[row(D_MODEL), _resident((1, D_MODEL)), _resident((D_MODEL, D_IN)),
                  _resident((1, HEAD_DIM)), _resident((1, HEAD_DIM))],
        out_specs=[row(D_CONV), row(D_ATTN), row(D_KV), row(D_KV)],
        out_shape=[jax.ShapeDtypeStruct((n, D_CONV), F32), jax.ShapeDtypeStruct((n, D_ATTN), q_dtype),
                   jax.ShapeDtypeStruct((n, D_KV), F32), jax.ShapeDtypeStruct((n, D_KV), F32)],
        scratch_shapes=[pltpu.VMEM((tm, D_MODEL), BF16)],
        compiler_params=_params(1),
        name="in_proj",
    )(x2, g1, w_in_bf, qg, kg)


N_CCHUNK = D_CONV // LANES
CONV_ROWS = 64


def _conv_ln_swish(ue_ref, cw_ref, cb_ref, lg_ref, lb_ref, conv_ref, cat_ref, rows, first_row):
    base = first_row - CONV_HIST
    rb = min(CONV_ROWS, rows)

    def chunk(c, carry):
        wc = cw_ref[c]
        for r0 in range(0, rows, rb):
            acc = jnp.broadcast_to(cb_ref[c], (rb, LANES))
            for tap in range(CONV_WIDTH):
                acc = acc + wc[tap:tap + 1, :] * ue_ref[c, base + r0 + tap:base + r0 + tap + rb, :]
            conv_ref[c, r0:r0 + rb, :] = acc
        return carry

    lax.fori_loop(0, N_CCHUNK, chunk, 0)

    tot = jnp.zeros((rows, 1), F32)
    for c in range(N_CCHUNK):
        tot = tot + jnp.sum(conv_ref[c], axis=-1, keepdims=True)
    mean = tot / D_CONV
    var = jnp.zeros((rows, 1), F32)
    for c in range(N_CCHUNK):
        xc = conv_ref[c] - mean
        var = var + jnp.sum(xc * xc, axis=-1, keepdims=True)
    rstd = lax.rsqrt(var / D_CONV + EPS)
    for c in range(N_CCHUNK):
        y = (conv_ref[c] - mean) * rstd * lg_ref[c] + lb_ref[c]
        cat_ref[:, c * LANES:(c + 1) * LANES] = (y * jax.nn.sigmoid(y)).astype(cat_ref.dtype)


def _sink_softmax_rows(s, sink):
    m = jnp.maximum(jnp.max(s, axis=-1, keepdims=True), sink)
    p = jnp.exp(s - m)
    return p / (jnp.sum(p, axis=-1, keepdims=True) + jnp.exp(sink - m))


def _alibi_slope(head):
    return 2.0 ** (-8.0 * (head + 1) / N_HEADS)


def _mixer_prompt_body(sink_ref, u_ref, uh_ref, q_ref, k_ref, kh_ref, v_ref, vh_ref, cw_ref, cb_ref, lg_ref,
                       lb_ref, cat_ref, ue_ref, conv_ref, *, tm, halo):
    j = pl.program_id(1)
    has_prev = j > 0
    for c in range(N_CCHUNK):
        cs = slice(c * LANES, (c + 1) * LANES)
        ue_ref[c, 0:halo, :] = jnp.where(has_prev, uh_ref[:, cs], 0.0)
        ue_ref[c, halo:halo + tm, :] = u_ref[:, cs]
    _conv_ln_swish(ue_ref, cw_ref, cb_ref, lg_ref, lb_ref, conv_ref, cat_ref, tm, halo)

    qi = lax.broadcasted_iota(I32, (BLOCK_Q, 2 * BLOCK_Q), 0)
    kj = lax.broadcasted_iota(I32, (BLOCK_Q, 2 * BLOCK_Q), 1)
    dist = qi + BLOCK_Q - kj
    distf = dist.astype(F32)
    band = jnp.where(dist >= 0, jnp.where(dist < WINDOW, 0.0, NEG_INF), NEG_INF)
    band_first = jnp.where(kj >= BLOCK_Q, band, jnp.where(has_prev, band, NEG_INF))

    for qb in range(tm // BLOCK_Q):
        rows = slice(qb * BLOCK_Q, (qb + 1) * BLOCK_Q)
        prev = slice((qb - 1) * BLOCK_Q, qb * BLOCK_Q)
        mask = band_first if qb == 0 else band
        for kv in range(N_KV_HEADS):
            hs = slice(kv * HEAD_DIM, (kv + 1) * HEAD_DIM)
            k_prev = kh_ref[:, hs] if qb == 0 else k_ref[prev, hs]
            v_prev = vh_ref[:, hs] if qb == 0 else v_ref[prev, hs]
            kk = jnp.concatenate([k_prev, k_ref[rows, hs]], axis=0).astype(BF16)
            vv = jnp.concatenate([v_prev, v_ref[rows, hs]], axis=0).astype(BF16)
            heads = [kv * GQA_GROUP + g for g in range(GQA_GROUP)]
            qs = jnp.concatenate([q_ref[rows, h * HEAD_DIM:(h + 1) * HEAD_DIM] for h in heads], axis=0)
            s = lax.dot_general(qs, kk, (((1,), (1,)), ((), ())), preferred_element_type=F32)
            ps = []
            for g, h in enumerate(heads):
                sg = s[g * BLOCK_Q:(g + 1) * BLOCK_Q] * SCALE - _alibi_slope(h) * distf + mask
                ps.append(_sink_softmax_rows(sg, sink_ref[h]).astype(BF16))
            o = _dot(jnp.concatenate(ps, axis=0), vv)
            for g, h in enumerate(heads):
                cat_ref[rows, D_CONV + h * HEAD_DIM:D_CONV + (h + 1) * HEAD_DIM] = (
                    o[g * BLOCK_Q:(g + 1) * BLOCK_Q].astype(cat_ref.dtype))


def _mixer_prompt(sinks, u, q, k, v, cw, cb, lg, lb, tm):
    b, t, _ = u.shape
    halo = 32
    hpb = tm // halo
    kpb = tm // BLOCK_Q
    main = lambda w: pl.BlockSpec((None, tm, w), lambda bi, j: (bi, j, 0))
    body = functools.partial(_mixer_prompt_body, tm=tm, halo=halo)
    return pl.pallas_call(
        body,
        grid=(b, t // tm),
        in_specs=[pl.BlockSpec(memory_space=pltpu.SMEM),
                  main(D_CONV),
                  pl.BlockSpec((None, halo, D_CONV), lambda bi, j: (bi, jnp.maximum(j * hpb - 1, 0), 0)),
                  main(D_ATTN),
                  main(D_KV),
                  pl.BlockSpec((None, BLOCK_Q, D_KV), lambda bi, j: (bi, jnp.maximum(j * kpb - 1, 0), 0)),
                  main(D_KV),
                  pl.BlockSpec((None, BLOCK_Q, D_KV), lambda bi, j: (bi, jnp.maximum(j * kpb - 1, 0), 0)),
                  _resident((N_CCHUNK, CONV_WIDTH, LANES)), _resident((N_CCHUNK, 1, LANES)),
                  _resident((N_CCHUNK, 1, LANES)), _resident((N_CCHUNK, 1, LANES))],
        out_specs=main(D_MODEL),
        out_shape=jax.ShapeDtypeStruct((b, t, D_MODEL), BF16),
        scratch_shapes=[pltpu.VMEM((N_CCHUNK, halo + tm, LANES), F32),
                        pltpu.VMEM((N_CCHUNK, tm, LANES), F32)],
        compiler_params=_params(2),
        name="mixer_prompt",
    )(sinks, u, u, q, k, k, v, v, cw, cb, lg, lb)


KEY_PAD = 8


def _mixer_sample_body(sink_ref, u_ref, st_ref, q_ref, k_ref, v_ref, ck_ref, cv_ref, cw_ref, cb_ref, lg_ref,
                       lb_ref, cat_ref, nst_ref, nk_ref, nv_ref, ue_ref, conv_ref, kk_ref, vv_ref, *, t_new):
    hist = st_ref.shape[0]
    w_past = ck_ref.shape[0]
    for c in range(N_CCHUNK):
        cs = slice(c * LANES, (c + 1) * LANES)
        ue_ref[c, 0:hist, :] = st_ref[:, cs]
        ue_ref[c, hist:hist + t_new, :] = u_ref[:, cs]
    _conv_ln_swish(ue_ref, cw_ref, cb_ref, lg_ref, lb_ref, conv_ref, cat_ref, t_new, hist)
    nst_ref[0:hist - t_new, :] = st_ref[t_new:hist, :]
    nst_ref[hist - t_new:hist, :] = u_ref[...]

    nk_ref[0:w_past - t_new, :] = ck_ref[t_new:w_past, :]
    nk_ref[w_past - t_new:w_past, :] = k_ref[...]
    nv_ref[0:w_past - t_new, :] = cv_ref[t_new:w_past, :]
    nv_ref[w_past - t_new:w_past, :] = v_ref[...]

    n_keys = w_past + KEY_PAD
    kk_ref[0:w_past, :] = ck_ref[...]
    kk_ref[w_past:n_keys, :] = jnp.zeros((KEY_PAD, D_KV), F32)
    kk_ref[w_past:w_past + t_new, :] = k_ref[...]
    vv_ref[0:w_past, :] = cv_ref[...]
    vv_ref[w_past:n_keys, :] = jnp.zeros((KEY_PAD, D_KV), F32)
    vv_ref[w_past:w_past + t_new, :] = v_ref[...]

    qi = lax.broadcasted_iota(I32, (t_new, n_keys), 0)
    kj = lax.broadcasted_iota(I32, (t_new, n_keys), 1)
    dist = qi + w_past - kj
    distf = dist.astype(F32)
    in_window = jnp.where(dist >= 0, jnp.where(dist < WINDOW, 0.0, NEG_INF), NEG_INF)
    mask = jnp.where(kj < w_past + t_new, in_window, NEG_INF)
    for kv in range(N_KV_HEADS):
        hs = slice(kv * HEAD_DIM, (kv + 1) * HEAD_DIM)
        kk = kk_ref[:, hs].astype(BF16)
        vv = vv_ref[:, hs].astype(BF16)
        for g in range(GQA_GROUP):
            h = kv * GQA_GROUP + g
            qh = q_ref[:, h * HEAD_DIM:(h + 1) * HEAD_DIM].astype(BF16)
            s = lax.dot_general(qh, kk, (((1,), (1,)), ((), ())), preferred_element_type=F32)
            sg = s * SCALE - _alibi_slope(h) * distf + mask
            p = _sink_softmax_rows(sg, sink_ref[h]).astype(BF16)
            cat_ref[:, D_CONV + h * HEAD_DIM:D_CONV + (h + 1) * HEAD_DIM] = _dot(p, vv).astype(cat_ref.dtype)


def _mixer_sample(sinks, u, state, q, k, v, ck, cv, cw, cb, lg, lb):
    b, t_new, _ = u.shape
    hist = state.shape[1]
    w_past = ck.shape[1]
    per = lambda r, w: pl.BlockSpec((None, r, w), lambda bi: (bi, 0, 0))
    body = functools.partial(_mixer_sample_body, t_new=t_new)
    return pl.pallas_call(
        body,
        grid=(b,),
        in_specs=[pl.BlockSpec(memory_space=pltpu.SMEM),
                  per(t_new, D_CONV), per(hist, D_CONV), per(t_new, D_ATTN), per(t_new, D_KV), per(t_new, D_KV),
                  per(w_past, D_KV), per(w_past, D_KV),
                  _resident((N_CCHUNK, CONV_WIDTH, LANES)), _resident((N_CCHUNK, 1, LANES)),
                  _resident((N_CCHUNK, 1, LANES)), _resident((N_CCHUNK, 1, LANES))],
        out_specs=[per(t_new, D_MODEL), per(hist, D_CONV), per(w_past, D_KV), per(w_past, D_KV)],
        out_shape=[jax.ShapeDtypeStruct((b, t_new, D_MODEL), F32),
                   jax.ShapeDtypeStruct((b, hist, D_CONV), F32),
                   jax.ShapeDtypeStruct((b, w_past, D_KV), F32),
                   jax.ShapeDtypeStruct((b, w_past, D_KV), F32)],
        scratch_shapes=[pltpu.VMEM((N_CCHUNK, hist + SUBLANES, LANES), F32),
                        pltpu.VMEM((N_CCHUNK, t_new, LANES), F32),
                        pltpu.VMEM((w_past + KEY_PAD, D_KV), F32),
                        pltpu.VMEM((w_past + KEY_PAD, D_KV), F32)],
        compiler_params=_params(1),
        name="mixer_sample",
    )(sinks, u, state, q, k, v, ck, cv, cw, cb, lg, lb)


OUT_CHUNK = 512
D_HALF = D_MODEL // 2
U32 = jnp.uint32


def _pack_bf16_pairs(hi, lo):
    hi_bits = lax.bitcast_convert_type(hi.astype(BF16).astype(F32), U32)
    lo_bits = lax.bitcast_convert_type(lo.astype(BF16).astype(F32), U32)
    return hi_bits | (lo_bits >> 16)


def _unpack_bf16_pairs(words):
    hi = lax.bitcast_convert_type(words & U32(0xFFFF0000), F32)
    lo = lax.bitcast_convert_type(words << 16, F32)
    return hi, lo


def _out_proj_body(catp_ref, cats_ref, xp_ref, xs_ref, wo_ref, g2_ref, wr_ref, x1_ref, h2p_ref, lg_ref, h_ref, *,
                   n_prompt_tiles):
    is_prompt = pl.program_id(0) < n_prompt_tiles
    tm = catp_ref.shape[0]
    cat = jnp.where(is_prompt, catp_ref[...], cats_ref[...])
    ss = jnp.zeros((tm, 1), F32)
    for c in range(D_MODEL // OUT_CHUNK):
        cs = slice(c * OUT_CHUNK, (c + 1) * OUT_CHUNK)
        y = jnp.where(is_prompt, xp_ref[:, cs], xs_ref[:, cs]) + _dot(cat, wo_ref[:, cs])
        x1_ref[:, cs] = y
        ss = ss + jnp.sum(y * y, axis=-1, keepdims=True)
    r = lax.rsqrt(ss / D_MODEL + EPS)
    for c in range(D_MODEL // OUT_CHUNK):
        cs = slice(c * OUT_CHUNK, (c + 1) * OUT_CHUNK)
        h_ref[:, cs] = x1_ref[:, cs] * r * g2_ref[:, cs]
    for c in range(D_HALF // OUT_CHUNK):
        cs = slice(c * OUT_CHUNK, (c + 1) * OUT_CHUNK)
        cs_lo = slice(D_HALF + c * OUT_CHUNK, D_HALF + (c + 1) * OUT_CHUNK)
        h2p_ref[:, cs] = _pack_bf16_pairs(h_ref[:, cs], h_ref[:, cs_lo])
    lg_ref[...] = _dot(h_ref[...].astype(BF16), wr_ref[...])


def _out_proj(cat_p, cat_s, xp2, xs2, wo_bf, g2, wr_bf, tm):
    n_prompt_tiles = cat_p.shape[0] // tm
    n_rows = (n_prompt_tiles + 1) * tm
    prompt = lambda w: pl.BlockSpec((tm, w), lambda i: (jnp.minimum(i, n_prompt_tiles - 1), 0))
    out_row = lambda w: pl.BlockSpec((tm, w), lambda i: (i, 0))
    body = functools.partial(_out_proj_body, n_prompt_tiles=n_prompt_tiles)
    return pl.pallas_call(
        body,
        grid=(n_prompt_tiles + 1,),
        in_specs=[prompt(D_MODEL), _resident((tm, D_MODEL)), prompt(D_MODEL), _resident((tm, D_MODEL)),
                  _resident((D_MODEL, D_MODEL)), _resident((1, D_MODEL)), _resident((D_MODEL, LANES))],
        out_specs=[out_row(D_MODEL), out_row(D_HALF), out_row(LANES)],
        out_shape=[jax.ShapeDtypeStruct((n_rows, D_MODEL), F32), jax.ShapeDtypeStruct((n_rows, D_HALF), U32),
                   jax.ShapeDtypeStruct((n_rows, LANES), F32)],
        scratch_shapes=[pltpu.VMEM((tm, D_MODEL), F32)],
        compiler_params=_params(1),
        name="out_proj",
    )(cat_p, cat_s, xp2, xs2, wo_bf, g2, wr_bf)


ROUTE_CHUNK = 128


def _route_body(lg_ref, bias_ref, slot_ref, gate_ref, blk_ref, cum_ref, sel_ref, *, n_tok):
    n_chunks = n_tok // ROUTE_CHUNK
    lane = lax.broadcasted_iota(I32, (ROUTE_CHUNK, LANES), 1).astype(F32)
    ri = lax.broadcasted_iota(I32, (ROUTE_CHUNK, ROUTE_CHUNK), 0)
    ci = lax.broadcasted_iota(I32, (ROUTE_CHUNK, ROUTE_CHUNK), 1)
    lower = jnp.where(ci < ri, 1.0, 0.0).astype(BF16)
    upper = jnp.where(ri < ci, 1.0, 0.0).astype(BF16)
    is_group = lane < N_EXPERT_GROUPS

    def first_max(vals):
        m = jnp.max(vals, axis=-1, keepdims=True)
        idx = jnp.min(jnp.where(vals == m, lane, float(LANES)), axis=-1, keepdims=True)
        return m, idx

    def assign(i, carry):
        rows = pl.ds(pl.multiple_of(i * ROUTE_CHUNK, ROUTE_CHUNK), ROUTE_CHUNK)
        l = lg_ref[rows, :] + bias_ref[...]
        gl = jnp.where(is_group, l, NEG_INF)
        g_max, g_idx = first_max(gl)
        g_top = 1.0 / jnp.sum(jnp.exp(gl - g_max), axis=-1, keepdims=True)
        lo = EXPERT_LANE0 + g_idx * EXPERTS_PER_GROUP
        el = jnp.where(lane >= lo, jnp.where(lane < lo + EXPERTS_PER_GROUP, l, NEG_INF), NEG_INF)
        m1, i1 = first_max(el)
        p = jnp.exp(el - m1)
        probs = p / jnp.sum(p, axis=-1, keepdims=True)
        e1 = jnp.sum(jnp.where(lane == i1, probs, 0.0), axis=-1, keepdims=True)
        _, i2 = first_max(jnp.where(lane == i1, NEG_INF, el))
        e2 = jnp.sum(jnp.where(lane == i2, probs, 0.0), axis=-1, keepdims=True)
        gate1 = g_top * e1 / (e1 + e2)
        gate2 = g_top * e2 / (e1 + e2)
        gate_ref[rows, :] = jnp.where(lane == 0, gate1, jnp.where(lane == 1, gate2, 0.0))
        sel_ref[rows, :] = jnp.where(lane == 0, i1, jnp.where(lane == 1, i2, 0.0))
        onehot = jnp.where(lane == i1, 1.0, jnp.where(lane == i2, 1.0, 0.0))
        before = _dot(lower, onehot.astype(BF16)) + carry
        cum_ref[rows, :] = before
        return carry + jnp.sum(onehot, axis=0, keepdims=True)

    counts = lax.fori_loop(0, n_chunks, assign, jnp.zeros((1, LANES), F32))
    n_blocks = jnp.floor((counts + (MOE_BLOCK - 1)) / MOE_BLOCK)
    nb8 = jnp.broadcast_to(n_blocks, (SUBLANES, LANES)).astype(BF16)
    blk_start = _dot(nb8, upper)[0:1, :]
    row_start = blk_start * MOE_BLOCK

    def place(i, carry):
        rows = pl.ds(pl.multiple_of(i * ROUTE_CHUNK, ROUTE_CHUNK), ROUTE_CHUNK)
        pos = cum_ref[rows, :] + row_start
        sel = sel_ref[rows, :]
        i1 = sel[:, 0:1]
        i2 = sel[:, 1:2]
        s1 = jnp.sum(jnp.where(lane == i1, pos, 0.0), axis=-1, keepdims=True)
        s2 = jnp.sum(jnp.where(lane == i2, pos, 0.0), axis=-1, keepdims=True)
        slot_ref[rows, :] = jnp.where(lane == 0, s1, jnp.where(lane == 1, s2, 0.0)).astype(I32)
        return carry

    lax.fori_loop(0, n_chunks, place, 0)

    blk_ref[...] = jnp.broadcast_to(blk_start, (SUBLANES, LANES)).astype(I32)


def _route(logits, bias, n_tok):
    body = functools.partial(_route_body, n_tok=n_tok)
    whole = lambda r: pl.BlockSpec((r, LANES), lambda i: (0, 0))
    return pl.pallas_call(
        body,
        grid=(1,),
        in_specs=[whole(n_tok), whole(1)],
        out_specs=[whole(n_tok), whole(n_tok), whole(SUBLANES)],
        out_shape=[jax.ShapeDtypeStruct((n_tok, LANES), I32), jax.ShapeDtypeStruct((n_tok, LANES), F32),
                   jax.ShapeDtypeStruct((SUBLANES, LANES), I32)],
        scratch_shapes=[pltpu.VMEM((n_tok, LANES), F32), pltpu.VMEM((n_tok, LANES), F32)],
        compiler_params=_params(1),
        name="route",
    )(logits, bias)


INVERT_UNROLL = 16


def _invert_body(s0_ref, s1_ref, tok_ref):
    n_slots = tok_ref.shape[0]
    n_tok = s0_ref.shape[0]

    def clear(i, c):
        tok_ref[i] = 0
        return c

    lax.fori_loop(0, n_slots, clear, 0, unroll=INVERT_UNROLL)

    def put(t, c):
        tok_ref[s0_ref[t]] = t
        tok_ref[s1_ref[t]] = t
        return c

    lax.fori_loop(0, n_tok, put, 0, unroll=INVERT_UNROLL)


def _invert(slot0, slot1, n_slots):
    smem = pl.BlockSpec(memory_space=pltpu.SMEM)
    return pl.pallas_call(
        _invert_body,
        in_specs=[smem, smem],
        out_specs=smem,
        out_shape=jax.ShapeDtypeStruct((n_slots,), I32),
        name="invert",
    )(slot0, slot1)


GATHER_AHEAD = 2
GATHER_SLOTS = GATHER_AHEAD + 1


def _row_gather_start(idx_ref, base, src_hbm, dst, sem, n_rows, priorities):
    for r in range(n_rows):
        tok = idx_ref[base + r]
        pltpu.make_async_copy(src_hbm.at[pl.ds(tok, 1), :], dst.at[pl.ds(r, 1), :], sem).start(
            priority=priorities[r % len(priorities)])


def _experts_body(bstart_ref, tok_ref, h2p_hbm, w1_ref, w3_ref, w2_ref, eo_hbm, xbuf, obuf, w1b, w3b, w2b, gsem,
                  osem, *, n_blocks):
    e = pl.program_id(0)
    n_exp = pl.num_programs(0)
    b0 = bstart_ref[e]
    nb = bstart_ref[e + 1] - b0
    n_used = bstart_ref[n_exp]

    def gather(block):
        s = lax.rem(block, GATHER_SLOTS)
        src_block = jnp.minimum(block, n_used - 1)
        _row_gather_start(tok_ref, src_block * MOE_BLOCK, h2p_hbm, xbuf.at[s], gsem.at[s], MOE_BLOCK, (1, 0))

    def gather_wait(block):
        s = lax.rem(block, GATHER_SLOTS)
        pltpu.make_async_copy(xbuf.at[s], xbuf.at[s], gsem.at[s]).wait()

    def out_copy(block, s):
        rows = pl.ds(pl.multiple_of(block * MOE_BLOCK, MOE_BLOCK), MOE_BLOCK)
        return pltpu.make_async_copy(obuf.at[s], eo_hbm.at[rows, :], osem.at[s])

    @pl.when(e == 0)
    def _():
        for k in range(GATHER_AHEAD):
            gather(k)

    @pl.when(nb > 0)
    def _():
        w1b[...] = w1_ref[...].astype(BF16)
        w3b[...] = w3_ref[...].astype(BF16)
        w2b[...] = w2_ref[...].astype(BF16)

    def block(j, carry):
        b = b0 + j
        s = lax.rem(b, 2)
        gather(b + GATHER_AHEAD)
        gather_wait(b)
        hi, lo = _unpack_bf16_pairs(xbuf[lax.rem(b, GATHER_SLOTS)])
        x = jnp.concatenate([hi.astype(BF16), lo.astype(BF16)], axis=1)
        a = _dot(x, w1b[...])
        g = _dot(x, w3b[...])
        hdn = (a * jax.nn.sigmoid(a) * g).astype(BF16)
        o = _dot(hdn, w2b[...])

        @pl.when(b >= 2)
        def _():
            out_copy(b - 2, s).wait()

        obuf[s] = _pack_bf16_pairs(o[:, :D_HALF], o[:, D_HALF:])
        out_copy(b, s).start()
        return carry

    lax.fori_loop(0, nb, block, 0)

    @pl.when(e == n_exp - 1)
    def _():
        for k in range(GATHER_AHEAD):
            gather_wait(n_used + k)

        @pl.when(n_used >= 2)
        def _():
            out_copy(n_used - 2, lax.rem(n_used, 2)).wait()

        out_copy(n_used - 1, lax.rem(n_used - 1, 2)).wait()
        obuf[0] = jnp.zeros((MOE_BLOCK, D_HALF), U32)

        def fill(tb, carry):
            out_copy(tb, 0).start()
            return carry

        lax.fori_loop(n_used, n_blocks, fill, 0)

        def drain(tb, carry):
            out_copy(tb, 0).wait()
            return carry

        lax.fori_loop(n_used, n_blocks, drain, 0)


def _experts(bstart, tok, h2p, w1, w3, w2, n_blocks):
    def wspec(r, c):
        return pl.BlockSpec((None, r, c), lambda e, bstart_ref, tok_ref: (e, 0, 0))

    grid_spec = pltpu.PrefetchScalarGridSpec(
        num_scalar_prefetch=2,
        grid=(N_EXPERTS,),
        in_specs=[pl.BlockSpec(memory_space=pl.ANY),
                  wspec(D_MODEL, D_EXPERT), wspec(D_MODEL, D_EXPERT), wspec(D_EXPERT, D_MODEL)],
        out_specs=pl.BlockSpec(memory_space=pl.ANY),
        scratch_shapes=[pltpu.VMEM((GATHER_SLOTS, MOE_BLOCK, D_HALF), U32), pltpu.VMEM((2, MOE_BLOCK, D_HALF), U32),
                        pltpu.VMEM((D_MODEL, D_EXPERT), BF16), pltpu.VMEM((D_MODEL, D_EXPERT), BF16),
                        pltpu.VMEM((D_EXPERT, D_MODEL), BF16),
                        pltpu.SemaphoreType.DMA((GATHER_SLOTS,)), pltpu.SemaphoreType.DMA((2,))],
    )
    return pl.pallas_call(
        functools.partial(_experts_body, n_blocks=n_blocks),
        grid_spec=grid_spec,
        out_shape=jax.ShapeDtypeStruct((n_blocks * MOE_BLOCK, D_HALF), U32),
        compiler_params=_params(1),
        name="experts",
    )(bstart, tok, h2p, w1, w3, w2)


def _combine_body(s0_ref, s1_ref, x1_ref, gate_ref, eo_hbm, yp_ref, ys_ref, buf, sem, *, n_tiles, n_prompt_tiles):
    i = pl.program_id(0)
    slot = lax.rem(i, 2)
    tm = x1_ref.shape[0]

    def start(tile, s):
        _row_gather_start(s0_ref, tile * tm, eo_hbm, buf.at[s, 0], sem.at[s], tm, (0, 1))
        _row_gather_start(s1_ref, tile * tm, eo_hbm, buf.at[s, 1], sem.at[s], tm, (0, 1))

    @pl.when(i == 0)
    def _():
        start(0, 0)

    @pl.when(i + 1 < n_tiles)
    def _():
        start(i + 1, 1 - slot)

    pltpu.make_async_copy(buf.at[slot], buf.at[slot], sem.at[slot]).wait()
    g = gate_ref[...]
    hi0, lo0 = _unpack_bf16_pairs(buf[slot, 0])
    hi1, lo1 = _unpack_bf16_pairs(buf[slot, 1])
    y_hi = x1_ref[:, :D_HALF] + g[:, 0:1] * hi0 + g[:, 1:2] * hi1
    y_lo = x1_ref[:, D_HALF:] + g[:, 0:1] * lo0 + g[:, 1:2] * lo1

    @pl.when(i < n_prompt_tiles)
    def _():
        yp_ref[:, :D_HALF] = y_hi
        yp_ref[:, D_HALF:] = y_lo

    @pl.when(i >= n_prompt_tiles)
    def _():
        ys_ref[:, :D_HALF] = y_hi
        ys_ref[:, D_HALF:] = y_lo


def _combine(slot0, slot1, x1, gates, eo, n_prompt, n_tok, tm):
    n_tiles = n_tok // tm
    n_prompt_tiles = n_prompt // tm
    body = functools.partial(_combine_body, n_tiles=n_tiles, n_prompt_tiles=n_prompt_tiles)
    grid_spec = pltpu.PrefetchScalarGridSpec(
        num_scalar_prefetch=2,
        grid=(n_tiles,),
        in_specs=[pl.BlockSpec((tm, D_MODEL), lambda i, a, b: (i, 0)),
                  pl.BlockSpec((tm, LANES), lambda i, a, b: (i, 0)),
                  pl.BlockSpec(memory_space=pl.ANY)],
        out_specs=[pl.BlockSpec((tm, D_MODEL), lambda i, a, b: (jnp.minimum(i, n_prompt_tiles - 1), 0)),
                   pl.BlockSpec((tm, D_MODEL), lambda i, a, b: (jnp.maximum(i - n_prompt_tiles, 0), 0))],
        scratch_shapes=[pltpu.VMEM((2, 2, tm, D_HALF), U32), pltpu.SemaphoreType.DMA((2,))],
    )
    return pl.pallas_call(
        body,
        grid_spec=grid_spec,
        out_shape=[jax.ShapeDtypeStruct((n_prompt, D_MODEL), F32),
                   jax.ShapeDtypeStruct((n_tok - n_prompt, D_MODEL), F32)],
        compiler_params=_params(1),
        name="combine",
    )(slot0, slot1, x1, gates, eo)


def _layer(x_prompt, x_sample, state_conv, cache_k, cache_v, norm1_g, w_in, conv_w, conv_b, conv_norm_g,
           conv_norm_b, q_norm_g, k_norm_g, attn_sinks, w_out, norm2_g, w_rg, b_rg, w_re, b_re, w1, w3, w2):
    b, t, _ = x_prompt.shape
    sb, st, _ = x_sample.shape
    n_p, n_s = b * t, sb * st
    n_tok = n_p + n_s
    w_past = cache_k.shape[1]

    w_in_bf = w_in.astype(BF16)
    w_out_bf = w_out.astype(BF16)
    g1 = norm1_g.reshape(1, D_MODEL)
    g2 = norm2_g.reshape(1, D_MODEL)
    qg = q_norm_g.reshape(1, HEAD_DIM)
    kg = k_norm_g.reshape(1, HEAD_DIM)
    chunked = lambda a: a.reshape(-1, N_CCHUNK, LANES).transpose(1, 0, 2)
    cw, cb, lg, lb = chunked(conv_w), chunked(conv_b), chunked(conv_norm_g), chunked(conv_norm_b)
    w_router = jnp.zeros((D_MODEL, LANES), F32)
    w_router = w_router.at[:, :N_EXPERT_GROUPS].set(w_rg).at[:, EXPERT_LANE0:EXPERT_LANE0 + N_EXPERTS].set(w_re)
    b_router = jnp.zeros((1, LANES), F32)
    b_router = b_router.at[0, :N_EXPERT_GROUPS].set(b_rg).at[0, EXPERT_LANE0:EXPERT_LANE0 + N_EXPERTS].set(b_re)
    w_router_bf = w_router.astype(BF16)

    xp2 = x_prompt.reshape(n_p, D_MODEL)
    xs2 = x_sample.reshape(n_s, D_MODEL)

    u_p, q_p, k_p, v_p = _in_proj(xp2, g1, w_in_bf, qg, kg, 512, BF16)
    u_s, q_s, k_s, v_s = _in_proj(xs2, g1, w_in_bf, qg, kg, n_s, F32)

    r3 = lambda a, bb: a.reshape(bb, -1, a.shape[-1])
    cat_p = _mixer_prompt(attn_sinks, r3(u_p, b), r3(q_p, b), r3(k_p, b), r3(v_p, b), cw, cb, lg, lb, 256)
    cat_s, conv_s, knew_s, vnew_s = _mixer_sample(
        attn_sinks, r3(u_s, sb), state_conv, r3(q_s, sb), r3(k_s, sb), r3(v_s, sb),
        cache_k.reshape(sb, w_past, D_KV), cache_v.reshape(sb, w_past, D_KV), cw, cb, lg, lb)

    tm_o = 256
    pad_rows = lambda a: jnp.pad(a, ((0, tm_o - n_s), (0, 0)))
    x1, h2p, logits = _out_proj(cat_p.reshape(n_p, D_MODEL), pad_rows(cat_s.reshape(n_s, D_MODEL).astype(BF16)),
                               xp2, pad_rows(xs2), w_out_bf, g2, w_router_bf, tm_o)

    n_blocks = -(-(n_tok * 2) // MOE_BLOCK) + N_EXPERTS
    slots, gates, blk = _route(logits, b_router, n_tok)
    slot0, slot1 = slots[:, 0], slots[:, 1]
    tok = _invert(slot0, slot1, n_blocks * MOE_BLOCK)
    bstart = blk[0, EXPERT_LANE0:EXPERT_LANE0 + N_EXPERTS + 1]
    eo = _experts(bstart, tok, h2p, w1, w3, w2, n_blocks)
    y_p, y_s = _combine(slot0, slot1, x1, gates, eo, n_p, n_tok, COMBINE_TILE)

    conv_p = r3(u_p, b)[:, t - CONV_HIST:, :]
    knew_p = r3(k_p, b)[:, t - WINDOW:, :].reshape(b, WINDOW, N_KV_HEADS, HEAD_DIM)
    vnew_p = r3(v_p, b)[:, t - WINDOW:, :].reshape(b, WINDOW, N_KV_HEADS, HEAD_DIM)
    return (y_p.reshape(b, t, D_MODEL), y_s.reshape(sb, st, D_MODEL), conv_p, knew_p, vnew_p, conv_s,
            knew_s.reshape(sb, w_past, N_KV_HEADS, HEAD_DIM), vnew_s.reshape(sb, w_past, N_KV_HEADS, HEAD_DIM))


def kernel(x_prompt, x_sample, state_conv, cache_k, cache_v, norm1_g, w_in, conv_w, conv_b, conv_norm_g, conv_norm_b, q_norm_g, k_norm_g, attn_sinks, w_out, norm2_g, w_router_group, b_router_group, w_router_expert, b_router_expert, w1, w3, w2):
    depth = w_in.shape[0]
    assert depth == 1, "single-layer step"
    outs = _layer(x_prompt, x_sample, state_conv[0], cache_k[0], cache_v[0], norm1_g[0], w_in[0], conv_w[0],
                  conv_b[0], conv_norm_g[0], conv_norm_b[0], q_norm_g[0], k_norm_g[0], attn_sinks[0], w_out[0],
                  norm2_g[0], w_router_group[0], b_router_group[0], w_router_expert[0], b_router_expert[0],
                  w1[0], w3[0], w2[0])
    y_p, y_s = outs[0], outs[1]
    return (y_p, y_s) + tuple(o[None] for o in outs[2:])
```

```python
import functools
import math

import jax
import jax.numpy as jnp
from jax import lax
from jax.experimental import pallas as pl
from jax.experimental.pallas import tpu as pltpu

F32 = jnp.float32
BF16 = jnp.bfloat16
I32 = jnp.int32

D_MODEL = 2048
D_CONV = 1024
CONV_WIDTH = 31
CONV_HIST = CONV_WIDTH - 1
D_ATTN = 1024
HEAD_DIM = 128
N_HEADS = 8
N_KV_HEADS = 2
GQA_GROUP = N_HEADS // N_KV_HEADS
D_KV = N_KV_HEADS * HEAD_DIM
WINDOW = 128
BLOCK_Q = 128
SCALE = 1.0 / math.sqrt(HEAD_DIM)
N_EXPERT_GROUPS = 4
EXPERTS_PER_GROUP = 8
N_EXPERTS = N_EXPERT_GROUPS * EXPERTS_PER_GROUP
D_EXPERT = 512
MOE_BLOCK = 256
COMBINE_TILE = 128
D_IN = 2 * D_CONV + D_ATTN + 2 * D_KV
EPS = 1e-6
PAST_LEN = 16384

LANES = 128
SUBLANES = 8
MXU_COLS = 256
VMEM_LIMIT_BYTES = 56 * 1024 * 1024
NEG_INF = float("-inf")
EXPERT_LANE0 = N_EXPERT_GROUPS


def _params(n_axes):
    return pltpu.CompilerParams(dimension_semantics=("arbitrary",) * n_axes,
                                vmem_limit_bytes=VMEM_LIMIT_BYTES)


def _resident(shape):
    nd = len(shape)
    return pl.BlockSpec(shape, lambda *_: (0,) * nd, pipeline_mode=pl.Buffered(1))


def _dot(a, b):
    return jnp.dot(a, b, preferred_element_type=F32)


def _in_proj_body(x_ref, g1_ref, w_ref, qg_ref, kg_ref, u_ref, q_ref, k_ref, v_ref, n_ref):
    x = x_ref[...]
    ms = jnp.mean(x * x, axis=-1, keepdims=True)
    n_ref[...] = (x * lax.rsqrt(ms + EPS) * g1_ref[...]).astype(BF16)

    def head_norm(h, g):
        return h * lax.rsqrt(jnp.mean(h * h, axis=-1, keepdims=True) + EPS) * g

    ch = MXU_COLS
    for c in range(D_CONV // ch):
        a = _dot(n_ref[...], w_ref[:, c * ch:(c + 1) * ch])
        g = _dot(n_ref[...], w_ref[:, D_CONV + c * ch:D_CONV + (c + 1) * ch])
        u_ref[:, c * ch:(c + 1) * ch] = a * jax.nn.sigmoid(g)
    q_off = 2 * D_CONV
    for c in range(D_ATTN // ch):
        qq = _dot(n_ref[...], w_ref[:, q_off + c * ch:q_off + (c + 1) * ch])
        for j in range(ch // HEAD_DIM):
            qh = head_norm(qq[:, j * HEAD_DIM:(j + 1) * HEAD_DIM], qg_ref[...])
            q_ref[:, c * ch + j * HEAD_DIM:c * ch + (j + 1) * HEAD_DIM] = qh.astype(q_ref.dtype)
    k_off = q_off + D_ATTN
    kk = _dot(n_ref[...], w_ref[:, k_off:k_off + D_KV])
    for j in range(N_KV_HEADS):
        k_ref[:, j * HEAD_DIM:(j + 1) * HEAD_DIM] = head_norm(kk[:, j * HEAD_DIM:(j + 1) * HEAD_DIM], kg_ref[...])
    v_ref[...] = _dot(n_ref[...], w_ref[:, k_off + D_KV:k_off + 2 * D_KV])


def _in_proj(x2, g1, w_in_bf, qg, kg, tm, q_dtype):
    n = x2.shape[0]
    row = lambda w: pl.BlockSpec((tm, w), lambda i: (i, 0))
    return pl.pallas_call(
        _in_proj_body,
        grid=(n // tm,),
        in_specs=[row(D_MODEL), _resident((1, D_MODEL)), _resident((D_MODEL, D_IN)),
                  _resident((1, HEAD_DIM)), _resident((1, HEAD_DIM))],
        out_specs=[row(D_CONV), row(D_ATTN), row(D_KV), row(D_KV)],
        out_shape=[jax.ShapeDtypeStruct((n, D_CONV), F32), jax.ShapeDtypeStruct((n, D_ATTN), q_dtype),
                   jax.ShapeDtypeStruct((n, D_KV), F32), jax.ShapeDtypeStruct((n, D_KV), F32)],
        scratch_shapes=[pltpu.VMEM((tm, D_MODEL), BF16)],
        compiler_params=_params(1),
        name="in_proj",
    )(x2, g1, w_in_bf, qg, kg)


N_CCHUNK = D_CONV // LANES
CONV_ROWS = 64


def _conv_ln_swish(ue_ref, cw_ref, cb_ref, lg_ref, lb_ref, conv_ref, cat_ref, rows, first_row):
    base = first_row - CONV_HIST
    rb = min(CONV_ROWS, rows)

    def chunk(c, carry):
        wc = cw_ref[c]
        for r0 in range(0, rows, rb):
            acc = jnp.broadcast_to(cb_ref[c], (rb, LANES))
            for tap in range(CONV_WIDTH):
                acc = acc + wc[tap:tap + 1, :] * ue_ref[c, base + r0 + tap:base + r0 + tap + rb, :]
            conv_ref[c, r0:r0 + rb, :] = acc
        return carry

    lax.fori_loop(0, N_CCHUNK, chunk, 0)

    tot = jnp.zeros((rows, 1), F32)
    for c in range(N_CCHUNK):
        tot = tot + jnp.sum(conv_ref[c], axis=-1, keepdims=True)
    mean = tot / D_CONV
    var = jnp.zeros((rows, 1), F32)
    for c in range(N_CCHUNK):
        xc = conv_ref[c] - mean
        var = var + jnp.sum(xc * xc, axis=-1, keepdims=True)
    rstd = lax.rsqrt(var / D_CONV + EPS)
    for c in range(N_CCHUNK):
        y = (conv_ref[c] - mean) * rstd * lg_ref[c] + lb_ref[c]
        cat_ref[:, c * LANES:(c + 1) * LANES] = (y * jax.nn.sigmoid(y)).astype(cat_ref.dtype)


def _sink_softmax_rows(s, sink):
    m = jnp.maximum(jnp.max(s, axis=-1, keepdims=True), sink)
    p = jnp.exp(s - m)
    return p / (jnp.sum(p, axis=-1, keepdims=True) + jnp.exp(sink - m))


def _alibi_slope(head):
    return 2.0 ** (-8.0 * (head + 1) / N_HEADS)


def _mixer_prompt_body(sink_ref, u_ref, uh_ref, q_ref, k_ref, kh_ref, v_ref, vh_ref, cw_ref, cb_ref, lg_ref,
                       lb_ref, cat_ref, ue_ref, conv_ref, *, tm, halo):
    j = pl.program_id(1)
    has_prev = j > 0
    for c in range(N_CCHUNK):
        cs = slice(c * LANES, (c + 1) * LANES)
        ue_ref[c, 0:halo, :] = jnp.where(has_prev, uh_ref[:, cs], 0.0)
        ue_ref[c, halo:halo + tm, :] = u_ref[:, cs]
    _conv_ln_swish(ue_ref, cw_ref, cb_ref, lg_ref, lb_ref, conv_ref, cat_ref, tm, halo)

    qi = lax.broadcasted_iota(I32, (BLOCK_Q, 2 * BLOCK_Q), 0)
    kj = lax.broadcasted_iota(I32, (BLOCK_Q, 2 * BLOCK_Q), 1)
    dist = qi + BLOCK_Q - kj
    distf = dist.astype(F32)
    band = jnp.where(dist >= 0, jnp.where(dist < WINDOW, 0.0, NEG_INF), NEG_INF)
    band_first = jnp.where(kj >= BLOCK_Q, band, jnp.where(has_prev, band, NEG_INF))

    for qb in range(tm // BLOCK_Q):
        rows = slice(qb * BLOCK_Q, (qb + 1) * BLOCK_Q)
        prev = slice((qb - 1) * BLOCK_Q, qb * BLOCK_Q)
        mask = band_first if qb == 0 else band
        for kv in range(N_KV_HEADS):
            hs = slice(kv * HEAD_DIM, (kv + 1) * HEAD_DIM)
            k_prev = kh_ref[:, hs] if qb == 0 else k_ref[prev, hs]
            v_prev = vh_ref[:, hs] if qb == 0 else v_ref[prev, hs]
            kk = jnp.concatenate([k_prev, k_ref[rows, hs]], axis=0).astype(BF16)
            vv = jnp.concatenate([v_prev, v_ref[rows, hs]], axis=0).astype(BF16)
            heads = [kv * GQA_GROUP + g for g in range(GQA_GROUP)]
            qs = jnp.concatenate([q_ref[rows, h * HEAD_DIM:(h + 1) * HEAD_DIM] for h in heads], axis=0)
            s = lax.dot_general(qs, kk, (((1,), (1,)), ((), ())), preferred_element_type=F32)
            ps = []
            for g, h in enumerate(heads):
                sg = s[g * BLOCK_Q:(g + 1) * BLOCK_Q] * SCALE - _alibi_slope(h) * distf + mask
                ps.append(_sink_softmax_rows(sg, sink_ref[h]).astype(BF16))
            o = _dot(jnp.concatenate(ps, axis=0), vv)
            for g, h in enumerate(heads):
                cat_ref[rows, D_CONV + h * HEAD_DIM:D_CONV + (h + 1) * HEAD_DIM] = (
                    o[g * BLOCK_Q:(g + 1) * BLOCK_Q].astype(cat_ref.dtype))


def _mixer_prompt(sinks, u, q, k, v, cw, cb, lg, lb, tm):
    b, t, _ = u.shape
    halo = 32
    hpb = tm // halo
    kpb = tm // BLOCK_Q
    main = lambda w: pl.BlockSpec((None, tm, w), lambda bi, j: (bi, j, 0))
    body = functools.partial(_mixer_prompt_body, tm=tm, halo=halo)
    return pl.pallas_call(
        body,
        grid=(b, t // tm),
        in_specs=[pl.BlockSpec(memory_space=pltpu.SMEM),
                  main(D_CONV),
                  pl.BlockSpec((None, halo, D_CONV), lambda bi, j: (bi, jnp.maximum(j * hpb - 1, 0), 0)),
                  main(D_ATTN),
                  main(D_KV),
                  pl.BlockSpec((None, BLOCK_Q, D_KV), lambda bi, j: (bi, jnp.maximum(j * kpb - 1, 0), 0)),
                  main(D_KV),
                  pl.BlockSpec((None, BLOCK_Q, D_KV), lambda bi, j: (bi, jnp.maximum(j * kpb - 1, 0), 0)),
                  _resident((N_CCHUNK, CONV_WIDTH, LANES)), _resident((N_CCHUNK, 1, LANES)),
                  _resident((N_CCHUNK, 1, LANES)), _resident((N_CCHUNK, 1, LANES))],
        out_specs=main(D_MODEL),
        out_shape=jax.ShapeDtypeStruct((b, t, D_MODEL), BF16),
        scratch_shapes=[pltpu.VMEM((N_CCHUNK, halo + tm, LANES), F32),
                        pltpu.VMEM((N_CCHUNK, tm, LANES), F32)],
        compiler_params=_params(2),
        name="mixer_prompt",
    )(sinks, u, u, q, k, k, v, v, cw, cb, lg, lb)


KEY_PAD = 8


SAMPLE_SEQS = 8


def _sample_conv_ln_swish(st_ref, u_ref, cw_ref, cb_ref, lg_ref, lb_ref, cat_ref, t_new):
    n_seq, hist, _ = st_ref.shape
    conv = [[None] * N_CCHUNK for _ in range(t_new)]
    for c in range(N_CCHUNK):
        cs = slice(c * LANES, (c + 1) * LANES)
        wc = cw_ref[c]
        pos = [st_ref[:, p, cs] for p in range(hist)] + [u_ref[:, t, cs] for t in range(t_new)]
        for t in range(t_new):
            acc = jnp.broadcast_to(cb_ref[c], (n_seq, LANES))
            for tap in range(CONV_WIDTH):
                acc = acc + wc[tap:tap + 1, :] * pos[t + tap]
            conv[t][c] = acc
    for t in range(t_new):
        tot = jnp.zeros((n_seq, 1), F32)
        for a in conv[t]:
            tot = tot + jnp.sum(a, axis=-1, keepdims=True)
        mean = tot / D_CONV
        var = jnp.zeros((n_seq, 1), F32)
        for a in conv[t]:
            var = var + jnp.sum((a - mean) * (a - mean), axis=-1, keepdims=True)
        rstd = lax.rsqrt(var / D_CONV + EPS)
        for c, a in enumerate(conv[t]):
            y = (a - mean) * rstd * lg_ref[c] + lb_ref[c]
            cat_ref[:, t, c * LANES:(c + 1) * LANES] = (y * jax.nn.sigmoid(y)).astype(cat_ref.dtype)


def _mixer_sample_body(sink_ref, u_ref, st_ref, q_ref, k_ref, v_ref, ck_ref, cv_ref, cw_ref, cb_ref, lg_ref,
                       lb_ref, cat_ref, nst_ref, nk_ref, nv_ref, kk_ref, vv_ref, qs_ref, *, t_new):
    n_seq, hist, _ = st_ref.shape
    w_past = ck_ref.shape[1]
    n_keys = w_past + KEY_PAD
    q_rows = GQA_GROUP * t_new
    n_rows, n_cols = n_seq * q_rows, n_seq * n_keys

    _sample_conv_ln_swish(st_ref, u_ref, cw_ref, cb_ref, lg_ref, lb_ref, cat_ref, t_new)

    for i in range(n_seq):
        nst_ref[i, 0:hist - t_new, :] = st_ref[i, t_new:hist, :]
        nst_ref[i, hist - t_new:hist, :] = u_ref[i]

        nk_ref[i, 0:w_past - t_new, :] = ck_ref[i, t_new:w_past, :]
        nk_ref[i, w_past - t_new:w_past, :] = k_ref[i]
        nv_ref[i, 0:w_past - t_new, :] = cv_ref[i, t_new:w_past, :]
        nv_ref[i, w_past - t_new:w_past, :] = v_ref[i]

        kk_ref[i, 0:w_past, :] = ck_ref[i]
        kk_ref[i, w_past:n_keys, :] = jnp.zeros((KEY_PAD, D_KV), F32)
        kk_ref[i, w_past:w_past + t_new, :] = k_ref[i]
        vv_ref[i, 0:w_past, :] = cv_ref[i]
        vv_ref[i, w_past:n_keys, :] = jnp.zeros((KEY_PAD, D_KV), F32)
        vv_ref[i, w_past:w_past + t_new, :] = v_ref[i]

    row = lax.broadcasted_iota(I32, (n_rows, n_cols), 0)
    col = lax.broadcasted_iota(I32, (n_rows, n_cols), 1)
    tok_bits, row_bits = t_new.bit_length() - 1, q_rows.bit_length() - 1
    assert (1 << tok_bits, 1 << row_bits) == (t_new, q_rows), "token and row counts must be powers of two"
    tok = row & (t_new - 1)
    key = col - (row >> row_bits) * n_keys
    dist = tok + w_past - key
    distf = dist.astype(F32)
    mask = jnp.where(dist >= 0, jnp.where(dist < WINDOW, 0.0, NEG_INF), NEG_INF)
    row1 = lax.broadcasted_iota(I32, (n_rows, 1), 0)
    grp = (row1 >> tok_bits) & (GQA_GROUP - 1)

    for kv in range(N_KV_HEADS):
        hs = slice(kv * HEAD_DIM, (kv + 1) * HEAD_DIM)
        slope = jnp.zeros((n_rows, 1), F32)
        sink = jnp.zeros((n_rows, 1), F32)
        for g in range(GQA_GROUP):
            h = kv * GQA_GROUP + g
            slope = jnp.where(grp == g, _alibi_slope(h), slope)
            sink = jnp.where(grp == g, sink_ref[h], sink)
            for i in range(n_seq):
                r0 = i * q_rows + g * t_new
                qs_ref[r0:r0 + t_new, :] = q_ref[i, :, h * HEAD_DIM:(h + 1) * HEAD_DIM]
        kk = kk_ref[:, :, hs].reshape(n_cols, HEAD_DIM).astype(BF16)
        vv = vv_ref[:, :, hs].reshape(n_cols, HEAD_DIM).astype(BF16)
        s = lax.dot_general(qs_ref[...].astype(BF16), kk, (((1,), (1,)), ((), ())), preferred_element_type=F32)
        sg = s * SCALE - slope * distf + mask
        o = _dot(_sink_softmax_rows(sg, sink).astype(BF16), vv)
        for g in range(GQA_GROUP):
            h = kv * GQA_GROUP + g
            for i in range(n_seq):
                r0 = i * q_rows + g * t_new
                cat_ref[i, :, D_CONV + h * HEAD_DIM:D_CONV + (h + 1) * HEAD_DIM] = (
                    o[r0:r0 + t_new].astype(cat_ref.dtype))


def _mixer_sample(sinks, u, state, q, k, v, ck, cv, cw, cb, lg, lb):
    b, t_new, _ = u.shape
    hist = state.shape[1]
    w_past = ck.shape[1]
    n_seq = SAMPLE_SEQS
    per = lambda r, w: pl.BlockSpec((n_seq, r, w), lambda bi: (bi, 0, 0))
    body = functools.partial(_mixer_sample_body, t_new=t_new)
    return pl.pallas_call(
        body,
        grid=(b // n_seq,),
        in_specs=[pl.BlockSpec(memory_space=pltpu.SMEM),
                  per(t_new, D_CONV), per(hist, D_CONV), per(t_new, D_ATTN), per(t_new, D_KV), per(t_new, D_KV),
                  per(w_past, D_KV), per(w_past, D_KV),
                  _resident((N_CCHUNK, CONV_WIDTH, LANES)), _resident((N_CCHUNK, 1, LANES)),
                  _resident((N_CCHUNK, 1, LANES)), _resident((N_CCHUNK, 1, LANES))],
        out_specs=[per(t_new, D_MODEL), per(hist, D_CONV), per(w_past, D_KV), per(w_past, D_KV)],
        out_shape=[jax.ShapeDtypeStruct((b, t_new, D_MODEL), F32),
                   jax.ShapeDtypeStruct((b, hist, D_CONV), F32),
                   jax.ShapeDtypeStruct((b, w_past, D_KV), F32),
                   jax.ShapeDtypeStruct((b, w_past, D_KV), F32)],
        scratch_shapes=[pltpu.VMEM((n_seq, w_past + KEY_PAD, D_KV), F32),
                        pltpu.VMEM((n_seq, w_past + KEY_PAD, D_KV), F32),
                        pltpu.VMEM((n_seq * GQA_GROUP * t_new, HEAD_DIM), F32)],
        compiler_params=_params(1),
        name="mixer_sample",
    )(sinks, u, state, q, k, v, ck, cv, cw, cb, lg, lb)


OUT_CHUNK = 512
D_HALF = D_MODEL // 2
U32 = jnp.uint32


def _pack_bf16_pairs(hi, lo):
    hi_bits = lax.bitcast_convert_type(hi.astype(BF16).astype(F32), U32)
    lo_bits = lax.bitcast_convert_type(lo.astype(BF16).astype(F32), U32)
    return hi_bits | (lo_bits >> 16)


def _unpack_bf16_pairs(words):
    hi = lax.bitcast_convert_type(words & U32(0xFFFF0000), F32)
    lo = lax.bitcast_convert_type(words << 16, F32)
    return hi, lo


def _out_proj_body(catp_ref, cats_ref, xp_ref, xs_ref, wo_ref, g2_ref, wr_ref, x1_ref, h2p_ref, lg_ref, h_ref, *,
                   n_prompt_tiles):
    is_prompt = pl.program_id(0) < n_prompt_tiles
    tm = catp_ref.shape[0]
    cat = jnp.where(is_prompt, catp_ref[...], cats_ref[...])
    ss = jnp.zeros((tm, 1), F32)
    for c in range(D_MODEL // OUT_CHUNK):
        cs = slice(c * OUT_CHUNK, (c + 1) * OUT_CHUNK)
        y = jnp.where(is_prompt, xp_ref[:, cs], xs_ref[:, cs]) + _dot(cat, wo_ref[:, cs])
        x1_ref[:, cs] = y
        ss = ss + jnp.sum(y * y, axis=-1, keepdims=True)
    r = lax.rsqrt(ss / D_MODEL + EPS)
    for c in range(D_MODEL // OUT_CHUNK):
        cs = slice(c * OUT_CHUNK, (c + 1) * OUT_CHUNK)
        h_ref[:, cs] = x1_ref[:, cs] * r * g2_ref[:, cs]
    for c in range(D_HALF // OUT_CHUNK):
        cs = slice(c * OUT_CHUNK, (c + 1) * OUT_CHUNK)
        cs_lo = slice(D_HALF + c * OUT_CHUNK, D_HALF + (c + 1) * OUT_CHUNK)
        h2p_ref[:, cs] = _pack_bf16_pairs(h_ref[:, cs], h_ref[:, cs_lo])
    lg_ref[...] = _dot(h_ref[...].astype(BF16), wr_ref[...])


def _out_proj(cat_p, cat_s, xp2, xs2, wo_bf, g2, wr_bf, tm):
    n_prompt_tiles = cat_p.shape[0] // tm
    n_rows = (n_prompt_tiles + 1) * tm
    prompt = lambda w: pl.BlockSpec((tm, w), lambda i: (jnp.minimum(i, n_prompt_tiles - 1), 0))
    out_row = lambda w: pl.BlockSpec((tm, w), lambda i: (i, 0))
    body = functools.partial(_out_proj_body, n_prompt_tiles=n_prompt_tiles)
    return pl.pallas_call(
        body,
        grid=(n_prompt_tiles + 1,),
        in_specs=[prompt(D_MODEL), _resident((tm, D_MODEL)), prompt(D_MODEL), _resident((tm, D_MODEL)),
                  _resident((D_MODEL, D_MODEL)), _resident((1, D_MODEL)), _resident((D_MODEL, LANES))],
        out_specs=[out_row(D_MODEL), out_row(D_HALF), out_row(LANES)],
        out_shape=[jax.ShapeDtypeStruct((n_rows, D_MODEL), F32), jax.ShapeDtypeStruct((n_rows, D_HALF), U32),
                   jax.ShapeDtypeStruct((n_rows, LANES), F32)],
        scratch_shapes=[pltpu.VMEM((tm, D_MODEL), F32)],
        compiler_params=_params(1),
        name="out_proj",
    )(cat_p, cat_s, xp2, xs2, wo_bf, g2, wr_bf)


ROUTE_CHUNK = 128
ROUTE_UNROLL = 5


def _route_body(lg_ref, bias_ref, slot_ref, gate_ref, blk_ref, cum_ref, sel_ref, *, n_tok):
    n_chunks = n_tok // ROUTE_CHUNK
    lane = lax.broadcasted_iota(I32, (ROUTE_CHUNK, LANES), 1).astype(F32)
    ri = lax.broadcasted_iota(I32, (ROUTE_CHUNK, ROUTE_CHUNK), 0)
    ci = lax.broadcasted_iota(I32, (ROUTE_CHUNK, ROUTE_CHUNK), 1)
    lower = jnp.where(ci < ri, 1.0, 0.0).astype(BF16)
    upper = jnp.where(ri < ci, 1.0, 0.0).astype(BF16)
    diag = ri == ci
    is_group = lane < N_EXPERT_GROUPS

    def first_max(vals):
        m = jnp.max(vals, axis=-1, keepdims=True)
        idx = jnp.min(jnp.where(vals == m, lane, float(LANES)), axis=-1, keepdims=True)
        return m, idx

    def assign(i, carry):
        rows = pl.ds(pl.multiple_of(i * ROUTE_CHUNK, ROUTE_CHUNK), ROUTE_CHUNK)
        l = lg_ref[rows, :] + bias_ref[...]
        gl = jnp.where(is_group, l, NEG_INF)
        g_max, g_idx = first_max(gl)
        g_top = 1.0 / jnp.sum(jnp.exp(gl - g_max), axis=-1, keepdims=True)
        lo = EXPERT_LANE0 + g_idx * EXPERTS_PER_GROUP
        el = jnp.where(lane >= lo, jnp.where(lane < lo + EXPERTS_PER_GROUP, l, NEG_INF), NEG_INF)
        m1, i1 = first_max(el)
        p = jnp.exp(el - m1)
        probs = p / jnp.sum(p, axis=-1, keepdims=True)
        e1 = jnp.sum(jnp.where(lane == i1, probs, 0.0), axis=-1, keepdims=True)
        _, i2 = first_max(jnp.where(lane == i1, NEG_INF, el))
        e2 = jnp.sum(jnp.where(lane == i2, probs, 0.0), axis=-1, keepdims=True)
        gate1 = g_top * e1 / (e1 + e2)
        gate2 = g_top * e2 / (e1 + e2)
        gate_ref[rows, :] = jnp.where(lane == 0, gate1, jnp.where(lane == 1, gate2, 0.0))
        sel_ref[rows, :] = jnp.where(lane == 0, i1, jnp.where(lane == 1, i2, 0.0))
        onehot = jnp.where(lane == i1, 1.0, jnp.where(lane == i2, 1.0, 0.0))
        before = _dot(lower, onehot.astype(BF16)) + carry
        cum_ref[rows, :] = before
        return carry + jnp.sum(onehot, axis=0, keepdims=True)

    counts = lax.fori_loop(0, n_chunks, assign, jnp.zeros((1, LANES), F32), unroll=ROUTE_UNROLL)
    n_blocks = jnp.floor((counts + (MOE_BLOCK - 1)) / MOE_BLOCK)
    nb8 = jnp.broadcast_to(n_blocks, (SUBLANES, LANES)).astype(BF16)
    blk_start = _dot(nb8, upper)[0:1, :]
    row_start = blk_start * MOE_BLOCK

    def place(i, carry):
        rows = pl.ds(pl.multiple_of(i * ROUTE_CHUNK, ROUTE_CHUNK), ROUTE_CHUNK)
        pos = cum_ref[rows, :] + row_start
        sel = sel_ref[rows, :]
        i1 = sel[:, 0:1]
        i2 = sel[:, 1:2]
        s1 = jnp.sum(jnp.where(lane == i1, pos, 0.0), axis=-1, keepdims=True)
        s2 = jnp.sum(jnp.where(lane == i2, pos, 0.0), axis=-1, keepdims=True)
        slot_ref[0, i] = jnp.sum(jnp.where(diag, s1, 0.0), axis=0, keepdims=True).astype(I32)
        slot_ref[1, i] = jnp.sum(jnp.where(diag, s2, 0.0), axis=0, keepdims=True).astype(I32)
        return carry

    lax.fori_loop(0, n_chunks, place, 0, unroll=ROUTE_UNROLL)

    sub = lax.broadcasted_iota(I32, (SUBLANES, LANES), 0)
    blk_ref[...] = jnp.where(sub == 1, counts, blk_start).astype(I32)


def _route(logits, bias, n_tok):
    body = functools.partial(_route_body, n_tok=n_tok)
    n_chunks = n_tok // ROUTE_CHUNK
    whole = lambda r: pl.BlockSpec((r, LANES), lambda i: (0, 0))
    return pl.pallas_call(
        body,
        grid=(1,),
        in_specs=[whole(n_tok), whole(1)],
        out_specs=[pl.BlockSpec((2, n_chunks, 1, LANES), lambda i: (0, 0, 0, 0)), whole(n_tok), whole(SUBLANES)],
        out_shape=[jax.ShapeDtypeStruct((2, n_chunks, 1, LANES), I32), jax.ShapeDtypeStruct((n_tok, LANES), F32),
                   jax.ShapeDtypeStruct((SUBLANES, LANES), I32)],
        scratch_shapes=[pltpu.VMEM((n_tok, LANES), F32), pltpu.VMEM((n_tok, LANES), F32)],
        compiler_params=_params(1),
        name="route",
    )(logits, bias)


INVERT_UNROLL = 16


CLEAR_SPAN = 8


def _invert_body(slot_ref, blk_ref, tok_ref, *, n_slots):
    n_tok = slot_ref.shape[0] // 2

    def clear_expert(e, c):
        first_pad = blk_ref[0, EXPERT_LANE0 + e] * MOE_BLOCK + blk_ref[1, EXPERT_LANE0 + e]
        end = jnp.where(e == N_EXPERTS - 1, n_slots, blk_ref[0, EXPERT_LANE0 + e + 1] * MOE_BLOCK)
        span_bits = CLEAR_SPAN.bit_length() - 1
        lo = lax.shift_right_logical(first_pad, span_bits) * CLEAR_SPAN

        def span(j, c2):
            for k in range(CLEAR_SPAN):
                tok_ref[lo + j * CLEAR_SPAN + k] = 0
            return c2

        lax.fori_loop(0, lax.shift_right_logical(end - lo, span_bits), span, 0)
        return c

    lax.fori_loop(0, N_EXPERTS, clear_expert, 0)

    def put(t, c):
        tok_ref[slot_ref[t]] = t
        tok_ref[slot_ref[n_tok + t]] = t
        return c

    lax.fori_loop(0, n_tok, put, 0, unroll=INVERT_UNROLL)


def _invert(slots, blk, n_slots):
    smem = pl.BlockSpec(memory_space=pltpu.SMEM)
    return pl.pallas_call(
        functools.partial(_invert_body, n_slots=n_slots),
        in_specs=[smem, smem],
        out_specs=smem,
        out_shape=jax.ShapeDtypeStruct((n_slots,), I32),
        name="invert",
    )(slots, blk)


GATHER_AHEAD = 2
GATHER_SLOTS = GATHER_AHEAD + 1


def _row_gather_start(idx_ref, base, src_hbm, dst, sem, n_rows, priorities):
    for r in range(n_rows):
        tok = idx_ref[base + r]
        pltpu.make_async_copy(src_hbm.at[pl.ds(tok, 1), :], dst.at[pl.ds(r, 1), :], sem).start(
            priority=priorities[r % len(priorities)])


def _experts_body(bstart_ref, tok_ref, h2p_hbm, w1_ref, w3_ref, w2_ref, eo_hbm, xbuf, obuf, w1b, w3b, w2b, gsem,
                  osem, *, n_blocks):
    e = pl.program_id(0)
    n_exp = pl.num_programs(0)
    b0 = bstart_ref[0, EXPERT_LANE0 + e]
    nb = bstart_ref[0, EXPERT_LANE0 + e + 1] - b0
    n_used = bstart_ref[0, EXPERT_LANE0 + n_exp]

    def gather(block):
        s = lax.rem(block, GATHER_SLOTS)
        src_block = jnp.minimum(block, n_used - 1)
        _row_gather_start(tok_ref, src_block * MOE_BLOCK, h2p_hbm, xbuf.at[s], gsem.at[s], MOE_BLOCK, (1, 0))

    def gather_wait(block):
        s = lax.rem(block, GATHER_SLOTS)
        pltpu.make_async_copy(xbuf.at[s], xbuf.at[s], gsem.at[s]).wait()

    def out_copy(block, s):
        rows = pl.ds(pl.multiple_of(block * MOE_BLOCK, MOE_BLOCK), MOE_BLOCK)
        return pltpu.make_async_copy(obuf.at[s], eo_hbm.at[rows, :], osem.at[s])

    @pl.when(e == 0)
    def _():
        for k in range(GATHER_AHEAD):
            gather(k)

    @pl.when(nb > 0)
    def _():
        w1b[...] = w1_ref[...].astype(BF16)
        w3b[...] = w3_ref[...].astype(BF16)
        w2b[...] = w2_ref[...].astype(BF16)

    def block(j, carry):
        b = b0 + j
        s = lax.rem(b, 2)
        gather(b + GATHER_AHEAD)
        gather_wait(b)
        hi, lo = _unpack_bf16_pairs(xbuf[lax.rem(b, GATHER_SLOTS)])
        x = jnp.concatenate([hi.astype(BF16), lo.astype(BF16)], axis=1)
        a = _dot(x, w1b[...])
        g = _dot(x, w3b[...])
        hdn = (a * jax.nn.sigmoid(a) * g).astype(BF16)
        o = _dot(hdn, w2b[...])

        @pl.when(b >= 2)
        def _():
            out_copy(b - 2, s).wait()

        obuf[s] = _pack_bf16_pairs(o[:, :D_HALF], o[:, D_HALF:])
        out_copy(b, s).start()
        return carry

    lax.fori_loop(0, nb, block, 0)

    @pl.when(e == n_exp - 1)
    def _():
        for k in range(GATHER_AHEAD):
            gather_wait(n_used + k)

        @pl.when(n_used >= 2)
        def _():
            out_copy(n_used - 2, lax.rem(n_used, 2)).wait()

        out_copy(n_used - 1, lax.rem(n_used - 1, 2)).wait()
        obuf[0] = jnp.zeros((MOE_BLOCK, D_HALF), U32)

        def fill(tb, carry):
            out_copy(tb, 0).start()
            return carry

        lax.fori_loop(n_used, n_blocks, fill, 0)

        def drain(tb, carry):
            out_copy(tb, 0).wait()
            return carry

        lax.fori_loop(n_used, n_blocks, drain, 0)


def _experts(bstart, tok, h2p, w1, w3, w2, n_blocks):
    def wspec(r, c):
        return pl.BlockSpec((None, r, c), lambda e, bstart_ref, tok_ref: (e, 0, 0))

    grid_spec = pltpu.PrefetchScalarGridSpec(
        num_scalar_prefetch=2,
        grid=(N_EXPERTS,),
        in_specs=[pl.BlockSpec(memory_space=pl.ANY),
                  wspec(D_MODEL, D_EXPERT), wspec(D_MODEL, D_EXPERT), wspec(D_EXPERT, D_MODEL)],
        out_specs=pl.BlockSpec(memory_space=pl.ANY),
        scratch_shapes=[pltpu.VMEM((GATHER_SLOTS, MOE_BLOCK, D_HALF), U32), pltpu.VMEM((2, MOE_BLOCK, D_HALF), U32),
                        pltpu.VMEM((D_MODEL, D_EXPERT), BF16), pltpu.VMEM((D_MODEL, D_EXPERT), BF16),
                        pltpu.VMEM((D_EXPERT, D_MODEL), BF16),
                        pltpu.SemaphoreType.DMA((GATHER_SLOTS,)), pltpu.SemaphoreType.DMA((2,))],
    )
    return pl.pallas_call(
        functools.partial(_experts_body, n_blocks=n_blocks),
        grid_spec=grid_spec,
        out_shape=jax.ShapeDtypeStruct((n_blocks * MOE_BLOCK, D_HALF), U32),
        compiler_params=_params(1),
        name="experts",
    )(bstart, tok, h2p, w1, w3, w2)


def _combine_body(slot_ref, x1_ref, gate_ref, eo_hbm, yp_ref, ys_ref, buf, sem, *, n_tok, n_prompt_tiles):
    i = pl.program_id(0)
    tm = x1_ref.shape[0]
    n_tiles = n_tok // tm
    slot = lax.rem(i, GATHER_SLOTS)

    def start(tile):
        s = lax.rem(tile, GATHER_SLOTS)
        base = jnp.minimum(tile, n_tiles - 1) * tm
        _row_gather_start(slot_ref, base, eo_hbm, buf.at[s, 0], sem.at[s], tm, (0, 1))
        _row_gather_start(slot_ref, n_tok + base, eo_hbm, buf.at[s, 1], sem.at[s], tm, (0, 1))

    def wait(tile):
        s = lax.rem(tile, GATHER_SLOTS)
        pltpu.make_async_copy(buf.at[s], buf.at[s], sem.at[s]).wait()

    @pl.when(i == 0)
    def _():
        for k in range(GATHER_AHEAD):
            start(k)

    start(i + GATHER_AHEAD)
    wait(i)

    @pl.when(i == n_tiles - 1)
    def _():
        for k in range(GATHER_AHEAD):
            wait(n_tiles + k)

    g = gate_ref[...]
    hi0, lo0 = _unpack_bf16_pairs(buf[slot, 0])
    hi1, lo1 = _unpack_bf16_pairs(buf[slot, 1])
    y_hi = x1_ref[:, :D_HALF] + g[:, 0:1] * hi0 + g[:, 1:2] * hi1
    y_lo = x1_ref[:, D_HALF:] + g[:, 0:1] * lo0 + g[:, 1:2] * lo1

    @pl.when(i < n_prompt_tiles)
    def _():
        yp_ref[:, :D_HALF] = y_hi
        yp_ref[:, D_HALF:] = y_lo

    @pl.when(i >= n_prompt_tiles)
    def _():
        ys_ref[:, :D_HALF] = y_hi
        ys_ref[:, D_HALF:] = y_lo


def _combine(slots, x1, gates, eo, n_prompt, n_tok, tm):
    n_tiles = n_tok // tm
    n_prompt_tiles = n_prompt // tm
    body = functools.partial(_combine_body, n_tok=n_tok, n_prompt_tiles=n_prompt_tiles)
    grid_spec = pltpu.PrefetchScalarGridSpec(
        num_scalar_prefetch=1,
        grid=(n_tiles,),
        in_specs=[pl.BlockSpec((tm, D_MODEL), lambda i, s: (i, 0)),
                  pl.BlockSpec((tm, LANES), lambda i, s: (i, 0)),
                  pl.BlockSpec(memory_space=pl.ANY)],
        out_specs=[pl.BlockSpec((tm, D_MODEL), lambda i, s: (jnp.minimum(i, n_prompt_tiles - 1), 0)),
                   pl.BlockSpec((tm, D_MODEL), lambda i, s: (jnp.maximum(i - n_prompt_tiles, 0), 0))],
        scratch_shapes=[pltpu.VMEM((GATHER_SLOTS, 2, tm, D_HALF), U32), pltpu.SemaphoreType.DMA((GATHER_SLOTS,))],
    )
    return pl.pallas_call(
        body,
        grid_spec=grid_spec,
        out_shape=[jax.ShapeDtypeStruct((n_prompt, D_MODEL), F32),
                   jax.ShapeDtypeStruct((n_tok - n_prompt, D_MODEL), F32)],
        compiler_params=_params(1),
        name="combine",
    )(slots, x1, gates, eo)


def _layer(x_prompt, x_sample, state_conv, cache_k, cache_v, norm1_g, w_in, conv_w, conv_b, conv_norm_g,
           conv_norm_b, q_norm_g, k_norm_g, attn_sinks, w_out, norm2_g, w_rg, b_rg, w_re, b_re, w1, w3, w2):
    b, t, _ = x_prompt.shape
    sb, st, _ = x_sample.shape
    n_p, n_s = b * t, sb * st
    n_tok = n_p + n_s
    w_past = cache_k.shape[1]

    w_in_bf = w_in.astype(BF16)
    w_out_bf = w_out.astype(BF16)
    g1 = norm1_g.reshape(1, D_MODEL)
    g2 = norm2_g.reshape(1, D_MODEL)
    qg = q_norm_g.reshape(1, HEAD_DIM)
    kg = k_norm_g.reshape(1, HEAD_DIM)
    chunked = lambda a: a.reshape(-1, N_CCHUNK, LANES).transpose(1, 0, 2)
    cw, cb, lg, lb = chunked(conv_w), chunked(conv_b), chunked(conv_norm_g), chunked(conv_norm_b)
    w_router = jnp.zeros((D_MODEL, LANES), F32)
    w_router = w_router.at[:, :N_EXPERT_GROUPS].set(w_rg).at[:, EXPERT_LANE0:EXPERT_LANE0 + N_EXPERTS].set(w_re)
    b_router = jnp.zeros((1, LANES), F32)
    b_router = b_router.at[0, :N_EXPERT_GROUPS].set(b_rg).at[0, EXPERT_LANE0:EXPERT_LANE0 + N_EXPERTS].set(b_re)
    w_router_bf = w_router.astype(BF16)

    xp2 = x_prompt.reshape(n_p, D_MODEL)
    xs2 = x_sample.reshape(n_s, D_MODEL)

    u_p, q_p, k_p, v_p = _in_proj(xp2, g1, w_in_bf, qg, kg, 512, BF16)
    u_s, q_s, k_s, v_s = _in_proj(xs2, g1, w_in_bf, qg, kg, n_s, F32)

    r3 = lambda a, bb: a.reshape(bb, -1, a.shape[-1])
    cat_p = _mixer_prompt(attn_sinks, r3(u_p, b), r3(q_p, b), r3(k_p, b), r3(v_p, b), cw, cb, lg, lb, 256)
    cat_s, conv_s, knew_s, vnew_s = _mixer_sample(
        attn_sinks, r3(u_s, sb), state_conv, r3(q_s, sb), r3(k_s, sb), r3(v_s, sb),
        cache_k.reshape(sb, w_past, D_KV), cache_v.reshape(sb, w_past, D_KV), cw, cb, lg, lb)

    tm_o = 256
    pad_rows = lambda a: jnp.pad(a, ((0, tm_o - n_s), (0, 0)))
    x1, h2p, logits = _out_proj(cat_p.reshape(n_p, D_MODEL), pad_rows(cat_s.reshape(n_s, D_MODEL).astype(BF16)),
                               xp2, pad_rows(xs2), w_out_bf, g2, w_router_bf, tm_o)

    n_blocks = -(-(n_tok * 2) // MOE_BLOCK) + N_EXPERTS
    slots, gates, blk = _route(logits, b_router, n_tok)
    slots = slots.reshape(2 * n_tok)
    tok = _invert(slots, blk, n_blocks * MOE_BLOCK)
    eo = _experts(blk, tok, h2p, w1, w3, w2, n_blocks)
    y_p, y_s = _combine(slots, x1, gates, eo, n_p, n_tok, COMBINE_TILE)

    conv_p = r3(u_p, b)[:, t - CONV_HIST:, :]
    knew_p = r3(k_p, b)[:, t - WINDOW:, :].reshape(b, WINDOW, N_KV_HEADS, HEAD_DIM)
    vnew_p = r3(v_p, b)[:, t - WINDOW:, :].reshape(b, WINDOW, N_KV_HEADS, HEAD_DIM)
    return (y_p.reshape(b, t, D_MODEL), y_s.reshape(sb, st, D_MODEL), conv_p, knew_p, vnew_p, conv_s,
            knew_s.reshape(sb, w_past, N_KV_HEADS, HEAD_DIM), vnew_s.reshape(sb, w_past, N_KV_HEADS, HEAD_DIM))


def kernel(x_prompt, x_sample, state_conv, cache_k, cache_v, norm1_g, w_in, conv_w, conv_b, conv_norm_g, conv_norm_b, q_norm_g, k_norm_g, attn_sinks, w_out, norm2_g, w_router_group, b_router_group, w_router_expert, b_router_expert, w1, w3, w2):
    depth = w_in.shape[0]
    assert depth == 1, "single-layer step"
    outs = _layer(x_prompt, x_sample, state_conv[0], cache_k[0], cache_v[0], norm1_g[0], w_in[0], conv_w[0],
                  conv_b[0], conv_norm_g[0], conv_norm_b[0], q_norm_g[0], k_norm_g[0], attn_sinks[0], w_out[0],
                  norm2_g[0], w_router_group[0], b_router_group[0], w_router_expert[0], b_router_expert[0],
                  w1[0], w3[0], w2[0])
    y_p, y_s = outs[0], outs[1]
    return (y_p, y_s) + tuple(o[None] for o in outs[2:])
```

```python
import functools
import math

import jax
import jax.numpy as jnp
from jax import lax
from jax.experimental import pallas as pl
from jax.experimental.pallas import tpu as pltpu

F32 = jnp.float32
BF16 = jnp.bfloat16
I32 = jnp.int32

D_MODEL = 2048
D_CONV = 1024
CONV_WIDTH = 31
CONV_HIST = CONV_WIDTH - 1
D_ATTN = 1024
HEAD_DIM = 128
N_HEADS = 8
N_KV_HEADS = 2
GQA_GROUP = N_HEADS // N_KV_HEADS
D_KV = N_KV_HEADS * HEAD_DIM
WINDOW = 128
BLOCK_Q = 128
SCALE = 1.0 / math.sqrt(HEAD_DIM)
N_EXPERT_GROUPS = 4
EXPERTS_PER_GROUP = 8
N_EXPERTS = N_EXPERT_GROUPS * EXPERTS_PER_GROUP
D_EXPERT = 512
MOE_BLOCK = 256
COMBINE_TILE = 128
D_IN = 2 * D_CONV + D_ATTN + 2 * D_KV
EPS = 1e-6
PAST_LEN = 16384

LANES = 128
SUBLANES = 8
MXU_COLS = 256
VMEM_LIMIT_BYTES = 56 * 1024 * 1024
NEG_INF = float("-inf")
EXPERT_LANE0 = N_EXPERT_GROUPS


def _params(n_axes):
    return pltpu.CompilerParams(dimension_semantics=("arbitrary",) * n_axes,
                                vmem_limit_bytes=VMEM_LIMIT_BYTES)


def _resident(shape):
    nd = len(shape)
    return pl.BlockSpec(shape, lambda *_: (0,) * nd, pipeline_mode=pl.Buffered(1))


def _dot(a, b):
    return jnp.dot(a, b, preferred_element_type=F32)


def _in_proj_body(x_ref, g1_ref, w_ref, qg_ref, kg_ref, u_ref, q_ref, k_ref, v_ref, n_ref):
    x = x_ref[...]
    ms = jnp.mean(x * x, axis=-1, keepdims=True)
    n_ref[...] = (x * lax.rsqrt(ms + EPS) * g1_ref[...]).astype(BF16)

    def head_norm(h, g):
        return h * lax.rsqrt(jnp.mean(h * h, axis=-1, keepdims=True) + EPS) * g

    ch = MXU_COLS
    for c in range(D_CONV // ch):
        a = _dot(n_ref[...], w_ref[:, c * ch:(c + 1) * ch])
        g = _dot(n_ref[...], w_ref[:, D_CONV + c * ch:D_CONV + (c + 1) * ch])
        u_ref[:, c * ch:(c + 1) * ch] = a * jax.nn.sigmoid(g)
    q_off = 2 * D_CONV
    for c in range(D_ATTN // ch):
        qq = _dot(n_ref[...], w_ref[:, q_off + c * ch:q_off + (c + 1) * ch])
        for j in range(ch // HEAD_DIM):
            qh = head_norm(qq[:, j * HEAD_DIM:(j + 1) * HEAD_DIM], qg_ref[...])
            q_ref[:, c * ch + j * HEAD_DIM:c * ch + (j + 1) * HEAD_DIM] = qh.astype(q_ref.dtype)
    k_off = q_off + D_ATTN
    kk = _dot(n_ref[...], w_ref[:, k_off:k_off + D_KV])
    for j in range(N_KV_HEADS):
        k_ref[:, j * HEAD_DIM:(j + 1) * HEAD_DIM] = head_norm(kk[:, j * HEAD_DIM:(j + 1) * HEAD_DIM], kg_ref[...])
    v_ref[...] = _dot(n_ref[...], w_ref[:, k_off + D_KV:k_off + 2 * D_KV])


def _in_proj_conv_body(x_ref, g1_ref, w_ref, qg_ref, kg_ref, cw_ref, cb_ref, lg_ref, lb_ref, c_ref, q_ref, k_ref,
                       v_ref, ut_ref, kt_ref, vt_ref, n_ref, ue_ref, conv_ref, *, tiles_per_seq):
    tm = x_ref.shape[0]
    first = lax.rem(pl.program_id(0), tiles_per_seq) == 0

    @pl.when(first)
    def _():
        for c in range(N_CCHUNK):
            ue_ref[c, 0:CONV_HALO, :] = jnp.zeros((CONV_HALO, LANES), F32)

    @pl.when(jnp.logical_not(first))
    def _():
        for c in range(N_CCHUNK):
            ue_ref[c, 0:CONV_HALO, :] = ue_ref[c, tm:tm + CONV_HALO, :]

    x = x_ref[...]
    ms = jnp.mean(x * x, axis=-1, keepdims=True)
    n_ref[...] = (x * lax.rsqrt(ms + EPS) * g1_ref[...]).astype(BF16)

    def head_norm(h, g):
        return h * lax.rsqrt(jnp.mean(h * h, axis=-1, keepdims=True) + EPS) * g

    ch = MXU_COLS
    lanes_per = ch // LANES
    n_glu = D_CONV // ch
    q_off = 2 * D_CONV

    def glu(c):
        a = _dot(n_ref[...], w_ref[:, c * ch:(c + 1) * ch])
        g = _dot(n_ref[...], w_ref[:, D_CONV + c * ch:D_CONV + (c + 1) * ch])
        u = a * jax.nn.sigmoid(g)
        for half in range(lanes_per):
            ue_ref[c * lanes_per + half, CONV_HALO:CONV_HALO + tm, :] = u[:, half * LANES:(half + 1) * LANES]

    def conv(c):
        for half in range(lanes_per):
            lc = c * lanes_per + half
            _conv_chunk(ue_ref, lc, cw_ref[lc], cb_ref[lc], conv_ref, tm)

    def q_heads(c):
        qq = _dot(n_ref[...], w_ref[:, q_off + c * ch:q_off + (c + 1) * ch])
        for j in range(ch // HEAD_DIM):
            qh = head_norm(qq[:, j * HEAD_DIM:(j + 1) * HEAD_DIM], qg_ref[...])
            q_ref[:, c * ch + j * HEAD_DIM:c * ch + (j + 1) * HEAD_DIM] = qh.astype(q_ref.dtype)

    glu(0)
    for c in range(n_glu):
        if c + 1 < n_glu:
            glu(c + 1)
        else:
            q_heads(0)
        conv(c)
    q_heads(1)
    _ln_swish(conv_ref, lg_ref, lb_ref, c_ref, tm)
    for c in range(2, D_ATTN // ch):
        q_heads(c)
    k_off = q_off + D_ATTN
    kk = _dot(n_ref[...], w_ref[:, k_off:k_off + D_KV])
    vv = _dot(n_ref[...], w_ref[:, k_off + D_KV:k_off + 2 * D_KV])
    v_ref[...] = vv
    for j in range(N_KV_HEADS):
        hs = slice(j * HEAD_DIM, (j + 1) * HEAD_DIM)
        kn = head_norm(kk[:, hs], kg_ref[...])
        k_ref[:, hs] = kn
        kt_ref[:, j, :] = kn[tm - WINDOW:, :]
        vt_ref[:, j, :] = vv[tm - WINDOW:, hs]
    for c in range(N_CCHUNK):
        ut_ref[:, c * LANES:(c + 1) * LANES] = ue_ref[c, CONV_HALO + tm - CONV_HIST:CONV_HALO + tm, :]


def _in_proj_conv(x2, g1, w_in_bf, qg, kg, cw, cb, lg, lb, tm, n_seq):
    n = x2.shape[0]
    tiles_per_seq = n // n_seq // tm
    row = lambda w: pl.BlockSpec((tm, w), lambda i: (i, 0))
    seq = lambda *dims: pl.BlockSpec((None,) + dims, lambda i: (i // tiles_per_seq,) + (0,) * len(dims))
    body = functools.partial(_in_proj_conv_body, tiles_per_seq=tiles_per_seq)
    return pl.pallas_call(
        body,
        grid=(n // tm,),
        in_specs=[row(D_MODEL), _resident((1, D_MODEL)), _resident((D_MODEL, D_IN)),
                  _resident((1, HEAD_DIM)), _resident((1, HEAD_DIM)),
                  _resident((N_CCHUNK, CONV_WIDTH, LANES)), _resident((N_CCHUNK, 1, LANES)),
                  _resident((N_CCHUNK, 1, LANES)), _resident((N_CCHUNK, 1, LANES))],
        out_specs=[row(D_CONV), row(D_ATTN), row(D_KV), row(D_KV),
                   seq(CONV_HIST, D_CONV), seq(WINDOW, N_KV_HEADS, HEAD_DIM), seq(WINDOW, N_KV_HEADS, HEAD_DIM)],
        out_shape=[jax.ShapeDtypeStruct((n, D_CONV), BF16), jax.ShapeDtypeStruct((n, D_ATTN), BF16),
                   jax.ShapeDtypeStruct((n, D_KV), F32), jax.ShapeDtypeStruct((n, D_KV), F32),
                   jax.ShapeDtypeStruct((n_seq, CONV_HIST, D_CONV), F32),
                   jax.ShapeDtypeStruct((n_seq, WINDOW, N_KV_HEADS, HEAD_DIM), F32),
                   jax.ShapeDtypeStruct((n_seq, WINDOW, N_KV_HEADS, HEAD_DIM), F32)],
        scratch_shapes=[pltpu.VMEM((tm, D_MODEL), BF16),
                        pltpu.VMEM((N_CCHUNK, CONV_HALO + tm, LANES), F32),
                        pltpu.VMEM((N_CCHUNK, tm, LANES), F32)],
        compiler_params=_params(1),
        name="in_proj_conv",
    )(x2, g1, w_in_bf, qg, kg, cw, cb, lg, lb)


def _in_proj(x2, g1, w_in_bf, qg, kg, tm, q_dtype):
    n = x2.shape[0]
    row = lambda w: pl.BlockSpec((tm, w), lambda i: (i, 0))
    return pl.pallas_call(
        _in_proj_body,
        grid=(n // tm,),
        in_specs=[row(D_MODEL), _resident((1, D_MODEL)), _resident((D_MODEL, D_IN)),
                  _resident((1, HEAD_DIM)), _resident((1, HEAD_DIM))],
        out_specs=[row(D_CONV), row(D_ATTN), row(D_KV), row(D_KV)],
        out_shape=[jax.ShapeDtypeStruct((n, D_CONV), F32), jax.ShapeDtypeStruct((n, D_ATTN), q_dtype),
                   jax.ShapeDtypeStruct((n, D_KV), F32), jax.ShapeDtypeStruct((n, D_KV), F32)],
        scratch_shapes=[pltpu.VMEM((tm, D_MODEL), BF16)],
        compiler_params=_params(1),
        name="in_proj",
    )(x2, g1, w_in_bf, qg, kg)


N_CCHUNK = D_CONV // LANES
CONV_ROWS = 64


CONV_HALO = 32


def _conv_chunk(ue_ref, c, wc, bias, conv_ref, rows):
    base = CONV_HALO - CONV_HIST
    for r0 in range(0, rows, CONV_ROWS):
        acc = jnp.broadcast_to(bias, (CONV_ROWS, LANES))
        for tap in range(CONV_WIDTH):
            acc = acc + wc[tap:tap + 1, :] * ue_ref[c, base + r0 + tap:base + r0 + tap + CONV_ROWS, :]
        conv_ref[c, r0:r0 + CONV_ROWS, :] = acc


def _ln_swish(conv_ref, lg_ref, lb_ref, cat_ref, rows):
    tot = jnp.zeros((rows, 1), F32)
    for c in range(N_CCHUNK):
        tot = tot + jnp.sum(conv_ref[c], axis=-1, keepdims=True)
    mean = tot / D_CONV
    var = jnp.zeros((rows, 1), F32)
    for c in range(N_CCHUNK):
        xc = conv_ref[c] - mean
        var = var + jnp.sum(xc * xc, axis=-1, keepdims=True)
    rstd = lax.rsqrt(var / D_CONV + EPS)
    for c in range(N_CCHUNK):
        y = (conv_ref[c] - mean) * rstd * lg_ref[c] + lb_ref[c]
        cat_ref[:, c * LANES:(c + 1) * LANES] = (y * jax.nn.sigmoid(y)).astype(cat_ref.dtype)


def _sink_softmax_rows(s, sink):
    m = jnp.maximum(jnp.max(s, axis=-1, keepdims=True), sink)
    p = jnp.exp(s - m)
    return p / (jnp.sum(p, axis=-1, keepdims=True) + jnp.exp(sink - m))


def _alibi_slope(head):
    return 2.0 ** (-8.0 * (head + 1) / N_HEADS)


def _mixer_prompt_body(sink_ref, q_ref, k_ref, kh_ref, v_ref, vh_ref, cat_ref, *, tm):
    j = pl.program_id(1)
    has_prev = j > 0
    qi = lax.broadcasted_iota(I32, (BLOCK_Q, 2 * BLOCK_Q), 0)
    kj = lax.broadcasted_iota(I32, (BLOCK_Q, 2 * BLOCK_Q), 1)
    dist = qi + BLOCK_Q - kj
    distf = dist.astype(F32)
    band = jnp.where(dist >= 0, jnp.where(dist < WINDOW, 0.0, NEG_INF), NEG_INF)
    band_first = jnp.where(kj >= BLOCK_Q, band, jnp.where(has_prev, band, NEG_INF))

    for qb in range(tm // BLOCK_Q):
        rows = slice(qb * BLOCK_Q, (qb + 1) * BLOCK_Q)
        prev = slice((qb - 1) * BLOCK_Q, qb * BLOCK_Q)
        mask = band_first if qb == 0 else band
        for kv in range(N_KV_HEADS):
            hs = slice(kv * HEAD_DIM, (kv + 1) * HEAD_DIM)
            k_prev = kh_ref[:, hs] if qb == 0 else k_ref[prev, hs]
            v_prev = vh_ref[:, hs] if qb == 0 else v_ref[prev, hs]
            kk = jnp.concatenate([k_prev, k_ref[rows, hs]], axis=0).astype(BF16)
            vv = jnp.concatenate([v_prev, v_ref[rows, hs]], axis=0).astype(BF16)
            heads = [kv * GQA_GROUP + g for g in range(GQA_GROUP)]
            qs = jnp.concatenate([q_ref[rows, h * HEAD_DIM:(h + 1) * HEAD_DIM] for h in heads], axis=0)
            s = lax.dot_general(qs, kk, (((1,), (1,)), ((), ())), preferred_element_type=F32)
            ps = []
            for g, h in enumerate(heads):
                sg = s[g * BLOCK_Q:(g + 1) * BLOCK_Q] * SCALE - _alibi_slope(h) * distf + mask
                ps.append(_sink_softmax_rows(sg, sink_ref[h]).astype(BF16))
            o = _dot(jnp.concatenate(ps, axis=0), vv)
            for g, h in enumerate(heads):
                cat_ref[rows, h * HEAD_DIM:(h + 1) * HEAD_DIM] = o[g * BLOCK_Q:(g + 1) * BLOCK_Q].astype(cat_ref.dtype)


def _mixer_prompt(sinks, q, k, v, tm):
    b, t, _ = q.shape
    kpb = tm // BLOCK_Q
    main = lambda w: pl.BlockSpec((None, tm, w), lambda bi, j: (bi, j, 0))
    prev_block = pl.BlockSpec((None, BLOCK_Q, D_KV), lambda bi, j: (bi, jnp.maximum(j * kpb - 1, 0), 0))
    body = functools.partial(_mixer_prompt_body, tm=tm)
    return pl.pallas_call(
        body,
        grid=(b, t // tm),
        in_specs=[pl.BlockSpec(memory_space=pltpu.SMEM), main(D_ATTN), main(D_KV), prev_block, main(D_KV),
                  prev_block],
        out_specs=main(D_ATTN),
        out_shape=jax.ShapeDtypeStruct((b, t, D_ATTN), BF16),
        compiler_params=_params(2),
        name="mixer_prompt",
    )(sinks, q, k, k, v, v)


KEY_PAD = 8


SAMPLE_SEQS = 8


def _sample_conv_ln_swish(st_ref, u_ref, cw_ref, cb_ref, lg_ref, lb_ref, cat_ref, t_new):
    n_seq, hist, _ = st_ref.shape
    conv = [[None] * N_CCHUNK for _ in range(t_new)]
    for c in range(N_CCHUNK):
        cs = slice(c * LANES, (c + 1) * LANES)
        wc = cw_ref[c]
        pos = [st_ref[:, p, cs] for p in range(hist)] + [u_ref[:, t, cs] for t in range(t_new)]
        for t in range(t_new):
            acc = jnp.broadcast_to(cb_ref[c], (n_seq, LANES))
            for tap in range(CONV_WIDTH):
                acc = acc + wc[tap:tap + 1, :] * pos[t + tap]
            conv[t][c] = acc
    for t in range(t_new):
        tot = jnp.zeros((n_seq, 1), F32)
        for a in conv[t]:
            tot = tot + jnp.sum(a, axis=-1, keepdims=True)
        mean = tot / D_CONV
        var = jnp.zeros((n_seq, 1), F32)
        for a in conv[t]:
            var = var + jnp.sum((a - mean) * (a - mean), axis=-1, keepdims=True)
        rstd = lax.rsqrt(var / D_CONV + EPS)
        for c, a in enumerate(conv[t]):
            y = (a - mean) * rstd * lg_ref[c] + lb_ref[c]
            cat_ref[:, t, c * LANES:(c + 1) * LANES] = (y * jax.nn.sigmoid(y)).astype(cat_ref.dtype)


def _mixer_sample_body(sink_ref, u_ref, st_ref, q_ref, k_ref, v_ref, ck_ref, cv_ref, cw_ref, cb_ref, lg_ref,
                       lb_ref, cat_ref, nst_ref, nk_ref, nv_ref, kk_ref, vv_ref, qs_ref, *, t_new):
    n_seq, hist, _ = st_ref.shape
    w_past = ck_ref.shape[1]
    n_keys = w_past + KEY_PAD
    q_rows = GQA_GROUP * t_new
    n_rows, n_cols = n_seq * q_rows, n_seq * n_keys

    _sample_conv_ln_swish(st_ref, u_ref, cw_ref, cb_ref, lg_ref, lb_ref, cat_ref, t_new)

    for i in range(n_seq):
        nst_ref[i, 0:hist - t_new, :] = st_ref[i, t_new:hist, :]
        nst_ref[i, hist - t_new:hist, :] = u_ref[i]

        nk_ref[i, 0:w_past - t_new, :] = ck_ref[i, t_new:w_past, :]
        nk_ref[i, w_past - t_new:w_past, :] = k_ref[i]
        nv_ref[i, 0:w_past - t_new, :] = cv_ref[i, t_new:w_past, :]
        nv_ref[i, w_past - t_new:w_past, :] = v_ref[i]

        kk_ref[i, 0:w_past, :] = ck_ref[i]
        kk_ref[i, w_past:n_keys, :] = jnp.zeros((KEY_PAD, D_KV), F32)
        kk_ref[i, w_past:w_past + t_new, :] = k_ref[i]
        vv_ref[i, 0:w_past, :] = cv_ref[i]
        vv_ref[i, w_past:n_keys, :] = jnp.zeros((KEY_PAD, D_KV), F32)
        vv_ref[i, w_past:w_past + t_new, :] = v_ref[i]

    row = lax.broadcasted_iota(I32, (n_rows, n_cols), 0)
    col = lax.broadcasted_iota(I32, (n_rows, n_cols), 1)
    tok_bits, row_bits = t_new.bit_length() - 1, q_rows.bit_length() - 1
    assert (1 << tok_bits, 1 << row_bits) == (t_new, q_rows), "token and row counts must be powers of two"
    tok = row & (t_new - 1)
    key = col - (row >> row_bits) * n_keys
    dist = tok + w_past - key
    distf = dist.astype(F32)
    mask = jnp.where(dist >= 0, jnp.where(dist < WINDOW, 0.0, NEG_INF), NEG_INF)
    row1 = lax.broadcasted_iota(I32, (n_rows, 1), 0)
    grp = (row1 >> tok_bits) & (GQA_GROUP - 1)

    for kv in range(N_KV_HEADS):
        hs = slice(kv * HEAD_DIM, (kv + 1) * HEAD_DIM)
        slope = jnp.zeros((n_rows, 1), F32)
        sink = jnp.zeros((n_rows, 1), F32)
        for g in range(GQA_GROUP):
            h = kv * GQA_GROUP + g
            slope = jnp.where(grp == g, _alibi_slope(h), slope)
            sink = jnp.where(grp == g, sink_ref[h], sink)
            for i in range(n_seq):
                r0 = i * q_rows + g * t_new
                qs_ref[r0:r0 + t_new, :] = q_ref[i, :, h * HEAD_DIM:(h + 1) * HEAD_DIM]
        kk = kk_ref[:, :, hs].reshape(n_cols, HEAD_DIM).astype(BF16)
        vv = vv_ref[:, :, hs].reshape(n_cols, HEAD_DIM).astype(BF16)
        s = lax.dot_general(qs_ref[...].astype(BF16), kk, (((1,), (1,)), ((), ())), preferred_element_type=F32)
        sg = s * SCALE - slope * distf + mask
        o = _dot(_sink_softmax_rows(sg, sink).astype(BF16), vv)
        for g in range(GQA_GROUP):
            h = kv * GQA_GROUP + g
            for i in range(n_seq):
                r0 = i * q_rows + g * t_new
                cat_ref[i, :, D_CONV + h * HEAD_DIM:D_CONV + (h + 1) * HEAD_DIM] = (
                    o[r0:r0 + t_new].astype(cat_ref.dtype))


def _mixer_sample(sinks, u, state, q, k, v, ck, cv, cw, cb, lg, lb):
    b, t_new, _ = u.shape
    hist = state.shape[1]
    w_past = ck.shape[1]
    n_seq = SAMPLE_SEQS
    per = lambda r, w: pl.BlockSpec((n_seq, r, w), lambda bi: (bi, 0, 0))
    body = functools.partial(_mixer_sample_body, t_new=t_new)
    return pl.pallas_call(
        body,
        grid=(b // n_seq,),
        in_specs=[pl.BlockSpec(memory_space=pltpu.SMEM),
                  per(t_new, D_CONV), per(hist, D_CONV), per(t_new, D_ATTN), per(t_new, D_KV), per(t_new, D_KV),
                  per(w_past, D_KV), per(w_past, D_KV),
                  _resident((N_CCHUNK, CONV_WIDTH, LANES)), _resident((N_CCHUNK, 1, LANES)),
                  _resident((N_CCHUNK, 1, LANES)), _resident((N_CCHUNK, 1, LANES))],
        out_specs=[per(t_new, D_MODEL), per(hist, D_CONV), per(w_past, D_KV), per(w_past, D_KV)],
        out_shape=[jax.ShapeDtypeStruct((b, t_new, D_MODEL), F32),
                   jax.ShapeDtypeStruct((b, hist, D_CONV), F32),
                   jax.ShapeDtypeStruct((b, w_past, D_KV), F32),
                   jax.ShapeDtypeStruct((b, w_past, D_KV), F32)],
        scratch_shapes=[pltpu.VMEM((n_seq, w_past + KEY_PAD, D_KV), F32),
                        pltpu.VMEM((n_seq, w_past + KEY_PAD, D_KV), F32),
                        pltpu.VMEM((n_seq * GQA_GROUP * t_new, HEAD_DIM), F32)],
        compiler_params=_params(1),
        name="mixer_sample",
    )(sinks, u, state, q, k, v, ck, cv, cw, cb, lg, lb)


OUT_CHUNK = 512
D_HALF = D_MODEL // 2
U32 = jnp.uint32


def _pack_bf16_pairs(hi, lo):
    hi_bits = lax.bitcast_convert_type(hi.astype(BF16).astype(F32), U32)
    lo_bits = lax.bitcast_convert_type(lo.astype(BF16).astype(F32), U32)
    return hi_bits | (lo_bits >> 16)


def _unpack_bf16_pairs(words):
    hi = lax.bitcast_convert_type(words & U32(0xFFFF0000), F32)
    lo = lax.bitcast_convert_type(words << 16, F32)
    return hi, lo


def _out_proj_body(catc_ref, cata_ref, cats_ref, xp_ref, xs_ref, wo_ref, g2_ref, wr_ref, x1_ref, h2p_ref, lg_ref,
                   h_ref, *, n_prompt_tiles):
    is_prompt = pl.program_id(0) < n_prompt_tiles
    tm = catc_ref.shape[0]
    cat = jnp.where(is_prompt, jnp.concatenate([catc_ref[...], cata_ref[...]], axis=1), cats_ref[...])
    ss = jnp.zeros((tm, 1), F32)
    for c in range(D_MODEL // OUT_CHUNK):
        cs = slice(c * OUT_CHUNK, (c + 1) * OUT_CHUNK)
        y = jnp.where(is_prompt, xp_ref[:, cs], xs_ref[:, cs]) + _dot(cat, wo_ref[:, cs])
        x1_ref[:, cs] = y
        ss = ss + jnp.sum(y * y, axis=-1, keepdims=True)
    r = lax.rsqrt(ss / D_MODEL + EPS)
    for c in range(D_MODEL // OUT_CHUNK):
        cs = slice(c * OUT_CHUNK, (c + 1) * OUT_CHUNK)
        h_ref[:, cs] = x1_ref[:, cs] * r * g2_ref[:, cs]
    for c in range(D_HALF // OUT_CHUNK):
        cs = slice(c * OUT_CHUNK, (c + 1) * OUT_CHUNK)
        cs_lo = slice(D_HALF + c * OUT_CHUNK, D_HALF + (c + 1) * OUT_CHUNK)
        h2p_ref[:, cs] = _pack_bf16_pairs(h_ref[:, cs], h_ref[:, cs_lo])
    lg_ref[...] = _dot(h_ref[...].astype(BF16), wr_ref[...])


def _out_proj(cat_conv, cat_attn, cat_s, xp2, xs2, wo_bf, g2, wr_bf, tm):
    n_prompt_tiles = cat_conv.shape[0] // tm
    n_rows = (n_prompt_tiles + 1) * tm
    prompt = lambda w: pl.BlockSpec((tm, w), lambda i: (jnp.minimum(i, n_prompt_tiles - 1), 0))
    out_row = lambda w: pl.BlockSpec((tm, w), lambda i: (i, 0))
    body = functools.partial(_out_proj_body, n_prompt_tiles=n_prompt_tiles)
    return pl.pallas_call(
        body,
        grid=(n_prompt_tiles + 1,),
        in_specs=[prompt(D_CONV), prompt(D_ATTN), _resident((tm, D_MODEL)), prompt(D_MODEL),
                  _resident((tm, D_MODEL)),
                  _resident((D_MODEL, D_MODEL)), _resident((1, D_MODEL)), _resident((D_MODEL, LANES))],
        out_specs=[out_row(D_MODEL), out_row(D_HALF), out_row(LANES)],
        out_shape=[jax.ShapeDtypeStruct((n_rows, D_MODEL), F32), jax.ShapeDtypeStruct((n_rows, D_HALF), U32),
                   jax.ShapeDtypeStruct((n_rows, LANES), F32)],
        scratch_shapes=[pltpu.VMEM((tm, D_MODEL), F32)],
        compiler_params=_params(1),
        name="out_proj",
    )(cat_conv, cat_attn, cat_s, xp2, xs2, wo_bf, g2, wr_bf)


ROUTE_CHUNK = 128
ROUTE_UNROLL = 5


def _route_body(lg_ref, bias_ref, slot_ref, gate_ref, blk_ref, cum_ref, sel_ref, *, n_tok):
    n_chunks = n_tok // ROUTE_CHUNK
    lane = lax.broadcasted_iota(I32, (ROUTE_CHUNK, LANES), 1).astype(F32)
    ri = lax.broadcasted_iota(I32, (ROUTE_CHUNK, ROUTE_CHUNK), 0)
    ci = lax.broadcasted_iota(I32, (ROUTE_CHUNK, ROUTE_CHUNK), 1)
    lower = jnp.where(ci < ri, 1.0, 0.0).astype(BF16)
    upper = jnp.where(ri < ci, 1.0, 0.0).astype(BF16)
    diag = ri == ci
    is_group = lane < N_EXPERT_GROUPS

    def first_max(vals):
        m = jnp.max(vals, axis=-1, keepdims=True)
        idx = jnp.min(jnp.where(vals == m, lane, float(LANES)), axis=-1, keepdims=True)
        return m, idx

    def assign(i, carry):
        rows = pl.ds(pl.multiple_of(i * ROUTE_CHUNK, ROUTE_CHUNK), ROUTE_CHUNK)
        l = lg_ref[rows, :] + bias_ref[...]
        gl = jnp.where(is_group, l, NEG_INF)
        g_max, g_idx = first_max(gl)
        g_top = 1.0 / jnp.sum(jnp.exp(gl - g_max), axis=-1, keepdims=True)
        lo = EXPERT_LANE0 + g_idx * EXPERTS_PER_GROUP
        el = jnp.where(lane >= lo, jnp.where(lane < lo + EXPERTS_PER_GROUP, l, NEG_INF), NEG_INF)
        m1, i1 = first_max(el)
        p = jnp.exp(el - m1)
        probs = p / jnp.sum(p, axis=-1, keepdims=True)
        e1 = jnp.sum(jnp.where(lane == i1, probs, 0.0), axis=-1, keepdims=True)
        _, i2 = first_max(jnp.where(lane == i1, NEG_INF, el))
        e2 = jnp.sum(jnp.where(lane == i2, probs, 0.0), axis=-1, keepdims=True)
        gate1 = g_top * e1 / (e1 + e2)
        gate2 = g_top * e2 / (e1 + e2)
        gate_ref[rows, :] = jnp.where(lane == 0, gate1, jnp.where(lane == 1, gate2, 0.0))
        sel_ref[rows, :] = jnp.where(lane == 0, i1, jnp.where(lane == 1, i2, 0.0))
        onehot = jnp.where(lane == i1, 1.0, jnp.where(lane == i2, 1.0, 0.0))
        before = _dot(lower, onehot.astype(BF16)) + carry
        cum_ref[rows, :] = before
        return carry + jnp.sum(onehot, axis=0, keepdims=True)

    counts = lax.fori_loop(0, n_chunks, assign, jnp.zeros((1, LANES), F32), unroll=ROUTE_UNROLL)
    n_blocks = jnp.floor((counts + (MOE_BLOCK - 1)) / MOE_BLOCK)
    nb8 = jnp.broadcast_to(n_blocks, (SUBLANES, LANES)).astype(BF16)
    blk_start = _dot(nb8, upper)[0:1, :]
    row_start = blk_start * MOE_BLOCK

    def place(i, carry):
        rows = pl.ds(pl.multiple_of(i * ROUTE_CHUNK, ROUTE_CHUNK), ROUTE_CHUNK)
        pos = cum_ref[rows, :] + row_start
        sel = sel_ref[rows, :]
        i1 = sel[:, 0:1]
        i2 = sel[:, 1:2]
        s1 = jnp.sum(jnp.where(lane == i1, pos, 0.0), axis=-1, keepdims=True)
        s2 = jnp.sum(jnp.where(lane == i2, pos, 0.0), axis=-1, keepdims=True)
        slot_ref[0, i] = jnp.sum(jnp.where(diag, s1, 0.0), axis=0, keepdims=True).astype(I32)
        slot_ref[1, i] = jnp.sum(jnp.where(diag, s2, 0.0), axis=0, keepdims=True).astype(I32)
        return carry

    lax.fori_loop(0, n_chunks, place, 0, unroll=ROUTE_UNROLL)

    sub = lax.broadcasted_iota(I32, (SUBLANES, LANES), 0)
    blk_ref[...] = jnp.where(sub == 1, counts, blk_start).astype(I32)


def _route(logits, bias, n_tok):
    body = functools.partial(_route_body, n_tok=n_tok)
    n_chunks = n_tok // ROUTE_CHUNK
    whole = lambda r: pl.BlockSpec((r, LANES), lambda i: (0, 0))
    return pl.pallas_call(
        body,
        grid=(1,),
        in_specs=[whole(n_tok), whole(1)],
        out_specs=[pl.BlockSpec((2, n_chunks, 1, LANES), lambda i: (0, 0, 0, 0)), whole(n_tok), whole(SUBLANES)],
        out_shape=[jax.ShapeDtypeStruct((2, n_chunks, 1, LANES), I32), jax.ShapeDtypeStruct((n_tok, LANES), F32),
                   jax.ShapeDtypeStruct((SUBLANES, LANES), I32)],
        scratch_shapes=[pltpu.VMEM((n_tok, LANES), F32), pltpu.VMEM((n_tok, LANES), F32)],
        compiler_params=_params(1),
        name="route",
    )(logits, bias)


INVERT_UNROLL = 16


CLEAR_SPAN = 8


def _invert_body(slot_ref, blk_ref, tok_ref, *, n_slots):
    n_tok = slot_ref.shape[0] // 2

    def clear_expert(e, c):
        first_pad = blk_ref[0, EXPERT_LANE0 + e] * MOE_BLOCK + blk_ref[1, EXPERT_LANE0 + e]
        end = jnp.where(e == N_EXPERTS - 1, n_slots, blk_ref[0, EXPERT_LANE0 + e + 1] * MOE_BLOCK)
        span_bits = CLEAR_SPAN.bit_length() - 1
        lo = lax.shift_right_logical(first_pad, span_bits) * CLEAR_SPAN

        def span(j, c2):
            for k in range(CLEAR_SPAN):
                tok_ref[lo + j * CLEAR_SPAN + k] = 0
            return c2

        lax.fori_loop(0, lax.shift_right_logical(end - lo, span_bits), span, 0)
        return c

    lax.fori_loop(0, N_EXPERTS, clear_expert, 0)

    def put(t, c):
        tok_ref[slot_ref[t]] = t
        tok_ref[slot_ref[n_tok + t]] = t
        return c

    lax.fori_loop(0, n_tok, put, 0, unroll=INVERT_UNROLL)


def _invert(slots, blk, n_slots):
    smem = pl.BlockSpec(memory_space=pltpu.SMEM)
    return pl.pallas_call(
        functools.partial(_invert_body, n_slots=n_slots),
        in_specs=[smem, smem],
        out_specs=smem,
        out_shape=jax.ShapeDtypeStruct((n_slots,), I32),
        name="invert",
    )(slots, blk)


GATHER_AHEAD = 2
GATHER_SLOTS = GATHER_AHEAD + 1


def _row_gather_start(idx_ref, base, src_hbm, dst, sem, n_rows, priorities):
    for r in range(n_rows):
        tok = idx_ref[base + r]
        pltpu.make_async_copy(src_hbm.at[pl.ds(tok, 1), :], dst.at[pl.ds(r, 1), :], sem).start(
            priority=priorities[r % len(priorities)])


def _experts_body(bstart_ref, tok_ref, h2p_hbm, w1_ref, w3_ref, w2_ref, eo_hbm, xbuf, obuf, w1b, w3b, w2b, gsem,
                  osem, *, n_blocks):
    e = pl.program_id(0)
    n_exp = pl.num_programs(0)
    b0 = bstart_ref[0, EXPERT_LANE0 + e]
    nb = bstart_ref[0, EXPERT_LANE0 + e + 1] - b0
    n_used = bstart_ref[0, EXPERT_LANE0 + n_exp]

    def gather(block):
        s = lax.rem(block, GATHER_SLOTS)
        src_block = jnp.minimum(block, n_used - 1)
        _row_gather_start(tok_ref, src_block * MOE_BLOCK, h2p_hbm, xbuf.at[s], gsem.at[s], MOE_BLOCK, (1, 0))

    def gather_wait(block):
        s = lax.rem(block, GATHER_SLOTS)
        pltpu.make_async_copy(xbuf.at[s], xbuf.at[s], gsem.at[s]).wait()

    def out_copy(block, s):
        rows = pl.ds(pl.multiple_of(block * MOE_BLOCK, MOE_BLOCK), MOE_BLOCK)
        return pltpu.make_async_copy(obuf.at[s], eo_hbm.at[rows, :], osem.at[s])

    @pl.when(e == 0)
    def _():
        for k in range(GATHER_AHEAD):
            gather(k)

    @pl.when(nb > 0)
    def _():
        w1b[...] = w1_ref[...].astype(BF16)
        w3b[...] = w3_ref[...].astype(BF16)
        w2b[...] = w2_ref[...].astype(BF16)

    def block(j, carry):
        b = b0 + j
        s = lax.rem(b, 2)
        gather(b + GATHER_AHEAD)
        gather_wait(b)
        hi, lo = _unpack_bf16_pairs(xbuf[lax.rem(b, GATHER_SLOTS)])
        x = jnp.concatenate([hi.astype(BF16), lo.astype(BF16)], axis=1)
        a = _dot(x, w1b[...])
        g = _dot(x, w3b[...])
        hdn = (a * jax.nn.sigmoid(a) * g).astype(BF16)
        o = _dot(hdn, w2b[...])

        @pl.when(b >= 2)
        def _():
            out_copy(b - 2, s).wait()

        obuf[s] = _pack_bf16_pairs(o[:, :D_HALF], o[:, D_HALF:])
        out_copy(b, s).start()
        return carry

    lax.fori_loop(0, nb, block, 0)

    @pl.when(e == n_exp - 1)
    def _():
        for k in range(GATHER_AHEAD):
            gather_wait(n_used + k)

        @pl.when(n_used >= 2)
        def _():
            out_copy(n_used - 2, lax.rem(n_used, 2)).wait()

        out_copy(n_used - 1, lax.rem(n_used - 1, 2)).wait()
        obuf[0] = jnp.zeros((MOE_BLOCK, D_HALF), U32)

        def fill(tb, carry):
            out_copy(tb, 0).start()
            return carry

        lax.fori_loop(n_used, n_blocks, fill, 0)

        def drain(tb, carry):
            out_copy(tb, 0).wait()
            return carry

        lax.fori_loop(n_used, n_blocks, drain, 0)


def _experts(bstart, tok, h2p, w1, w3, w2, n_blocks):
    def wspec(r, c):
        return pl.BlockSpec((None, r, c), lambda e, bstart_ref, tok_ref: (e, 0, 0))

    grid_spec = pltpu.PrefetchScalarGridSpec(
        num_scalar_prefetch=2,
        grid=(N_EXPERTS,),
        in_specs=[pl.BlockSpec(memory_space=pl.ANY),
                  wspec(D_MODEL, D_EXPERT), wspec(D_MODEL, D_EXPERT), wspec(D_EXPERT, D_MODEL)],
        out_specs=pl.BlockSpec(memory_space=pl.ANY),
        scratch_shapes=[pltpu.VMEM((GATHER_SLOTS, MOE_BLOCK, D_HALF), U32), pltpu.VMEM((2, MOE_BLOCK, D_HALF), U32),
                        pltpu.VMEM((D_MODEL, D_EXPERT), BF16), pltpu.VMEM((D_MODEL, D_EXPERT), BF16),
                        pltpu.VMEM((D_EXPERT, D_MODEL), BF16),
                        pltpu.SemaphoreType.DMA((GATHER_SLOTS,)), pltpu.SemaphoreType.DMA((2,))],
    )
    return pl.pallas_call(
        functools.partial(_experts_body, n_blocks=n_blocks),
        grid_spec=grid_spec,
        out_shape=jax.ShapeDtypeStruct((n_blocks * MOE_BLOCK, D_HALF), U32),
        compiler_params=_params(1),
        name="experts",
    )(bstart, tok, h2p, w1, w3, w2)


def _combine_body(slot_ref, x1_ref, gate_ref, eo_hbm, yp_ref, ys_ref, buf, sem, *, n_tok, n_prompt_tiles):
    i = pl.program_id(0)
    tm = x1_ref.shape[0]
    n_tiles = n_tok // tm
    slot = lax.rem(i, GATHER_SLOTS)

    def start(tile):
        s = lax.rem(tile, GATHER_SLOTS)
        base = jnp.minimum(tile, n_tiles - 1) * tm
        _row_gather_start(slot_ref, base, eo_hbm, buf.at[s, 0], sem.at[s], tm, (0, 1))
        _row_gather_start(slot_ref, n_tok + base, eo_hbm, buf.at[s, 1], sem.at[s], tm, (0, 1))

    def wait(tile):
        s = lax.rem(tile, GATHER_SLOTS)
        pltpu.make_async_copy(buf.at[s], buf.at[s], sem.at[s]).wait()

    @pl.when(i == 0)
    def _():
        for k in range(GATHER_AHEAD):
            start(k)

    start(i + GATHER_AHEAD)
    wait(i)

    @pl.when(i == n_tiles - 1)
    def _():
        for k in range(GATHER_AHEAD):
            wait(n_tiles + k)

    g = gate_ref[...]
    hi0, lo0 = _unpack_bf16_pairs(buf[slot, 0])
    hi1, lo1 = _unpack_bf16_pairs(buf[slot, 1])
    y_hi = x1_ref[:, :D_HALF] + g[:, 0:1] * hi0 + g[:, 1:2] * hi1
    y_lo = x1_ref[:, D_HALF:] + g[:, 0:1] * lo0 + g[:, 1:2] * lo1

    @pl.when(i < n_prompt_tiles)
    def _():
        yp_ref[:, :D_HALF] = y_hi
        yp_ref[:, D_HALF:] = y_lo

    @pl.when(i >= n_prompt_tiles)
    def _():
        ys_ref[:, :D_HALF] = y_hi
        ys_ref[:, D_HALF:] = y_lo


def _combine(slots, x1, gates, eo, n_prompt, n_tok, tm):
    n_tiles = n_tok // tm
    n_prompt_tiles = n_prompt // tm
    body = functools.partial(_combine_body, n_tok=n_tok, n_prompt_tiles=n_prompt_tiles)
    grid_spec = pltpu.PrefetchScalarGridSpec(
        num_scalar_prefetch=1,
        grid=(n_tiles,),
        in_specs=[pl.BlockSpec((tm, D_MODEL), lambda i, s: (i, 0)),
                  pl.BlockSpec((tm, LANES), lambda i, s: (i, 0)),
                  pl.BlockSpec(memory_space=pl.ANY)],
        out_specs=[pl.BlockSpec((tm, D_MODEL), lambda i, s: (jnp.minimum(i, n_prompt_tiles - 1), 0)),
                   pl.BlockSpec((tm, D_MODEL), lambda i, s: (jnp.maximum(i - n_prompt_tiles, 0), 0))],
        scratch_shapes=[pltpu.VMEM((GATHER_SLOTS, 2, tm, D_HALF), U32), pltpu.SemaphoreType.DMA((GATHER_SLOTS,))],
    )
    return pl.pallas_call(
        body,
        grid_spec=grid_spec,
        out_shape=[jax.ShapeDtypeStruct((n_prompt, D_MODEL), F32),
                   jax.ShapeDtypeStruct((n_tok - n_prompt, D_MODEL), F32)],
        compiler_params=_params(1),
        name="combine",
    )(slots, x1, gates, eo)


def _layer(x_prompt, x_sample, state_conv, cache_k, cache_v, norm1_g, w_in, conv_w, conv_b, conv_norm_g,
           conv_norm_b, q_norm_g, k_norm_g, attn_sinks, w_out, norm2_g, w_rg, b_rg, w_re, b_re, w1, w3, w2):
    b, t, _ = x_prompt.shape
    sb, st, _ = x_sample.shape
    n_p, n_s = b * t, sb * st
    n_tok = n_p + n_s
    w_past = cache_k.shape[1]

    w_in_bf = w_in.astype(BF16)
    w_out_bf = w_out.astype(BF16)
    g1 = norm1_g.reshape(1, D_MODEL)
    g2 = norm2_g.reshape(1, D_MODEL)
    qg = q_norm_g.reshape(1, HEAD_DIM)
    kg = k_norm_g.reshape(1, HEAD_DIM)
    chunked = lambda a: a.reshape(-1, N_CCHUNK, LANES).transpose(1, 0, 2)
    cw, cb, lg, lb = chunked(conv_w), chunked(conv_b), chunked(conv_norm_g), chunked(conv_norm_b)
    pad_lanes = LANES - N_EXPERT_GROUPS - N_EXPERTS
    w_router_bf = jnp.concatenate([w_rg, w_re, jnp.zeros((D_MODEL, pad_lanes), F32)], axis=1).astype(BF16)
    b_router = jnp.concatenate([b_rg, b_re, jnp.zeros((pad_lanes,), F32)]).reshape(1, LANES)

    xp2 = x_prompt.reshape(n_p, D_MODEL)
    xs2 = x_sample.reshape(n_s, D_MODEL)

    c_p, q_p, k_p, v_p, conv_p, knew_p, vnew_p = _in_proj_conv(xp2, g1, w_in_bf, qg, kg, cw, cb, lg, lb, 512, b)
    u_s, q_s, k_s, v_s = _in_proj(xs2, g1, w_in_bf, qg, kg, n_s, F32)

    r3 = lambda a, bb: a.reshape(bb, -1, a.shape[-1])
    a_p = _mixer_prompt(attn_sinks, r3(q_p, b), r3(k_p, b), r3(v_p, b), 256)
    cat_s, conv_s, knew_s, vnew_s = _mixer_sample(
        attn_sinks, r3(u_s, sb), state_conv, r3(q_s, sb), r3(k_s, sb), r3(v_s, sb),
        cache_k.reshape(sb, w_past, D_KV), cache_v.reshape(sb, w_past, D_KV), cw, cb, lg, lb)

    tm_o = 256
    pad_rows = lambda a: jnp.pad(a, ((0, tm_o - n_s), (0, 0)))
    x1, h2p, logits = _out_proj(c_p, a_p.reshape(n_p, D_ATTN), pad_rows(cat_s.reshape(n_s, D_MODEL).astype(BF16)),
                               xp2, pad_rows(xs2), w_out_bf, g2, w_router_bf, tm_o)

    n_blocks = -(-(n_tok * 2) // MOE_BLOCK) + N_EXPERTS
    slots, gates, blk = _route(logits, b_router, n_tok)
    slots = slots.reshape(2 * n_tok)
    tok = _invert(slots, blk, n_blocks * MOE_BLOCK)
    eo = _experts(blk, tok, h2p, w1, w3, w2, n_blocks)
    y_p, y_s = _combine(slots, x1, gates, eo, n_p, n_tok, COMBINE_TILE)

    return (y_p.reshape(b, t, D_MODEL), y_s.reshape(sb, st, D_MODEL), conv_p, knew_p, vnew_p, conv_s,
            knew_s.reshape(sb, w_past, N_KV_HEADS, HEAD_DIM), vnew_s.reshape(sb, w_past, N_KV_HEADS, HEAD_DIM))


def kernel(x_prompt, x_sample, state_conv, cache_k, cache_v, norm1_g, w_in, conv_w, conv_b, conv_norm_g, conv_norm_b, q_norm_g, k_norm_g, attn_sinks, w_out, norm2_g, w_router_group, b_router_group, w_router_expert, b_router_expert, w1, w3, w2):
    depth = w_in.shape[0]
    assert depth == 1, "single-layer step"
    outs = _layer(x_prompt, x_sample, state_conv[0], cache_k[0], cache_v[0], norm1_g[0], w_in[0], conv_w[0],
                  conv_b[0], conv_norm_g[0], conv_norm_b[0], q_norm_g[0], k_norm_g[0], attn_sinks[0], w_out[0],
                  norm2_g[0], w_router_group[0], b_router_group[0], w_router_expert[0], b_router_expert[0],
                  w1[0], w3[0], w2[0])
    y_p, y_s = outs[0], outs[1]
    return (y_p, y_s) + tuple(o[None] for o in outs[2:])
```

```python
import functools
import math

import jax
import jax.numpy as jnp
from jax import lax
from jax.experimental import pallas as pl
from jax.experimental.pallas import tpu as pltpu

F32 = jnp.float32
BF16 = jnp.bfloat16
I32 = jnp.int32

D_MODEL = 2048
D_CONV = 1024
CONV_WIDTH = 31
CONV_HIST = CONV_WIDTH - 1
D_ATTN = 1024
HEAD_DIM = 128
N_HEADS = 8
N_KV_HEADS = 2
GQA_GROUP = N_HEADS // N_KV_HEADS
D_KV = N_KV_HEADS * HEAD_DIM
WINDOW = 128
BLOCK_Q = 128
SCALE = 1.0 / math.sqrt(HEAD_DIM)
N_EXPERT_GROUPS = 4
EXPERTS_PER_GROUP = 8
N_EXPERTS = N_EXPERT_GROUPS * EXPERTS_PER_GROUP
D_EXPERT = 512
MOE_BLOCK = 256
COMBINE_TILE = 128
D_IN = 2 * D_CONV + D_ATTN + 2 * D_KV
EPS = 1e-6
PAST_LEN = 16384

LANES = 128
SUBLANES = 8
MXU_COLS = 256
VMEM_LIMIT_BYTES = 56 * 1024 * 1024
NEG_INF = float("-inf")
EXPERT_LANE0 = N_EXPERT_GROUPS


def _params(n_axes):
    return pltpu.CompilerParams(dimension_semantics=("arbitrary",) * n_axes,
                                vmem_limit_bytes=VMEM_LIMIT_BYTES)


def _resident(shape):
    nd = len(shape)
    return pl.BlockSpec(shape, lambda *_: (0,) * nd, pipeline_mode=pl.Buffered(1))


def _dot(a, b):
    return jnp.dot(a, b, preferred_element_type=F32)


def _in_proj_body(x_ref, g1_ref, w_ref, qg_ref, kg_ref, u_ref, q_ref, k_ref, v_ref, n_ref):
    x = x_ref[...]
    ms = jnp.mean(x * x, axis=-1, keepdims=True)
    n_ref[...] = (x * lax.rsqrt(ms + EPS) * g1_ref[...]).astype(BF16)

    def head_norm(h, g):
        return h * lax.rsqrt(jnp.mean(h * h, axis=-1, keepdims=True) + EPS) * g

    ch = MXU_COLS
    for c in range(D_CONV // ch):
        a = _dot(n_ref[...], w_ref[:, c * ch:(c + 1) * ch])
        g = _dot(n_ref[...], w_ref[:, D_CONV + c * ch:D_CONV + (c + 1) * ch])
        u_ref[:, c * ch:(c + 1) * ch] = a * jax.nn.sigmoid(g)
    q_off = 2 * D_CONV
    for c in range(D_ATTN // ch):
        qq = _dot(n_ref[...], w_ref[:, q_off + c * ch:q_off + (c + 1) * ch])
        for j in range(ch // HEAD_DIM):
            qh = head_norm(qq[:, j * HEAD_DIM:(j + 1) * HEAD_DIM], qg_ref[...])
            q_ref[:, c * ch + j * HEAD_DIM:c * ch + (j + 1) * HEAD_DIM] = qh.astype(q_ref.dtype)
    k_off = q_off + D_ATTN
    kk = _dot(n_ref[...], w_ref[:, k_off:k_off + D_KV])
    for j in range(N_KV_HEADS):
        k_ref[:, j * HEAD_DIM:(j + 1) * HEAD_DIM] = head_norm(kk[:, j * HEAD_DIM:(j + 1) * HEAD_DIM], kg_ref[...])
    v_ref[...] = _dot(n_ref[...], w_ref[:, k_off + D_KV:k_off + 2 * D_KV])


def _in_proj_conv_body(x_ref, g1_ref, w_ref, qg_ref, kg_ref, cw_ref, cb_ref, lg_ref, lb_ref, c_ref, q_ref, k_ref,
                       v_ref, ut_ref, kt_ref, vt_ref, n_ref, ue_ref, conv_ref, *, tiles_per_seq):
    tm = x_ref.shape[0]
    first = lax.rem(pl.program_id(0), tiles_per_seq) == 0

    @pl.when(first)
    def _():
        for c in range(N_CCHUNK):
            ue_ref[c, 0:CONV_HALO, :] = jnp.zeros((CONV_HALO, LANES), F32)

    @pl.when(jnp.logical_not(first))
    def _():
        for c in range(N_CCHUNK):
            ue_ref[c, 0:CONV_HALO, :] = ue_ref[c, tm:tm + CONV_HALO, :]

    x = x_ref[...]
    ms = jnp.mean(x * x, axis=-1, keepdims=True)
    n_ref[...] = (x * lax.rsqrt(ms + EPS) * g1_ref[...]).astype(BF16)

    def head_norm(h, g):
        return h * lax.rsqrt(jnp.mean(h * h, axis=-1, keepdims=True) + EPS) * g

    ch = MXU_COLS
    lanes_per = ch // LANES
    n_glu = D_CONV // ch
    q_off = 2 * D_CONV

    def glu(c):
        a = _dot(n_ref[...], w_ref[:, c * ch:(c + 1) * ch])
        g = _dot(n_ref[...], w_ref[:, D_CONV + c * ch:D_CONV + (c + 1) * ch])
        u = a * jax.nn.sigmoid(g)
        for half in range(lanes_per):
            ue_ref[c * lanes_per + half, CONV_HALO:CONV_HALO + tm, :] = u[:, half * LANES:(half + 1) * LANES]

    def conv(c):
        for half in range(lanes_per):
            lc = c * lanes_per + half
            _conv_chunk(ue_ref, lc, cw_ref[lc], cb_ref[lc], conv_ref, tm)

    def q_heads(c):
        qq = _dot(n_ref[...], w_ref[:, q_off + c * ch:q_off + (c + 1) * ch])
        for j in range(ch // HEAD_DIM):
            qh = head_norm(qq[:, j * HEAD_DIM:(j + 1) * HEAD_DIM], qg_ref[...])
            q_ref[:, c * ch + j * HEAD_DIM:c * ch + (j + 1) * HEAD_DIM] = qh.astype(q_ref.dtype)

    glu(0)
    for c in range(n_glu):
        if c + 1 < n_glu:
            glu(c + 1)
        else:
            q_heads(0)
        conv(c)
    q_heads(1)
    _ln_swish(conv_ref, lg_ref, lb_ref, c_ref, tm)
    for c in range(2, D_ATTN // ch):
        q_heads(c)
    k_off = q_off + D_ATTN
    kk = _dot(n_ref[...], w_ref[:, k_off:k_off + D_KV])
    vv = _dot(n_ref[...], w_ref[:, k_off + D_KV:k_off + 2 * D_KV])
    v_ref[...] = vv
    for j in range(N_KV_HEADS):
        hs = slice(j * HEAD_DIM, (j + 1) * HEAD_DIM)
        kn = head_norm(kk[:, hs], kg_ref[...])
        k_ref[:, hs] = kn
        kt_ref[:, j, :] = kn[tm - WINDOW:, :]
        vt_ref[:, j, :] = vv[tm - WINDOW:, hs]
    for c in range(N_CCHUNK):
        ut_ref[:, c * LANES:(c + 1) * LANES] = ue_ref[c, CONV_HALO + tm - CONV_HIST:CONV_HALO + tm, :]


def _in_proj_conv(x2, g1, w_in_bf, qg, kg, cw, cb, lg, lb, tm, n_seq):
    n = x2.shape[0]
    tiles_per_seq = n // n_seq // tm
    row = lambda w: pl.BlockSpec((tm, w), lambda i: (i, 0))
    seq = lambda *dims: pl.BlockSpec((None,) + dims, lambda i: (i // tiles_per_seq,) + (0,) * len(dims))
    body = functools.partial(_in_proj_conv_body, tiles_per_seq=tiles_per_seq)
    return pl.pallas_call(
        body,
        grid=(n // tm,),
        in_specs=[row(D_MODEL), _resident((1, D_MODEL)), _resident((D_MODEL, D_IN)),
                  _resident((1, HEAD_DIM)), _resident((1, HEAD_DIM)),
                  _resident((N_CCHUNK, CONV_WIDTH, LANES)), _resident((N_CCHUNK, 1, LANES)),
                  _resident((N_CCHUNK, 1, LANES)), _resident((N_CCHUNK, 1, LANES))],
        out_specs=[row(D_CONV), row(D_ATTN), row(D_KV), row(D_KV),
                   seq(CONV_HIST, D_CONV), seq(WINDOW, N_KV_HEADS, HEAD_DIM), seq(WINDOW, N_KV_HEADS, HEAD_DIM)],
        out_shape=[jax.ShapeDtypeStruct((n, D_CONV), BF16), jax.ShapeDtypeStruct((n, D_ATTN), BF16),
                   jax.ShapeDtypeStruct((n, D_KV), F32), jax.ShapeDtypeStruct((n, D_KV), F32),
                   jax.ShapeDtypeStruct((n_seq, CONV_HIST, D_CONV), F32),
                   jax.ShapeDtypeStruct((n_seq, WINDOW, N_KV_HEADS, HEAD_DIM), F32),
                   jax.ShapeDtypeStruct((n_seq, WINDOW, N_KV_HEADS, HEAD_DIM), F32)],
        scratch_shapes=[pltpu.VMEM((tm, D_MODEL), BF16),
                        pltpu.VMEM((N_CCHUNK, CONV_HALO + tm, LANES), F32),
                        pltpu.VMEM((N_CCHUNK, tm, LANES), F32)],
        compiler_params=_params(1),
        name="in_proj_conv",
    )(x2, g1, w_in_bf, qg, kg, cw, cb, lg, lb)


def _in_proj(x2, g1, w_in_bf, qg, kg, tm, q_dtype):
    n = x2.shape[0]
    row = lambda w: pl.BlockSpec((tm, w), lambda i: (i, 0))
    return pl.pallas_call(
        _in_proj_body,
        grid=(n // tm,),
        in_specs=[row(D_MODEL), _resident((1, D_MODEL)), _resident((D_MODEL, D_IN)),
                  _resident((1, HEAD_DIM)), _resident((1, HEAD_DIM))],
        out_specs=[row(D_CONV), row(D_ATTN), row(D_KV), row(D_KV)],
        out_shape=[jax.ShapeDtypeStruct((n, D_CONV), F32), jax.ShapeDtypeStruct((n, D_ATTN), q_dtype),
                   jax.ShapeDtypeStruct((n, D_KV), F32), jax.ShapeDtypeStruct((n, D_KV), F32)],
        scratch_shapes=[pltpu.VMEM((tm, D_MODEL), BF16)],
        compiler_params=_params(1),
        name="in_proj",
    )(x2, g1, w_in_bf, qg, kg)


N_CCHUNK = D_CONV // LANES
CONV_ROWS = 64


CONV_HALO = 32


def _conv_chunk(ue_ref, c, wc, bias, conv_ref, rows):
    base = CONV_HALO - CONV_HIST
    for r0 in range(0, rows, CONV_ROWS):
        acc = jnp.broadcast_to(bias, (CONV_ROWS, LANES))
        for tap in range(CONV_WIDTH):
            acc = acc + wc[tap:tap + 1, :] * ue_ref[c, base + r0 + tap:base + r0 + tap + CONV_ROWS, :]
        conv_ref[c, r0:r0 + CONV_ROWS, :] = acc


def _ln_swish(conv_ref, lg_ref, lb_ref, cat_ref, rows):
    tot = jnp.zeros((rows, 1), F32)
    for c in range(N_CCHUNK):
        tot = tot + jnp.sum(conv_ref[c], axis=-1, keepdims=True)
    mean = tot / D_CONV
    var = jnp.zeros((rows, 1), F32)
    for c in range(N_CCHUNK):
        xc = conv_ref[c] - mean
        var = var + jnp.sum(xc * xc, axis=-1, keepdims=True)
    rstd = lax.rsqrt(var / D_CONV + EPS)
    for c in range(N_CCHUNK):
        y = (conv_ref[c] - mean) * rstd * lg_ref[c] + lb_ref[c]
        cat_ref[:, c * LANES:(c + 1) * LANES] = (y * jax.nn.sigmoid(y)).astype(cat_ref.dtype)


def _sink_softmax_rows(s, sink):
    m = jnp.maximum(jnp.max(s, axis=-1, keepdims=True), sink)
    p = jnp.exp(s - m)
    return p / (jnp.sum(p, axis=-1, keepdims=True) + jnp.exp(sink - m))


def _alibi_slope(head):
    return 2.0 ** (-8.0 * (head + 1) / N_HEADS)


def _mixer_prompt_body(sink_ref, q_ref, k_ref, kh_ref, v_ref, vh_ref, cat_ref, *, tm):
    j = pl.program_id(1)
    has_prev = j > 0
    qi = lax.broadcasted_iota(I32, (BLOCK_Q, 2 * BLOCK_Q), 0)
    kj = lax.broadcasted_iota(I32, (BLOCK_Q, 2 * BLOCK_Q), 1)
    dist = qi + BLOCK_Q - kj
    distf = dist.astype(F32)
    band = jnp.where(dist >= 0, jnp.where(dist < WINDOW, 0.0, NEG_INF), NEG_INF)
    band_first = jnp.where(kj >= BLOCK_Q, band, jnp.where(has_prev, band, NEG_INF))

    for qb in range(tm // BLOCK_Q):
        rows = slice(qb * BLOCK_Q, (qb + 1) * BLOCK_Q)
        prev = slice((qb - 1) * BLOCK_Q, qb * BLOCK_Q)
        mask = band_first if qb == 0 else band
        for kv in range(N_KV_HEADS):
            hs = slice(kv * HEAD_DIM, (kv + 1) * HEAD_DIM)
            k_prev = kh_ref[:, hs] if qb == 0 else k_ref[prev, hs]
            v_prev = vh_ref[:, hs] if qb == 0 else v_ref[prev, hs]
            kk = jnp.concatenate([k_prev, k_ref[rows, hs]], axis=0).astype(BF16)
            vv = jnp.concatenate([v_prev, v_ref[rows, hs]], axis=0).astype(BF16)
            heads = [kv * GQA_GROUP + g for g in range(GQA_GROUP)]
            qs = jnp.concatenate([q_ref[rows, h * HEAD_DIM:(h + 1) * HEAD_DIM] for h in heads], axis=0)
            s = lax.dot_general(qs, kk, (((1,), (1,)), ((), ())), preferred_element_type=F32)
            ps = []
            for g, h in enumerate(heads):
                sg = s[g * BLOCK_Q:(g + 1) * BLOCK_Q] * SCALE - _alibi_slope(h) * distf + mask
                ps.append(_sink_softmax_rows(sg, sink_ref[h]).astype(BF16))
            o = _dot(jnp.concatenate(ps, axis=0), vv)
            for g, h in enumerate(heads):
                cat_ref[rows, h * HEAD_DIM:(h + 1) * HEAD_DIM] = o[g * BLOCK_Q:(g + 1) * BLOCK_Q].astype(cat_ref.dtype)


def _mixer_prompt(sinks, q, k, v, tm):
    b, t, _ = q.shape
    kpb = tm // BLOCK_Q
    main = lambda w: pl.BlockSpec((None, tm, w), lambda bi, j: (bi, j, 0))
    prev_block = pl.BlockSpec((None, BLOCK_Q, D_KV), lambda bi, j: (bi, jnp.maximum(j * kpb - 1, 0), 0))
    body = functools.partial(_mixer_prompt_body, tm=tm)
    return pl.pallas_call(
        body,
        grid=(b, t // tm),
        in_specs=[pl.BlockSpec(memory_space=pltpu.SMEM), main(D_ATTN), main(D_KV), prev_block, main(D_KV),
                  prev_block],
        out_specs=main(D_ATTN),
        out_shape=jax.ShapeDtypeStruct((b, t, D_ATTN), BF16),
        compiler_params=_params(2),
        name="mixer_prompt",
    )(sinks, q, k, k, v, v)


KEY_PAD = 8


SAMPLE_SEQS = 8


def _sample_conv_ln_swish(st_ref, u_ref, cw_ref, cb_ref, lg_ref, lb_ref, cat_ref, t_new):
    n_seq, hist, _ = st_ref.shape
    conv = [[None] * N_CCHUNK for _ in range(t_new)]
    for c in range(N_CCHUNK):
        cs = slice(c * LANES, (c + 1) * LANES)
        wc = cw_ref[c]
        pos = [st_ref[:, p, cs] for p in range(hist)] + [u_ref[:, t, cs] for t in range(t_new)]
        for t in range(t_new):
            acc = jnp.broadcast_to(cb_ref[c], (n_seq, LANES))
            for tap in range(CONV_WIDTH):
                acc = acc + wc[tap:tap + 1, :] * pos[t + tap]
            conv[t][c] = acc
    for t in range(t_new):
        tot = jnp.zeros((n_seq, 1), F32)
        for a in conv[t]:
            tot = tot + jnp.sum(a, axis=-1, keepdims=True)
        mean = tot / D_CONV
        var = jnp.zeros((n_seq, 1), F32)
        for a in conv[t]:
            var = var + jnp.sum((a - mean) * (a - mean), axis=-1, keepdims=True)
        rstd = lax.rsqrt(var / D_CONV + EPS)
        for c, a in enumerate(conv[t]):
            y = (a - mean) * rstd * lg_ref[c] + lb_ref[c]
            cat_ref[:, t, c * LANES:(c + 1) * LANES] = (y * jax.nn.sigmoid(y)).astype(cat_ref.dtype)


def _mixer_sample_body(sink_ref, u_ref, st_ref, q_ref, k_ref, v_ref, ck_ref, cv_ref, cw_ref, cb_ref, lg_ref,
                       lb_ref, cat_ref, nst_ref, nk_ref, nv_ref, kk_ref, vv_ref, qs_ref, *, t_new):
    n_seq, hist, _ = st_ref.shape
    w_past = ck_ref.shape[1]
    n_keys = w_past + KEY_PAD
    q_rows = GQA_GROUP * t_new
    n_rows, n_cols = n_seq * q_rows, n_seq * n_keys

    _sample_conv_ln_swish(st_ref, u_ref, cw_ref, cb_ref, lg_ref, lb_ref, cat_ref, t_new)

    for i in range(n_seq):
        nst_ref[i, 0:hist - t_new, :] = st_ref[i, t_new:hist, :]
        nst_ref[i, hist - t_new:hist, :] = u_ref[i]

        nk_ref[i, 0:w_past - t_new] = ck_ref[i, t_new:w_past]
        nv_ref[i, 0:w_past - t_new] = cv_ref[i, t_new:w_past]
        kk_ref[i, w_past:n_keys, :] = jnp.zeros((KEY_PAD, D_KV), F32)
        vv_ref[i, w_past:n_keys, :] = jnp.zeros((KEY_PAD, D_KV), F32)
        kk_ref[i, w_past:w_past + t_new, :] = k_ref[i]
        vv_ref[i, w_past:w_past + t_new, :] = v_ref[i]
        for h in range(N_KV_HEADS):
            hs = slice(h * HEAD_DIM, (h + 1) * HEAD_DIM)
            kk_ref[i, 0:w_past, hs] = ck_ref[i, :, h, :]
            vv_ref[i, 0:w_past, hs] = cv_ref[i, :, h, :]
            nk_ref[i, w_past - t_new:w_past, h, :] = k_ref[i, :, hs]
            nv_ref[i, w_past - t_new:w_past, h, :] = v_ref[i, :, hs]

    row = lax.broadcasted_iota(I32, (n_rows, n_cols), 0)
    col = lax.broadcasted_iota(I32, (n_rows, n_cols), 1)
    tok_bits, row_bits = t_new.bit_length() - 1, q_rows.bit_length() - 1
    assert (1 << tok_bits, 1 << row_bits) == (t_new, q_rows), "token and row counts must be powers of two"
    tok = row & (t_new - 1)
    key = col - (row >> row_bits) * n_keys
    dist = tok + w_past - key
    distf = dist.astype(F32)
    mask = jnp.where(dist >= 0, jnp.where(dist < WINDOW, 0.0, NEG_INF), NEG_INF)
    row1 = lax.broadcasted_iota(I32, (n_rows, 1), 0)
    grp = (row1 >> tok_bits) & (GQA_GROUP - 1)

    for kv in range(N_KV_HEADS):
        hs = slice(kv * HEAD_DIM, (kv + 1) * HEAD_DIM)
        slope = jnp.zeros((n_rows, 1), F32)
        sink = jnp.zeros((n_rows, 1), F32)
        for g in range(GQA_GROUP):
            h = kv * GQA_GROUP + g
            slope = jnp.where(grp == g, _alibi_slope(h), slope)
            sink = jnp.where(grp == g, sink_ref[h], sink)
            for i in range(n_seq):
                r0 = i * q_rows + g * t_new
                qs_ref[r0:r0 + t_new, :] = q_ref[i, :, h * HEAD_DIM:(h + 1) * HEAD_DIM]
        kk = kk_ref[:, :, hs].reshape(n_cols, HEAD_DIM).astype(BF16)
        vv = vv_ref[:, :, hs].reshape(n_cols, HEAD_DIM).astype(BF16)
        s = lax.dot_general(qs_ref[...].astype(BF16), kk, (((1,), (1,)), ((), ())), preferred_element_type=F32)
        sg = s * SCALE - slope * distf + mask
        o = _dot(_sink_softmax_rows(sg, sink).astype(BF16), vv)
        for g in range(GQA_GROUP):
            h = kv * GQA_GROUP + g
            for i in range(n_seq):
                r0 = i * q_rows + g * t_new
                cat_ref[i, :, D_CONV + h * HEAD_DIM:D_CONV + (h + 1) * HEAD_DIM] = (
                    o[r0:r0 + t_new].astype(cat_ref.dtype))


def _mixer_sample(sinks, u, state, q, k, v, ck, cv, cw, cb, lg, lb):
    b, t_new, _ = u.shape
    hist = state.shape[2]
    w_past = ck.shape[2]
    n_seq = SAMPLE_SEQS
    per = lambda r, w: pl.BlockSpec((n_seq, r, w), lambda bi: (bi, 0, 0))
    layer = lambda *dims: pl.BlockSpec((None, n_seq) + dims, lambda bi: (0, bi) + (0,) * len(dims))
    cache = layer(w_past, N_KV_HEADS, HEAD_DIM)
    body = functools.partial(_mixer_sample_body, t_new=t_new)
    return pl.pallas_call(
        body,
        grid=(b // n_seq,),
        in_specs=[pl.BlockSpec(memory_space=pltpu.SMEM),
                  per(t_new, D_CONV), layer(hist, D_CONV), per(t_new, D_ATTN), per(t_new, D_KV), per(t_new, D_KV),
                  cache, cache,
                  _resident((N_CCHUNK, CONV_WIDTH, LANES)), _resident((N_CCHUNK, 1, LANES)),
                  _resident((N_CCHUNK, 1, LANES)), _resident((N_CCHUNK, 1, LANES))],
        out_specs=[per(t_new, D_MODEL), layer(hist, D_CONV), cache, cache],
        out_shape=[jax.ShapeDtypeStruct((b, t_new, D_MODEL), F32),
                   jax.ShapeDtypeStruct(state.shape, F32),
                   jax.ShapeDtypeStruct(ck.shape, F32),
                   jax.ShapeDtypeStruct(cv.shape, F32)],
        scratch_shapes=[pltpu.VMEM((n_seq, w_past + KEY_PAD, D_KV), F32),
                        pltpu.VMEM((n_seq, w_past + KEY_PAD, D_KV), F32),
                        pltpu.VMEM((n_seq * GQA_GROUP * t_new, HEAD_DIM), F32)],
        compiler_params=_params(1),
        name="mixer_sample",
    )(sinks, u, state, q, k, v, ck, cv, cw, cb, lg, lb)


OUT_CHUNK = 512
OUT_TILE = 512
D_HALF = D_MODEL // 2
U32 = jnp.uint32


def _pack_bf16_pairs(hi, lo):
    hi_bits = lax.bitcast_convert_type(hi.astype(BF16).astype(F32), U32)
    lo_bits = lax.bitcast_convert_type(lo.astype(BF16).astype(F32), U32)
    return hi_bits | (lo_bits >> 16)


def _unpack_bf16_pairs(words):
    hi = lax.bitcast_convert_type(words & U32(0xFFFF0000), F32)
    lo = lax.bitcast_convert_type(words << 16, F32)
    return hi, lo


def _out_proj_body(catc_ref, cata_ref, cats_ref, xp_ref, xs_ref, wo_ref, g2_ref, wr_ref, x1_ref, h2p_ref, lg_ref,
                   h_ref, *, n_prompt_tiles):
    is_prompt = pl.program_id(0) < n_prompt_tiles
    tm = catc_ref.shape[0]

    def sample_rows(v):
        return jnp.concatenate([v, jnp.zeros((tm - v.shape[0], v.shape[1]), v.dtype)], axis=0)

    cat = jnp.where(is_prompt, jnp.concatenate([catc_ref[...], cata_ref[...]], axis=1),
                    sample_rows(cats_ref[...].astype(BF16)))
    ss = jnp.zeros((tm, 1), F32)
    for c in range(D_MODEL // OUT_CHUNK):
        cs = slice(c * OUT_CHUNK, (c + 1) * OUT_CHUNK)
        y = jnp.where(is_prompt, xp_ref[:, cs], sample_rows(xs_ref[:, cs])) + _dot(cat, wo_ref[:, cs])
        x1_ref[:, cs] = y
        ss = ss + jnp.sum(y * y, axis=-1, keepdims=True)
    r = lax.rsqrt(ss / D_MODEL + EPS)
    for c in range(D_MODEL // OUT_CHUNK):
        cs = slice(c * OUT_CHUNK, (c + 1) * OUT_CHUNK)
        h_ref[:, cs] = x1_ref[:, cs] * r * g2_ref[:, cs]
    for c in range(D_HALF // OUT_CHUNK):
        cs = slice(c * OUT_CHUNK, (c + 1) * OUT_CHUNK)
        cs_lo = slice(D_HALF + c * OUT_CHUNK, D_HALF + (c + 1) * OUT_CHUNK)
        h2p_ref[:, cs] = _pack_bf16_pairs(h_ref[:, cs], h_ref[:, cs_lo])
    lg_ref[...] = _dot(h_ref[...].astype(BF16), wr_ref[...])


def _out_proj(cat_conv, cat_attn, cat_s, xp2, xs2, wo_bf, g2, wr_bf, tm):
    n_prompt_tiles = cat_conv.shape[0] // tm
    n_rows = (n_prompt_tiles + 1) * tm
    n_s = cat_s.shape[0]
    prompt = lambda w: pl.BlockSpec((tm, w), lambda i: (jnp.minimum(i, n_prompt_tiles - 1), 0))
    out_row = lambda w: pl.BlockSpec((tm, w), lambda i: (i, 0))
    body = functools.partial(_out_proj_body, n_prompt_tiles=n_prompt_tiles)
    return pl.pallas_call(
        body,
        grid=(n_prompt_tiles + 1,),
        in_specs=[prompt(D_CONV), prompt(D_ATTN), _resident((n_s, D_MODEL)), prompt(D_MODEL),
                  _resident((n_s, D_MODEL)),
                  _resident((D_MODEL, D_MODEL)), _resident((1, D_MODEL)), _resident((D_MODEL, LANES))],
        out_specs=[out_row(D_MODEL), out_row(D_HALF), out_row(LANES)],
        out_shape=[jax.ShapeDtypeStruct((n_rows, D_MODEL), F32), jax.ShapeDtypeStruct((n_rows, D_HALF), U32),
                   jax.ShapeDtypeStruct((n_rows, LANES), F32)],
        scratch_shapes=[pltpu.VMEM((tm, D_MODEL), F32)],
        compiler_params=_params(1),
        name="out_proj",
    )(cat_conv, cat_attn, cat_s, xp2, xs2, wo_bf, g2, wr_bf)


ROUTE_CHUNK = 128
ROUTE_UNROLL = 5


def _route_body(lg_ref, bias_ref, slot_ref, gate_ref, blk_ref, cum_ref, sel_ref, *, n_tok):
    n_chunks = n_tok // ROUTE_CHUNK
    lane = lax.broadcasted_iota(I32, (ROUTE_CHUNK, LANES), 1).astype(F32)
    ri = lax.broadcasted_iota(I32, (ROUTE_CHUNK, ROUTE_CHUNK), 0)
    ci = lax.broadcasted_iota(I32, (ROUTE_CHUNK, ROUTE_CHUNK), 1)
    lower = jnp.where(ci < ri, 1.0, 0.0).astype(BF16)
    upper = jnp.where(ri < ci, 1.0, 0.0).astype(BF16)
    diag = ri == ci
    is_group = lane < N_EXPERT_GROUPS

    def first_max(vals):
        m = jnp.max(vals, axis=-1, keepdims=True)
        idx = jnp.min(jnp.where(vals == m, lane, float(LANES)), axis=-1, keepdims=True)
        return m, idx

    def assign(i, carry):
        rows = pl.ds(pl.multiple_of(i * ROUTE_CHUNK, ROUTE_CHUNK), ROUTE_CHUNK)
        l = lg_ref[rows, :] + bias_ref[...]
        gl = jnp.where(is_group, l, NEG_INF)
        g_max, g_idx = first_max(gl)
        g_top = 1.0 / jnp.sum(jnp.exp(gl - g_max), axis=-1, keepdims=True)
        lo = EXPERT_LANE0 + g_idx * EXPERTS_PER_GROUP
        el = jnp.where(lane >= lo, jnp.where(lane < lo + EXPERTS_PER_GROUP, l, NEG_INF), NEG_INF)
        m1, i1 = first_max(el)
        p = jnp.exp(el - m1)
        probs = p / jnp.sum(p, axis=-1, keepdims=True)
        e1 = jnp.sum(jnp.where(lane == i1, probs, 0.0), axis=-1, keepdims=True)
        _, i2 = first_max(jnp.where(lane == i1, NEG_INF, el))
        e2 = jnp.sum(jnp.where(lane == i2, probs, 0.0), axis=-1, keepdims=True)
        gate1 = g_top * e1 / (e1 + e2)
        gate2 = g_top * e2 / (e1 + e2)
        gate_ref[rows, :] = jnp.where(lane == 0, gate1, jnp.where(lane == 1, gate2, 0.0))
        sel_ref[rows, :] = jnp.where(lane == 0, i1, jnp.where(lane == 1, i2, 0.0))
        onehot = jnp.where(lane == i1, 1.0, jnp.where(lane == i2, 1.0, 0.0))
        before = _dot(lower, onehot.astype(BF16)) + carry
        cum_ref[rows, :] = before
        return carry + jnp.sum(onehot, axis=0, keepdims=True)

    counts = lax.fori_loop(0, n_chunks, assign, jnp.zeros((1, LANES), F32), unroll=ROUTE_UNROLL)
    n_blocks = jnp.floor((counts + (MOE_BLOCK - 1)) / MOE_BLOCK)
    nb8 = jnp.broadcast_to(n_blocks, (SUBLANES, LANES)).astype(BF16)
    blk_start = _dot(nb8, upper)[0:1, :]
    row_start = blk_start * MOE_BLOCK

    def place(i, carry):
        rows = pl.ds(pl.multiple_of(i * ROUTE_CHUNK, ROUTE_CHUNK), ROUTE_CHUNK)
        pos = cum_ref[rows, :] + row_start
        sel = sel_ref[rows, :]
        i1 = sel[:, 0:1]
        i2 = sel[:, 1:2]
        s1 = jnp.sum(jnp.where(lane == i1, pos, 0.0), axis=-1, keepdims=True)
        s2 = jnp.sum(jnp.where(lane == i2, pos, 0.0), axis=-1, keepdims=True)
        slot_ref[0, i] = jnp.sum(jnp.where(diag, s1, 0.0), axis=0, keepdims=True).astype(I32)
        slot_ref[1, i] = jnp.sum(jnp.where(diag, s2, 0.0), axis=0, keepdims=True).astype(I32)
        return carry

    lax.fori_loop(0, n_chunks, place, 0, unroll=ROUTE_UNROLL)

    sub = lax.broadcasted_iota(I32, (SUBLANES, LANES), 0)
    blk_ref[...] = jnp.where(sub == 1, counts, blk_start).astype(I32)


def _route(logits, bias, n_tok):
    body = functools.partial(_route_body, n_tok=n_tok)
    n_chunks = n_tok // ROUTE_CHUNK
    whole = lambda r: pl.BlockSpec((r, LANES), lambda i: (0, 0))
    return pl.pallas_call(
        body,
        grid=(1,),
        in_specs=[whole(n_tok), whole(1)],
        out_specs=[pl.BlockSpec((2, n_chunks, 1, LANES), lambda i: (0, 0, 0, 0)), whole(n_tok), whole(SUBLANES)],
        out_shape=[jax.ShapeDtypeStruct((2, n_chunks, 1, LANES), I32), jax.ShapeDtypeStruct((n_tok, LANES), F32),
                   jax.ShapeDtypeStruct((SUBLANES, LANES), I32)],
        scratch_shapes=[pltpu.VMEM((n_tok, LANES), F32), pltpu.VMEM((n_tok, LANES), F32)],
        compiler_params=_params(1),
        name="route",
    )(logits, bias)


INVERT_UNROLL = 16


CLEAR_SPAN = 8


def _invert_body(slot_ref, blk_ref, tok_ref, *, n_slots):
    n_tok = slot_ref.shape[0] // 2

    def clear_expert(e, c):
        first_pad = blk_ref[0, EXPERT_LANE0 + e] * MOE_BLOCK + blk_ref[1, EXPERT_LANE0 + e]
        end = jnp.where(e == N_EXPERTS - 1, n_slots, blk_ref[0, EXPERT_LANE0 + e + 1] * MOE_BLOCK)
        span_bits = CLEAR_SPAN.bit_length() - 1
        lo = lax.shift_right_logical(first_pad, span_bits) * CLEAR_SPAN

        def span(j, c2):
            for k in range(CLEAR_SPAN):
                tok_ref[lo + j * CLEAR_SPAN + k] = 0
            return c2

        lax.fori_loop(0, lax.shift_right_logical(end - lo, span_bits), span, 0)
        return c

    lax.fori_loop(0, N_EXPERTS, clear_expert, 0)

    def put(t, c):
        tok_ref[slot_ref[t]] = t
        tok_ref[slot_ref[n_tok + t]] = t
        return c

    lax.fori_loop(0, n_tok, put, 0, unroll=INVERT_UNROLL)


def _invert(slots, blk, n_slots):
    smem = pl.BlockSpec(memory_space=pltpu.SMEM)
    return pl.pallas_call(
        functools.partial(_invert_body, n_slots=n_slots),
        in_specs=[smem, smem],
        out_specs=smem,
        out_shape=jax.ShapeDtypeStruct((n_slots,), I32),
        name="invert",
    )(slots, blk)


GATHER_AHEAD = 2
GATHER_SLOTS = GATHER_AHEAD + 1


def _row_gather_start(idx_ref, base, src_hbm, dst, sem, n_rows, priorities):
    for r in range(n_rows):
        tok = idx_ref[base + r]
        pltpu.make_async_copy(src_hbm.at[pl.ds(tok, 1), :], dst.at[pl.ds(r, 1), :], sem).start(
            priority=priorities[r % len(priorities)])


def _experts_body(bstart_ref, tok_ref, h2p_hbm, w1_ref, w3_ref, w2_ref, eo_hbm, xbuf, obuf, w1b, w3b, w2b, gsem,
                  osem, *, n_blocks):
    e = pl.program_id(0)
    n_exp = pl.num_programs(0)
    b0 = bstart_ref[0, EXPERT_LANE0 + e]
    nb = bstart_ref[0, EXPERT_LANE0 + e + 1] - b0
    n_used = bstart_ref[0, EXPERT_LANE0 + n_exp]

    def gather(block):
        s = lax.rem(block, GATHER_SLOTS)
        src_block = jnp.minimum(block, n_used - 1)
        _row_gather_start(tok_ref, src_block * MOE_BLOCK, h2p_hbm, xbuf.at[s], gsem.at[s], MOE_BLOCK, (1, 0))

    def gather_wait(block):
        s = lax.rem(block, GATHER_SLOTS)
        pltpu.make_async_copy(xbuf.at[s], xbuf.at[s], gsem.at[s]).wait()

    def out_copy(block, s):
        rows = pl.ds(pl.multiple_of(block * MOE_BLOCK, MOE_BLOCK), MOE_BLOCK)
        return pltpu.make_async_copy(obuf.at[s], eo_hbm.at[rows, :], osem.at[s])

    @pl.when(e == 0)
    def _():
        for k in range(GATHER_AHEAD):
            gather(k)

    @pl.when(nb > 0)
    def _():
        w1b[...] = w1_ref[...].astype(BF16)
        w3b[...] = w3_ref[...].astype(BF16)
        w2b[...] = w2_ref[...].astype(BF16)

    def block(j, carry):
        b = b0 + j
        s = lax.rem(b, 2)
        gather(b + GATHER_AHEAD)
        gather_wait(b)
        hi, lo = _unpack_bf16_pairs(xbuf[lax.rem(b, GATHER_SLOTS)])
        x = jnp.concatenate([hi.astype(BF16), lo.astype(BF16)], axis=1)
        a = _dot(x, w1b[...])
        g = _dot(x, w3b[...])
        hdn = (a * jax.nn.sigmoid(a) * g).astype(BF16)
        o = _dot(hdn, w2b[...])

        @pl.when(b >= 2)
        def _():
            out_copy(b - 2, s).wait()

        obuf[s] = _pack_bf16_pairs(o[:, :D_HALF], o[:, D_HALF:])
        out_copy(b, s).start()
        return carry

    lax.fori_loop(0, nb, block, 0)

    @pl.when(e == n_exp - 1)
    def _():
        for k in range(GATHER_AHEAD):
            gather_wait(n_used + k)

        @pl.when(n_used >= 2)
        def _():
            out_copy(n_used - 2, lax.rem(n_used, 2)).wait()

        out_copy(n_used - 1, lax.rem(n_used - 1, 2)).wait()
        obuf[0] = jnp.zeros((MOE_BLOCK, D_HALF), U32)

        def fill(tb, carry):
            out_copy(tb, 0).start()
            return carry

        lax.fori_loop(n_used, n_blocks, fill, 0)

        def drain(tb, carry):
            out_copy(tb, 0).wait()
            return carry

        lax.fori_loop(n_used, n_blocks, drain, 0)


def _experts(bstart, tok, h2p, w1, w3, w2, n_blocks):
    def wspec(r, c):
        return pl.BlockSpec((None, r, c), lambda e, bstart_ref, tok_ref: (e, 0, 0))

    grid_spec = pltpu.PrefetchScalarGridSpec(
        num_scalar_prefetch=2,
        grid=(N_EXPERTS,),
        in_specs=[pl.BlockSpec(memory_space=pl.ANY),
                  wspec(D_MODEL, D_EXPERT), wspec(D_MODEL, D_EXPERT), wspec(D_EXPERT, D_MODEL)],
        out_specs=pl.BlockSpec(memory_space=pl.ANY),
        scratch_shapes=[pltpu.VMEM((GATHER_SLOTS, MOE_BLOCK, D_HALF), U32), pltpu.VMEM((2, MOE_BLOCK, D_HALF), U32),
                        pltpu.VMEM((D_MODEL, D_EXPERT), BF16), pltpu.VMEM((D_MODEL, D_EXPERT), BF16),
                        pltpu.VMEM((D_EXPERT, D_MODEL), BF16),
                        pltpu.SemaphoreType.DMA((GATHER_SLOTS,)), pltpu.SemaphoreType.DMA((2,))],
    )
    return pl.pallas_call(
        functools.partial(_experts_body, n_blocks=n_blocks),
        grid_spec=grid_spec,
        out_shape=jax.ShapeDtypeStruct((n_blocks * MOE_BLOCK, D_HALF), U32),
        compiler_params=_params(1),
        name="experts",
    )(bstart, tok, h2p, w1, w3, w2)


def _combine_body(slot_ref, x1_ref, gate_ref, eo_hbm, yp_ref, ys_ref, buf, sem, *, n_tok, n_prompt_tiles):
    i = pl.program_id(0)
    tm = x1_ref.shape[0]
    n_tiles = n_tok // tm
    slot = lax.rem(i, GATHER_SLOTS)

    def start(tile):
        s = lax.rem(tile, GATHER_SLOTS)
        base = jnp.minimum(tile, n_tiles - 1) * tm
        _row_gather_start(slot_ref, base, eo_hbm, buf.at[s, 0], sem.at[s], tm, (0, 1))
        _row_gather_start(slot_ref, n_tok + base, eo_hbm, buf.at[s, 1], sem.at[s], tm, (0, 1))

    def wait(tile):
        s = lax.rem(tile, GATHER_SLOTS)
        pltpu.make_async_copy(buf.at[s], buf.at[s], sem.at[s]).wait()

    @pl.when(i == 0)
    def _():
        for k in range(GATHER_AHEAD):
            start(k)

    start(i + GATHER_AHEAD)
    wait(i)

    @pl.when(i == n_tiles - 1)
    def _():
        for k in range(GATHER_AHEAD):
            wait(n_tiles + k)

    g = gate_ref[...]
    hi0, lo0 = _unpack_bf16_pairs(buf[slot, 0])
    hi1, lo1 = _unpack_bf16_pairs(buf[slot, 1])
    y_hi = x1_ref[:, :D_HALF] + g[:, 0:1] * hi0 + g[:, 1:2] * hi1
    y_lo = x1_ref[:, D_HALF:] + g[:, 0:1] * lo0 + g[:, 1:2] * lo1

    @pl.when(i < n_prompt_tiles)
    def _():
        yp_ref[:, :D_HALF] = y_hi
        yp_ref[:, D_HALF:] = y_lo

    @pl.when(i >= n_prompt_tiles)
    def _():
        ys_ref[:, :D_HALF] = y_hi
        ys_ref[:, D_HALF:] = y_lo


def _combine(slots, x1, gates, eo, n_prompt, n_tok, tm):
    n_tiles = n_tok // tm
    n_prompt_tiles = n_prompt // tm
    body = functools.partial(_combine_body, n_tok=n_tok, n_prompt_tiles=n_prompt_tiles)
    grid_spec = pltpu.PrefetchScalarGridSpec(
        num_scalar_prefetch=1,
        grid=(n_tiles,),
        in_specs=[pl.BlockSpec((tm, D_MODEL), lambda i, s: (i, 0)),
                  pl.BlockSpec((tm, LANES), lambda i, s: (i, 0)),
                  pl.BlockSpec(memory_space=pl.ANY)],
        out_specs=[pl.BlockSpec((tm, D_MODEL), lambda i, s: (jnp.minimum(i, n_prompt_tiles - 1), 0)),
                   pl.BlockSpec((tm, D_MODEL), lambda i, s: (jnp.maximum(i - n_prompt_tiles, 0), 0))],
        scratch_shapes=[pltpu.VMEM((GATHER_SLOTS, 2, tm, D_HALF), U32), pltpu.SemaphoreType.DMA((GATHER_SLOTS,))],
    )
    return pl.pallas_call(
        body,
        grid_spec=grid_spec,
        out_shape=[jax.ShapeDtypeStruct((n_prompt, D_MODEL), F32),
                   jax.ShapeDtypeStruct((n_tok - n_prompt, D_MODEL), F32)],
        compiler_params=_params(1),
        name="combine",
    )(slots, x1, gates, eo)


def _layer(x_prompt, x_sample, state_conv, cache_k, cache_v, norm1_g, w_in, conv_w, conv_b, conv_norm_g,
           conv_norm_b, q_norm_g, k_norm_g, attn_sinks, w_out, norm2_g, w_rg, b_rg, w_re, b_re, w1, w3, w2):
    b, t, _ = x_prompt.shape
    sb, st, _ = x_sample.shape
    n_p, n_s = b * t, sb * st
    n_tok = n_p + n_s

    w_in_bf = w_in.astype(BF16)
    w_out_bf = w_out.astype(BF16)
    g1 = norm1_g.reshape(1, D_MODEL)
    g2 = norm2_g.reshape(1, D_MODEL)
    qg = q_norm_g.reshape(1, HEAD_DIM)
    kg = k_norm_g.reshape(1, HEAD_DIM)
    chunked = lambda a: a.reshape(-1, N_CCHUNK, LANES).transpose(1, 0, 2)
    cw, cb, lg, lb = chunked(conv_w), chunked(conv_b), chunked(conv_norm_g), chunked(conv_norm_b)
    pad_lanes = LANES - N_EXPERT_GROUPS - N_EXPERTS
    w_router_bf = jnp.concatenate([w_rg, w_re, jnp.zeros((D_MODEL, pad_lanes), F32)], axis=1).astype(BF16)
    b_router = jnp.concatenate([b_rg, b_re, jnp.zeros((pad_lanes,), F32)]).reshape(1, LANES)

    xp2 = x_prompt.reshape(n_p, D_MODEL)
    xs2 = x_sample.reshape(n_s, D_MODEL)

    c_p, q_p, k_p, v_p, conv_p, knew_p, vnew_p = _in_proj_conv(xp2, g1, w_in_bf, qg, kg, cw, cb, lg, lb, 512, b)
    u_s, q_s, k_s, v_s = _in_proj(xs2, g1, w_in_bf, qg, kg, n_s, F32)

    r3 = lambda a, bb: a.reshape(bb, -1, a.shape[-1])
    a_p = _mixer_prompt(attn_sinks, r3(q_p, b), r3(k_p, b), r3(v_p, b), 256)
    cat_s, conv_s, knew_s, vnew_s = _mixer_sample(
        attn_sinks, r3(u_s, sb), state_conv, r3(q_s, sb), r3(k_s, sb), r3(v_s, sb), cache_k, cache_v, cw, cb, lg, lb)

    x1, h2p, logits = _out_proj(c_p, a_p.reshape(n_p, D_ATTN), cat_s.reshape(n_s, D_MODEL), xp2, xs2, w_out_bf, g2,
                                w_router_bf, OUT_TILE)

    n_blocks = -(-(n_tok * 2) // MOE_BLOCK) + N_EXPERTS
    slots, gates, blk = _route(logits, b_router, n_tok)
    slots = slots.reshape(2 * n_tok)
    tok = _invert(slots, blk, n_blocks * MOE_BLOCK)
    eo = _experts(blk, tok, h2p, w1, w3, w2, n_blocks)
    y_p, y_s = _combine(slots, x1, gates, eo, n_p, n_tok, COMBINE_TILE)

    return (y_p.reshape(b, t, D_MODEL), y_s.reshape(sb, st, D_MODEL), conv_p[None], knew_p[None], vnew_p[None],
            conv_s, knew_s, vnew_s)


def kernel(x_prompt, x_sample, state_conv, cache_k, cache_v, norm1_g, w_in, conv_w, conv_b, conv_norm_g, conv_norm_b, q_norm_g, k_norm_g, attn_sinks, w_out, norm2_g, w_router_group, b_router_group, w_router_expert, b_router_expert, w1, w3, w2):
    depth = w_in.shape[0]
    assert depth == 1, "single-layer step"
    return _layer(x_prompt, x_sample, state_conv, cache_k, cache_v, norm1_g[0], w_in[0], conv_w[0],
                  conv_b[0], conv_norm_g[0], conv_norm_b[0], q_norm_g[0], k_norm_g[0], attn_sinks[0], w_out[0],
                  norm2_g[0], w_router_group[0], b_router_group[0], w_router_expert[0], b_router_expert[0],
                  w1[0], w3[0], w2[0])
```

```python
import functools
import math

import jax
import jax.numpy as jnp
from jax import lax
from jax.experimental import pallas as pl
from jax.experimental.pallas import tpu as pltpu

F32 = jnp.float32
BF16 = jnp.bfloat16
I32 = jnp.int32

D_MODEL = 2048
D_CONV = 1024
CONV_WIDTH = 31
CONV_HIST = CONV_WIDTH - 1
D_ATTN = 1024
HEAD_DIM = 128
N_HEADS = 8
N_KV_HEADS = 2
GQA_GROUP = N_HEADS // N_KV_HEADS
D_KV = N_KV_HEADS * HEAD_DIM
WINDOW = 128
BLOCK_Q = 128
SCALE = 1.0 / math.sqrt(HEAD_DIM)
N_EXPERT_GROUPS = 4
EXPERTS_PER_GROUP = 8
N_EXPERTS = N_EXPERT_GROUPS * EXPERTS_PER_GROUP
D_EXPERT = 512
MOE_BLOCK = 256
COMBINE_TILE = 128
D_IN = 2 * D_CONV + D_ATTN + 2 * D_KV
EPS = 1e-6
PAST_LEN = 16384

LANES = 128
SUBLANES = 8
MXU_COLS = 256
VMEM_LIMIT_BYTES = 56 * 1024 * 1024
NEG_INF = float("-inf")
EXPERT_LANE0 = N_EXPERT_GROUPS


def _params(n_axes):
    return pltpu.CompilerParams(dimension_semantics=("arbitrary",) * n_axes,
                                vmem_limit_bytes=VMEM_LIMIT_BYTES)


def _resident(shape):
    nd = len(shape)
    return pl.BlockSpec(shape, lambda *_: (0,) * nd, pipeline_mode=pl.Buffered(1))


def _dot(a, b):
    return jnp.dot(a, b, preferred_element_type=F32)


def _in_proj_body(x_ref, g1_ref, w_ref, qg_ref, kg_ref, u_ref, q_ref, k_ref, v_ref, n_ref):
    x = x_ref[...]
    ms = jnp.mean(x * x, axis=-1, keepdims=True)
    n_ref[...] = (x * lax.rsqrt(ms + EPS) * g1_ref[...]).astype(BF16)

    def head_norm(h, g):
        return h * lax.rsqrt(jnp.mean(h * h, axis=-1, keepdims=True) + EPS) * g

    ch = MXU_COLS
    for c in range(D_CONV // ch):
        a = _dot(n_ref[...], w_ref[:, c * ch:(c + 1) * ch])
        g = _dot(n_ref[...], w_ref[:, D_CONV + c * ch:D_CONV + (c + 1) * ch])
        u_ref[:, c * ch:(c + 1) * ch] = a * jax.nn.sigmoid(g)
    q_off = 2 * D_CONV
    for c in range(D_ATTN // ch):
        qq = _dot(n_ref[...], w_ref[:, q_off + c * ch:q_off + (c + 1) * ch])
        for j in range(ch // HEAD_DIM):
            qh = head_norm(qq[:, j * HEAD_DIM:(j + 1) * HEAD_DIM], qg_ref[...])
            q_ref[:, c * ch + j * HEAD_DIM:c * ch + (j + 1) * HEAD_DIM] = qh.astype(q_ref.dtype)
    k_off = q_off + D_ATTN
    kk = _dot(n_ref[...], w_ref[:, k_off:k_off + D_KV])
    for j in range(N_KV_HEADS):
        k_ref[:, j * HEAD_DIM:(j + 1) * HEAD_DIM] = head_norm(kk[:, j * HEAD_DIM:(j + 1) * HEAD_DIM], kg_ref[...])
    v_ref[...] = _dot(n_ref[...], w_ref[:, k_off + D_KV:k_off + 2 * D_KV])


def _in_proj_conv_body(x_ref, g1_ref, w_ref, qg_ref, kg_ref, cw_ref, cb_ref, lg_ref, lb_ref, c_ref, q_ref, k_ref,
                       v_ref, ut_ref, kt_ref, vt_ref, n_ref, ue_ref, conv_ref, *, tiles_per_seq):
    tm = x_ref.shape[0]
    first = lax.rem(pl.program_id(0), tiles_per_seq) == 0

    @pl.when(first)
    def _():
        for c in range(N_CCHUNK):
            ue_ref[c, 0:CONV_HALO, :] = jnp.zeros((CONV_HALO, LANES), F32)

    @pl.when(jnp.logical_not(first))
    def _():
        for c in range(N_CCHUNK):
            ue_ref[c, 0:CONV_HALO, :] = ue_ref[c, tm:tm + CONV_HALO, :]

    x = x_ref[...]
    ms = jnp.mean(x * x, axis=-1, keepdims=True)
    n_ref[...] = (x * lax.rsqrt(ms + EPS) * g1_ref[...]).astype(BF16)

    def head_norm(h, g):
        return h * lax.rsqrt(jnp.mean(h * h, axis=-1, keepdims=True) + EPS) * g

    ch = MXU_COLS
    lanes_per = ch // LANES
    n_glu = D_CONV // ch
    q_off = 2 * D_CONV

    row_blocks = [slice(r, r + DOT_ROWS) for r in range(0, tm, DOT_ROWS)]

    for c in range(n_glu):
        for rows in row_blocks:
            a = _dot(n_ref[rows, :], w_ref[:, c * ch:(c + 1) * ch])
            g = _dot(n_ref[rows, :], w_ref[:, D_CONV + c * ch:D_CONV + (c + 1) * ch])
            u = a * jax.nn.sigmoid(g)
            for half in range(lanes_per):
                ue_ref[c * lanes_per + half, CONV_HALO + rows.start:CONV_HALO + rows.stop, :] = (
                    u[:, half * LANES:(half + 1) * LANES])
        for half in range(lanes_per):
            lc = c * lanes_per + half
            _conv_chunk(ue_ref, lc, cw_ref[lc], cb_ref[lc], conv_ref, tm)
    _ln_swish(conv_ref, lg_ref, lb_ref, c_ref, tm)

    for c in range(D_ATTN // ch):
        for rows in row_blocks:
            qq = _dot(n_ref[rows, :], w_ref[:, q_off + c * ch:q_off + (c + 1) * ch])
            for j in range(ch // HEAD_DIM):
                qh = head_norm(qq[:, j * HEAD_DIM:(j + 1) * HEAD_DIM], qg_ref[...])
                q_ref[rows, c * ch + j * HEAD_DIM:c * ch + (j + 1) * HEAD_DIM] = qh.astype(q_ref.dtype)
    k_off = q_off + D_ATTN
    for rows in row_blocks:
        kk = _dot(n_ref[rows, :], w_ref[:, k_off:k_off + D_KV])
        vv = _dot(n_ref[rows, :], w_ref[:, k_off + D_KV:k_off + 2 * D_KV])
        v_ref[rows, :] = vv
        for j in range(N_KV_HEADS):
            hs = slice(j * HEAD_DIM, (j + 1) * HEAD_DIM)
            k_ref[rows, hs] = head_norm(kk[:, hs], kg_ref[...])
    for j in range(N_KV_HEADS):
        hs = slice(j * HEAD_DIM, (j + 1) * HEAD_DIM)
        kt_ref[:, j, :] = k_ref[tm - WINDOW:tm, hs]
        vt_ref[:, j, :] = v_ref[tm - WINDOW:tm, hs]
    for c in range(N_CCHUNK):
        ut_ref[:, c * LANES:(c + 1) * LANES] = ue_ref[c, CONV_HALO + tm - CONV_HIST:CONV_HALO + tm, :]


def _in_proj_conv(x2, g1, w_in_bf, qg, kg, cw, cb, lg, lb, tm, n_seq):
    n = x2.shape[0]
    tiles_per_seq = n // n_seq // tm
    row = lambda w: pl.BlockSpec((tm, w), lambda i: (i, 0))
    seq = lambda *dims: pl.BlockSpec((None,) + dims, lambda i: (i // tiles_per_seq,) + (0,) * len(dims))
    body = functools.partial(_in_proj_conv_body, tiles_per_seq=tiles_per_seq)
    return pl.pallas_call(
        body,
        grid=(n // tm,),
        in_specs=[row(D_MODEL), _resident((1, D_MODEL)), _resident((D_MODEL, D_IN)),
                  _resident((1, HEAD_DIM)), _resident((1, HEAD_DIM)),
                  _resident((N_CCHUNK, CONV_WIDTH, LANES)), _resident((N_CCHUNK, 1, LANES)),
                  _resident((N_CCHUNK, 1, LANES)), _resident((N_CCHUNK, 1, LANES))],
        out_specs=[row(D_CONV), row(D_ATTN), row(D_KV), row(D_KV),
                   seq(CONV_HIST, D_CONV), seq(WINDOW, N_KV_HEADS, HEAD_DIM), seq(WINDOW, N_KV_HEADS, HEAD_DIM)],
        out_shape=[jax.ShapeDtypeStruct((n, D_CONV), BF16), jax.ShapeDtypeStruct((n, D_ATTN), BF16),
                   jax.ShapeDtypeStruct((n, D_KV), F32), jax.ShapeDtypeStruct((n, D_KV), F32),
                   jax.ShapeDtypeStruct((n_seq, CONV_HIST, D_CONV), F32),
                   jax.ShapeDtypeStruct((n_seq, WINDOW, N_KV_HEADS, HEAD_DIM), F32),
                   jax.ShapeDtypeStruct((n_seq, WINDOW, N_KV_HEADS, HEAD_DIM), F32)],
        scratch_shapes=[pltpu.VMEM((tm, D_MODEL), BF16),
                        pltpu.VMEM((N_CCHUNK, CONV_HALO + tm, LANES), F32),
                        pltpu.VMEM((N_CCHUNK, tm, LANES), F32)],
        compiler_params=_params(1),
        name="in_proj_conv",
    )(x2, g1, w_in_bf, qg, kg, cw, cb, lg, lb)


def _in_proj(x2, g1, w_in_bf, qg, kg, tm, q_dtype):
    n = x2.shape[0]
    row = lambda w: pl.BlockSpec((tm, w), lambda i: (i, 0))
    return pl.pallas_call(
        _in_proj_body,
        grid=(n // tm,),
        in_specs=[row(D_MODEL), _resident((1, D_MODEL)), _resident((D_MODEL, D_IN)),
                  _resident((1, HEAD_DIM)), _resident((1, HEAD_DIM))],
        out_specs=[row(D_CONV), row(D_ATTN), row(D_KV), row(D_KV)],
        out_shape=[jax.ShapeDtypeStruct((n, D_CONV), F32), jax.ShapeDtypeStruct((n, D_ATTN), q_dtype),
                   jax.ShapeDtypeStruct((n, D_KV), F32), jax.ShapeDtypeStruct((n, D_KV), F32)],
        scratch_shapes=[pltpu.VMEM((tm, D_MODEL), BF16)],
        compiler_params=_params(1),
        name="in_proj",
    )(x2, g1, w_in_bf, qg, kg)


N_CCHUNK = D_CONV // LANES
CONV_ROWS = 64


CONV_HALO = 32
DOT_ROWS = 256


def _conv_chunk(ue_ref, c, wc, bias, conv_ref, rows):
    base = CONV_HALO - CONV_HIST
    for r0 in range(0, rows, CONV_ROWS):
        acc = jnp.broadcast_to(bias, (CONV_ROWS, LANES))
        for tap in range(CONV_WIDTH):
            acc = acc + wc[tap:tap + 1, :] * ue_ref[c, base + r0 + tap:base + r0 + tap + CONV_ROWS, :]
        conv_ref[c, r0:r0 + CONV_ROWS, :] = acc


def _ln_swish(conv_ref, lg_ref, lb_ref, cat_ref, rows):
    tot = jnp.zeros((rows, 1), F32)
    for c in range(N_CCHUNK):
        tot = tot + jnp.sum(conv_ref[c], axis=-1, keepdims=True)
    mean = tot / D_CONV
    var = jnp.zeros((rows, 1), F32)
    for c in range(N_CCHUNK):
        xc = conv_ref[c] - mean
        var = var + jnp.sum(xc * xc, axis=-1, keepdims=True)
    rstd = lax.rsqrt(var / D_CONV + EPS)
    for c in range(N_CCHUNK):
        y = (conv_ref[c] - mean) * rstd * lg_ref[c] + lb_ref[c]
        cat_ref[:, c * LANES:(c + 1) * LANES] = (y * jax.nn.sigmoid(y)).astype(cat_ref.dtype)


def _sink_softmax_rows(s, sink):
    m = jnp.maximum(jnp.max(s, axis=-1, keepdims=True), sink)
    p = jnp.exp(s - m)
    return p / (jnp.sum(p, axis=-1, keepdims=True) + jnp.exp(sink - m))


def _alibi_slope(head):
    return 2.0 ** (-8.0 * (head + 1) / N_HEADS)


def _mixer_prompt_body(sink_ref, q_ref, k_ref, kh_ref, v_ref, vh_ref, cat_ref, *, tm):
    j = pl.program_id(1)
    has_prev = j > 0
    qi = lax.broadcasted_iota(I32, (BLOCK_Q, 2 * BLOCK_Q), 0)
    kj = lax.broadcasted_iota(I32, (BLOCK_Q, 2 * BLOCK_Q), 1)
    dist = qi + BLOCK_Q - kj
    distf = dist.astype(F32)
    band = jnp.where(dist >= 0, jnp.where(dist < WINDOW, 0.0, NEG_INF), NEG_INF)
    band_first = jnp.where(kj >= BLOCK_Q, band, jnp.where(has_prev, band, NEG_INF))

    for qb in range(tm // BLOCK_Q):
        rows = slice(qb * BLOCK_Q, (qb + 1) * BLOCK_Q)
        prev = slice((qb - 1) * BLOCK_Q, qb * BLOCK_Q)
        mask = band_first if qb == 0 else band
        for kv in range(N_KV_HEADS):
            hs = slice(kv * HEAD_DIM, (kv + 1) * HEAD_DIM)
            k_prev = kh_ref[:, hs] if qb == 0 else k_ref[prev, hs]
            v_prev = vh_ref[:, hs] if qb == 0 else v_ref[prev, hs]
            kk = jnp.concatenate([k_prev, k_ref[rows, hs]], axis=0).astype(BF16)
            vv = jnp.concatenate([v_prev, v_ref[rows, hs]], axis=0).astype(BF16)
            heads = [kv * GQA_GROUP + g for g in range(GQA_GROUP)]
            qs = jnp.concatenate([q_ref[rows, h * HEAD_DIM:(h + 1) * HEAD_DIM] for h in heads], axis=0)
            s = lax.dot_general(qs, kk, (((1,), (1,)), ((), ())), preferred_element_type=F32)
            ps = []
            for g, h in enumerate(heads):
                sg = s[g * BLOCK_Q:(g + 1) * BLOCK_Q] * SCALE - _alibi_slope(h) * distf + mask
                ps.append(_sink_softmax_rows(sg, sink_ref[h]).astype(BF16))
            o = _dot(jnp.concatenate(ps, axis=0), vv)
            for g, h in enumerate(heads):
                cat_ref[rows, h * HEAD_DIM:(h + 1) * HEAD_DIM] = o[g * BLOCK_Q:(g + 1) * BLOCK_Q].astype(cat_ref.dtype)


def _mixer_prompt(sinks, q, k, v, tm):
    b, t, _ = q.shape
    kpb = tm // BLOCK_Q
    main = lambda w: pl.BlockSpec((None, tm, w), lambda bi, j: (bi, j, 0))
    prev_block = pl.BlockSpec((None, BLOCK_Q, D_KV), lambda bi, j: (bi, jnp.maximum(j * kpb - 1, 0), 0))
    body = functools.partial(_mixer_prompt_body, tm=tm)
    return pl.pallas_call(
        body,
        grid=(b, t // tm),
        in_specs=[pl.BlockSpec(memory_space=pltpu.SMEM), main(D_ATTN), main(D_KV), prev_block, main(D_KV),
                  prev_block],
        out_specs=main(D_ATTN),
        out_shape=jax.ShapeDtypeStruct((b, t, D_ATTN), BF16),
        compiler_params=_params(2),
        name="mixer_prompt",
    )(sinks, q, k, k, v, v)


KEY_PAD = 8


SAMPLE_SEQS = 8


def _sample_conv_ln_swish(st_ref, u_ref, cw_ref, cb_ref, lg_ref, lb_ref, cat_ref, t_new):
    n_seq, hist, _ = st_ref.shape
    conv = [[None] * N_CCHUNK for _ in range(t_new)]
    for c in range(N_CCHUNK):
        cs = slice(c * LANES, (c + 1) * LANES)
        wc = cw_ref[c]
        pos = [st_ref[:, p, cs] for p in range(hist)] + [u_ref[:, t, cs] for t in range(t_new)]
        for t in range(t_new):
            acc = jnp.broadcast_to(cb_ref[c], (n_seq, LANES))
            for tap in range(CONV_WIDTH):
                acc = acc + wc[tap:tap + 1, :] * pos[t + tap]
            conv[t][c] = acc
    for t in range(t_new):
        tot = jnp.zeros((n_seq, 1), F32)
        for a in conv[t]:
            tot = tot + jnp.sum(a, axis=-1, keepdims=True)
        mean = tot / D_CONV
        var = jnp.zeros((n_seq, 1), F32)
        for a in conv[t]:
            var = var + jnp.sum((a - mean) * (a - mean), axis=-1, keepdims=True)
        rstd = lax.rsqrt(var / D_CONV + EPS)
        for c, a in enumerate(conv[t]):
            y = (a - mean) * rstd * lg_ref[c] + lb_ref[c]
            cat_ref[:, t, c * LANES:(c + 1) * LANES] = (y * jax.nn.sigmoid(y)).astype(cat_ref.dtype)


def _mixer_sample_body(sink_ref, u_ref, st_ref, q_ref, k_ref, v_ref, ck_ref, cv_ref, cw_ref, cb_ref, lg_ref,
                       lb_ref, cat_ref, nst_ref, nk_ref, nv_ref, kk_ref, vv_ref, qs_ref, *, t_new):
    n_seq, hist, _ = st_ref.shape
    w_past = ck_ref.shape[1]
    n_keys = w_past + KEY_PAD
    q_rows = GQA_GROUP * t_new
    n_rows, n_cols = n_seq * q_rows, n_seq * n_keys

    _sample_conv_ln_swish(st_ref, u_ref, cw_ref, cb_ref, lg_ref, lb_ref, cat_ref, t_new)

    for i in range(n_seq):
        nst_ref[i, 0:hist - t_new, :] = st_ref[i, t_new:hist, :]
        nst_ref[i, hist - t_new:hist, :] = u_ref[i]

        nk_ref[i, 0:w_past - t_new] = ck_ref[i, t_new:w_past]
        nv_ref[i, 0:w_past - t_new] = cv_ref[i, t_new:w_past]
        kk_ref[i, w_past:n_keys, :] = jnp.zeros((KEY_PAD, D_KV), F32)
        vv_ref[i, w_past:n_keys, :] = jnp.zeros((KEY_PAD, D_KV), F32)
        kk_ref[i, w_past:w_past + t_new, :] = k_ref[i]
        vv_ref[i, w_past:w_past + t_new, :] = v_ref[i]
        for h in range(N_KV_HEADS):
            hs = slice(h * HEAD_DIM, (h + 1) * HEAD_DIM)
            kk_ref[i, 0:w_past, hs] = ck_ref[i, :, h, :]
            vv_ref[i, 0:w_past, hs] = cv_ref[i, :, h, :]
            nk_ref[i, w_past - t_new:w_past, h, :] = k_ref[i, :, hs]
            nv_ref[i, w_past - t_new:w_past, h, :] = v_ref[i, :, hs]

    row = lax.broadcasted_iota(I32, (n_rows, n_cols), 0)
    col = lax.broadcasted_iota(I32, (n_rows, n_cols), 1)
    tok_bits, row_bits = t_new.bit_length() - 1, q_rows.bit_length() - 1
    assert (1 << tok_bits, 1 << row_bits) == (t_new, q_rows), "token and row counts must be powers of two"
    tok = row & (t_new - 1)
    key = col - (row >> row_bits) * n_keys
    dist = tok + w_past - key
    distf = dist.astype(F32)
    mask = jnp.where(dist >= 0, jnp.where(dist < WINDOW, 0.0, NEG_INF), NEG_INF)
    row1 = lax.broadcasted_iota(I32, (n_rows, 1), 0)
    grp = (row1 >> tok_bits) & (GQA_GROUP - 1)

    for kv in range(N_KV_HEADS):
        hs = slice(kv * HEAD_DIM, (kv + 1) * HEAD_DIM)
        slope = jnp.zeros((n_rows, 1), F32)
        sink = jnp.zeros((n_rows, 1), F32)
        for g in range(GQA_GROUP):
            h = kv * GQA_GROUP + g
            slope = jnp.where(grp == g, _alibi_slope(h), slope)
            sink = jnp.where(grp == g, sink_ref[h], sink)
            for i in range(n_seq):
                r0 = i * q_rows + g * t_new
                qs_ref[r0:r0 + t_new, :] = q_ref[i, :, h * HEAD_DIM:(h + 1) * HEAD_DIM]
        kk = kk_ref[:, :, hs].reshape(n_cols, HEAD_DIM).astype(BF16)
        vv = vv_ref[:, :, hs].reshape(n_cols, HEAD_DIM).astype(BF16)
        s = lax.dot_general(qs_ref[...].astype(BF16), kk, (((1,), (1,)), ((), ())), preferred_element_type=F32)
        sg = s * SCALE - slope * distf + mask
        o = _dot(_sink_softmax_rows(sg, sink).astype(BF16), vv)
        for g in range(GQA_GROUP):
            h = kv * GQA_GROUP + g
            for i in range(n_seq):
                r0 = i * q_rows + g * t_new
                cat_ref[i, :, D_CONV + h * HEAD_DIM:D_CONV + (h + 1) * HEAD_DIM] = (
                    o[r0:r0 + t_new].astype(cat_ref.dtype))


def _mixer_sample(sinks, u, state, q, k, v, ck, cv, cw, cb, lg, lb):
    b, t_new, _ = u.shape
    hist = state.shape[2]
    w_past = ck.shape[2]
    n_seq = SAMPLE_SEQS
    per = lambda r, w: pl.BlockSpec((n_seq, r, w), lambda bi: (bi, 0, 0))
    layer = lambda *dims: pl.BlockSpec((None, n_seq) + dims, lambda bi: (0, bi) + (0,) * len(dims))
    cache = layer(w_past, N_KV_HEADS, HEAD_DIM)
    body = functools.partial(_mixer_sample_body, t_new=t_new)
    return pl.pallas_call(
        body,
        grid=(b // n_seq,),
        in_specs=[pl.BlockSpec(memory_space=pltpu.SMEM),
                  per(t_new, D_CONV), layer(hist, D_CONV), per(t_new, D_ATTN), per(t_new, D_KV), per(t_new, D_KV),
                  cache, cache,
                  _resident((N_CCHUNK, CONV_WIDTH, LANES)), _resident((N_CCHUNK, 1, LANES)),
                  _resident((N_CCHUNK, 1, LANES)), _resident((N_CCHUNK, 1, LANES))],
        out_specs=[per(t_new, D_MODEL), layer(hist, D_CONV), cache, cache],
        out_shape=[jax.ShapeDtypeStruct((b, t_new, D_MODEL), F32),
                   jax.ShapeDtypeStruct(state.shape, F32),
                   jax.ShapeDtypeStruct(ck.shape, F32),
                   jax.ShapeDtypeStruct(cv.shape, F32)],
        scratch_shapes=[pltpu.VMEM((n_seq, w_past + KEY_PAD, D_KV), F32),
                        pltpu.VMEM((n_seq, w_past + KEY_PAD, D_KV), F32),
                        pltpu.VMEM((n_seq * GQA_GROUP * t_new, HEAD_DIM), F32)],
        compiler_params=_params(1),
        name="mixer_sample",
    )(sinks, u, state, q, k, v, ck, cv, cw, cb, lg, lb)


OUT_CHUNK = 512
OUT_TILE = 256
D_HALF = D_MODEL // 2
U32 = jnp.uint32


def _pack_bf16_pairs(hi, lo):
    hi_bits = lax.bitcast_convert_type(hi.astype(BF16).astype(F32), U32)
    lo_bits = lax.bitcast_convert_type(lo.astype(BF16).astype(F32), U32)
    return hi_bits | (lo_bits >> 16)


def _unpack_bf16_pairs(words):
    hi = lax.bitcast_convert_type(words & U32(0xFFFF0000), F32)
    lo = lax.bitcast_convert_type(words << 16, F32)
    return hi, lo


def _out_proj_body(catc_ref, cata_ref, cats_ref, xp_ref, xs_ref, wo_ref, g2_ref, wr_ref, x1_ref, h2p_ref, lg_ref,
                   h_ref, *, n_prompt_tiles):
    is_prompt = pl.program_id(0) < n_prompt_tiles
    tm = catc_ref.shape[0]

    def sample_rows(v):
        return jnp.concatenate([v, jnp.zeros((tm - v.shape[0], v.shape[1]), v.dtype)], axis=0)

    cat = jnp.where(is_prompt, jnp.concatenate([catc_ref[...], cata_ref[...]], axis=1),
                    sample_rows(cats_ref[...].astype(BF16)))
    ss = jnp.zeros((tm, 1), F32)
    for c in range(D_MODEL // OUT_CHUNK):
        cs = slice(c * OUT_CHUNK, (c + 1) * OUT_CHUNK)
        y = jnp.where(is_prompt, xp_ref[:, cs], sample_rows(xs_ref[:, cs])) + _dot(cat, wo_ref[:, cs])
        x1_ref[:, cs] = y
        ss = ss + jnp.sum(y * y, axis=-1, keepdims=True)
    r = lax.rsqrt(ss / D_MODEL + EPS)
    for c in range(D_MODEL // OUT_CHUNK):
        cs = slice(c * OUT_CHUNK, (c + 1) * OUT_CHUNK)
        h_ref[:, cs] = x1_ref[:, cs] * r * g2_ref[:, cs]
    for c in range(D_HALF // OUT_CHUNK):
        cs = slice(c * OUT_CHUNK, (c + 1) * OUT_CHUNK)
        cs_lo = slice(D_HALF + c * OUT_CHUNK, D_HALF + (c + 1) * OUT_CHUNK)
        h2p_ref[:, cs] = _pack_bf16_pairs(h_ref[:, cs], h_ref[:, cs_lo])
    lg_ref[...] = _dot(h_ref[...].astype(BF16), wr_ref[...])


def _out_proj(cat_conv, cat_attn, cat_s, xp2, xs2, wo_bf, g2, wr_bf, tm):
    n_prompt_tiles = cat_conv.shape[0] // tm
    n_rows = (n_prompt_tiles + 1) * tm
    n_s = cat_s.shape[0]
    prompt = lambda w: pl.BlockSpec((tm, w), lambda i: (jnp.minimum(i, n_prompt_tiles - 1), 0))
    out_row = lambda w: pl.BlockSpec((tm, w), lambda i: (i, 0))
    body = functools.partial(_out_proj_body, n_prompt_tiles=n_prompt_tiles)
    return pl.pallas_call(
        body,
        grid=(n_prompt_tiles + 1,),
        in_specs=[prompt(D_CONV), prompt(D_ATTN), _resident((n_s, D_MODEL)), prompt(D_MODEL),
                  _resident((n_s, D_MODEL)),
                  _resident((D_MODEL, D_MODEL)), _resident((1, D_MODEL)), _resident((D_MODEL, LANES))],
        out_specs=[out_row(D_MODEL), out_row(D_HALF), out_row(LANES)],
        out_shape=[jax.ShapeDtypeStruct((n_rows, D_MODEL), F32), jax.ShapeDtypeStruct((n_rows, D_HALF), U32),
                   jax.ShapeDtypeStruct((n_rows, LANES), F32)],
        scratch_shapes=[pltpu.VMEM((tm, D_MODEL), F32)],
        compiler_params=_params(1),
        name="out_proj",
    )(cat_conv, cat_attn, cat_s, xp2, xs2, wo_bf, g2, wr_bf)


ROUTE_CHUNK = 128
ROUTE_UNROLL = 5


def _route_body(lg_ref, bias_ref, slot_ref, gate_ref, blk_ref, cum_ref, sel_ref, *, n_tok):
    n_chunks = n_tok // ROUTE_CHUNK
    lane = lax.broadcasted_iota(I32, (ROUTE_CHUNK, LANES), 1).astype(F32)
    ri = lax.broadcasted_iota(I32, (ROUTE_CHUNK, ROUTE_CHUNK), 0)
    ci = lax.broadcasted_iota(I32, (ROUTE_CHUNK, ROUTE_CHUNK), 1)
    lower = jnp.where(ci < ri, 1.0, 0.0).astype(BF16)
    upper = jnp.where(ri < ci, 1.0, 0.0).astype(BF16)
    diag = ri == ci
    is_group = lane < N_EXPERT_GROUPS

    def first_max(vals):
        m = jnp.max(vals, axis=-1, keepdims=True)
        idx = jnp.min(jnp.where(vals == m, lane, float(LANES)), axis=-1, keepdims=True)
        return m, idx

    def assign(i, carry):
        rows = pl.ds(pl.multiple_of(i * ROUTE_CHUNK, ROUTE_CHUNK), ROUTE_CHUNK)
        l = lg_ref[rows, :] + bias_ref[...]
        gl = jnp.where(is_group, l, NEG_INF)
        g_max, g_idx = first_max(gl)
        g_top = 1.0 / jnp.sum(jnp.exp(gl - g_max), axis=-1, keepdims=True)
        lo = EXPERT_LANE0 + g_idx * EXPERTS_PER_GROUP
        el = jnp.where(lane >= lo, jnp.where(lane < lo + EXPERTS_PER_GROUP, l, NEG_INF), NEG_INF)
        m1, i1 = first_max(el)
        p = jnp.exp(el - m1)
        probs = p / jnp.sum(p, axis=-1, keepdims=True)
        e1 = jnp.sum(jnp.where(lane == i1, probs, 0.0), axis=-1, keepdims=True)
        _, i2 = first_max(jnp.where(lane == i1, NEG_INF, el))
        e2 = jnp.sum(jnp.where(lane == i2, probs, 0.0), axis=-1, keepdims=True)
        gate1 = g_top * e1 / (e1 + e2)
        gate2 = g_top * e2 / (e1 + e2)
        gate_ref[rows, :] = jnp.where(lane == 0, gate1, jnp.where(lane == 1, gate2, 0.0))
        sel_ref[rows, :] = jnp.where(lane == 0, i1, jnp.where(lane == 1, i2, 0.0))
        onehot = jnp.where(lane == i1, 1.0, jnp.where(lane == i2, 1.0, 0.0))
        before = _dot(lower, onehot.astype(BF16)) + carry
        cum_ref[rows, :] = before
        return carry + jnp.sum(onehot, axis=0, keepdims=True)

    counts = lax.fori_loop(0, n_chunks, assign, jnp.zeros((1, LANES), F32), unroll=ROUTE_UNROLL)
    n_blocks = jnp.floor((counts + (MOE_BLOCK - 1)) / MOE_BLOCK)
    nb8 = jnp.broadcast_to(n_blocks, (SUBLANES, LANES)).astype(BF16)
    blk_start = _dot(nb8, upper)[0:1, :]
    row_start = blk_start * MOE_BLOCK

    def place(i, carry):
        rows = pl.ds(pl.multiple_of(i * ROUTE_CHUNK, ROUTE_CHUNK), ROUTE_CHUNK)
        pos = cum_ref[rows, :] + row_start
        sel = sel_ref[rows, :]
        i1 = sel[:, 0:1]
        i2 = sel[:, 1:2]
        s1 = jnp.sum(jnp.where(lane == i1, pos, 0.0), axis=-1, keepdims=True)
        s2 = jnp.sum(jnp.where(lane == i2, pos, 0.0), axis=-1, keepdims=True)
        slot_ref[0, i] = jnp.sum(jnp.where(diag, s1, 0.0), axis=0, keepdims=True).astype(I32)
        slot_ref[1, i] = jnp.sum(jnp.where(diag, s2, 0.0), axis=0, keepdims=True).astype(I32)
        return carry

    lax.fori_loop(0, n_chunks, place, 0, unroll=ROUTE_UNROLL)

    sub = lax.broadcasted_iota(I32, (SUBLANES, LANES), 0)
    blk_ref[...] = jnp.where(sub == 1, counts, blk_start).astype(I32)


def _route(logits, bias, n_tok):
    body = functools.partial(_route_body, n_tok=n_tok)
    n_chunks = n_tok // ROUTE_CHUNK
    whole = lambda r: pl.BlockSpec((r, LANES), lambda i: (0, 0))
    return pl.pallas_call(
        body,
        grid=(1,),
        in_specs=[whole(n_tok), whole(1)],
        out_specs=[pl.BlockSpec((2, n_chunks, 1, LANES), lambda i: (0, 0, 0, 0)), whole(n_tok), whole(SUBLANES)],
        out_shape=[jax.ShapeDtypeStruct((2, n_chunks, 1, LANES), I32), jax.ShapeDtypeStruct((n_tok, LANES), F32),
                   jax.ShapeDtypeStruct((SUBLANES, LANES), I32)],
        scratch_shapes=[pltpu.VMEM((n_tok, LANES), F32), pltpu.VMEM((n_tok, LANES), F32)],
        compiler_params=_params(1),
        name="route",
    )(logits, bias)


INVERT_UNROLL = 16


CLEAR_SPAN = 8


def _invert_body(slot_ref, blk_ref, tok_ref, *, n_slots):
    n_tok = slot_ref.shape[0] // 2

    def clear_expert(e, c):
        first_pad = blk_ref[0, EXPERT_LANE0 + e] * MOE_BLOCK + blk_ref[1, EXPERT_LANE0 + e]
        end = jnp.where(e == N_EXPERTS - 1, n_slots, blk_ref[0, EXPERT_LANE0 + e + 1] * MOE_BLOCK)
        span_bits = CLEAR_SPAN.bit_length() - 1
        lo = lax.shift_right_logical(first_pad, span_bits) * CLEAR_SPAN

        def span(j, c2):
            for k in range(CLEAR_SPAN):
                tok_ref[lo + j * CLEAR_SPAN + k] = 0
            return c2

        lax.fori_loop(0, lax.shift_right_logical(end - lo, span_bits), span, 0)
        return c

    lax.fori_loop(0, N_EXPERTS, clear_expert, 0)

    def put(t, c):
        tok_ref[slot_ref[t]] = t
        tok_ref[slot_ref[n_tok + t]] = t
        return c

    lax.fori_loop(0, n_tok, put, 0, unroll=INVERT_UNROLL)


def _invert(slots, blk, n_slots):
    smem = pl.BlockSpec(memory_space=pltpu.SMEM)
    return pl.pallas_call(
        functools.partial(_invert_body, n_slots=n_slots),
        in_specs=[smem, smem],
        out_specs=smem,
        out_shape=jax.ShapeDtypeStruct((n_slots,), I32),
        name="invert",
    )(slots, blk)


GATHER_AHEAD = 2
GATHER_SLOTS = GATHER_AHEAD + 1


def _row_gather_start(idx_ref, base, src_hbm, dst, sem, n_rows, priorities):
    for r in range(n_rows):
        tok = idx_ref[base + r]
        pltpu.make_async_copy(src_hbm.at[pl.ds(tok, 1), :], dst.at[pl.ds(r, 1), :], sem).start(
            priority=priorities[r % len(priorities)])


def _experts_body(bstart_ref, tok_ref, h2p_hbm, w1_ref, w3_ref, w2_ref, eo_hbm, xbuf, obuf, w1b, w3b, w2b, gsem,
                  osem, *, n_blocks):
    e = pl.program_id(0)
    n_exp = pl.num_programs(0)
    b0 = bstart_ref[0, EXPERT_LANE0 + e]
    nb = bstart_ref[0, EXPERT_LANE0 + e + 1] - b0
    n_used = bstart_ref[0, EXPERT_LANE0 + n_exp]

    def gather(block):
        s = lax.rem(block, GATHER_SLOTS)
        src_block = jnp.minimum(block, n_used - 1)
        _row_gather_start(tok_ref, src_block * MOE_BLOCK, h2p_hbm, xbuf.at[s], gsem.at[s], MOE_BLOCK, (1, 0))

    def gather_wait(block):
        s = lax.rem(block, GATHER_SLOTS)
        pltpu.make_async_copy(xbuf.at[s], xbuf.at[s], gsem.at[s]).wait()

    def out_copy(block, s):
        rows = pl.ds(pl.multiple_of(block * MOE_BLOCK, MOE_BLOCK), MOE_BLOCK)
        return pltpu.make_async_copy(obuf.at[s], eo_hbm.at[rows, :], osem.at[s])

    @pl.when(e == 0)
    def _():
        for k in range(GATHER_AHEAD):
            gather(k)

    @pl.when(nb > 0)
    def _():
        w1b[...] = w1_ref[...].astype(BF16)
        w3b[...] = w3_ref[...].astype(BF16)
        w2b[...] = w2_ref[...].astype(BF16)

    def block(j, carry):
        b = b0 + j
        s = lax.rem(b, 2)
        gather(b + GATHER_AHEAD)
        gather_wait(b)
        hi, lo = _unpack_bf16_pairs(xbuf[lax.rem(b, GATHER_SLOTS)])
        x = jnp.concatenate([hi.astype(BF16), lo.astype(BF16)], axis=1)
        a = _dot(x, w1b[...])
        g = _dot(x, w3b[...])
        hdn = (a * jax.nn.sigmoid(a) * g).astype(BF16)
        o = _dot(hdn, w2b[...])

        @pl.when(b >= 2)
        def _():
            out_copy(b - 2, s).wait()

        obuf[s] = _pack_bf16_pairs(o[:, :D_HALF], o[:, D_HALF:])
        out_copy(b, s).start()
        return carry

    lax.fori_loop(0, nb, block, 0)

    @pl.when(e == n_exp - 1)
    def _():
        for k in range(GATHER_AHEAD):
            gather_wait(n_used + k)

        @pl.when(n_used >= 2)
        def _():
            out_copy(n_used - 2, lax.rem(n_used, 2)).wait()

        out_copy(n_used - 1, lax.rem(n_used - 1, 2)).wait()
        obuf[0] = jnp.zeros((MOE_BLOCK, D_HALF), U32)

        def fill(tb, carry):
            out_copy(tb, 0).start()
            return carry

        lax.fori_loop(n_used, n_blocks, fill, 0)

        def drain(tb, carry):
            out_copy(tb, 0).wait()
            return carry

        lax.fori_loop(n_used, n_blocks, drain, 0)


def _experts(bstart, tok, h2p, w1, w3, w2, n_blocks):
    def wspec(r, c):
        return pl.BlockSpec((None, r, c), lambda e, bstart_ref, tok_ref: (e, 0, 0))

    grid_spec = pltpu.PrefetchScalarGridSpec(
        num_scalar_prefetch=2,
        grid=(N_EXPERTS,),
        in_specs=[pl.BlockSpec(memory_space=pl.ANY),
                  wspec(D_MODEL, D_EXPERT), wspec(D_MODEL, D_EXPERT), wspec(D_EXPERT, D_MODEL)],
        out_specs=pl.BlockSpec(memory_space=pl.ANY),
        scratch_shapes=[pltpu.VMEM((GATHER_SLOTS, MOE_BLOCK, D_HALF), U32), pltpu.VMEM((2, MOE_BLOCK, D_HALF), U32),
                        pltpu.VMEM((D_MODEL, D_EXPERT), BF16), pltpu.VMEM((D_MODEL, D_EXPERT), BF16),
                        pltpu.VMEM((D_EXPERT, D_MODEL), BF16),
                        pltpu.SemaphoreType.DMA((GATHER_SLOTS,)), pltpu.SemaphoreType.DMA((2,))],
    )
    return pl.pallas_call(
        functools.partial(_experts_body, n_blocks=n_blocks),
        grid_spec=grid_spec,
        out_shape=jax.ShapeDtypeStruct((n_blocks * MOE_BLOCK, D_HALF), U32),
        compiler_params=_params(1),
        name="experts",
    )(bstart, tok, h2p, w1, w3, w2)


def _combine_body(slot_ref, x1_ref, gate_ref, eo_hbm, yp_ref, ys_ref, buf, sem, *, n_tok, n_prompt_tiles):
    i = pl.program_id(0)
    tm = x1_ref.shape[0]
    n_tiles = n_tok // tm
    slot = lax.rem(i, GATHER_SLOTS)

    def start(tile):
        s = lax.rem(tile, GATHER_SLOTS)
        base = jnp.minimum(tile, n_tiles - 1) * tm
        _row_gather_start(slot_ref, base, eo_hbm, buf.at[s, 0], sem.at[s], tm, (0, 1))
        _row_gather_start(slot_ref, n_tok + base, eo_hbm, buf.at[s, 1], sem.at[s], tm, (0, 1))

    def wait(tile):
        s = lax.rem(tile, GATHER_SLOTS)
        pltpu.make_async_copy(buf.at[s], buf.at[s], sem.at[s]).wait()

    @pl.when(i == 0)
    def _():
        for k in range(GATHER_AHEAD):
            start(k)

    start(i + GATHER_AHEAD)
    wait(i)

    @pl.when(i == n_tiles - 1)
    def _():
        for k in range(GATHER_AHEAD):
            wait(n_tiles + k)

    g = gate_ref[...]
    hi0, lo0 = _unpack_bf16_pairs(buf[slot, 0])
    hi1, lo1 = _unpack_bf16_pairs(buf[slot, 1])
    y_hi = x1_ref[:, :D_HALF] + g[:, 0:1] * hi0 + g[:, 1:2] * hi1
    y_lo = x1_ref[:, D_HALF:] + g[:, 0:1] * lo0 + g[:, 1:2] * lo1

    @pl.when(i < n_prompt_tiles)
    def _():
        yp_ref[:, :D_HALF] = y_hi
        yp_ref[:, D_HALF:] = y_lo

    @pl.when(i >= n_prompt_tiles)
    def _():
        ys_ref[:, :D_HALF] = y_hi
        ys_ref[:, D_HALF:] = y_lo


def _combine(slots, x1, gates, eo, n_prompt, n_tok, tm):
    n_tiles = n_tok // tm
    n_prompt_tiles = n_prompt // tm
    body = functools.partial(_combine_body, n_tok=n_tok, n_prompt_tiles=n_prompt_tiles)
    grid_spec = pltpu.PrefetchScalarGridSpec(
        num_scalar_prefetch=1,
        grid=(n_tiles,),
        in_specs=[pl.BlockSpec((tm, D_MODEL), lambda i, s: (i, 0)),
                  pl.BlockSpec((tm, LANES), lambda i, s: (i, 0)),
                  pl.BlockSpec(memory_space=pl.ANY)],
        out_specs=[pl.BlockSpec((tm, D_MODEL), lambda i, s: (jnp.minimum(i, n_prompt_tiles - 1), 0)),
                   pl.BlockSpec((tm, D_MODEL), lambda i, s: (jnp.maximum(i - n_prompt_tiles, 0), 0))],
        scratch_shapes=[pltpu.VMEM((GATHER_SLOTS, 2, tm, D_HALF), U32), pltpu.SemaphoreType.DMA((GATHER_SLOTS,))],
    )
    return pl.pallas_call(
        body,
        grid_spec=grid_spec,
        out_shape=[jax.ShapeDtypeStruct((n_prompt, D_MODEL), F32),
                   jax.ShapeDtypeStruct((n_tok - n_prompt, D_MODEL), F32)],
        compiler_params=_params(1),
        name="combine",
    )(slots, x1, gates, eo)


def _layer(x_prompt, x_sample, state_conv, cache_k, cache_v, norm1_g, w_in, conv_w, conv_b, conv_norm_g,
           conv_norm_b, q_norm_g, k_norm_g, attn_sinks, w_out, norm2_g, w_rg, b_rg, w_re, b_re, w1, w3, w2):
    b, t, _ = x_prompt.shape
    sb, st, _ = x_sample.shape
    n_p, n_s = b * t, sb * st
    n_tok = n_p + n_s

    w_in_bf = w_in.astype(BF16)
    w_out_bf = w_out.astype(BF16)
    g1 = norm1_g.reshape(1, D_MODEL)
    g2 = norm2_g.reshape(1, D_MODEL)
    qg = q_norm_g.reshape(1, HEAD_DIM)
    kg = k_norm_g.reshape(1, HEAD_DIM)
    chunked = lambda a: a.reshape(-1, N_CCHUNK, LANES).transpose(1, 0, 2)
    cw, cb, lg, lb = chunked(conv_w), chunked(conv_b), chunked(conv_norm_g), chunked(conv_norm_b)
    pad_lanes = LANES - N_EXPERT_GROUPS - N_EXPERTS
    w_router_bf = jnp.concatenate([w_rg, w_re, jnp.zeros((D_MODEL, pad_lanes), F32)], axis=1).astype(BF16)
    b_router = jnp.concatenate([b_rg, b_re, jnp.zeros((pad_lanes,), F32)]).reshape(1, LANES)

    xp2 = x_prompt.reshape(n_p, D_MODEL)
    xs2 = x_sample.reshape(n_s, D_MODEL)

    c_p, q_p, k_p, v_p, conv_p, knew_p, vnew_p = _in_proj_conv(xp2, g1, w_in_bf, qg, kg, cw, cb, lg, lb, 512, b)
    u_s, q_s, k_s, v_s = _in_proj(xs2, g1, w_in_bf, qg, kg, n_s, F32)

    r3 = lambda a, bb: a.reshape(bb, -1, a.shape[-1])
    a_p = _mixer_prompt(attn_sinks, r3(q_p, b), r3(k_p, b), r3(v_p, b), 512)
    cat_s, conv_s, knew_s, vnew_s = _mixer_sample(
        attn_sinks, r3(u_s, sb), state_conv, r3(q_s, sb), r3(k_s, sb), r3(v_s, sb), cache_k, cache_v, cw, cb, lg, lb)

    x1, h2p, logits = _out_proj(c_p, a_p.reshape(n_p, D_ATTN), cat_s.reshape(n_s, D_MODEL), xp2, xs2, w_out_bf, g2,
                                w_router_bf, OUT_TILE)

    n_blocks = -(-(n_tok * 2) // MOE_BLOCK) + N_EXPERTS
    slots, gates, blk = _route(logits, b_router, n_tok)
    slots = slots.reshape(2 * n_tok)
    tok = _invert(slots, blk, n_blocks * MOE_BLOCK)
    eo = _experts(blk, tok, h2p, w1, w3, w2, n_blocks)
    y_p, y_s = _combine(slots, x1, gates, eo, n_p, n_tok, COMBINE_TILE)

    return (y_p.reshape(b, t, D_MODEL), y_s.reshape(sb, st, D_MODEL), conv_p[None], knew_p[None], vnew_p[None],
            conv_s, knew_s, vnew_s)


def kernel(x_prompt, x_sample, state_conv, cache_k, cache_v, norm1_g, w_in, conv_w, conv_b, conv_norm_g, conv_norm_b, q_norm_g, k_norm_g, attn_sinks, w_out, norm2_g, w_router_group, b_router_group, w_router_expert, b_router_expert, w1, w3, w2):
    depth = w_in.shape[0]
    assert depth == 1, "single-layer step"
    return _layer(x_prompt, x_sample, state_conv, cache_k, cache_v, norm1_g[0], w_in[0], conv_w[0],
                  conv_b[0], conv_norm_g[0], conv_norm_b[0], q_norm_g[0], k_norm_g[0], attn_sinks[0], w_out[0],
                  norm2_g[0], w_router_group[0], b_router_group[0], w_router_expert[0], b_router_expert[0],
                  w1[0], w3[0], w2[0])
```

```python
import functools
import math

import jax
import jax.numpy as jnp
from jax import lax
from jax.experimental import pallas as pl
from jax.experimental.pallas import tpu as pltpu

F32 = jnp.float32
BF16 = jnp.bfloat16
I32 = jnp.int32

D_MODEL = 2048
D_CONV = 1024
CONV_WIDTH = 31
CONV_HIST = CONV_WIDTH - 1
D_ATTN = 1024
HEAD_DIM = 128
N_HEADS = 8
N_KV_HEADS = 2
GQA_GROUP = N_HEADS // N_KV_HEADS
D_KV = N_KV_HEADS * HEAD_DIM
WINDOW = 128
BLOCK_Q = 128
SCALE = 1.0 / math.sqrt(HEAD_DIM)
N_EXPERT_GROUPS = 4
EXPERTS_PER_GROUP = 8
N_EXPERTS = N_EXPERT_GROUPS * EXPERTS_PER_GROUP
D_EXPERT = 512
MOE_BLOCK = 256
COMBINE_TILE = 128
D_IN = 2 * D_CONV + D_ATTN + 2 * D_KV
EPS = 1e-6
PAST_LEN = 16384

LANES = 128
SUBLANES = 8
MXU_COLS = 256
VMEM_LIMIT_BYTES = 56 * 1024 * 1024
NEG_INF = float("-inf")
EXPERT_LANE0 = N_EXPERT_GROUPS


def _params(n_axes):
    return pltpu.CompilerParams(dimension_semantics=("arbitrary",) * n_axes,
                                vmem_limit_bytes=VMEM_LIMIT_BYTES)


def _resident(shape):
    nd = len(shape)
    return pl.BlockSpec(shape, lambda *_: (0,) * nd, pipeline_mode=pl.Buffered(1))


def _dot(a, b):
    return jnp.dot(a, b, preferred_element_type=F32)


def _in_proj_body(x_ref, g1_ref, w_ref, qg_ref, kg_ref, u_ref, q_ref, k_ref, v_ref, n_ref):
    x = x_ref[...]
    ms = jnp.mean(x * x, axis=-1, keepdims=True)
    n_ref[...] = (x * lax.rsqrt(ms + EPS) * g1_ref[...]).astype(BF16)

    def head_norm(h, g):
        return h * lax.rsqrt(jnp.mean(h * h, axis=-1, keepdims=True) + EPS) * g

    ch = MXU_COLS
    for c in range(D_CONV // ch):
        a = _dot(n_ref[...], w_ref[:, c * ch:(c + 1) * ch])
        g = _dot(n_ref[...], w_ref[:, D_CONV + c * ch:D_CONV + (c + 1) * ch])
        u_ref[:, c * ch:(c + 1) * ch] = a * jax.nn.sigmoid(g)
    q_off = 2 * D_CONV
    for c in range(D_ATTN // ch):
        qq = _dot(n_ref[...], w_ref[:, q_off + c * ch:q_off + (c + 1) * ch])
        for j in range(ch // HEAD_DIM):
            qh = head_norm(qq[:, j * HEAD_DIM:(j + 1) * HEAD_DIM], qg_ref[...])
            q_ref[:, c * ch + j * HEAD_DIM:c * ch + (j + 1) * HEAD_DIM] = qh.astype(q_ref.dtype)
    k_off = q_off + D_ATTN
    kk = _dot(n_ref[...], w_ref[:, k_off:k_off + D_KV])
    for j in range(N_KV_HEADS):
        k_ref[:, j * HEAD_DIM:(j + 1) * HEAD_DIM] = head_norm(kk[:, j * HEAD_DIM:(j + 1) * HEAD_DIM], kg_ref[...])
    v_ref[...] = _dot(n_ref[...], w_ref[:, k_off + D_KV:k_off + 2 * D_KV])


def _in_proj_conv_body(x_ref, g1_ref, w_ref, qg_ref, kg_ref, cw_ref, cb_ref, lg_ref, lb_ref, c_ref, q_ref, k_ref,
                       v_ref, ut_ref, kt_ref, vt_ref, n_ref, ue_ref, conv_ref, *, tiles_per_seq):
    tm = x_ref.shape[0]
    first = lax.rem(pl.program_id(0), tiles_per_seq) == 0

    @pl.when(first)
    def _():
        for c in range(N_CCHUNK):
            ue_ref[c, 0:CONV_HALO, :] = jnp.zeros((CONV_HALO, LANES), F32)

    @pl.when(jnp.logical_not(first))
    def _():
        for c in range(N_CCHUNK):
            ue_ref[c, 0:CONV_HALO, :] = ue_ref[c, tm:tm + CONV_HALO, :]

    x = x_ref[...]
    ms = jnp.mean(x * x, axis=-1, keepdims=True)
    n_ref[...] = (x * lax.rsqrt(ms + EPS) * g1_ref[...]).astype(BF16)

    def head_norm(h, g):
        return h * lax.rsqrt(jnp.mean(h * h, axis=-1, keepdims=True) + EPS) * g

    ch = MXU_COLS
    lanes_per = ch // LANES
    n_glu = D_CONV // ch
    q_off = 2 * D_CONV

    row_blocks = [slice(r, r + DOT_ROWS) for r in range(0, tm, DOT_ROWS)]

    for c in range(n_glu):
        for rows in row_blocks:
            a = _dot(n_ref[rows, :], w_ref[:, c * ch:(c + 1) * ch])
            g = _dot(n_ref[rows, :], w_ref[:, D_CONV + c * ch:D_CONV + (c + 1) * ch])
            u = a * jax.nn.sigmoid(g)
            for half in range(lanes_per):
                ue_ref[c * lanes_per + half, CONV_HALO + rows.start:CONV_HALO + rows.stop, :] = (
                    u[:, half * LANES:(half + 1) * LANES])
        for half in range(lanes_per):
            lc = c * lanes_per + half
            _conv_chunk(ue_ref, lc, cw_ref[lc], cb_ref[lc], conv_ref, tm)
    _ln_swish(conv_ref, lg_ref, lb_ref, c_ref, tm)

    for c in range(D_ATTN // ch):
        for rows in row_blocks:
            qq = _dot(n_ref[rows, :], w_ref[:, q_off + c * ch:q_off + (c + 1) * ch])
            for j in range(ch // HEAD_DIM):
                qh = head_norm(qq[:, j * HEAD_DIM:(j + 1) * HEAD_DIM], qg_ref[...])
                q_ref[rows, c * ch + j * HEAD_DIM:c * ch + (j + 1) * HEAD_DIM] = qh.astype(q_ref.dtype)
    k_off = q_off + D_ATTN
    for rows in row_blocks:
        kk = _dot(n_ref[rows, :], w_ref[:, k_off:k_off + D_KV])
        vv = _dot(n_ref[rows, :], w_ref[:, k_off + D_KV:k_off + 2 * D_KV])
        v_ref[rows, :] = vv
        for j in range(N_KV_HEADS):
            hs = slice(j * HEAD_DIM, (j + 1) * HEAD_DIM)
            k_ref[rows, hs] = head_norm(kk[:, hs], kg_ref[...])
    for j in range(N_KV_HEADS):
        hs = slice(j * HEAD_DIM, (j + 1) * HEAD_DIM)
        kt_ref[:, j, :] = k_ref[tm - WINDOW:tm, hs]
        vt_ref[:, j, :] = v_ref[tm - WINDOW:tm, hs]
    for c in range(N_CCHUNK):
        ut_ref[:, c * LANES:(c + 1) * LANES] = ue_ref[c, CONV_HALO + tm - CONV_HIST:CONV_HALO + tm, :]


def _in_proj_conv(x2, g1, w_in_bf, qg, kg, cw, cb, lg, lb, tm, n_seq):
    n = x2.shape[0]
    tiles_per_seq = n // n_seq // tm
    row = lambda w: pl.BlockSpec((tm, w), lambda i: (i, 0))
    seq = lambda *dims: pl.BlockSpec((None,) + dims, lambda i: (i // tiles_per_seq,) + (0,) * len(dims))
    body = functools.partial(_in_proj_conv_body, tiles_per_seq=tiles_per_seq)
    return pl.pallas_call(
        body,
        grid=(n // tm,),
        in_specs=[row(D_MODEL), _resident((1, D_MODEL)), _resident((D_MODEL, D_IN)),
                  _resident((1, HEAD_DIM)), _resident((1, HEAD_DIM)),
                  _resident((N_CCHUNK, CONV_WIDTH, LANES)), _resident((N_CCHUNK, 1, LANES)),
                  _resident((N_CCHUNK, 1, LANES)), _resident((N_CCHUNK, 1, LANES))],
        out_specs=[row(D_CONV), row(D_ATTN), row(D_KV), row(D_KV),
                   seq(CONV_HIST, D_CONV), seq(WINDOW, N_KV_HEADS, HEAD_DIM), seq(WINDOW, N_KV_HEADS, HEAD_DIM)],
        out_shape=[jax.ShapeDtypeStruct((n, D_CONV), BF16), jax.ShapeDtypeStruct((n, D_ATTN), BF16),
                   jax.ShapeDtypeStruct((n, D_KV), F32), jax.ShapeDtypeStruct((n, D_KV), F32),
                   jax.ShapeDtypeStruct((n_seq, CONV_HIST, D_CONV), F32),
                   jax.ShapeDtypeStruct((n_seq, WINDOW, N_KV_HEADS, HEAD_DIM), F32),
                   jax.ShapeDtypeStruct((n_seq, WINDOW, N_KV_HEADS, HEAD_DIM), F32)],
        scratch_shapes=[pltpu.VMEM((tm, D_MODEL), BF16),
                        pltpu.VMEM((N_CCHUNK, CONV_HALO + tm, LANES), F32),
                        pltpu.VMEM((N_CCHUNK, tm, LANES), F32)],
        compiler_params=_params(1),
        name="in_proj_conv",
    )(x2, g1, w_in_bf, qg, kg, cw, cb, lg, lb)


def _in_proj(x2, g1, w_in_bf, qg, kg, tm, q_dtype):
    n = x2.shape[0]
    row = lambda w: pl.BlockSpec((tm, w), lambda i: (i, 0))
    return pl.pallas_call(
        _in_proj_body,
        grid=(n // tm,),
        in_specs=[row(D_MODEL), _resident((1, D_MODEL)), _resident((D_MODEL, D_IN)),
                  _resident((1, HEAD_DIM)), _resident((1, HEAD_DIM))],
        out_specs=[row(D_CONV), row(D_ATTN), row(D_KV), row(D_KV)],
        out_shape=[jax.ShapeDtypeStruct((n, D_CONV), F32), jax.ShapeDtypeStruct((n, D_ATTN), q_dtype),
                   jax.ShapeDtypeStruct((n, D_KV), F32), jax.ShapeDtypeStruct((n, D_KV), F32)],
        scratch_shapes=[pltpu.VMEM((tm, D_MODEL), BF16)],
        compiler_params=_params(1),
        name="in_proj",
    )(x2, g1, w_in_bf, qg, kg)


N_CCHUNK = D_CONV // LANES
CONV_ROWS = 64


CONV_HALO = 32
DOT_ROWS = 256


def _conv_chunk(ue_ref, c, wc, bias, conv_ref, rows):
    base = CONV_HALO - CONV_HIST
    for r0 in range(0, rows, CONV_ROWS):
        acc = jnp.broadcast_to(bias, (CONV_ROWS, LANES))
        for tap in range(CONV_WIDTH):
            acc = acc + wc[tap:tap + 1, :] * ue_ref[c, base + r0 + tap:base + r0 + tap + CONV_ROWS, :]
        conv_ref[c, r0:r0 + CONV_ROWS, :] = acc


def _ln_swish(conv_ref, lg_ref, lb_ref, cat_ref, rows):
    tot = jnp.zeros((rows, 1), F32)
    for c in range(N_CCHUNK):
        tot = tot + jnp.sum(conv_ref[c], axis=-1, keepdims=True)
    mean = tot / D_CONV
    var = jnp.zeros((rows, 1), F32)
    for c in range(N_CCHUNK):
        xc = conv_ref[c] - mean
        var = var + jnp.sum(xc * xc, axis=-1, keepdims=True)
    rstd = lax.rsqrt(var / D_CONV + EPS)
    for c in range(N_CCHUNK):
        y = (conv_ref[c] - mean) * rstd * lg_ref[c] + lb_ref[c]
        cat_ref[:, c * LANES:(c + 1) * LANES] = (y * jax.nn.sigmoid(y)).astype(cat_ref.dtype)


def _sink_softmax_rows(s, sink):
    m = jnp.maximum(jnp.max(s, axis=-1, keepdims=True), sink)
    p = jnp.exp(s - m)
    return p / (jnp.sum(p, axis=-1, keepdims=True) + jnp.exp(sink - m))


def _alibi_slope(head):
    return 2.0 ** (-8.0 * (head + 1) / N_HEADS)


def _mixer_prompt_body(sink_ref, q_ref, k_ref, kh_ref, v_ref, vh_ref, cat_ref, *, tm):
    j = pl.program_id(1)
    has_prev = j > 0
    qi = lax.broadcasted_iota(I32, (BLOCK_Q, 2 * BLOCK_Q), 0)
    kj = lax.broadcasted_iota(I32, (BLOCK_Q, 2 * BLOCK_Q), 1)
    dist = qi + BLOCK_Q - kj
    distf = dist.astype(F32)
    band = jnp.where(dist >= 0, jnp.where(dist < WINDOW, 0.0, NEG_INF), NEG_INF)
    band_first = jnp.where(kj >= BLOCK_Q, band, jnp.where(has_prev, band, NEG_INF))

    for qb in range(tm // BLOCK_Q):
        rows = slice(qb * BLOCK_Q, (qb + 1) * BLOCK_Q)
        prev = slice((qb - 1) * BLOCK_Q, qb * BLOCK_Q)
        mask = band_first if qb == 0 else band
        for kv in range(N_KV_HEADS):
            hs = slice(kv * HEAD_DIM, (kv + 1) * HEAD_DIM)
            k_prev = kh_ref[:, hs] if qb == 0 else k_ref[prev, hs]
            v_prev = vh_ref[:, hs] if qb == 0 else v_ref[prev, hs]
            kk = jnp.concatenate([k_prev, k_ref[rows, hs]], axis=0).astype(BF16)
            vv = jnp.concatenate([v_prev, v_ref[rows, hs]], axis=0).astype(BF16)
            heads = [kv * GQA_GROUP + g for g in range(GQA_GROUP)]
            qs = jnp.concatenate([q_ref[rows, h * HEAD_DIM:(h + 1) * HEAD_DIM] for h in heads], axis=0)
            s = lax.dot_general(qs, kk, (((1,), (1,)), ((), ())), preferred_element_type=F32)
            ps = []
            for g, h in enumerate(heads):
                sg = s[g * BLOCK_Q:(g + 1) * BLOCK_Q] * SCALE - _alibi_slope(h) * distf + mask
                ps.append(_sink_softmax_rows(sg, sink_ref[h]).astype(BF16))
            o = _dot(jnp.concatenate(ps, axis=0), vv)
            for g, h in enumerate(heads):
                cat_ref[rows, h * HEAD_DIM:(h + 1) * HEAD_DIM] = o[g * BLOCK_Q:(g + 1) * BLOCK_Q].astype(cat_ref.dtype)


def _mixer_prompt(sinks, q, k, v, tm):
    b, t, _ = q.shape
    kpb = tm // BLOCK_Q
    main = lambda w: pl.BlockSpec((None, tm, w), lambda bi, j: (bi, j, 0))
    prev_block = pl.BlockSpec((None, BLOCK_Q, D_KV), lambda bi, j: (bi, jnp.maximum(j * kpb - 1, 0), 0))
    body = functools.partial(_mixer_prompt_body, tm=tm)
    return pl.pallas_call(
        body,
        grid=(b, t // tm),
        in_specs=[pl.BlockSpec(memory_space=pltpu.SMEM), main(D_ATTN), main(D_KV), prev_block, main(D_KV),
                  prev_block],
        out_specs=main(D_ATTN),
        out_shape=jax.ShapeDtypeStruct((b, t, D_ATTN), BF16),
        compiler_params=_params(2),
        name="mixer_prompt",
    )(sinks, q, k, k, v, v)


KEY_PAD = 8


SAMPLE_SEQS = 8


def _sample_conv_ln_swish(st_ref, u_ref, cw_ref, cb_ref, lg_ref, lb_ref, cat_ref, t_new):
    hist, n_seq, _ = st_ref.shape
    conv = [[None] * N_CCHUNK for _ in range(t_new)]
    for c in range(N_CCHUNK):
        cs = slice(c * LANES, (c + 1) * LANES)
        wc = cw_ref[c]
        pos = [st_ref[p, :, cs] for p in range(hist)] + [u_ref[t, :, cs] for t in range(t_new)]
        for t in range(t_new):
            acc = jnp.broadcast_to(cb_ref[c], (n_seq, LANES))
            for tap in range(CONV_WIDTH):
                acc = acc + wc[tap:tap + 1, :] * pos[t + tap]
            conv[t][c] = acc
    for t in range(t_new):
        tot = jnp.zeros((n_seq, 1), F32)
        for a in conv[t]:
            tot = tot + jnp.sum(a, axis=-1, keepdims=True)
        mean = tot / D_CONV
        var = jnp.zeros((n_seq, 1), F32)
        for a in conv[t]:
            var = var + jnp.sum((a - mean) * (a - mean), axis=-1, keepdims=True)
        rstd = lax.rsqrt(var / D_CONV + EPS)
        for c, a in enumerate(conv[t]):
            y = (a - mean) * rstd * lg_ref[c] + lb_ref[c]
            cat_ref[t, :, c * LANES:(c + 1) * LANES] = (y * jax.nn.sigmoid(y)).astype(cat_ref.dtype)


def _mixer_sample_body(sink_ref, u_ref, st_ref, q_ref, k_ref, v_ref, ck_ref, cv_ref, cw_ref, cb_ref, lg_ref,
                       lb_ref, cat_ref, nst_ref, nk_ref, nv_ref, kk_ref, vv_ref, qs_ref, *, t_new):
    hist, n_seq, _ = st_ref.shape
    w_past = ck_ref.shape[1]
    n_keys = w_past + KEY_PAD
    n_rows, n_cols = GQA_GROUP * t_new * n_seq, n_seq * n_keys

    _sample_conv_ln_swish(st_ref, u_ref, cw_ref, cb_ref, lg_ref, lb_ref, cat_ref, t_new)
    nst_ref[0:hist - t_new] = st_ref[t_new:hist]
    nst_ref[hist - t_new:hist] = u_ref[...]

    for i in range(n_seq):
        nk_ref[i, 0:w_past - t_new] = ck_ref[i, t_new:w_past]
        nv_ref[i, 0:w_past - t_new] = cv_ref[i, t_new:w_past]
        kk_ref[i, w_past:n_keys, :] = jnp.zeros((KEY_PAD, D_KV), F32)
        vv_ref[i, w_past:n_keys, :] = jnp.zeros((KEY_PAD, D_KV), F32)
        for t in range(t_new):
            kk_ref[i, w_past + t:w_past + t + 1, :] = k_ref[t, i:i + 1, :]
            vv_ref[i, w_past + t:w_past + t + 1, :] = v_ref[t, i:i + 1, :]
        for h in range(N_KV_HEADS):
            hs = slice(h * HEAD_DIM, (h + 1) * HEAD_DIM)
            kk_ref[i, 0:w_past, hs] = ck_ref[i, :, h, :]
            vv_ref[i, 0:w_past, hs] = cv_ref[i, :, h, :]
            for t in range(t_new):
                nk_ref[i, w_past - t_new + t, h:h + 1, :] = k_ref[t, i:i + 1, hs]
                nv_ref[i, w_past - t_new + t, h:h + 1, :] = v_ref[t, i:i + 1, hs]

    row = lax.broadcasted_iota(I32, (n_rows, n_cols), 0)
    col = lax.broadcasted_iota(I32, (n_rows, n_cols), 1)
    seq_bits, tok_bits = n_seq.bit_length() - 1, t_new.bit_length() - 1
    assert (1 << seq_bits, 1 << tok_bits) == (n_seq, t_new), "sequence and token counts must be powers of two"
    tok = (row >> seq_bits) & (t_new - 1)
    key = col - (row & (n_seq - 1)) * n_keys
    dist = tok + w_past - key
    distf = dist.astype(F32)
    mask = jnp.where(dist >= 0, jnp.where(dist < WINDOW, 0.0, NEG_INF), NEG_INF)
    row1 = lax.broadcasted_iota(I32, (n_rows, 1), 0)
    grp = row1 >> (seq_bits + tok_bits)
    blk = lambda g, t: slice((g * t_new + t) * n_seq, (g * t_new + t + 1) * n_seq)

    for kv in range(N_KV_HEADS):
        hs = slice(kv * HEAD_DIM, (kv + 1) * HEAD_DIM)
        slope = jnp.zeros((n_rows, 1), F32)
        sink = jnp.zeros((n_rows, 1), F32)
        for g in range(GQA_GROUP):
            h = kv * GQA_GROUP + g
            slope = jnp.where(grp == g, _alibi_slope(h), slope)
            sink = jnp.where(grp == g, sink_ref[h], sink)
            for t in range(t_new):
                qs_ref[blk(g, t), :] = q_ref[t, :, h * HEAD_DIM:(h + 1) * HEAD_DIM]
        kk = kk_ref[:, :, hs].reshape(n_cols, HEAD_DIM).astype(BF16)
        vv = vv_ref[:, :, hs].reshape(n_cols, HEAD_DIM).astype(BF16)
        s = lax.dot_general(qs_ref[...].astype(BF16), kk, (((1,), (1,)), ((), ())), preferred_element_type=F32)
        sg = s * SCALE - slope * distf + mask
        o = _dot(_sink_softmax_rows(sg, sink).astype(BF16), vv)
        for g in range(GQA_GROUP):
            h = kv * GQA_GROUP + g
            for t in range(t_new):
                cat_ref[t, :, D_CONV + h * HEAD_DIM:D_CONV + (h + 1) * HEAD_DIM] = o[blk(g, t)].astype(cat_ref.dtype)


def _mixer_sample(sinks, u, state, q, k, v, ck, cv, cw, cb, lg, lb):
    t_new, b, _ = u.shape
    hist = state.shape[0]
    w_past = ck.shape[2]
    n_seq = SAMPLE_SEQS
    per = lambda r, w: pl.BlockSpec((r, n_seq, w), lambda bi: (0, bi, 0))
    cache = pl.BlockSpec((None, n_seq, w_past, N_KV_HEADS, HEAD_DIM), lambda bi: (0, bi, 0, 0, 0))
    body = functools.partial(_mixer_sample_body, t_new=t_new)
    return pl.pallas_call(
        body,
        grid=(b // n_seq,),
        in_specs=[pl.BlockSpec(memory_space=pltpu.SMEM),
                  per(t_new, D_CONV), per(hist, D_CONV), per(t_new, D_ATTN), per(t_new, D_KV), per(t_new, D_KV),
                  cache, cache,
                  _resident((N_CCHUNK, CONV_WIDTH, LANES)), _resident((N_CCHUNK, 1, LANES)),
                  _resident((N_CCHUNK, 1, LANES)), _resident((N_CCHUNK, 1, LANES))],
        out_specs=[per(t_new, D_MODEL), per(hist, D_CONV), cache, cache],
        out_shape=[jax.ShapeDtypeStruct((t_new, b, D_MODEL), F32),
                   jax.ShapeDtypeStruct(state.shape, F32),
                   jax.ShapeDtypeStruct(ck.shape, F32),
                   jax.ShapeDtypeStruct(cv.shape, F32)],
        scratch_shapes=[pltpu.VMEM((n_seq, w_past + KEY_PAD, D_KV), F32),
                        pltpu.VMEM((n_seq, w_past + KEY_PAD, D_KV), F32),
                        pltpu.VMEM((n_seq * GQA_GROUP * t_new, HEAD_DIM), F32)],
        compiler_params=_params(1),
        name="mixer_sample",
    )(sinks, u, state, q, k, v, ck, cv, cw, cb, lg, lb)


OUT_CHUNK = 512
OUT_TILE = 256
D_HALF = D_MODEL // 2
U32 = jnp.uint32


def _pack_bf16_pairs(hi, lo):
    hi_bits = lax.bitcast_convert_type(hi.astype(BF16).astype(F32), U32)
    lo_bits = lax.bitcast_convert_type(lo.astype(BF16).astype(F32), U32)
    return hi_bits | (lo_bits >> 16)


def _unpack_bf16_pairs(words):
    hi = lax.bitcast_convert_type(words & U32(0xFFFF0000), F32)
    lo = lax.bitcast_convert_type(words << 16, F32)
    return hi, lo


def _out_proj_body(catc_ref, cata_ref, cats_ref, xp_ref, xs_ref, wo_ref, g2_ref, wr_ref, x1_ref, h2p_ref, lg_ref,
                   h_ref, *, n_prompt_tiles):
    is_prompt = pl.program_id(0) < n_prompt_tiles
    tm = catc_ref.shape[0]

    def sample_rows(v):
        return jnp.concatenate([v, jnp.zeros((tm - v.shape[0], v.shape[1]), v.dtype)], axis=0)

    cat = jnp.where(is_prompt, jnp.concatenate([catc_ref[...], cata_ref[...]], axis=1),
                    sample_rows(cats_ref[...].astype(BF16)))
    ss = jnp.zeros((tm, 1), F32)
    for c in range(D_MODEL // OUT_CHUNK):
        cs = slice(c * OUT_CHUNK, (c + 1) * OUT_CHUNK)
        y = jnp.where(is_prompt, xp_ref[:, cs], sample_rows(xs_ref[:, cs])) + _dot(cat, wo_ref[:, cs])
        x1_ref[:, cs] = y
        ss = ss + jnp.sum(y * y, axis=-1, keepdims=True)
    r = lax.rsqrt(ss / D_MODEL + EPS)
    for c in range(D_MODEL // OUT_CHUNK):
        cs = slice(c * OUT_CHUNK, (c + 1) * OUT_CHUNK)
        h_ref[:, cs] = x1_ref[:, cs] * r * g2_ref[:, cs]
    for c in range(D_HALF // OUT_CHUNK):
        cs = slice(c * OUT_CHUNK, (c + 1) * OUT_CHUNK)
        cs_lo = slice(D_HALF + c * OUT_CHUNK, D_HALF + (c + 1) * OUT_CHUNK)
        h2p_ref[:, cs] = _pack_bf16_pairs(h_ref[:, cs], h_ref[:, cs_lo])
    lg_ref[...] = _dot(h_ref[...].astype(BF16), wr_ref[...])


def _out_proj(cat_conv, cat_attn, cat_s, xp2, xs2, wo_bf, g2, wr_bf, tm):
    n_prompt_tiles = cat_conv.shape[0] // tm
    n_rows = (n_prompt_tiles + 1) * tm
    n_s = cat_s.shape[0]
    prompt = lambda w: pl.BlockSpec((tm, w), lambda i: (jnp.minimum(i, n_prompt_tiles - 1), 0))
    out_row = lambda w: pl.BlockSpec((tm, w), lambda i: (i, 0))
    body = functools.partial(_out_proj_body, n_prompt_tiles=n_prompt_tiles)
    return pl.pallas_call(
        body,
        grid=(n_prompt_tiles + 1,),
        in_specs=[prompt(D_CONV), prompt(D_ATTN), _resident((n_s, D_MODEL)), prompt(D_MODEL),
                  _resident((n_s, D_MODEL)),
                  _resident((D_MODEL, D_MODEL)), _resident((1, D_MODEL)), _resident((D_MODEL, LANES))],
        out_specs=[out_row(D_MODEL), out_row(D_HALF), out_row(LANES)],
        out_shape=[jax.ShapeDtypeStruct((n_rows, D_MODEL), F32), jax.ShapeDtypeStruct((n_rows, D_HALF), U32),
                   jax.ShapeDtypeStruct((n_rows, LANES), F32)],
        scratch_shapes=[pltpu.VMEM((tm, D_MODEL), F32)],
        compiler_params=_params(1),
        name="out_proj",
    )(cat_conv, cat_attn, cat_s, xp2, xs2, wo_bf, g2, wr_bf)


ROUTE_CHUNK = 128
ROUTE_UNROLL = 5


def _route_body(lg_ref, bias_ref, slot_ref, gate_ref, blk_ref, cum_ref, sel_ref, *, n_tok):
    n_chunks = n_tok // ROUTE_CHUNK
    lane = lax.broadcasted_iota(I32, (ROUTE_CHUNK, LANES), 1).astype(F32)
    ri = lax.broadcasted_iota(I32, (ROUTE_CHUNK, ROUTE_CHUNK), 0)
    ci = lax.broadcasted_iota(I32, (ROUTE_CHUNK, ROUTE_CHUNK), 1)
    lower = jnp.where(ci < ri, 1.0, 0.0).astype(BF16)
    upper = jnp.where(ri < ci, 1.0, 0.0).astype(BF16)
    diag = ri == ci
    is_group = lane < N_EXPERT_GROUPS

    def first_max(vals):
        m = jnp.max(vals, axis=-1, keepdims=True)
        idx = jnp.min(jnp.where(vals == m, lane, float(LANES)), axis=-1, keepdims=True)
        return m, idx

    def assign(i, carry):
        rows = pl.ds(pl.multiple_of(i * ROUTE_CHUNK, ROUTE_CHUNK), ROUTE_CHUNK)
        l = lg_ref[rows, :] + bias_ref[...]
        gl = jnp.where(is_group, l, NEG_INF)
        g_max, g_idx = first_max(gl)
        g_top = 1.0 / jnp.sum(jnp.exp(gl - g_max), axis=-1, keepdims=True)
        lo = EXPERT_LANE0 + g_idx * EXPERTS_PER_GROUP
        el = jnp.where(lane >= lo, jnp.where(lane < lo + EXPERTS_PER_GROUP, l, NEG_INF), NEG_INF)
        m1, i1 = first_max(el)
        p = jnp.exp(el - m1)
        probs = p / jnp.sum(p, axis=-1, keepdims=True)
        e1 = jnp.sum(jnp.where(lane == i1, probs, 0.0), axis=-1, keepdims=True)
        _, i2 = first_max(jnp.where(lane == i1, NEG_INF, el))
        e2 = jnp.sum(jnp.where(lane == i2, probs, 0.0), axis=-1, keepdims=True)
        gate1 = g_top * e1 / (e1 + e2)
        gate2 = g_top * e2 / (e1 + e2)
        gate_ref[rows, :] = jnp.where(lane == 0, gate1, jnp.where(lane == 1, gate2, 0.0))
        sel_ref[rows, :] = jnp.where(lane == 0, i1, jnp.where(lane == 1, i2, 0.0))
        onehot = jnp.where(lane == i1, 1.0, jnp.where(lane == i2, 1.0, 0.0))
        before = _dot(lower, onehot.astype(BF16)) + carry
        cum_ref[rows, :] = before
        return carry + jnp.sum(onehot, axis=0, keepdims=True)

    counts = lax.fori_loop(0, n_chunks, assign, jnp.zeros((1, LANES), F32), unroll=ROUTE_UNROLL)
    n_blocks = jnp.floor((counts + (MOE_BLOCK - 1)) / MOE_BLOCK)
    nb8 = jnp.broadcast_to(n_blocks, (SUBLANES, LANES)).astype(BF16)
    blk_start = _dot(nb8, upper)[0:1, :]
    row_start = blk_start * MOE_BLOCK

    def place(i, carry):
        rows = pl.ds(pl.multiple_of(i * ROUTE_CHUNK, ROUTE_CHUNK), ROUTE_CHUNK)
        pos = cum_ref[rows, :] + row_start
        sel = sel_ref[rows, :]
        i1 = sel[:, 0:1]
        i2 = sel[:, 1:2]
        s1 = jnp.sum(jnp.where(lane == i1, pos, 0.0), axis=-1, keepdims=True)
        s2 = jnp.sum(jnp.where(lane == i2, pos, 0.0), axis=-1, keepdims=True)
        slot_ref[0, i] = jnp.sum(jnp.where(diag, s1, 0.0), axis=0, keepdims=True).astype(I32)
        slot_ref[1, i] = jnp.sum(jnp.where(diag, s2, 0.0), axis=0, keepdims=True).astype(I32)
        return carry

    lax.fori_loop(0, n_chunks, place, 0, unroll=ROUTE_UNROLL)

    sub = lax.broadcasted_iota(I32, (SUBLANES, LANES), 0)
    blk_ref[...] = jnp.where(sub == 1, counts, blk_start).astype(I32)


def _route(logits, bias, n_tok):
    body = functools.partial(_route_body, n_tok=n_tok)
    n_chunks = n_tok // ROUTE_CHUNK
    whole = lambda r: pl.BlockSpec((r, LANES), lambda i: (0, 0))
    return pl.pallas_call(
        body,
        grid=(1,),
        in_specs=[whole(n_tok), whole(1)],
        out_specs=[pl.BlockSpec((2, n_chunks, 1, LANES), lambda i: (0, 0, 0, 0)), whole(n_tok), whole(SUBLANES)],
        out_shape=[jax.ShapeDtypeStruct((2, n_chunks, 1, LANES), I32), jax.ShapeDtypeStruct((n_tok, LANES), F32),
                   jax.ShapeDtypeStruct((SUBLANES, LANES), I32)],
        scratch_shapes=[pltpu.VMEM((n_tok, LANES), F32), pltpu.VMEM((n_tok, LANES), F32)],
        compiler_params=_params(1),
        name="route",
    )(logits, bias)


INVERT_UNROLL = 16


CLEAR_SPAN = 8


def _invert_body(slot_ref, blk_ref, tok_ref, *, n_slots):
    n_tok = slot_ref.shape[0] // 2

    def clear_expert(e, c):
        first_pad = blk_ref[0, EXPERT_LANE0 + e] * MOE_BLOCK + blk_ref[1, EXPERT_LANE0 + e]
        end = jnp.where(e == N_EXPERTS - 1, n_slots, blk_ref[0, EXPERT_LANE0 + e + 1] * MOE_BLOCK)
        span_bits = CLEAR_SPAN.bit_length() - 1
        lo = lax.shift_right_logical(first_pad, span_bits) * CLEAR_SPAN

        def span(j, c2):
            for k in range(CLEAR_SPAN):
                tok_ref[lo + j * CLEAR_SPAN + k] = 0
            return c2

        lax.fori_loop(0, lax.shift_right_logical(end - lo, span_bits), span, 0)
        return c

    lax.fori_loop(0, N_EXPERTS, clear_expert, 0)

    def put(t, c):
        tok_ref[slot_ref[t]] = t
        tok_ref[slot_ref[n_tok + t]] = t
        return c

    lax.fori_loop(0, n_tok, put, 0, unroll=INVERT_UNROLL)


def _invert(slots, blk, n_slots):
    smem = pl.BlockSpec(memory_space=pltpu.SMEM)
    return pl.pallas_call(
        functools.partial(_invert_body, n_slots=n_slots),
        in_specs=[smem, smem],
        out_specs=smem,
        out_shape=jax.ShapeDtypeStruct((n_slots,), I32),
        name="invert",
    )(slots, blk)


GATHER_AHEAD = 2
GATHER_SLOTS = GATHER_AHEAD + 1


def _row_gather_start(idx_ref, base, src_hbm, dst, sem, n_rows, priorities):
    for r in range(n_rows):
        tok = idx_ref[base + r]
        pltpu.make_async_copy(src_hbm.at[pl.ds(tok, 1), :], dst.at[pl.ds(r, 1), :], sem).start(
            priority=priorities[r % len(priorities)])


def _experts_body(bstart_ref, tok_ref, h2p_hbm, w1_ref, w3_ref, w2_ref, eo_hbm, xbuf, obuf, w1b, w3b, w2b, gsem,
                  osem, *, n_blocks):
    e = pl.program_id(0)
    n_exp = pl.num_programs(0)
    b0 = bstart_ref[0, EXPERT_LANE0 + e]
    nb = bstart_ref[0, EXPERT_LANE0 + e + 1] - b0
    n_used = bstart_ref[0, EXPERT_LANE0 + n_exp]

    def gather(block):
        s = lax.rem(block, GATHER_SLOTS)
        src_block = jnp.minimum(block, n_used - 1)
        _row_gather_start(tok_ref, src_block * MOE_BLOCK, h2p_hbm, xbuf.at[s], gsem.at[s], MOE_BLOCK, (1, 0))

    def gather_wait(block):
        s = lax.rem(block, GATHER_SLOTS)
        pltpu.make_async_copy(xbuf.at[s], xbuf.at[s], gsem.at[s]).wait()

    def out_copy(block, s):
        rows = pl.ds(pl.multiple_of(block * MOE_BLOCK, MOE_BLOCK), MOE_BLOCK)
        return pltpu.make_async_copy(obuf.at[s], eo_hbm.at[rows, :], osem.at[s])

    @pl.when(e == 0)
    def _():
        for k in range(GATHER_AHEAD):
            gather(k)

    @pl.when(nb > 0)
    def _():
        w1b[...] = w1_ref[...].astype(BF16)
        w3b[...] = w3_ref[...].astype(BF16)
        w2b[...] = w2_ref[...].astype(BF16)

    def block(j, carry):
        b = b0 + j
        s = lax.rem(b, 2)
        gather(b + GATHER_AHEAD)
        gather_wait(b)
        hi, lo = _unpack_bf16_pairs(xbuf[lax.rem(b, GATHER_SLOTS)])
        x = jnp.concatenate([hi.astype(BF16), lo.astype(BF16)], axis=1)
        a = _dot(x, w1b[...])
        g = _dot(x, w3b[...])
        hdn = (a * jax.nn.sigmoid(a) * g).astype(BF16)
        o = _dot(hdn, w2b[...])

        @pl.when(b >= 2)
        def _():
            out_copy(b - 2, s).wait()

        obuf[s] = _pack_bf16_pairs(o[:, :D_HALF], o[:, D_HALF:])
        out_copy(b, s).start()
        return carry

    lax.fori_loop(0, nb, block, 0)

    @pl.when(e == n_exp - 1)
    def _():
        for k in range(GATHER_AHEAD):
            gather_wait(n_used + k)

        @pl.when(n_used >= 2)
        def _():
            out_copy(n_used - 2, lax.rem(n_used, 2)).wait()

        out_copy(n_used - 1, lax.rem(n_used - 1, 2)).wait()
        obuf[0] = jnp.zeros((MOE_BLOCK, D_HALF), U32)

        def fill(tb, carry):
            out_copy(tb, 0).start()
            return carry

        lax.fori_loop(n_used, n_blocks, fill, 0)

        def drain(tb, carry):
            out_copy(tb, 0).wait()
            return carry

        lax.fori_loop(n_used, n_blocks, drain, 0)


def _experts(bstart, tok, h2p, w1, w3, w2, n_blocks):
    def wspec(r, c):
        return pl.BlockSpec((None, r, c), lambda e, bstart_ref, tok_ref: (e, 0, 0))

    grid_spec = pltpu.PrefetchScalarGridSpec(
        num_scalar_prefetch=2,
        grid=(N_EXPERTS,),
        in_specs=[pl.BlockSpec(memory_space=pl.ANY),
                  wspec(D_MODEL, D_EXPERT), wspec(D_MODEL, D_EXPERT), wspec(D_EXPERT, D_MODEL)],
        out_specs=pl.BlockSpec(memory_space=pl.ANY),
        scratch_shapes=[pltpu.VMEM((GATHER_SLOTS, MOE_BLOCK, D_HALF), U32), pltpu.VMEM((2, MOE_BLOCK, D_HALF), U32),
                        pltpu.VMEM((D_MODEL, D_EXPERT), BF16), pltpu.VMEM((D_MODEL, D_EXPERT), BF16),
                        pltpu.VMEM((D_EXPERT, D_MODEL), BF16),
                        pltpu.SemaphoreType.DMA((GATHER_SLOTS,)), pltpu.SemaphoreType.DMA((2,))],
    )
    return pl.pallas_call(
        functools.partial(_experts_body, n_blocks=n_blocks),
        grid_spec=grid_spec,
        out_shape=jax.ShapeDtypeStruct((n_blocks * MOE_BLOCK, D_HALF), U32),
        compiler_params=_params(1),
        name="experts",
    )(bstart, tok, h2p, w1, w3, w2)


def _combine_body(slot_ref, x1_ref, gate_ref, eo_hbm, yp_ref, ys_ref, buf, sem, *, n_tok, n_prompt_tiles):
    i = pl.program_id(0)
    tm = x1_ref.shape[0]
    n_tiles = n_tok // tm
    slot = lax.rem(i, GATHER_SLOTS)

    def start(tile):
        s = lax.rem(tile, GATHER_SLOTS)
        base = jnp.minimum(tile, n_tiles - 1) * tm
        _row_gather_start(slot_ref, base, eo_hbm, buf.at[s, 0], sem.at[s], tm, (0, 1))
        _row_gather_start(slot_ref, n_tok + base, eo_hbm, buf.at[s, 1], sem.at[s], tm, (0, 1))

    def wait(tile):
        s = lax.rem(tile, GATHER_SLOTS)
        pltpu.make_async_copy(buf.at[s], buf.at[s], sem.at[s]).wait()

    @pl.when(i == 0)
    def _():
        for k in range(GATHER_AHEAD):
            start(k)

    start(i + GATHER_AHEAD)
    wait(i)

    @pl.when(i == n_tiles - 1)
    def _():
        for k in range(GATHER_AHEAD):
            wait(n_tiles + k)

    g = gate_ref[...]
    hi0, lo0 = _unpack_bf16_pairs(buf[slot, 0])
    hi1, lo1 = _unpack_bf16_pairs(buf[slot, 1])
    y_hi = x1_ref[:, :D_HALF] + g[:, 0:1] * hi0 + g[:, 1:2] * hi1
    y_lo = x1_ref[:, D_HALF:] + g[:, 0:1] * lo0 + g[:, 1:2] * lo1

    @pl.when(i < n_prompt_tiles)
    def _():
        yp_ref[:, :D_HALF] = y_hi
        yp_ref[:, D_HALF:] = y_lo

    @pl.when(i >= n_prompt_tiles)
    def _():
        ys_ref[:, :D_HALF] = y_hi
        ys_ref[:, D_HALF:] = y_lo


def _combine(slots, x1, gates, eo, n_prompt, n_tok, tm):
    n_tiles = n_tok // tm
    n_prompt_tiles = n_prompt // tm
    body = functools.partial(_combine_body, n_tok=n_tok, n_prompt_tiles=n_prompt_tiles)
    grid_spec = pltpu.PrefetchScalarGridSpec(
        num_scalar_prefetch=1,
        grid=(n_tiles,),
        in_specs=[pl.BlockSpec((tm, D_MODEL), lambda i, s: (i, 0)),
                  pl.BlockSpec((tm, LANES), lambda i, s: (i, 0)),
                  pl.BlockSpec(memory_space=pl.ANY)],
        out_specs=[pl.BlockSpec((tm, D_MODEL), lambda i, s: (jnp.minimum(i, n_prompt_tiles - 1), 0)),
                   pl.BlockSpec((tm, D_MODEL), lambda i, s: (jnp.maximum(i - n_prompt_tiles, 0), 0))],
        scratch_shapes=[pltpu.VMEM((GATHER_SLOTS, 2, tm, D_HALF), U32), pltpu.SemaphoreType.DMA((GATHER_SLOTS,))],
    )
    return pl.pallas_call(
        body,
        grid_spec=grid_spec,
        out_shape=[jax.ShapeDtypeStruct((n_prompt, D_MODEL), F32),
                   jax.ShapeDtypeStruct((n_tok - n_prompt, D_MODEL), F32)],
        compiler_params=_params(1),
        name="combine",
    )(slots, x1, gates, eo)


def _layer(x_prompt, x_sample, state_conv, cache_k, cache_v, norm1_g, w_in, conv_w, conv_b, conv_norm_g,
           conv_norm_b, q_norm_g, k_norm_g, attn_sinks, w_out, norm2_g, w_rg, b_rg, w_re, b_re, w1, w3, w2):
    b, t, _ = x_prompt.shape
    sb, st, _ = x_sample.shape
    n_p, n_s = b * t, sb * st
    n_tok = n_p + n_s

    w_in_bf = w_in.astype(BF16)
    w_out_bf = w_out.astype(BF16)
    g1 = norm1_g.reshape(1, D_MODEL)
    g2 = norm2_g.reshape(1, D_MODEL)
    qg = q_norm_g.reshape(1, HEAD_DIM)
    kg = k_norm_g.reshape(1, HEAD_DIM)
    chunked = lambda a: a.reshape(-1, N_CCHUNK, LANES).transpose(1, 0, 2)
    cw, cb, lg, lb = chunked(conv_w), chunked(conv_b), chunked(conv_norm_g), chunked(conv_norm_b)
    pad_lanes = LANES - N_EXPERT_GROUPS - N_EXPERTS
    w_router_bf = jnp.concatenate([w_rg, w_re, jnp.zeros((D_MODEL, pad_lanes), F32)], axis=1).astype(BF16)
    b_router = jnp.concatenate([b_rg, b_re, jnp.zeros((pad_lanes,), F32)]).reshape(1, LANES)

    xp2 = x_prompt.reshape(n_p, D_MODEL)
    xs2 = x_sample.transpose(1, 0, 2).reshape(n_s, D_MODEL)

    c_p, q_p, k_p, v_p, conv_p, knew_p, vnew_p = _in_proj_conv(xp2, g1, w_in_bf, qg, kg, cw, cb, lg, lb, 512, b)
    u_s, q_s, k_s, v_s = _in_proj(xs2, g1, w_in_bf, qg, kg, n_s, F32)

    r3 = lambda a, bb: a.reshape(bb, -1, a.shape[-1])
    a_p = _mixer_prompt(attn_sinks, r3(q_p, b), r3(k_p, b), r3(v_p, b), 512)
    cat_s, conv_s, knew_s, vnew_s = _mixer_sample(
        attn_sinks, r3(u_s, st), state_conv[0].transpose(1, 0, 2), r3(q_s, st), r3(k_s, st), r3(v_s, st),
        cache_k, cache_v, cw, cb, lg, lb)
    conv_s = conv_s.transpose(1, 0, 2)[None]

    x1, h2p, logits = _out_proj(c_p, a_p.reshape(n_p, D_ATTN), cat_s.reshape(n_s, D_MODEL), xp2, xs2, w_out_bf, g2,
                                w_router_bf, OUT_TILE)

    n_blocks = -(-(n_tok * 2) // MOE_BLOCK) + N_EXPERTS
    slots, gates, blk = _route(logits, b_router, n_tok)
    slots = slots.reshape(2 * n_tok)
    tok = _invert(slots, blk, n_blocks * MOE_BLOCK)
    eo = _experts(blk, tok, h2p, w1, w3, w2, n_blocks)
    y_p, y_s = _combine(slots, x1, gates, eo, n_p, n_tok, COMBINE_TILE)

    return (y_p.reshape(b, t, D_MODEL), y_s.reshape(st, sb, D_MODEL).transpose(1, 0, 2), conv_p[None], knew_p[None],
            vnew_p[None], conv_s, knew_s, vnew_s)


def kernel(x_prompt, x_sample, state_conv, cache_k, cache_v, norm1_g, w_in, conv_w, conv_b, conv_norm_g, conv_norm_b, q_norm_g, k_norm_g, attn_sinks, w_out, norm2_g, w_router_group, b_router_group, w_router_expert, b_router_expert, w1, w3, w2):
    depth = w_in.shape[0]
    assert depth == 1, "single-layer step"
    return _layer(x_prompt, x_sample, state_conv, cache_k, cache_v, norm1_g[0], w_in[0], conv_w[0],
                  conv_b[0], conv_norm_g[0], conv_norm_b[0], q_norm_g[0], k_norm_g[0], attn_sinks[0], w_out[0],
                  norm2_g[0], w_router_group[0], b_router_group[0], w_router_expert[0], b_router_expert[0],
                  w1[0], w3[0], w2[0])
```

```python
import functools
import math

import jax
import jax.numpy as jnp
from jax import lax
from jax.experimental import pallas as pl
from jax.experimental.pallas import tpu as pltpu

F32 = jnp.float32
BF16 = jnp.bfloat16
I32 = jnp.int32

D_MODEL = 2048
D_CONV = 1024
CONV_WIDTH = 31
CONV_HIST = CONV_WIDTH - 1
D_ATTN = 1024
HEAD_DIM = 128
N_HEADS = 8
N_KV_HEADS = 2
GQA_GROUP = N_HEADS // N_KV_HEADS
D_KV = N_KV_HEADS * HEAD_DIM
WINDOW = 128
BLOCK_Q = 128
SCALE = 1.0 / math.sqrt(HEAD_DIM)
N_EXPERT_GROUPS = 4
EXPERTS_PER_GROUP = 8
N_EXPERTS = N_EXPERT_GROUPS * EXPERTS_PER_GROUP
D_EXPERT = 512
MOE_BLOCK = 256
COMBINE_TILE = 128
D_IN = 2 * D_CONV + D_ATTN + 2 * D_KV
EPS = 1e-6
PAST_LEN = 16384

LANES = 128
SUBLANES = 8
MXU_COLS = 256
VMEM_LIMIT_BYTES = 56 * 1024 * 1024
NEG_INF = float("-inf")
EXPERT_LANE0 = N_EXPERT_GROUPS


def _params(n_axes):
    return pltpu.CompilerParams(dimension_semantics=("arbitrary",) * n_axes,
                                vmem_limit_bytes=VMEM_LIMIT_BYTES)


def _resident(shape):
    nd = len(shape)
    return pl.BlockSpec(shape, lambda *_: (0,) * nd, pipeline_mode=pl.Buffered(1))


def _dot(a, b):
    return jnp.dot(a, b, preferred_element_type=F32)


def _in_proj_body(x_ref, g1_ref, w_ref, qg_ref, kg_ref, u_ref, q_ref, k_ref, v_ref, n_ref):
    x = x_ref[...]
    ms = jnp.mean(x * x, axis=-1, keepdims=True)
    n_ref[...] = (x * lax.rsqrt(ms + EPS) * g1_ref[...]).astype(BF16)

    def head_norm(h, g):
        return h * lax.rsqrt(jnp.mean(h * h, axis=-1, keepdims=True) + EPS) * g

    ch = MXU_COLS
    for c in range(D_CONV // ch):
        a = _dot(n_ref[...], w_ref[:, c * ch:(c + 1) * ch])
        g = _dot(n_ref[...], w_ref[:, D_CONV + c * ch:D_CONV + (c + 1) * ch])
        u_ref[:, c * ch:(c + 1) * ch] = a * jax.nn.sigmoid(g)
    q_off = 2 * D_CONV
    for c in range(D_ATTN // ch):
        qq = _dot(n_ref[...], w_ref[:, q_off + c * ch:q_off + (c + 1) * ch])
        for j in range(ch // HEAD_DIM):
            qh = head_norm(qq[:, j * HEAD_DIM:(j + 1) * HEAD_DIM], qg_ref[...])
            q_ref[:, c * ch + j * HEAD_DIM:c * ch + (j + 1) * HEAD_DIM] = qh.astype(q_ref.dtype)
    k_off = q_off + D_ATTN
    kk = _dot(n_ref[...], w_ref[:, k_off:k_off + D_KV])
    for j in range(N_KV_HEADS):
        k_ref[:, j * HEAD_DIM:(j + 1) * HEAD_DIM] = head_norm(kk[:, j * HEAD_DIM:(j + 1) * HEAD_DIM], kg_ref[...])
    v_ref[...] = _dot(n_ref[...], w_ref[:, k_off + D_KV:k_off + 2 * D_KV])


def _in_proj_conv_body(x_ref, g1_ref, w_ref, qg_ref, kg_ref, cw_ref, cb_ref, lg_ref, lb_ref, c_ref, q_ref, k_ref,
                       v_ref, ut_ref, kt_ref, vt_ref, n_ref, ue_ref, conv_ref, *, tiles_per_seq):
    tm = x_ref.shape[0]
    first = lax.rem(pl.program_id(0), tiles_per_seq) == 0

    @pl.when(first)
    def _():
        for c in range(N_CCHUNK):
            ue_ref[c, 0:CONV_HALO, :] = jnp.zeros((CONV_HALO, LANES), F32)

    @pl.when(jnp.logical_not(first))
    def _():
        for c in range(N_CCHUNK):
            ue_ref[c, 0:CONV_HALO, :] = ue_ref[c, tm:tm + CONV_HALO, :]

    x = x_ref[...]
    ms = jnp.mean(x * x, axis=-1, keepdims=True)
    n_ref[...] = (x * lax.rsqrt(ms + EPS) * g1_ref[...]).astype(BF16)

    def head_norm(h, g):
        return h * lax.rsqrt(jnp.mean(h * h, axis=-1, keepdims=True) + EPS) * g

    ch = MXU_COLS
    lanes_per = ch // LANES
    n_glu = D_CONV // ch
    q_off = 2 * D_CONV

    row_blocks = [slice(r, r + DOT_ROWS) for r in range(0, tm, DOT_ROWS)]

    for c in range(n_glu):
        for rows in row_blocks:
            a = _dot(n_ref[rows, :], w_ref[:, c * ch:(c + 1) * ch])
            g = _dot(n_ref[rows, :], w_ref[:, D_CONV + c * ch:D_CONV + (c + 1) * ch])
            u = a * jax.nn.sigmoid(g)
            for half in range(lanes_per):
                ue_ref[c * lanes_per + half, CONV_HALO + rows.start:CONV_HALO + rows.stop, :] = (
                    u[:, half * LANES:(half + 1) * LANES])
        for half in range(lanes_per):
            lc = c * lanes_per + half
            _conv_chunk(ue_ref, lc, cw_ref[lc], cb_ref[lc], conv_ref, tm)
    _ln_swish(conv_ref, lg_ref, lb_ref, c_ref, tm)

    for c in range(D_ATTN // ch):
        for rows in row_blocks:
            qq = _dot(n_ref[rows, :], w_ref[:, q_off + c * ch:q_off + (c + 1) * ch])
            for j in range(ch // HEAD_DIM):
                qh = head_norm(qq[:, j * HEAD_DIM:(j + 1) * HEAD_DIM], qg_ref[...])
                q_ref[rows, c * ch + j * HEAD_DIM:c * ch + (j + 1) * HEAD_DIM] = qh.astype(q_ref.dtype)
    k_off = q_off + D_ATTN
    for rows in row_blocks:
        kk = _dot(n_ref[rows, :], w_ref[:, k_off:k_off + D_KV])
        vv = _dot(n_ref[rows, :], w_ref[:, k_off + D_KV:k_off + 2 * D_KV])
        v_ref[rows, :] = vv
        for j in range(N_KV_HEADS):
            hs = slice(j * HEAD_DIM, (j + 1) * HEAD_DIM)
            k_ref[rows, hs] = head_norm(kk[:, hs], kg_ref[...])
    for j in range(N_KV_HEADS):
        hs = slice(j * HEAD_DIM, (j + 1) * HEAD_DIM)
        kt_ref[:, j, :] = k_ref[tm - WINDOW:tm, hs]
        vt_ref[:, j, :] = v_ref[tm - WINDOW:tm, hs]
    for c in range(N_CCHUNK):
        ut_ref[:, c * LANES:(c + 1) * LANES] = ue_ref[c, CONV_HALO + tm - CONV_HIST:CONV_HALO + tm, :]


def _in_proj_conv(x2, g1, w_in_bf, qg, kg, cw, cb, lg, lb, tm, n_seq):
    n = x2.shape[0]
    tiles_per_seq = n // n_seq // tm
    row = lambda w: pl.BlockSpec((tm, w), lambda i: (i, 0))
    seq = lambda *dims: pl.BlockSpec((None,) + dims, lambda i: (i // tiles_per_seq,) + (0,) * len(dims))
    body = functools.partial(_in_proj_conv_body, tiles_per_seq=tiles_per_seq)
    return pl.pallas_call(
        body,
        grid=(n // tm,),
        in_specs=[row(D_MODEL), _resident((1, D_MODEL)), _resident((D_MODEL, D_IN)),
                  _resident((1, HEAD_DIM)), _resident((1, HEAD_DIM)),
                  _resident((N_CCHUNK, CONV_WIDTH, LANES)), _resident((N_CCHUNK, 1, LANES)),
                  _resident((N_CCHUNK, 1, LANES)), _resident((N_CCHUNK, 1, LANES))],
        out_specs=[row(D_CONV), row(D_ATTN), row(D_KV), row(D_KV),
                   seq(CONV_HIST, D_CONV), seq(WINDOW, N_KV_HEADS, HEAD_DIM), seq(WINDOW, N_KV_HEADS, HEAD_DIM)],
        out_shape=[jax.ShapeDtypeStruct((n, D_CONV), BF16), jax.ShapeDtypeStruct((n, D_ATTN), BF16),
                   jax.ShapeDtypeStruct((n, D_KV), F32), jax.ShapeDtypeStruct((n, D_KV), F32),
                   jax.ShapeDtypeStruct((n_seq, CONV_HIST, D_CONV), F32),
                   jax.ShapeDtypeStruct((n_seq, WINDOW, N_KV_HEADS, HEAD_DIM), F32),
                   jax.ShapeDtypeStruct((n_seq, WINDOW, N_KV_HEADS, HEAD_DIM), F32)],
        scratch_shapes=[pltpu.VMEM((tm, D_MODEL), BF16),
                        pltpu.VMEM((N_CCHUNK, CONV_HALO + tm, LANES), F32),
                        pltpu.VMEM((N_CCHUNK, tm, LANES), F32)],
        compiler_params=_params(1),
        name="in_proj_conv",
    )(x2, g1, w_in_bf, qg, kg, cw, cb, lg, lb)


def _in_proj(x2, g1, w_in_bf, qg, kg, tm, q_dtype):
    n = x2.shape[0]
    row = lambda w: pl.BlockSpec((tm, w), lambda i: (i, 0))
    return pl.pallas_call(
        _in_proj_body,
        grid=(n // tm,),
        in_specs=[row(D_MODEL), _resident((1, D_MODEL)), _resident((D_MODEL, D_IN)),
                  _resident((1, HEAD_DIM)), _resident((1, HEAD_DIM))],
        out_specs=[row(D_CONV), row(D_ATTN), row(D_KV), row(D_KV)],
        out_shape=[jax.ShapeDtypeStruct((n, D_CONV), F32), jax.ShapeDtypeStruct((n, D_ATTN), q_dtype),
                   jax.ShapeDtypeStruct((n, D_KV), F32), jax.ShapeDtypeStruct((n, D_KV), F32)],
        scratch_shapes=[pltpu.VMEM((tm, D_MODEL), BF16)],
        compiler_params=_params(1),
        name="in_proj",
    )(x2, g1, w_in_bf, qg, kg)


N_CCHUNK = D_CONV // LANES
CONV_ROWS = 64


CONV_HALO = 32
DOT_ROWS = 256


def _conv_chunk(ue_ref, c, wc, bias, conv_ref, rows):
    base = CONV_HALO - CONV_HIST
    for r0 in range(0, rows, CONV_ROWS):
        acc = jnp.broadcast_to(bias, (CONV_ROWS, LANES))
        for tap in range(CONV_WIDTH):
            acc = acc + wc[tap:tap + 1, :] * ue_ref[c, base + r0 + tap:base + r0 + tap + CONV_ROWS, :]
        conv_ref[c, r0:r0 + CONV_ROWS, :] = acc


def _ln_swish(conv_ref, lg_ref, lb_ref, cat_ref, rows):
    tot = jnp.zeros((rows, 1), F32)
    for c in range(N_CCHUNK):
        tot = tot + jnp.sum(conv_ref[c], axis=-1, keepdims=True)
    mean = tot / D_CONV
    var = jnp.zeros((rows, 1), F32)
    for c in range(N_CCHUNK):
        xc = conv_ref[c] - mean
        var = var + jnp.sum(xc * xc, axis=-1, keepdims=True)
    rstd = lax.rsqrt(var / D_CONV + EPS)
    for c in range(N_CCHUNK):
        y = (conv_ref[c] - mean) * rstd * lg_ref[c] + lb_ref[c]
        cat_ref[:, c * LANES:(c + 1) * LANES] = (y * jax.nn.sigmoid(y)).astype(cat_ref.dtype)


def _sink_softmax_rows(s, sink):
    m = jnp.maximum(jnp.max(s, axis=-1, keepdims=True), sink)
    p = jnp.exp(s - m)
    return p / (jnp.sum(p, axis=-1, keepdims=True) + jnp.exp(sink - m))


def _alibi_slope(head):
    return 2.0 ** (-8.0 * (head + 1) / N_HEADS)


def _mixer_prompt_body(sink_ref, q_ref, k_ref, kh_ref, v_ref, vh_ref, cat_ref, *, tm):
    j = pl.program_id(1)
    has_prev = j > 0
    qi = lax.broadcasted_iota(I32, (BLOCK_Q, 2 * BLOCK_Q), 0)
    kj = lax.broadcasted_iota(I32, (BLOCK_Q, 2 * BLOCK_Q), 1)
    dist = qi + BLOCK_Q - kj
    distf = dist.astype(F32)
    band = jnp.where(dist >= 0, jnp.where(dist < WINDOW, 0.0, NEG_INF), NEG_INF)
    band_first = jnp.where(kj >= BLOCK_Q, band, jnp.where(has_prev, band, NEG_INF))

    for qb in range(tm // BLOCK_Q):
        rows = slice(qb * BLOCK_Q, (qb + 1) * BLOCK_Q)
        prev = slice((qb - 1) * BLOCK_Q, qb * BLOCK_Q)
        mask = band_first if qb == 0 else band
        for kv in range(N_KV_HEADS):
            hs = slice(kv * HEAD_DIM, (kv + 1) * HEAD_DIM)
            k_prev = kh_ref[:, hs] if qb == 0 else k_ref[prev, hs]
            v_prev = vh_ref[:, hs] if qb == 0 else v_ref[prev, hs]
            kk = jnp.concatenate([k_prev, k_ref[rows, hs]], axis=0).astype(BF16)
            vv = jnp.concatenate([v_prev, v_ref[rows, hs]], axis=0).astype(BF16)
            heads = [kv * GQA_GROUP + g for g in range(GQA_GROUP)]
            qs = jnp.concatenate([q_ref[rows, h * HEAD_DIM:(h + 1) * HEAD_DIM] for h in heads], axis=0)
            s = lax.dot_general(qs, kk, (((1,), (1,)), ((), ())), preferred_element_type=F32)
            ps = []
            for g, h in enumerate(heads):
                sg = s[g * BLOCK_Q:(g + 1) * BLOCK_Q] * SCALE - _alibi_slope(h) * distf + mask
                ps.append(_sink_softmax_rows(sg, sink_ref[h]).astype(BF16))
            o = _dot(jnp.concatenate(ps, axis=0), vv)
            for g, h in enumerate(heads):
                cat_ref[rows, h * HEAD_DIM:(h + 1) * HEAD_DIM] = o[g * BLOCK_Q:(g + 1) * BLOCK_Q].astype(cat_ref.dtype)


def _mixer_prompt(sinks, q, k, v, tm):
    b, t, _ = q.shape
    kpb = tm // BLOCK_Q
    main = lambda w: pl.BlockSpec((None, tm, w), lambda bi, j: (bi, j, 0))
    prev_block = pl.BlockSpec((None, BLOCK_Q, D_KV), lambda bi, j: (bi, jnp.maximum(j * kpb - 1, 0), 0))
    body = functools.partial(_mixer_prompt_body, tm=tm)
    return pl.pallas_call(
        body,
        grid=(b, t // tm),
        in_specs=[pl.BlockSpec(memory_space=pltpu.SMEM), main(D_ATTN), main(D_KV), prev_block, main(D_KV),
                  prev_block],
        out_specs=main(D_ATTN),
        out_shape=jax.ShapeDtypeStruct((b, t, D_ATTN), BF16),
        compiler_params=_params(2),
        name="mixer_prompt",
    )(sinks, q, k, k, v, v)


KEY_PAD = 8


SAMPLE_SEQS = 8


def _sample_conv_ln_swish(st_ref, u_ref, cw_ref, cb_ref, lg_ref, lb_ref, cat_ref, t_new):
    hist, n_seq, _ = st_ref.shape
    conv = [[None] * N_CCHUNK for _ in range(t_new)]
    for c in range(N_CCHUNK):
        cs = slice(c * LANES, (c + 1) * LANES)
        wc = cw_ref[c]
        pos = [st_ref[p, :, cs] for p in range(hist)] + [u_ref[t, :, cs] for t in range(t_new)]
        for t in range(t_new):
            acc = jnp.broadcast_to(cb_ref[c], (n_seq, LANES))
            for tap in range(CONV_WIDTH):
                acc = acc + wc[tap:tap + 1, :] * pos[t + tap]
            conv[t][c] = acc
    for t in range(t_new):
        tot = jnp.zeros((n_seq, 1), F32)
        for a in conv[t]:
            tot = tot + jnp.sum(a, axis=-1, keepdims=True)
        mean = tot / D_CONV
        var = jnp.zeros((n_seq, 1), F32)
        for a in conv[t]:
            var = var + jnp.sum((a - mean) * (a - mean), axis=-1, keepdims=True)
        rstd = lax.rsqrt(var / D_CONV + EPS)
        for c, a in enumerate(conv[t]):
            y = (a - mean) * rstd * lg_ref[c] + lb_ref[c]
            cat_ref[t, :, c * LANES:(c + 1) * LANES] = (y * jax.nn.sigmoid(y)).astype(cat_ref.dtype)


def _mixer_sample_body(sink_ref, u_ref, st_ref, q_ref, k_ref, v_ref, ck_ref, cv_ref, cw_ref, cb_ref, lg_ref,
                       lb_ref, cat_ref, nst_ref, nk_ref, nv_ref, kk_ref, vv_ref, qs_ref, *, t_new):
    hist, n_seq, _ = st_ref.shape
    w_past = ck_ref.shape[1]
    n_keys = w_past + KEY_PAD
    n_rows, n_cols = GQA_GROUP * t_new * n_seq, n_seq * n_keys

    _sample_conv_ln_swish(st_ref, u_ref, cw_ref, cb_ref, lg_ref, lb_ref, cat_ref, t_new)
    nst_ref[0:hist - t_new] = st_ref[t_new:hist]
    nst_ref[hist - t_new:hist] = u_ref[...]

    for i in range(n_seq):
        nk_ref[i, 0:w_past - t_new] = ck_ref[i, t_new:w_past]
        nv_ref[i, 0:w_past - t_new] = cv_ref[i, t_new:w_past]
        kk_ref[i, w_past:n_keys, :] = jnp.zeros((KEY_PAD, D_KV), F32)
        vv_ref[i, w_past:n_keys, :] = jnp.zeros((KEY_PAD, D_KV), F32)
        for t in range(t_new):
            kk_ref[i, w_past + t:w_past + t + 1, :] = k_ref[t, i:i + 1, :]
            vv_ref[i, w_past + t:w_past + t + 1, :] = v_ref[t, i:i + 1, :]
        for h in range(N_KV_HEADS):
            hs = slice(h * HEAD_DIM, (h + 1) * HEAD_DIM)
            kk_ref[i, 0:w_past, hs] = ck_ref[i, :, h, :]
            vv_ref[i, 0:w_past, hs] = cv_ref[i, :, h, :]
            for t in range(t_new):
                nk_ref[i, w_past - t_new + t, h:h + 1, :] = k_ref[t, i:i + 1, hs]
                nv_ref[i, w_past - t_new + t, h:h + 1, :] = v_ref[t, i:i + 1, hs]

    row = lax.broadcasted_iota(I32, (n_rows, n_cols), 0)
    col = lax.broadcasted_iota(I32, (n_rows, n_cols), 1)
    seq_bits, tok_bits = n_seq.bit_length() - 1, t_new.bit_length() - 1
    assert (1 << seq_bits, 1 << tok_bits) == (n_seq, t_new), "sequence and token counts must be powers of two"
    tok = (row >> seq_bits) & (t_new - 1)
    key = col - (row & (n_seq - 1)) * n_keys
    dist = tok + w_past - key
    distf = dist.astype(F32)
    mask = jnp.where(dist >= 0, jnp.where(dist < WINDOW, 0.0, NEG_INF), NEG_INF)
    row1 = lax.broadcasted_iota(I32, (n_rows, 1), 0)
    grp = row1 >> (seq_bits + tok_bits)
    blk = lambda g, t: slice((g * t_new + t) * n_seq, (g * t_new + t + 1) * n_seq)

    for kv in range(N_KV_HEADS):
        hs = slice(kv * HEAD_DIM, (kv + 1) * HEAD_DIM)
        slope = jnp.zeros((n_rows, 1), F32)
        sink = jnp.zeros((n_rows, 1), F32)
        for g in range(GQA_GROUP):
            h = kv * GQA_GROUP + g
            slope = jnp.where(grp == g, _alibi_slope(h), slope)
            sink = jnp.where(grp == g, sink_ref[h], sink)
            for t in range(t_new):
                qs_ref[blk(g, t), :] = q_ref[t, :, h * HEAD_DIM:(h + 1) * HEAD_DIM]
        kk = kk_ref[:, :, hs].reshape(n_cols, HEAD_DIM).astype(BF16)
        vv = vv_ref[:, :, hs].reshape(n_cols, HEAD_DIM).astype(BF16)
        s = lax.dot_general(qs_ref[...].astype(BF16), kk, (((1,), (1,)), ((), ())), preferred_element_type=F32)
        sg = s * SCALE - slope * distf + mask
        o = _dot(_sink_softmax_rows(sg, sink).astype(BF16), vv)
        for g in range(GQA_GROUP):
            h = kv * GQA_GROUP + g
            for t in range(t_new):
                cat_ref[t, :, D_CONV + h * HEAD_DIM:D_CONV + (h + 1) * HEAD_DIM] = o[blk(g, t)].astype(cat_ref.dtype)


def _mixer_sample(sinks, u, state, q, k, v, ck, cv, cw, cb, lg, lb):
    t_new, b, _ = u.shape
    hist = state.shape[0]
    w_past = ck.shape[2]
    n_seq = SAMPLE_SEQS
    per = lambda r, w: pl.BlockSpec((r, n_seq, w), lambda bi: (0, bi, 0))
    cache = pl.BlockSpec((None, n_seq, w_past, N_KV_HEADS, HEAD_DIM), lambda bi: (0, bi, 0, 0, 0))
    body = functools.partial(_mixer_sample_body, t_new=t_new)
    return pl.pallas_call(
        body,
        grid=(b // n_seq,),
        in_specs=[pl.BlockSpec(memory_space=pltpu.SMEM),
                  per(t_new, D_CONV), per(hist, D_CONV), per(t_new, D_ATTN), per(t_new, D_KV), per(t_new, D_KV),
                  cache, cache,
                  _resident((N_CCHUNK, CONV_WIDTH, LANES)), _resident((N_CCHUNK, 1, LANES)),
                  _resident((N_CCHUNK, 1, LANES)), _resident((N_CCHUNK, 1, LANES))],
        out_specs=[per(t_new, D_MODEL), per(hist, D_CONV), cache, cache],
        out_shape=[jax.ShapeDtypeStruct((t_new, b, D_MODEL), F32),
                   jax.ShapeDtypeStruct(state.shape, F32),
                   jax.ShapeDtypeStruct(ck.shape, F32),
                   jax.ShapeDtypeStruct(cv.shape, F32)],
        scratch_shapes=[pltpu.VMEM((n_seq, w_past + KEY_PAD, D_KV), F32),
                        pltpu.VMEM((n_seq, w_past + KEY_PAD, D_KV), F32),
                        pltpu.VMEM((n_seq * GQA_GROUP * t_new, HEAD_DIM), F32)],
        compiler_params=_params(1),
        name="mixer_sample",
    )(sinks, u, state, q, k, v, ck, cv, cw, cb, lg, lb)


OUT_CHUNK = 512
OUT_TILE = 256
D_HALF = D_MODEL // 2
U32 = jnp.uint32


def _pack_bf16_pairs(hi, lo):
    hi_bits = lax.bitcast_convert_type(hi.astype(BF16).astype(F32), U32)
    lo_bits = lax.bitcast_convert_type(lo.astype(BF16).astype(F32), U32)
    return hi_bits | (lo_bits >> 16)


def _unpack_bf16_pairs(words):
    hi = lax.bitcast_convert_type(words & U32(0xFFFF0000), F32)
    lo = lax.bitcast_convert_type(words << 16, F32)
    return hi, lo


def _out_proj_body(catc_ref, cata_ref, cats_ref, xp_ref, xs_ref, wo_ref, g2_ref, wr_ref, x1_ref, h2p_ref, lg_ref,
                   h_ref, *, n_prompt_tiles):
    is_prompt = pl.program_id(0) < n_prompt_tiles
    tm = catc_ref.shape[0]

    def sample_rows(v):
        return jnp.concatenate([v, jnp.zeros((tm - v.shape[0], v.shape[1]), v.dtype)], axis=0)

    cat = jnp.where(is_prompt, jnp.concatenate([catc_ref[...], cata_ref[...]], axis=1),
                    sample_rows(cats_ref[...].astype(BF16)))
    ss = jnp.zeros((tm, 1), F32)
    for c in range(D_MODEL // OUT_CHUNK):
        cs = slice(c * OUT_CHUNK, (c + 1) * OUT_CHUNK)
        y = jnp.where(is_prompt, xp_ref[:, cs], sample_rows(xs_ref[:, cs])) + _dot(cat, wo_ref[:, cs])
        x1_ref[:, cs] = y
        ss = ss + jnp.sum(y * y, axis=-1, keepdims=True)
    r = lax.rsqrt(ss / D_MODEL + EPS)
    for c in range(D_MODEL // OUT_CHUNK):
        cs = slice(c * OUT_CHUNK, (c + 1) * OUT_CHUNK)
        h_ref[:, cs] = x1_ref[:, cs] * r * g2_ref[:, cs]
    for c in range(D_HALF // OUT_CHUNK):
        cs = slice(c * OUT_CHUNK, (c + 1) * OUT_CHUNK)
        cs_lo = slice(D_HALF + c * OUT_CHUNK, D_HALF + (c + 1) * OUT_CHUNK)
        h2p_ref[:, cs] = _pack_bf16_pairs(h_ref[:, cs], h_ref[:, cs_lo])
    lg = _dot(h_ref[...].astype(BF16), wr_ref[...])
    for j in range(tm // ROUTE_CHUNK):
        lg_ref[j] = lg[j * ROUTE_CHUNK:(j + 1) * ROUTE_CHUNK, :].T[0:ROUTER_ROWS, :]


def _out_proj(cat_conv, cat_attn, cat_s, xp2, xs2, wo_bf, g2, wr_bf, tm):
    n_prompt_tiles = cat_conv.shape[0] // tm
    n_rows = (n_prompt_tiles + 1) * tm
    n_s = cat_s.shape[0]
    prompt = lambda w: pl.BlockSpec((tm, w), lambda i: (jnp.minimum(i, n_prompt_tiles - 1), 0))
    out_row = lambda w: pl.BlockSpec((tm, w), lambda i: (i, 0))
    body = functools.partial(_out_proj_body, n_prompt_tiles=n_prompt_tiles)
    return pl.pallas_call(
        body,
        grid=(n_prompt_tiles + 1,),
        in_specs=[prompt(D_CONV), prompt(D_ATTN), _resident((n_s, D_MODEL)), prompt(D_MODEL),
                  _resident((n_s, D_MODEL)),
                  _resident((D_MODEL, D_MODEL)), _resident((1, D_MODEL)), _resident((D_MODEL, LANES))],
        out_specs=[out_row(D_MODEL), out_row(D_HALF),
                   pl.BlockSpec((tm // ROUTE_CHUNK, ROUTER_ROWS, ROUTE_CHUNK), lambda i: (i, 0, 0))],
        out_shape=[jax.ShapeDtypeStruct((n_rows, D_MODEL), F32), jax.ShapeDtypeStruct((n_rows, D_HALF), U32),
                   jax.ShapeDtypeStruct((n_rows // ROUTE_CHUNK, ROUTER_ROWS, ROUTE_CHUNK), F32)],
        scratch_shapes=[pltpu.VMEM((tm, D_MODEL), F32)],
        compiler_params=_params(1),
        name="out_proj",
    )(cat_conv, cat_attn, cat_s, xp2, xs2, wo_bf, g2, wr_bf)


ROUTE_CHUNK = 128
ROUTE_UNROLL = 5


ROUTER_ROWS = 40


def _blk_start(blk_ref, e):
    return blk_ref[EXPERT_LANE0 + e, 0]


def _blk_count(blk_ref, e):
    return blk_ref[ROUTER_ROWS + EXPERT_LANE0 + e, 0]


def _route_body(lg_ref, bias_ref, slot_ref, gate_ref, blk_ref, cum_ref, sel_ref, *, n_tok):
    n_chunks = n_tok // ROUTE_CHUNK
    row = lax.broadcasted_iota(I32, (ROUTER_ROWS, ROUTE_CHUNK), 0).astype(F32)
    ri = lax.broadcasted_iota(I32, (ROUTE_CHUNK, ROUTE_CHUNK), 0)
    ci = lax.broadcasted_iota(I32, (ROUTE_CHUNK, ROUTE_CHUNK), 1)
    earlier_tok = jnp.where(ri < ci, 1.0, 0.0).astype(BF16)
    er = lax.broadcasted_iota(I32, (ROUTER_ROWS, ROUTER_ROWS), 0)
    ec = lax.broadcasted_iota(I32, (ROUTER_ROWS, ROUTER_ROWS), 1)
    earlier_row = jnp.where(ec < er, 1.0, 0.0).astype(BF16)
    sub = lax.broadcasted_iota(I32, (SUBLANES, ROUTE_CHUNK), 0)
    is_group = row < N_EXPERT_GROUPS

    def first_max(vals):
        m = jnp.max(vals, axis=0, keepdims=True)
        idx = jnp.min(jnp.where(vals == m, row, float(ROUTER_ROWS)), axis=0, keepdims=True)
        return m, idx

    def pair_rows(a, b):
        return jnp.where(sub == 0, a, jnp.where(sub == 1, b, 0.0))

    def assign(i, carry):
        l = lg_ref[i] + bias_ref[...]
        gl = jnp.where(is_group, l, NEG_INF)
        g_max, g_idx = first_max(gl)
        g_top = 1.0 / jnp.sum(jnp.exp(gl - g_max), axis=0, keepdims=True)
        lo = EXPERT_LANE0 + g_idx * EXPERTS_PER_GROUP
        el = jnp.where(row >= lo, jnp.where(row < lo + EXPERTS_PER_GROUP, l, NEG_INF), NEG_INF)
        m1, i1 = first_max(el)
        p = jnp.exp(el - m1)
        probs = p / jnp.sum(p, axis=0, keepdims=True)
        e1 = jnp.sum(jnp.where(row == i1, probs, 0.0), axis=0, keepdims=True)
        _, i2 = first_max(jnp.where(row == i1, NEG_INF, el))
        e2 = jnp.sum(jnp.where(row == i2, probs, 0.0), axis=0, keepdims=True)
        gate_ref[i] = pair_rows(g_top * e1 / (e1 + e2), g_top * e2 / (e1 + e2))
        sel_ref[i] = pair_rows(i1, i2)
        onehot = jnp.where(row == i1, 1.0, jnp.where(row == i2, 1.0, 0.0))
        cum_ref[i] = _dot(onehot.astype(BF16), earlier_tok) + carry
        return carry + jnp.sum(onehot, axis=1, keepdims=True)

    counts = lax.fori_loop(0, n_chunks, assign, jnp.zeros((ROUTER_ROWS, 1), F32), unroll=ROUTE_UNROLL)
    n_blocks = jnp.floor((counts + (MOE_BLOCK - 1)) / MOE_BLOCK)
    blk_start = _dot(earlier_row, jnp.broadcast_to(n_blocks, (ROUTER_ROWS, LANES)).astype(BF16))
    row_start = blk_start * MOE_BLOCK

    def place(i, carry):
        pos = cum_ref[i] + row_start
        sel = sel_ref[i]
        s1 = jnp.sum(jnp.where(row == sel[0:1, :], pos, 0.0), axis=0, keepdims=True)
        s2 = jnp.sum(jnp.where(row == sel[1:2, :], pos, 0.0), axis=0, keepdims=True)
        slot_ref[0, i] = s1.astype(I32)
        slot_ref[1, i] = s2.astype(I32)
        return carry

    lax.fori_loop(0, n_chunks, place, 0, unroll=ROUTE_UNROLL)

    blk_ref[0:ROUTER_ROWS, :] = blk_start.astype(I32)
    blk_ref[ROUTER_ROWS:2 * ROUTER_ROWS, :] = jnp.broadcast_to(counts, (ROUTER_ROWS, LANES)).astype(I32)


def _route(logits, bias, n_tok):
    body = functools.partial(_route_body, n_tok=n_tok)
    n_chunks = n_tok // ROUTE_CHUNK
    chunked = lambda r: pl.BlockSpec((n_chunks, r, LANES), lambda i: (0, 0, 0))
    table = pl.BlockSpec((2 * ROUTER_ROWS, LANES), lambda i: (0, 0))
    return pl.pallas_call(
        body,
        grid=(1,),
        in_specs=[chunked(ROUTER_ROWS), pl.BlockSpec((ROUTER_ROWS, LANES), lambda i: (0, 0))],
        out_specs=[pl.BlockSpec((2, n_chunks, 1, LANES), lambda i: (0, 0, 0, 0)), chunked(SUBLANES), table],
        out_shape=[jax.ShapeDtypeStruct((2, n_chunks, 1, LANES), I32),
                   jax.ShapeDtypeStruct((n_chunks, SUBLANES, LANES), F32),
                   jax.ShapeDtypeStruct((2 * ROUTER_ROWS, LANES), I32)],
        scratch_shapes=[pltpu.VMEM((n_chunks, ROUTER_ROWS, LANES), F32),
                        pltpu.VMEM((n_chunks, SUBLANES, LANES), F32)],
        compiler_params=_params(1),
        name="route",
    )(logits, bias)


INVERT_UNROLL = 16


CLEAR_SPAN = 8


def _invert_body(slot_ref, blk_ref, tok_ref, *, n_slots):
    n_tok = slot_ref.shape[0] // 2

    def clear_expert(e, c):
        first_pad = _blk_start(blk_ref, e) * MOE_BLOCK + _blk_count(blk_ref, e)
        end = jnp.where(e == N_EXPERTS - 1, n_slots, _blk_start(blk_ref, e + 1) * MOE_BLOCK)
        span_bits = CLEAR_SPAN.bit_length() - 1
        lo = lax.shift_right_logical(first_pad, span_bits) * CLEAR_SPAN

        def span(j, c2):
            for k in range(CLEAR_SPAN):
                tok_ref[lo + j * CLEAR_SPAN + k] = 0
            return c2

        lax.fori_loop(0, lax.shift_right_logical(end - lo, span_bits), span, 0)
        return c

    lax.fori_loop(0, N_EXPERTS, clear_expert, 0)

    def put(t, c):
        tok_ref[slot_ref[t]] = t
        tok_ref[slot_ref[n_tok + t]] = t
        return c

    lax.fori_loop(0, n_tok, put, 0, unroll=INVERT_UNROLL)


def _invert(slots, blk, n_slots):
    smem = pl.BlockSpec(memory_space=pltpu.SMEM)
    return pl.pallas_call(
        functools.partial(_invert_body, n_slots=n_slots),
        in_specs=[smem, smem],
        out_specs=smem,
        out_shape=jax.ShapeDtypeStruct((n_slots,), I32),
        name="invert",
    )(slots, blk)


GATHER_AHEAD = 2
GATHER_SLOTS = GATHER_AHEAD + 1


def _row_gather_start(idx_ref, base, src_hbm, dst, sem, n_rows, priorities):
    for r in range(n_rows):
        tok = idx_ref[base + r]
        pltpu.make_async_copy(src_hbm.at[pl.ds(tok, 1), :], dst.at[pl.ds(r, 1), :], sem).start(
            priority=priorities[r % len(priorities)])


def _experts_body(bstart_ref, tok_ref, h2p_hbm, w1_ref, w3_ref, w2_ref, eo_hbm, xbuf, obuf, w1b, w3b, w2b, gsem,
                  osem, *, n_blocks):
    e = pl.program_id(0)
    n_exp = pl.num_programs(0)
    b0 = _blk_start(bstart_ref, e)
    nb = _blk_start(bstart_ref, e + 1) - b0
    n_used = _blk_start(bstart_ref, n_exp)

    def gather(block):
        s = lax.rem(block, GATHER_SLOTS)
        src_block = jnp.minimum(block, n_used - 1)
        _row_gather_start(tok_ref, src_block * MOE_BLOCK, h2p_hbm, xbuf.at[s], gsem.at[s], MOE_BLOCK, (1, 0))

    def gather_wait(block):
        s = lax.rem(block, GATHER_SLOTS)
        pltpu.make_async_copy(xbuf.at[s], xbuf.at[s], gsem.at[s]).wait()

    def out_copy(block, s):
        rows = pl.ds(pl.multiple_of(block * MOE_BLOCK, MOE_BLOCK), MOE_BLOCK)
        return pltpu.make_async_copy(obuf.at[s], eo_hbm.at[rows, :], osem.at[s])

    @pl.when(e == 0)
    def _():
        for k in range(GATHER_AHEAD):
            gather(k)

    @pl.when(nb > 0)
    def _():
        w1b[...] = w1_ref[...].astype(BF16)
        w3b[...] = w3_ref[...].astype(BF16)
        w2b[...] = w2_ref[...].astype(BF16)

    def block(j, carry):
        b = b0 + j
        s = lax.rem(b, 2)
        gather(b + GATHER_AHEAD)
        gather_wait(b)
        hi, lo = _unpack_bf16_pairs(xbuf[lax.rem(b, GATHER_SLOTS)])
        x = jnp.concatenate([hi.astype(BF16), lo.astype(BF16)], axis=1)
        a = _dot(x, w1b[...])
        g = _dot(x, w3b[...])
        hdn = (a * jax.nn.sigmoid(a) * g).astype(BF16)
        o = _dot(hdn, w2b[...])

        @pl.when(b >= 2)
        def _():
            out_copy(b - 2, s).wait()

        obuf[s] = _pack_bf16_pairs(o[:, :D_HALF], o[:, D_HALF:])
        out_copy(b, s).start()
        return carry

    lax.fori_loop(0, nb, block, 0)

    @pl.when(e == n_exp - 1)
    def _():
        for k in range(GATHER_AHEAD):
            gather_wait(n_used + k)

        @pl.when(n_used >= 2)
        def _():
            out_copy(n_used - 2, lax.rem(n_used, 2)).wait()

        out_copy(n_used - 1, lax.rem(n_used - 1, 2)).wait()
        obuf[0] = jnp.zeros((MOE_BLOCK, D_HALF), U32)

        def fill(tb, carry):
            out_copy(tb, 0).start()
            return carry

        lax.fori_loop(n_used, n_blocks, fill, 0)

        def drain(tb, carry):
            out_copy(tb, 0).wait()
            return carry

        lax.fori_loop(n_used, n_blocks, drain, 0)


def _experts(bstart, tok, h2p, w1, w3, w2, n_blocks):
    def wspec(r, c):
        return pl.BlockSpec((None, r, c), lambda e, bstart_ref, tok_ref: (e, 0, 0))

    grid_spec = pltpu.PrefetchScalarGridSpec(
        num_scalar_prefetch=2,
        grid=(N_EXPERTS,),
        in_specs=[pl.BlockSpec(memory_space=pl.ANY),
                  wspec(D_MODEL, D_EXPERT), wspec(D_MODEL, D_EXPERT), wspec(D_EXPERT, D_MODEL)],
        out_specs=pl.BlockSpec(memory_space=pl.ANY),
        scratch_shapes=[pltpu.VMEM((GATHER_SLOTS, MOE_BLOCK, D_HALF), U32), pltpu.VMEM((2, MOE_BLOCK, D_HALF), U32),
                        pltpu.VMEM((D_MODEL, D_EXPERT), BF16), pltpu.VMEM((D_MODEL, D_EXPERT), BF16),
                        pltpu.VMEM((D_EXPERT, D_MODEL), BF16),
                        pltpu.SemaphoreType.DMA((GATHER_SLOTS,)), pltpu.SemaphoreType.DMA((2,))],
    )
    return pl.pallas_call(
        functools.partial(_experts_body, n_blocks=n_blocks),
        grid_spec=grid_spec,
        out_shape=jax.ShapeDtypeStruct((n_blocks * MOE_BLOCK, D_HALF), U32),
        compiler_params=_params(1),
        name="experts",
    )(bstart, tok, h2p, w1, w3, w2)


def _combine_body(slot_ref, x1_ref, gate_ref, eo_hbm, yp_ref, ys_ref, buf, sem, *, n_tok, n_prompt_tiles):
    i = pl.program_id(0)
    tm = x1_ref.shape[0]
    n_tiles = n_tok // tm
    slot = lax.rem(i, GATHER_SLOTS)

    def start(tile):
        s = lax.rem(tile, GATHER_SLOTS)
        base = jnp.minimum(tile, n_tiles - 1) * tm
        _row_gather_start(slot_ref, base, eo_hbm, buf.at[s, 0], sem.at[s], tm, (0, 1))
        _row_gather_start(slot_ref, n_tok + base, eo_hbm, buf.at[s, 1], sem.at[s], tm, (0, 1))

    def wait(tile):
        s = lax.rem(tile, GATHER_SLOTS)
        pltpu.make_async_copy(buf.at[s], buf.at[s], sem.at[s]).wait()

    @pl.when(i == 0)
    def _():
        for k in range(GATHER_AHEAD):
            start(k)

    start(i + GATHER_AHEAD)
    wait(i)

    @pl.when(i == n_tiles - 1)
    def _():
        for k in range(GATHER_AHEAD):
            wait(n_tiles + k)

    diag = lax.broadcasted_iota(I32, (tm, tm), 0) == lax.broadcasted_iota(I32, (tm, tm), 1)
    g0 = jnp.sum(jnp.where(diag, gate_ref[0:1, :], 0.0), axis=1, keepdims=True)
    g1 = jnp.sum(jnp.where(diag, gate_ref[1:2, :], 0.0), axis=1, keepdims=True)
    hi0, lo0 = _unpack_bf16_pairs(buf[slot, 0])
    hi1, lo1 = _unpack_bf16_pairs(buf[slot, 1])
    y_hi = x1_ref[:, :D_HALF] + g0 * hi0 + g1 * hi1
    y_lo = x1_ref[:, D_HALF:] + g0 * lo0 + g1 * lo1

    @pl.when(i < n_prompt_tiles)
    def _():
        yp_ref[:, :D_HALF] = y_hi
        yp_ref[:, D_HALF:] = y_lo

    @pl.when(i >= n_prompt_tiles)
    def _():
        ys_ref[:, :D_HALF] = y_hi
        ys_ref[:, D_HALF:] = y_lo


def _combine(slots, x1, gates, eo, n_prompt, n_tok, tm):
    n_tiles = n_tok // tm
    n_prompt_tiles = n_prompt // tm
    body = functools.partial(_combine_body, n_tok=n_tok, n_prompt_tiles=n_prompt_tiles)
    grid_spec = pltpu.PrefetchScalarGridSpec(
        num_scalar_prefetch=1,
        grid=(n_tiles,),
        in_specs=[pl.BlockSpec((tm, D_MODEL), lambda i, s: (i, 0)),
                  pl.BlockSpec((None, SUBLANES, tm), lambda i, s: (i, 0, 0)),
                  pl.BlockSpec(memory_space=pl.ANY)],
        out_specs=[pl.BlockSpec((tm, D_MODEL), lambda i, s: (jnp.minimum(i, n_prompt_tiles - 1), 0)),
                   pl.BlockSpec((tm, D_MODEL), lambda i, s: (jnp.maximum(i - n_prompt_tiles, 0), 0))],
        scratch_shapes=[pltpu.VMEM((GATHER_SLOTS, 2, tm, D_HALF), U32), pltpu.SemaphoreType.DMA((GATHER_SLOTS,))],
    )
    return pl.pallas_call(
        body,
        grid_spec=grid_spec,
        out_shape=[jax.ShapeDtypeStruct((n_prompt, D_MODEL), F32),
                   jax.ShapeDtypeStruct((n_tok - n_prompt, D_MODEL), F32)],
        compiler_params=_params(1),
        name="combine",
    )(slots, x1, gates, eo)


def _layer(x_prompt, x_sample, state_conv, cache_k, cache_v, norm1_g, w_in, conv_w, conv_b, conv_norm_g,
           conv_norm_b, q_norm_g, k_norm_g, attn_sinks, w_out, norm2_g, w_rg, b_rg, w_re, b_re, w1, w3, w2):
    b, t, _ = x_prompt.shape
    sb, st, _ = x_sample.shape
    n_p, n_s = b * t, sb * st
    n_tok = n_p + n_s

    w_in_bf = w_in.astype(BF16)
    w_out_bf = w_out.astype(BF16)
    g1 = norm1_g.reshape(1, D_MODEL)
    g2 = norm2_g.reshape(1, D_MODEL)
    qg = q_norm_g.reshape(1, HEAD_DIM)
    kg = k_norm_g.reshape(1, HEAD_DIM)
    chunked = lambda a: a.reshape(-1, N_CCHUNK, LANES).transpose(1, 0, 2)
    cw, cb, lg, lb = chunked(conv_w), chunked(conv_b), chunked(conv_norm_g), chunked(conv_norm_b)
    pad_rows = ROUTER_ROWS - N_EXPERT_GROUPS - N_EXPERTS
    pad_lanes = LANES - N_EXPERT_GROUPS - N_EXPERTS
    w_router_bf = jnp.concatenate([w_rg, w_re, jnp.zeros((D_MODEL, pad_lanes), F32)], axis=1).astype(BF16)
    b_router = jnp.broadcast_to(jnp.concatenate([b_rg, b_re, jnp.zeros((pad_rows,), F32)])[:, None],
                                (ROUTER_ROWS, LANES))

    xp2 = x_prompt.reshape(n_p, D_MODEL)
    xs2 = x_sample.transpose(1, 0, 2).reshape(n_s, D_MODEL)

    c_p, q_p, k_p, v_p, conv_p, knew_p, vnew_p = _in_proj_conv(xp2, g1, w_in_bf, qg, kg, cw, cb, lg, lb, 512, b)
    u_s, q_s, k_s, v_s = _in_proj(xs2, g1, w_in_bf, qg, kg, n_s, F32)

    r3 = lambda a, bb: a.reshape(bb, -1, a.shape[-1])
    a_p = _mixer_prompt(attn_sinks, r3(q_p, b), r3(k_p, b), r3(v_p, b), 512)
    cat_s, conv_s, knew_s, vnew_s = _mixer_sample(
        attn_sinks, r3(u_s, st), state_conv[0].transpose(1, 0, 2), r3(q_s, st), r3(k_s, st), r3(v_s, st),
        cache_k, cache_v, cw, cb, lg, lb)
    conv_s = conv_s.transpose(1, 0, 2)[None]

    x1, h2p, logits = _out_proj(c_p, a_p.reshape(n_p, D_ATTN), cat_s.reshape(n_s, D_MODEL), xp2, xs2, w_out_bf, g2,
                                w_router_bf, OUT_TILE)

    n_blocks = -(-(n_tok * 2) // MOE_BLOCK) + N_EXPERTS
    slots, gates, blk = _route(logits, b_router, n_tok)
    slots = slots.reshape(2 * n_tok)
    tok = _invert(slots, blk, n_blocks * MOE_BLOCK)
    eo = _experts(blk, tok, h2p, w1, w3, w2, n_blocks)
    y_p, y_s = _combine(slots, x1, gates, eo, n_p, n_tok, COMBINE_TILE)

    return (y_p.reshape(b, t, D_MODEL), y_s.reshape(st, sb, D_MODEL).transpose(1, 0, 2), conv_p[None], knew_p[None],
            vnew_p[None], conv_s, knew_s, vnew_s)


def kernel(x_prompt, x_sample, state_conv, cache_k, cache_v, norm1_g, w_in, conv_w, conv_b, conv_norm_g, conv_norm_b, q_norm_g, k_norm_g, attn_sinks, w_out, norm2_g, w_router_group, b_router_group, w_router_expert, b_router_expert, w1, w3, w2):
    depth = w_in.shape[0]
    assert depth == 1, "single-layer step"
    return _layer(x_prompt, x_sample, state_conv, cache_k, cache_v, norm1_g[0], w_in[0], conv_w[0],
                  conv_b[0], conv_norm_g[0], conv_norm_b[0], q_norm_g[0], k_norm_g[0], attn_sinks[0], w_out[0],
                  norm2_g[0], w_router_group[0], b_router_group[0], w_router_expert[0], b_router_expert[0],
                  w1[0], w3[0], w2[0])
```

```python
import functools
import math

import jax
import jax.numpy as jnp
from jax import lax
from jax.experimental import pallas as pl
from jax.experimental.pallas import tpu as pltpu

F32 = jnp.float32
BF16 = jnp.bfloat16
I32 = jnp.int32

D_MODEL = 2048
D_CONV = 1024
CONV_WIDTH = 31
CONV_HIST = CONV_WIDTH - 1
D_ATTN = 1024
HEAD_DIM = 128
N_HEADS = 8
N_KV_HEADS = 2
GQA_GROUP = N_HEADS // N_KV_HEADS
D_KV = N_KV_HEADS * HEAD_DIM
WINDOW = 128
BLOCK_Q = 128
SCALE = 1.0 / math.sqrt(HEAD_DIM)
N_EXPERT_GROUPS = 4
EXPERTS_PER_GROUP = 8
N_EXPERTS = N_EXPERT_GROUPS * EXPERTS_PER_GROUP
D_EXPERT = 512
MOE_BLOCK = 256
COMBINE_TILE = 128
D_IN = 2 * D_CONV + D_ATTN + 2 * D_KV
EPS = 1e-6
PAST_LEN = 16384

LANES = 128
SUBLANES = 8
MXU_COLS = 256
VMEM_LIMIT_BYTES = 56 * 1024 * 1024
NEG_INF = float("-inf")
EXPERT_LANE0 = N_EXPERT_GROUPS


def _params(n_axes):
    return pltpu.CompilerParams(dimension_semantics=("arbitrary",) * n_axes,
                                vmem_limit_bytes=VMEM_LIMIT_BYTES)


def _resident(shape):
    nd = len(shape)
    return pl.BlockSpec(shape, lambda *_: (0,) * nd, pipeline_mode=pl.Buffered(1))


def _dot(a, b):
    return jnp.dot(a, b, preferred_element_type=F32)


def _in_proj_body(x_ref, g1_ref, w_ref, qg_ref, kg_ref, u_ref, q_ref, k_ref, v_ref, n_ref):
    x = x_ref[...]
    ms = jnp.mean(x * x, axis=-1, keepdims=True)
    n_ref[...] = (x * lax.rsqrt(ms + EPS) * g1_ref[...]).astype(BF16)

    def head_norm(h, g):
        return h * lax.rsqrt(jnp.mean(h * h, axis=-1, keepdims=True) + EPS) * g

    ch = MXU_COLS
    for c in range(D_CONV // ch):
        a = _dot(n_ref[...], w_ref[:, c * ch:(c + 1) * ch])
        g = _dot(n_ref[...], w_ref[:, D_CONV + c * ch:D_CONV + (c + 1) * ch])
        u_ref[:, c * ch:(c + 1) * ch] = a * jax.nn.sigmoid(g)
    q_off = 2 * D_CONV
    for c in range(D_ATTN // ch):
        qq = _dot(n_ref[...], w_ref[:, q_off + c * ch:q_off + (c + 1) * ch])
        for j in range(ch // HEAD_DIM):
            qh = head_norm(qq[:, j * HEAD_DIM:(j + 1) * HEAD_DIM], qg_ref[...])
            q_ref[:, c * ch + j * HEAD_DIM:c * ch + (j + 1) * HEAD_DIM] = qh.astype(q_ref.dtype)
    k_off = q_off + D_ATTN
    kk = _dot(n_ref[...], w_ref[:, k_off:k_off + D_KV])
    for j in range(N_KV_HEADS):
        k_ref[:, j * HEAD_DIM:(j + 1) * HEAD_DIM] = head_norm(kk[:, j * HEAD_DIM:(j + 1) * HEAD_DIM], kg_ref[...])
    v_ref[...] = _dot(n_ref[...], w_ref[:, k_off + D_KV:k_off + 2 * D_KV])


CAST_COLS = 512


def _load_weight_as_bf16(w_hbm, w_bf, stage, sem):
    n_chunks = w_hbm.shape[1] // CAST_COLS

    def copy(c):
        return pltpu.make_async_copy(w_hbm.at[:, c * CAST_COLS:(c + 1) * CAST_COLS], stage.at[c % 2], sem.at[c % 2])

    copy(0).start()
    for c in range(n_chunks):
        if c + 1 < n_chunks:
            copy(c + 1).start()
        copy(c).wait()
        w_bf[:, c * CAST_COLS:(c + 1) * CAST_COLS] = stage[c % 2].astype(BF16)


def _in_proj_conv_body(x_ref, g1_ref, w_hbm, qg_ref, kg_ref, cw_ref, cb_ref, lg_ref, lb_ref, c_ref, q_ref, k_ref,
                       v_ref, ut_ref, kt_ref, vt_ref, wout_hbm, n_ref, ue_ref, conv_ref, w_ref, wstage, wsem, osem, *,
                       tiles_per_seq):
    tm = x_ref.shape[0]
    step = pl.program_id(0)
    first = lax.rem(step, tiles_per_seq) == 0
    w_out_copy = pltpu.make_async_copy(w_ref, wout_hbm, osem.at[0])

    @pl.when(step == 0)
    def _():
        _load_weight_as_bf16(w_hbm, w_ref, wstage, wsem)
        w_out_copy.start()

    @pl.when(step == pl.num_programs(0) - 1)
    def _():
        w_out_copy.wait()

    @pl.when(first)
    def _():
        for c in range(N_CCHUNK):
            ue_ref[c, 0:CONV_HALO, :] = jnp.zeros((CONV_HALO, LANES), F32)

    @pl.when(jnp.logical_not(first))
    def _():
        for c in range(N_CCHUNK):
            ue_ref[c, 0:CONV_HALO, :] = ue_ref[c, tm:tm + CONV_HALO, :]

    x = x_ref[...]
    ms = jnp.mean(x * x, axis=-1, keepdims=True)
    n_ref[...] = (x * lax.rsqrt(ms + EPS) * g1_ref[...]).astype(BF16)

    def head_norm(h, g):
        return h * lax.rsqrt(jnp.mean(h * h, axis=-1, keepdims=True) + EPS) * g

    ch = MXU_COLS
    lanes_per = ch // LANES
    n_glu = D_CONV // ch
    q_off = 2 * D_CONV

    row_blocks = [slice(r, r + DOT_ROWS) for r in range(0, tm, DOT_ROWS)]

    for c in range(n_glu):
        for rows in row_blocks:
            a = _dot(n_ref[rows, :], w_ref[:, c * ch:(c + 1) * ch])
            g = _dot(n_ref[rows, :], w_ref[:, D_CONV + c * ch:D_CONV + (c + 1) * ch])
            u = a * jax.nn.sigmoid(g)
            for half in range(lanes_per):
                ue_ref[c * lanes_per + half, CONV_HALO + rows.start:CONV_HALO + rows.stop, :] = (
                    u[:, half * LANES:(half + 1) * LANES])
        for half in range(lanes_per):
            lc = c * lanes_per + half
            _conv_chunk(ue_ref, lc, cw_ref[lc], cb_ref[lc], conv_ref, tm)
    _ln_swish(conv_ref, lg_ref, lb_ref, c_ref, tm)

    for c in range(D_ATTN // ch):
        for rows in row_blocks:
            qq = _dot(n_ref[rows, :], w_ref[:, q_off + c * ch:q_off + (c + 1) * ch])
            for j in range(ch // HEAD_DIM):
                qh = head_norm(qq[:, j * HEAD_DIM:(j + 1) * HEAD_DIM], qg_ref[...])
                q_ref[rows, c * ch + j * HEAD_DIM:c * ch + (j + 1) * HEAD_DIM] = qh.astype(q_ref.dtype)
    k_off = q_off + D_ATTN
    for rows in row_blocks:
        kk = _dot(n_ref[rows, :], w_ref[:, k_off:k_off + D_KV])
        vv = _dot(n_ref[rows, :], w_ref[:, k_off + D_KV:k_off + 2 * D_KV])
        v_ref[rows, :] = vv
        for j in range(N_KV_HEADS):
            hs = slice(j * HEAD_DIM, (j + 1) * HEAD_DIM)
            k_ref[rows, hs] = head_norm(kk[:, hs], kg_ref[...])
    for j in range(N_KV_HEADS):
        hs = slice(j * HEAD_DIM, (j + 1) * HEAD_DIM)
        kt_ref[:, j, :] = k_ref[tm - WINDOW:tm, hs]
        vt_ref[:, j, :] = v_ref[tm - WINDOW:tm, hs]
    for c in range(N_CCHUNK):
        ut_ref[:, c * LANES:(c + 1) * LANES] = ue_ref[c, CONV_HALO + tm - CONV_HIST:CONV_HALO + tm, :]


def _in_proj_conv(x2, g1, w_in, qg, kg, cw, cb, lg, lb, tm, n_seq):
    n = x2.shape[0]
    tiles_per_seq = n // n_seq // tm
    row = lambda w: pl.BlockSpec((tm, w), lambda i: (i, 0))
    seq = lambda *dims: pl.BlockSpec((None,) + dims, lambda i: (i // tiles_per_seq,) + (0,) * len(dims))
    hbm = pl.BlockSpec(memory_space=pl.ANY)
    body = functools.partial(_in_proj_conv_body, tiles_per_seq=tiles_per_seq)
    return pl.pallas_call(
        body,
        grid=(n // tm,),
        in_specs=[row(D_MODEL), _resident((1, D_MODEL)), hbm,
                  _resident((1, HEAD_DIM)), _resident((1, HEAD_DIM)),
                  _resident((N_CCHUNK, CONV_WIDTH, LANES)), _resident((N_CCHUNK, 1, LANES)),
                  _resident((N_CCHUNK, 1, LANES)), _resident((N_CCHUNK, 1, LANES))],
        out_specs=[row(D_CONV), row(D_ATTN), row(D_KV), row(D_KV),
                   seq(CONV_HIST, D_CONV), seq(WINDOW, N_KV_HEADS, HEAD_DIM), seq(WINDOW, N_KV_HEADS, HEAD_DIM), hbm],
        out_shape=[jax.ShapeDtypeStruct((n, D_CONV), BF16), jax.ShapeDtypeStruct((n, D_ATTN), BF16),
                   jax.ShapeDtypeStruct((n, D_KV), F32), jax.ShapeDtypeStruct((n, D_KV), F32),
                   jax.ShapeDtypeStruct((n_seq, CONV_HIST, D_CONV), F32),
                   jax.ShapeDtypeStruct((n_seq, WINDOW, N_KV_HEADS, HEAD_DIM), F32),
                   jax.ShapeDtypeStruct((n_seq, WINDOW, N_KV_HEADS, HEAD_DIM), F32),
                   jax.ShapeDtypeStruct((D_MODEL, D_IN), BF16)],
        scratch_shapes=[pltpu.VMEM((tm, D_MODEL), BF16),
                        pltpu.VMEM((N_CCHUNK, CONV_HALO + tm, LANES), F32),
                        pltpu.VMEM((N_CCHUNK, tm, LANES), F32),
                        pltpu.VMEM((D_MODEL, D_IN), BF16), pltpu.VMEM((2, D_MODEL, CAST_COLS), F32),
                        pltpu.SemaphoreType.DMA((2,)), pltpu.SemaphoreType.DMA((1,))],
        compiler_params=_params(1),
        name="in_proj_conv",
    )(x2, g1, w_in, qg, kg, cw, cb, lg, lb)


def _in_proj(x2, g1, w_in_bf, qg, kg, tm, q_dtype):
    n = x2.shape[0]
    row = lambda w: pl.BlockSpec((tm, w), lambda i: (i, 0))
    return pl.pallas_call(
        _in_proj_body,
        grid=(n // tm,),
        in_specs=[row(D_MODEL), _resident((1, D_MODEL)), _resident((D_MODEL, D_IN)),
                  _resident((1, HEAD_DIM)), _resident((1, HEAD_DIM))],
        out_specs=[row(D_CONV), row(D_ATTN), row(D_KV), row(D_KV)],
        out_shape=[jax.ShapeDtypeStruct((n, D_CONV), F32), jax.ShapeDtypeStruct((n, D_ATTN), q_dtype),
                   jax.ShapeDtypeStruct((n, D_KV), F32), jax.ShapeDtypeStruct((n, D_KV), F32)],
        scratch_shapes=[pltpu.VMEM((tm, D_MODEL), BF16)],
        compiler_params=_params(1),
        name="in_proj",
    )(x2, g1, w_in_bf, qg, kg)


N_CCHUNK = D_CONV // LANES
CONV_ROWS = 64


CONV_HALO = 32
DOT_ROWS = 256


def _conv_chunk(ue_ref, c, wc, bias, conv_ref, rows):
    base = CONV_HALO - CONV_HIST
    for r0 in range(0, rows, CONV_ROWS):
        acc = jnp.broadcast_to(bias, (CONV_ROWS, LANES))
        for tap in range(CONV_WIDTH):
            acc = acc + wc[tap:tap + 1, :] * ue_ref[c, base + r0 + tap:base + r0 + tap + CONV_ROWS, :]
        conv_ref[c, r0:r0 + CONV_ROWS, :] = acc


def _ln_swish(conv_ref, lg_ref, lb_ref, cat_ref, rows):
    tot = jnp.zeros((rows, 1), F32)
    for c in range(N_CCHUNK):
        tot = tot + jnp.sum(conv_ref[c], axis=-1, keepdims=True)
    mean = tot / D_CONV
    var = jnp.zeros((rows, 1), F32)
    for c in range(N_CCHUNK):
        xc = conv_ref[c] - mean
        var = var + jnp.sum(xc * xc, axis=-1, keepdims=True)
    rstd = lax.rsqrt(var / D_CONV + EPS)
    for c in range(N_CCHUNK):
        y = (conv_ref[c] - mean) * rstd * lg_ref[c] + lb_ref[c]
        cat_ref[:, c * LANES:(c + 1) * LANES] = (y * jax.nn.sigmoid(y)).astype(cat_ref.dtype)


def _sink_softmax_rows(s, sink):
    m = jnp.maximum(jnp.max(s, axis=-1, keepdims=True), sink)
    p = jnp.exp(s - m)
    return p / (jnp.sum(p, axis=-1, keepdims=True) + jnp.exp(sink - m))


def _alibi_slope(head):
    return 2.0 ** (-8.0 * (head + 1) / N_HEADS)


def _mixer_prompt_body(sink_ref, q_ref, k_ref, kh_ref, v_ref, vh_ref, cat_ref, *, tm):
    j = pl.program_id(1)
    has_prev = j > 0
    qi = lax.broadcasted_iota(I32, (BLOCK_Q, 2 * BLOCK_Q), 0)
    kj = lax.broadcasted_iota(I32, (BLOCK_Q, 2 * BLOCK_Q), 1)
    dist = qi + BLOCK_Q - kj
    distf = dist.astype(F32)
    band = jnp.where(dist >= 0, jnp.where(dist < WINDOW, 0.0, NEG_INF), NEG_INF)
    band_first = jnp.where(kj >= BLOCK_Q, band, jnp.where(has_prev, band, NEG_INF))

    for qb in range(tm // BLOCK_Q):
        rows = slice(qb * BLOCK_Q, (qb + 1) * BLOCK_Q)
        prev = slice((qb - 1) * BLOCK_Q, qb * BLOCK_Q)
        mask = band_first if qb == 0 else band
        for kv in range(N_KV_HEADS):
            hs = slice(kv * HEAD_DIM, (kv + 1) * HEAD_DIM)
            k_prev = kh_ref[:, hs] if qb == 0 else k_ref[prev, hs]
            v_prev = vh_ref[:, hs] if qb == 0 else v_ref[prev, hs]
            kk = jnp.concatenate([k_prev, k_ref[rows, hs]], axis=0).astype(BF16)
            vv = jnp.concatenate([v_prev, v_ref[rows, hs]], axis=0).astype(BF16)
            heads = [kv * GQA_GROUP + g for g in range(GQA_GROUP)]
            qs = jnp.concatenate([q_ref[rows, h * HEAD_DIM:(h + 1) * HEAD_DIM] for h in heads], axis=0)
            s = lax.dot_general(qs, kk, (((1,), (1,)), ((), ())), preferred_element_type=F32)
            ps = []
            for g, h in enumerate(heads):
                sg = s[g * BLOCK_Q:(g + 1) * BLOCK_Q] * SCALE - _alibi_slope(h) * distf + mask
                ps.append(_sink_softmax_rows(sg, sink_ref[h]).astype(BF16))
            o = _dot(jnp.concatenate(ps, axis=0), vv)
            for g, h in enumerate(heads):
                cat_ref[rows, h * HEAD_DIM:(h + 1) * HEAD_DIM] = o[g * BLOCK_Q:(g + 1) * BLOCK_Q].astype(cat_ref.dtype)


def _mixer_prompt(sinks, q, k, v, tm):
    b, t, _ = q.shape
    kpb = tm // BLOCK_Q
    main = lambda w: pl.BlockSpec((None, tm, w), lambda bi, j: (bi, j, 0))
    prev_block = pl.BlockSpec((None, BLOCK_Q, D_KV), lambda bi, j: (bi, jnp.maximum(j * kpb - 1, 0), 0))
    body = functools.partial(_mixer_prompt_body, tm=tm)
    return pl.pallas_call(
        body,
        grid=(b, t // tm),
        in_specs=[pl.BlockSpec(memory_space=pltpu.SMEM), main(D_ATTN), main(D_KV), prev_block, main(D_KV),
                  prev_block],
        out_specs=main(D_ATTN),
        out_shape=jax.ShapeDtypeStruct((b, t, D_ATTN), BF16),
        compiler_params=_params(2),
        name="mixer_prompt",
    )(sinks, q, k, k, v, v)


KEY_PAD = 8


SAMPLE_SEQS = 8


def _sample_conv_ln_swish(st_ref, u_ref, cw_ref, cb_ref, lg_ref, lb_ref, cat_ref, t_new):
    hist, n_seq, _ = st_ref.shape
    conv = [[None] * N_CCHUNK for _ in range(t_new)]
    for c in range(N_CCHUNK):
        cs = slice(c * LANES, (c + 1) * LANES)
        wc = cw_ref[c]
        pos = [st_ref[p, :, cs] for p in range(hist)] + [u_ref[t, :, cs] for t in range(t_new)]
        for t in range(t_new):
            acc = jnp.broadcast_to(cb_ref[c], (n_seq, LANES))
            for tap in range(CONV_WIDTH):
                acc = acc + wc[tap:tap + 1, :] * pos[t + tap]
            conv[t][c] = acc
    for t in range(t_new):
        tot = jnp.zeros((n_seq, 1), F32)
        for a in conv[t]:
            tot = tot + jnp.sum(a, axis=-1, keepdims=True)
        mean = tot / D_CONV
        var = jnp.zeros((n_seq, 1), F32)
        for a in conv[t]:
            var = var + jnp.sum((a - mean) * (a - mean), axis=-1, keepdims=True)
        rstd = lax.rsqrt(var / D_CONV + EPS)
        for c, a in enumerate(conv[t]):
            y = (a - mean) * rstd * lg_ref[c] + lb_ref[c]
            cat_ref[t, :, c * LANES:(c + 1) * LANES] = (y * jax.nn.sigmoid(y)).astype(cat_ref.dtype)


def _mixer_sample_body(sink_ref, u_ref, st_ref, q_ref, k_ref, v_ref, ck_ref, cv_ref, cw_ref, cb_ref, lg_ref,
                       lb_ref, cat_ref, nst_ref, nk_ref, nv_ref, kk_ref, vv_ref, qs_ref, *, t_new):
    hist, n_seq, _ = st_ref.shape
    w_past = ck_ref.shape[1]
    n_keys = w_past + KEY_PAD
    n_rows, n_cols = GQA_GROUP * t_new * n_seq, n_seq * n_keys

    _sample_conv_ln_swish(st_ref, u_ref, cw_ref, cb_ref, lg_ref, lb_ref, cat_ref, t_new)
    nst_ref[0:hist - t_new] = st_ref[t_new:hist]
    nst_ref[hist - t_new:hist] = u_ref[...]

    for i in range(n_seq):
        nk_ref[i, 0:w_past - t_new] = ck_ref[i, t_new:w_past]
        nv_ref[i, 0:w_past - t_new] = cv_ref[i, t_new:w_past]
        kk_ref[i, w_past:n_keys, :] = jnp.zeros((KEY_PAD, D_KV), F32)
        vv_ref[i, w_past:n_keys, :] = jnp.zeros((KEY_PAD, D_KV), F32)
        for t in range(t_new):
            kk_ref[i, w_past + t:w_past + t + 1, :] = k_ref[t, i:i + 1, :]
            vv_ref[i, w_past + t:w_past + t + 1, :] = v_ref[t, i:i + 1, :]
        for h in range(N_KV_HEADS):
            hs = slice(h * HEAD_DIM, (h + 1) * HEAD_DIM)
            kk_ref[i, 0:w_past, hs] = ck_ref[i, :, h, :]
            vv_ref[i, 0:w_past, hs] = cv_ref[i, :, h, :]
            for t in range(t_new):
                nk_ref[i, w_past - t_new + t, h:h + 1, :] = k_ref[t, i:i + 1, hs]
                nv_ref[i, w_past - t_new + t, h:h + 1, :] = v_ref[t, i:i + 1, hs]

    row = lax.broadcasted_iota(I32, (n_rows, n_cols), 0)
    col = lax.broadcasted_iota(I32, (n_rows, n_cols), 1)
    seq_bits, tok_bits = n_seq.bit_length() - 1, t_new.bit_length() - 1
    assert (1 << seq_bits, 1 << tok_bits) == (n_seq, t_new), "sequence and token counts must be powers of two"
    tok = (row >> seq_bits) & (t_new - 1)
    key = col - (row & (n_seq - 1)) * n_keys
    dist = tok + w_past - key
    distf = dist.astype(F32)
    mask = jnp.where(dist >= 0, jnp.where(dist < WINDOW, 0.0, NEG_INF), NEG_INF)
    row1 = lax.broadcasted_iota(I32, (n_rows, 1), 0)
    grp = row1 >> (seq_bits + tok_bits)
    blk = lambda g, t: slice((g * t_new + t) * n_seq, (g * t_new + t + 1) * n_seq)

    for kv in range(N_KV_HEADS):
        hs = slice(kv * HEAD_DIM, (kv + 1) * HEAD_DIM)
        slope = jnp.zeros((n_rows, 1), F32)
        sink = jnp.zeros((n_rows, 1), F32)
        for g in range(GQA_GROUP):
            h = kv * GQA_GROUP + g
            slope = jnp.where(grp == g, _alibi_slope(h), slope)
            sink = jnp.where(grp == g, sink_ref[h], sink)
            for t in range(t_new):
                qs_ref[blk(g, t), :] = q_ref[t, :, h * HEAD_DIM:(h + 1) * HEAD_DIM]
        kk = kk_ref[:, :, hs].reshape(n_cols, HEAD_DIM).astype(BF16)
        vv = vv_ref[:, :, hs].reshape(n_cols, HEAD_DIM).astype(BF16)
        s = lax.dot_general(qs_ref[...].astype(BF16), kk, (((1,), (1,)), ((), ())), preferred_element_type=F32)
        sg = s * SCALE - slope * distf + mask
        o = _dot(_sink_softmax_rows(sg, sink).astype(BF16), vv)
        for g in range(GQA_GROUP):
            h = kv * GQA_GROUP + g
            for t in range(t_new):
                cat_ref[t, :, D_CONV + h * HEAD_DIM:D_CONV + (h + 1) * HEAD_DIM] = o[blk(g, t)].astype(cat_ref.dtype)


def _mixer_sample(sinks, u, state, q, k, v, ck, cv, cw, cb, lg, lb):
    t_new, b, _ = u.shape
    hist = state.shape[0]
    w_past = ck.shape[2]
    n_seq = SAMPLE_SEQS
    per = lambda r, w: pl.BlockSpec((r, n_seq, w), lambda bi: (0, bi, 0))
    cache = pl.BlockSpec((None, n_seq, w_past, N_KV_HEADS, HEAD_DIM), lambda bi: (0, bi, 0, 0, 0))
    body = functools.partial(_mixer_sample_body, t_new=t_new)
    return pl.pallas_call(
        body,
        grid=(b // n_seq,),
        in_specs=[pl.BlockSpec(memory_space=pltpu.SMEM),
                  per(t_new, D_CONV), per(hist, D_CONV), per(t_new, D_ATTN), per(t_new, D_KV), per(t_new, D_KV),
                  cache, cache,
                  _resident((N_CCHUNK, CONV_WIDTH, LANES)), _resident((N_CCHUNK, 1, LANES)),
                  _resident((N_CCHUNK, 1, LANES)), _resident((N_CCHUNK, 1, LANES))],
        out_specs=[per(t_new, D_MODEL), per(hist, D_CONV), cache, cache],
        out_shape=[jax.ShapeDtypeStruct((t_new, b, D_MODEL), F32),
                   jax.ShapeDtypeStruct(state.shape, F32),
                   jax.ShapeDtypeStruct(ck.shape, F32),
                   jax.ShapeDtypeStruct(cv.shape, F32)],
        scratch_shapes=[pltpu.VMEM((n_seq, w_past + KEY_PAD, D_KV), F32),
                        pltpu.VMEM((n_seq, w_past + KEY_PAD, D_KV), F32),
                        pltpu.VMEM((n_seq * GQA_GROUP * t_new, HEAD_DIM), F32)],
        compiler_params=_params(1),
        name="mixer_sample",
    )(sinks, u, state, q, k, v, ck, cv, cw, cb, lg, lb)


OUT_CHUNK = 512
OUT_TILE = 256
D_HALF = D_MODEL // 2
U32 = jnp.uint32


def _pack_bf16_pairs(hi, lo):
    hi_bits = lax.bitcast_convert_type(hi.astype(BF16).astype(F32), U32)
    lo_bits = lax.bitcast_convert_type(lo.astype(BF16).astype(F32), U32)
    return hi_bits | (lo_bits >> 16)


def _unpack_bf16_pairs(words):
    hi = lax.bitcast_convert_type(words & U32(0xFFFF0000), F32)
    lo = lax.bitcast_convert_type(words << 16, F32)
    return hi, lo


def _out_proj_body(catc_ref, cata_ref, cats_ref, xp_ref, xs_ref, wo_hbm, g2_ref, wr_ref, x1_ref, h2p_ref, lg_ref,
                   h_ref, wo_ref, wstage, wsem, *, n_prompt_tiles):
    is_prompt = pl.program_id(0) < n_prompt_tiles
    tm = catc_ref.shape[0]

    @pl.when(pl.program_id(0) == 0)
    def _():
        _load_weight_as_bf16(wo_hbm, wo_ref, wstage, wsem)

    def sample_rows(v):
        return jnp.concatenate([v, jnp.zeros((tm - v.shape[0], v.shape[1]), v.dtype)], axis=0)

    cat = jnp.where(is_prompt, jnp.concatenate([catc_ref[...], cata_ref[...]], axis=1),
                    sample_rows(cats_ref[...].astype(BF16)))
    ss = jnp.zeros((tm, 1), F32)
    for c in range(D_MODEL // OUT_CHUNK):
        cs = slice(c * OUT_CHUNK, (c + 1) * OUT_CHUNK)
        y = jnp.where(is_prompt, xp_ref[:, cs], sample_rows(xs_ref[:, cs])) + _dot(cat, wo_ref[:, cs])
        x1_ref[:, cs] = y
        ss = ss + jnp.sum(y * y, axis=-1, keepdims=True)
    r = lax.rsqrt(ss / D_MODEL + EPS)
    for c in range(D_MODEL // OUT_CHUNK):
        cs = slice(c * OUT_CHUNK, (c + 1) * OUT_CHUNK)
        h_ref[:, cs] = x1_ref[:, cs] * r * g2_ref[:, cs]
    for c in range(D_HALF // OUT_CHUNK):
        cs = slice(c * OUT_CHUNK, (c + 1) * OUT_CHUNK)
        cs_lo = slice(D_HALF + c * OUT_CHUNK, D_HALF + (c + 1) * OUT_CHUNK)
        h2p_ref[:, cs] = _pack_bf16_pairs(h_ref[:, cs], h_ref[:, cs_lo])
    lg = _dot(h_ref[...].astype(BF16), wr_ref[...])
    for j in range(tm // ROUTE_CHUNK):
        lg_ref[j] = lg[j * ROUTE_CHUNK:(j + 1) * ROUTE_CHUNK, :].T[0:ROUTER_ROWS, :]


def _out_proj(cat_conv, cat_attn, cat_s, xp2, xs2, w_out, g2, wr_bf, tm):
    n_prompt_tiles = cat_conv.shape[0] // tm
    n_rows = (n_prompt_tiles + 1) * tm
    n_s = cat_s.shape[0]
    prompt = lambda w: pl.BlockSpec((tm, w), lambda i: (jnp.minimum(i, n_prompt_tiles - 1), 0))
    out_row = lambda w: pl.BlockSpec((tm, w), lambda i: (i, 0))
    body = functools.partial(_out_proj_body, n_prompt_tiles=n_prompt_tiles)
    return pl.pallas_call(
        body,
        grid=(n_prompt_tiles + 1,),
        in_specs=[prompt(D_CONV), prompt(D_ATTN), _resident((n_s, D_MODEL)), prompt(D_MODEL),
                  _resident((n_s, D_MODEL)),
                  pl.BlockSpec(memory_space=pl.ANY), _resident((1, D_MODEL)), _resident((D_MODEL, LANES))],
        out_specs=[out_row(D_MODEL), out_row(D_HALF),
                   pl.BlockSpec((tm // ROUTE_CHUNK, ROUTER_ROWS, ROUTE_CHUNK), lambda i: (i, 0, 0))],
        out_shape=[jax.ShapeDtypeStruct((n_rows, D_MODEL), F32), jax.ShapeDtypeStruct((n_rows, D_HALF), U32),
                   jax.ShapeDtypeStruct((n_rows // ROUTE_CHUNK, ROUTER_ROWS, ROUTE_CHUNK), F32)],
        scratch_shapes=[pltpu.VMEM((tm, D_MODEL), F32), pltpu.VMEM((D_MODEL, D_MODEL), BF16),
                        pltpu.VMEM((2, D_MODEL, CAST_COLS), F32), pltpu.SemaphoreType.DMA((2,))],
        compiler_params=_params(1),
        name="out_proj",
    )(cat_conv, cat_attn, cat_s, xp2, xs2, w_out, g2, wr_bf)


ROUTE_CHUNK = 128
ROUTE_UNROLL = 5


ROUTER_ROWS = 40


def _blk_start(blk_ref, e):
    return blk_ref[EXPERT_LANE0 + e, 0]


def _blk_count(blk_ref, e):
    return blk_ref[ROUTER_ROWS + EXPERT_LANE0 + e, 0]


def _route_body(lg_ref, bias_ref, slot_ref, gate_ref, blk_ref, cum_ref, sel_ref, *, n_tok):
    n_chunks = n_tok // ROUTE_CHUNK
    row = lax.broadcasted_iota(I32, (ROUTER_ROWS, ROUTE_CHUNK), 0).astype(F32)
    ri = lax.broadcasted_iota(I32, (ROUTE_CHUNK, ROUTE_CHUNK), 0)
    ci = lax.broadcasted_iota(I32, (ROUTE_CHUNK, ROUTE_CHUNK), 1)
    earlier_tok = jnp.where(ri < ci, 1.0, 0.0).astype(BF16)
    er = lax.broadcasted_iota(I32, (ROUTER_ROWS, ROUTER_ROWS), 0)
    ec = lax.broadcasted_iota(I32, (ROUTER_ROWS, ROUTER_ROWS), 1)
    earlier_row = jnp.where(ec < er, 1.0, 0.0).astype(BF16)
    sub = lax.broadcasted_iota(I32, (SUBLANES, ROUTE_CHUNK), 0)
    is_group = row < N_EXPERT_GROUPS

    def first_max(vals):
        m = jnp.max(vals, axis=0, keepdims=True)
        idx = jnp.min(jnp.where(vals == m, row, float(ROUTER_ROWS)), axis=0, keepdims=True)
        return m, idx

    def pair_rows(a, b):
        return jnp.where(sub == 0, a, jnp.where(sub == 1, b, 0.0))

    def assign(i, carry):
        l = lg_ref[i] + bias_ref[...]
        gl = jnp.where(is_group, l, NEG_INF)
        g_max, g_idx = first_max(gl)
        g_top = 1.0 / jnp.sum(jnp.exp(gl - g_max), axis=0, keepdims=True)
        lo = EXPERT_LANE0 + g_idx * EXPERTS_PER_GROUP
        el = jnp.where(row >= lo, jnp.where(row < lo + EXPERTS_PER_GROUP, l, NEG_INF), NEG_INF)
        m1, i1 = first_max(el)
        p = jnp.exp(el - m1)
        probs = p / jnp.sum(p, axis=0, keepdims=True)
        e1 = jnp.sum(jnp.where(row == i1, probs, 0.0), axis=0, keepdims=True)
        _, i2 = first_max(jnp.where(row == i1, NEG_INF, el))
        e2 = jnp.sum(jnp.where(row == i2, probs, 0.0), axis=0, keepdims=True)
        gate_ref[i] = pair_rows(g_top * e1 / (e1 + e2), g_top * e2 / (e1 + e2))
        sel_ref[i] = pair_rows(i1, i2)
        onehot = jnp.where(row == i1, 1.0, jnp.where(row == i2, 1.0, 0.0))
        cum_ref[i] = _dot(onehot.astype(BF16), earlier_tok) + carry
        return carry + jnp.sum(onehot, axis=1, keepdims=True)

    counts = lax.fori_loop(0, n_chunks, assign, jnp.zeros((ROUTER_ROWS, 1), F32), unroll=ROUTE_UNROLL)
    n_blocks = jnp.floor((counts + (MOE_BLOCK - 1)) / MOE_BLOCK)
    blk_start = _dot(earlier_row, jnp.broadcast_to(n_blocks, (ROUTER_ROWS, LANES)).astype(BF16))
    row_start = blk_start * MOE_BLOCK

    def place(i, carry):
        pos = cum_ref[i] + row_start
        sel = sel_ref[i]
        s1 = jnp.sum(jnp.where(row == sel[0:1, :], pos, 0.0), axis=0, keepdims=True)
        s2 = jnp.sum(jnp.where(row == sel[1:2, :], pos, 0.0), axis=0, keepdims=True)
        slot_ref[0, i] = s1.astype(I32)
        slot_ref[1, i] = s2.astype(I32)
        return carry

    lax.fori_loop(0, n_chunks, place, 0, unroll=ROUTE_UNROLL)

    blk_ref[0:ROUTER_ROWS, :] = blk_start.astype(I32)
    blk_ref[ROUTER_ROWS:2 * ROUTER_ROWS, :] = jnp.broadcast_to(counts, (ROUTER_ROWS, LANES)).astype(I32)


def _route(logits, bias, n_tok):
    body = functools.partial(_route_body, n_tok=n_tok)
    n_chunks = n_tok // ROUTE_CHUNK
    chunked = lambda r: pl.BlockSpec((n_chunks, r, LANES), lambda i: (0, 0, 0))
    table = pl.BlockSpec((2 * ROUTER_ROWS, LANES), lambda i: (0, 0))
    return pl.pallas_call(
        body,
        grid=(1,),
        in_specs=[chunked(ROUTER_ROWS), pl.BlockSpec((ROUTER_ROWS, LANES), lambda i: (0, 0))],
        out_specs=[pl.BlockSpec((2, n_chunks, 1, LANES), lambda i: (0, 0, 0, 0)), chunked(SUBLANES), table],
        out_shape=[jax.ShapeDtypeStruct((2, n_chunks, 1, LANES), I32),
                   jax.ShapeDtypeStruct((n_chunks, SUBLANES, LANES), F32),
                   jax.ShapeDtypeStruct((2 * ROUTER_ROWS, LANES), I32)],
        scratch_shapes=[pltpu.VMEM((n_chunks, ROUTER_ROWS, LANES), F32),
                        pltpu.VMEM((n_chunks, SUBLANES, LANES), F32)],
        compiler_params=_params(1),
        name="route",
    )(logits, bias)


INVERT_UNROLL = 16


CLEAR_SPAN = 8


def _invert_body(slot_ref, blk_ref, tok_ref, *, n_slots):
    n_tok = slot_ref.shape[0] // 2

    def clear_expert(e, c):
        first_pad = _blk_start(blk_ref, e) * MOE_BLOCK + _blk_count(blk_ref, e)
        end = jnp.where(e == N_EXPERTS - 1, n_slots, _blk_start(blk_ref, e + 1) * MOE_BLOCK)
        span_bits = CLEAR_SPAN.bit_length() - 1
        lo = lax.shift_right_logical(first_pad, span_bits) * CLEAR_SPAN

        def span(j, c2):
            for k in range(CLEAR_SPAN):
                tok_ref[lo + j * CLEAR_SPAN + k] = 0
            return c2

        lax.fori_loop(0, lax.shift_right_logical(end - lo, span_bits), span, 0)
        return c

    lax.fori_loop(0, N_EXPERTS, clear_expert, 0)

    def put(t, c):
        tok_ref[slot_ref[t]] = t
        tok_ref[slot_ref[n_tok + t]] = t
        return c

    lax.fori_loop(0, n_tok, put, 0, unroll=INVERT_UNROLL)


def _invert(slots, blk, n_slots):
    smem = pl.BlockSpec(memory_space=pltpu.SMEM)
    return pl.pallas_call(
        functools.partial(_invert_body, n_slots=n_slots),
        in_specs=[smem, smem],
        out_specs=smem,
        out_shape=jax.ShapeDtypeStruct((n_slots,), I32),
        name="invert",
    )(slots, blk)


GATHER_AHEAD = 2
GATHER_SLOTS = GATHER_AHEAD + 1


def _row_gather_start(idx_ref, base, src_hbm, dst, sem, n_rows, priorities):
    for r in range(n_rows):
        tok = idx_ref[base + r]
        pltpu.make_async_copy(src_hbm.at[pl.ds(tok, 1), :], dst.at[pl.ds(r, 1), :], sem).start(
            priority=priorities[r % len(priorities)])


def _experts_body(bstart_ref, tok_ref, h2p_hbm, w1_ref, w3_ref, w2_ref, eo_hbm, xbuf, obuf, w1b, w3b, w2b, gsem,
                  osem, *, n_blocks):
    e = pl.program_id(0)
    n_exp = pl.num_programs(0)
    b0 = _blk_start(bstart_ref, e)
    nb = _blk_start(bstart_ref, e + 1) - b0
    n_used = _blk_start(bstart_ref, n_exp)

    def gather(block):
        s = lax.rem(block, GATHER_SLOTS)
        src_block = jnp.minimum(block, n_used - 1)
        _row_gather_start(tok_ref, src_block * MOE_BLOCK, h2p_hbm, xbuf.at[s], gsem.at[s], MOE_BLOCK, (1, 0))

    def gather_wait(block):
        s = lax.rem(block, GATHER_SLOTS)
        pltpu.make_async_copy(xbuf.at[s], xbuf.at[s], gsem.at[s]).wait()

    def out_copy(block, s):
        rows = pl.ds(pl.multiple_of(block * MOE_BLOCK, MOE_BLOCK), MOE_BLOCK)
        return pltpu.make_async_copy(obuf.at[s], eo_hbm.at[rows, :], osem.at[s])

    @pl.when(e == 0)
    def _():
        for k in range(GATHER_AHEAD):
            gather(k)

    @pl.when(nb > 0)
    def _():
        w1b[...] = w1_ref[...].astype(BF16)
        w3b[...] = w3_ref[...].astype(BF16)
        w2b[...] = w2_ref[...].astype(BF16)

    def block(j, carry):
        b = b0 + j
        s = lax.rem(b, 2)
        gather(b + GATHER_AHEAD)
        gather_wait(b)
        hi, lo = _unpack_bf16_pairs(xbuf[lax.rem(b, GATHER_SLOTS)])
        x = jnp.concatenate([hi.astype(BF16), lo.astype(BF16)], axis=1)
        a = _dot(x, w1b[...])
        g = _dot(x, w3b[...])
        hdn = (a * jax.nn.sigmoid(a) * g).astype(BF16)
        o = _dot(hdn, w2b[...])

        @pl.when(b >= 2)
        def _():
            out_copy(b - 2, s).wait()

        obuf[s] = _pack_bf16_pairs(o[:, :D_HALF], o[:, D_HALF:])
        out_copy(b, s).start()
        return carry

    lax.fori_loop(0, nb, block, 0)

    @pl.when(e == n_exp - 1)
    def _():
        for k in range(GATHER_AHEAD):
            gather_wait(n_used + k)

        @pl.when(n_used >= 2)
        def _():
            out_copy(n_used - 2, lax.rem(n_used, 2)).wait()

        out_copy(n_used - 1, lax.rem(n_used - 1, 2)).wait()
        obuf[0] = jnp.zeros((MOE_BLOCK, D_HALF), U32)

        def fill(tb, carry):
            out_copy(tb, 0).start()
            return carry

        lax.fori_loop(n_used, n_blocks, fill, 0)

        def drain(tb, carry):
            out_copy(tb, 0).wait()
            return carry

        lax.fori_loop(n_used, n_blocks, drain, 0)


def _experts(bstart, tok, h2p, w1, w3, w2, n_blocks):
    def wspec(r, c):
        return pl.BlockSpec((None, r, c), lambda e, bstart_ref, tok_ref: (e, 0, 0))

    grid_spec = pltpu.PrefetchScalarGridSpec(
        num_scalar_prefetch=2,
        grid=(N_EXPERTS,),
        in_specs=[pl.BlockSpec(memory_space=pl.ANY),
                  wspec(D_MODEL, D_EXPERT), wspec(D_MODEL, D_EXPERT), wspec(D_EXPERT, D_MODEL)],
        out_specs=pl.BlockSpec(memory_space=pl.ANY),
        scratch_shapes=[pltpu.VMEM((GATHER_SLOTS, MOE_BLOCK, D_HALF), U32), pltpu.VMEM((2, MOE_BLOCK, D_HALF), U32),
                        pltpu.VMEM((D_MODEL, D_EXPERT), BF16), pltpu.VMEM((D_MODEL, D_EXPERT), BF16),
                        pltpu.VMEM((D_EXPERT, D_MODEL), BF16),
                        pltpu.SemaphoreType.DMA((GATHER_SLOTS,)), pltpu.SemaphoreType.DMA((2,))],
    )
    return pl.pallas_call(
        functools.partial(_experts_body, n_blocks=n_blocks),
        grid_spec=grid_spec,
        out_shape=jax.ShapeDtypeStruct((n_blocks * MOE_BLOCK, D_HALF), U32),
        compiler_params=_params(1),
        name="experts",
    )(bstart, tok, h2p, w1, w3, w2)


def _combine_body(slot_ref, x1_ref, gate_ref, eo_hbm, yp_ref, ys_ref, buf, sem, *, n_tok, n_prompt_tiles):
    i = pl.program_id(0)
    tm = x1_ref.shape[0]
    n_tiles = n_tok // tm
    slot = lax.rem(i, GATHER_SLOTS)

    def start(tile):
        s = lax.rem(tile, GATHER_SLOTS)
        base = jnp.minimum(tile, n_tiles - 1) * tm
        _row_gather_start(slot_ref, base, eo_hbm, buf.at[s, 0], sem.at[s], tm, (0, 1))
        _row_gather_start(slot_ref, n_tok + base, eo_hbm, buf.at[s, 1], sem.at[s], tm, (0, 1))

    def wait(tile):
        s = lax.rem(tile, GATHER_SLOTS)
        pltpu.make_async_copy(buf.at[s], buf.at[s], sem.at[s]).wait()

    @pl.when(i == 0)
    def _():
        for k in range(GATHER_AHEAD):
            start(k)

    start(i + GATHER_AHEAD)
    wait(i)

    @pl.when(i == n_tiles - 1)
    def _():
        for k in range(GATHER_AHEAD):
            wait(n_tiles + k)

    diag = lax.broadcasted_iota(I32, (tm, tm), 0) == lax.broadcasted_iota(I32, (tm, tm), 1)
    g0 = jnp.sum(jnp.where(diag, gate_ref[0:1, :], 0.0), axis=1, keepdims=True)
    g1 = jnp.sum(jnp.where(diag, gate_ref[1:2, :], 0.0), axis=1, keepdims=True)
    hi0, lo0 = _unpack_bf16_pairs(buf[slot, 0])
    hi1, lo1 = _unpack_bf16_pairs(buf[slot, 1])
    y_hi = x1_ref[:, :D_HALF] + g0 * hi0 + g1 * hi1
    y_lo = x1_ref[:, D_HALF:] + g0 * lo0 + g1 * lo1

    @pl.when(i < n_prompt_tiles)
    def _():
        yp_ref[:, :D_HALF] = y_hi
        yp_ref[:, D_HALF:] = y_lo

    @pl.when(i >= n_prompt_tiles)
    def _():
        ys_ref[:, :D_HALF] = y_hi
        ys_ref[:, D_HALF:] = y_lo


def _combine(slots, x1, gates, eo, n_prompt, n_tok, tm):
    n_tiles = n_tok // tm
    n_prompt_tiles = n_prompt // tm
    body = functools.partial(_combine_body, n_tok=n_tok, n_prompt_tiles=n_prompt_tiles)
    grid_spec = pltpu.PrefetchScalarGridSpec(
        num_scalar_prefetch=1,
        grid=(n_tiles,),
        in_specs=[pl.BlockSpec((tm, D_MODEL), lambda i, s: (i, 0)),
                  pl.BlockSpec((None, SUBLANES, tm), lambda i, s: (i, 0, 0)),
                  pl.BlockSpec(memory_space=pl.ANY)],
        out_specs=[pl.BlockSpec((tm, D_MODEL), lambda i, s: (jnp.minimum(i, n_prompt_tiles - 1), 0)),
                   pl.BlockSpec((tm, D_MODEL), lambda i, s: (jnp.maximum(i - n_prompt_tiles, 0), 0))],
        scratch_shapes=[pltpu.VMEM((GATHER_SLOTS, 2, tm, D_HALF), U32), pltpu.SemaphoreType.DMA((GATHER_SLOTS,))],
    )
    return pl.pallas_call(
        body,
        grid_spec=grid_spec,
        out_shape=[jax.ShapeDtypeStruct((n_prompt, D_MODEL), F32),
                   jax.ShapeDtypeStruct((n_tok - n_prompt, D_MODEL), F32)],
        compiler_params=_params(1),
        name="combine",
    )(slots, x1, gates, eo)


def _layer(x_prompt, x_sample, state_conv, cache_k, cache_v, norm1_g, w_in, conv_w, conv_b, conv_norm_g,
           conv_norm_b, q_norm_g, k_norm_g, attn_sinks, w_out, norm2_g, w_rg, b_rg, w_re, b_re, w1, w3, w2):
    b, t, _ = x_prompt.shape
    sb, st, _ = x_sample.shape
    n_p, n_s = b * t, sb * st
    n_tok = n_p + n_s

    g1 = norm1_g.reshape(1, D_MODEL)
    g2 = norm2_g.reshape(1, D_MODEL)
    qg = q_norm_g.reshape(1, HEAD_DIM)
    kg = k_norm_g.reshape(1, HEAD_DIM)
    chunked = lambda a: a.reshape(-1, N_CCHUNK, LANES).transpose(1, 0, 2)
    cw, cb, lg, lb = chunked(conv_w), chunked(conv_b), chunked(conv_norm_g), chunked(conv_norm_b)
    pad_rows = ROUTER_ROWS - N_EXPERT_GROUPS - N_EXPERTS
    pad_lanes = LANES - N_EXPERT_GROUPS - N_EXPERTS
    w_router_bf = jnp.concatenate([w_rg, w_re, jnp.zeros((D_MODEL, pad_lanes), F32)], axis=1).astype(BF16)
    b_router = jnp.broadcast_to(jnp.concatenate([b_rg, b_re, jnp.zeros((pad_rows,), F32)])[:, None],
                                (ROUTER_ROWS, LANES))

    xp2 = x_prompt.reshape(n_p, D_MODEL)
    xs2 = x_sample.transpose(1, 0, 2).reshape(n_s, D_MODEL)

    c_p, q_p, k_p, v_p, conv_p, knew_p, vnew_p, w_in_bf = _in_proj_conv(xp2, g1, w_in, qg, kg, cw, cb, lg, lb, 512, b)
    u_s, q_s, k_s, v_s = _in_proj(xs2, g1, w_in_bf, qg, kg, n_s, F32)

    r3 = lambda a, bb: a.reshape(bb, -1, a.shape[-1])
    a_p = _mixer_prompt(attn_sinks, r3(q_p, b), r3(k_p, b), r3(v_p, b), 512)
    cat_s, conv_s, knew_s, vnew_s = _mixer_sample(
        attn_sinks, r3(u_s, st), state_conv[0].transpose(1, 0, 2), r3(q_s, st), r3(k_s, st), r3(v_s, st),
        cache_k, cache_v, cw, cb, lg, lb)
    conv_s = conv_s.transpose(1, 0, 2)[None]

    x1, h2p, logits = _out_proj(c_p, a_p.reshape(n_p, D_ATTN), cat_s.reshape(n_s, D_MODEL), xp2, xs2, w_out, g2,
                                w_router_bf, OUT_TILE)

    n_blocks = -(-(n_tok * 2) // MOE_BLOCK) + N_EXPERTS
    slots, gates, blk = _route(logits, b_router, n_tok)
    slots = slots.reshape(2 * n_tok)
    tok = _invert(slots, blk, n_blocks * MOE_BLOCK)
    eo = _experts(blk, tok, h2p, w1, w3, w2, n_blocks)
    y_p, y_s = _combine(slots, x1, gates, eo, n_p, n_tok, COMBINE_TILE)

    return (y_p.reshape(b, t, D_MODEL), y_s.reshape(st, sb, D_MODEL).transpose(1, 0, 2), conv_p[None], knew_p[None],
            vnew_p[None], conv_s, knew_s, vnew_s)


def kernel(x_prompt, x_sample, state_conv, cache_k, cache_v, norm1_g, w_in, conv_w, conv_b, conv_norm_g, conv_norm_b, q_norm_g, k_norm_g, attn_sinks, w_out, norm2_g, w_router_group, b_router_group, w_router_expert, b_router_expert, w1, w3, w2):
    depth = w_in.shape[0]
    assert depth == 1, "single-layer step"
    return _layer(x_prompt, x_sample, state_conv, cache_k, cache_v, norm1_g[0], w_in[0], conv_w[0],
                  conv_b[0], conv_norm_g[0], conv_norm_b[0], q_norm_g[0], k_norm_g[0], attn_sinks[0], w_out[0],
                  norm2_g[0], w_router_group[0], b_router_group[0], w_router_expert[0], b_router_expert[0],
                  w1[0], w3[0], w2[0])
```

```python
import functools
import math

import jax
import jax.numpy as jnp
from jax import lax
from jax.experimental import pallas as pl
from jax.experimental.pallas import tpu as pltpu

F32 = jnp.float32
BF16 = jnp.bfloat16
I32 = jnp.int32

D_MODEL = 2048
D_CONV = 1024
CONV_WIDTH = 31
CONV_HIST = CONV_WIDTH - 1
D_ATTN = 1024
HEAD_DIM = 128
N_HEADS = 8
N_KV_HEADS = 2
GQA_GROUP = N_HEADS // N_KV_HEADS
D_KV = N_KV_HEADS * HEAD_DIM
WINDOW = 128
BLOCK_Q = 128
SCALE = 1.0 / math.sqrt(HEAD_DIM)
N_EXPERT_GROUPS = 4
EXPERTS_PER_GROUP = 8
N_EXPERTS = N_EXPERT_GROUPS * EXPERTS_PER_GROUP
D_EXPERT = 512
MOE_BLOCK = 256
COMBINE_TILE = 128
D_IN = 2 * D_CONV + D_ATTN + 2 * D_KV
EPS = 1e-6
PAST_LEN = 16384

LANES = 128
SUBLANES = 8
MXU_COLS = 256
VMEM_LIMIT_BYTES = 56 * 1024 * 1024
NEG_INF = float("-inf")
EXPERT_LANE0 = N_EXPERT_GROUPS


def _params(n_axes):
    return pltpu.CompilerParams(dimension_semantics=("arbitrary",) * n_axes,
                                vmem_limit_bytes=VMEM_LIMIT_BYTES)


def _resident(shape):
    nd = len(shape)
    return pl.BlockSpec(shape, lambda *_: (0,) * nd, pipeline_mode=pl.Buffered(1))


def _dot(a, b):
    return jnp.dot(a, b, preferred_element_type=F32)


def _in_proj_body(x_ref, g1_ref, w_ref, qg_ref, kg_ref, u_ref, q_ref, k_ref, v_ref, n_ref):
    x = x_ref[...]
    ms = jnp.mean(x * x, axis=-1, keepdims=True)
    n_ref[...] = (x * lax.rsqrt(ms + EPS) * g1_ref[...]).astype(BF16)

    def head_norm(h, g):
        return h * lax.rsqrt(jnp.mean(h * h, axis=-1, keepdims=True) + EPS) * g

    ch = MXU_COLS
    for c in range(D_CONV // ch):
        a = _dot(n_ref[...], w_ref[:, c * ch:(c + 1) * ch])
        g = _dot(n_ref[...], w_ref[:, D_CONV + c * ch:D_CONV + (c + 1) * ch])
        u_ref[:, c * ch:(c + 1) * ch] = a * jax.nn.sigmoid(g)
    q_off = 2 * D_CONV
    for c in range(D_ATTN // ch):
        qq = _dot(n_ref[...], w_ref[:, q_off + c * ch:q_off + (c + 1) * ch])
        for j in range(ch // HEAD_DIM):
            qh = head_norm(qq[:, j * HEAD_DIM:(j + 1) * HEAD_DIM], qg_ref[...])
            q_ref[:, c * ch + j * HEAD_DIM:c * ch + (j + 1) * HEAD_DIM] = qh.astype(q_ref.dtype)
    k_off = q_off + D_ATTN
    kk = _dot(n_ref[...], w_ref[:, k_off:k_off + D_KV])
    for j in range(N_KV_HEADS):
        k_ref[:, j * HEAD_DIM:(j + 1) * HEAD_DIM] = head_norm(kk[:, j * HEAD_DIM:(j + 1) * HEAD_DIM], kg_ref[...])
    v_ref[...] = _dot(n_ref[...], w_ref[:, k_off + D_KV:k_off + 2 * D_KV])


CAST_COLS = 512


def _load_weight_as_bf16(w_hbm, w_bf, stage, sem):
    n_chunks = w_hbm.shape[1] // CAST_COLS

    def copy(c):
        return pltpu.make_async_copy(w_hbm.at[:, c * CAST_COLS:(c + 1) * CAST_COLS], stage.at[c % 2], sem.at[c % 2])

    copy(0).start()
    for c in range(n_chunks):
        if c + 1 < n_chunks:
            copy(c + 1).start()
        copy(c).wait()
        w_bf[:, c * CAST_COLS:(c + 1) * CAST_COLS] = stage[c % 2].astype(BF16)


def _in_proj_conv_body(x_ref, g1_ref, w_hbm, qg_ref, kg_ref, cw_ref, cb_ref, lg_ref, lb_ref, c_ref, q_ref, k_ref,
                       v_ref, ut_ref, kt_ref, vt_ref, wout_hbm, n_ref, ue_ref, conv_ref, w_ref, wstage, wsem, osem, *,
                       tiles_per_seq):
    tm = x_ref.shape[0]
    step = pl.program_id(0)
    first = lax.rem(step, tiles_per_seq) == 0
    w_out_copy = pltpu.make_async_copy(w_ref, wout_hbm, osem.at[0])

    @pl.when(step == 0)
    def _():
        _load_weight_as_bf16(w_hbm, w_ref, wstage, wsem)
        w_out_copy.start()

    @pl.when(step == pl.num_programs(0) - 1)
    def _():
        w_out_copy.wait()

    @pl.when(first)
    def _():
        for c in range(N_CCHUNK):
            ue_ref[c, 0:CONV_HALO, :] = jnp.zeros((CONV_HALO, LANES), F32)

    @pl.when(jnp.logical_not(first))
    def _():
        for c in range(N_CCHUNK):
            ue_ref[c, 0:CONV_HALO, :] = ue_ref[c, tm:tm + CONV_HALO, :]

    x = x_ref[...]
    ms = jnp.mean(x * x, axis=-1, keepdims=True)
    n_ref[...] = (x * lax.rsqrt(ms + EPS) * g1_ref[...]).astype(BF16)

    def head_norm(h, g):
        return h * lax.rsqrt(jnp.mean(h * h, axis=-1, keepdims=True) + EPS) * g

    ch = MXU_COLS
    lanes_per = ch // LANES
    n_glu = D_CONV // ch
    q_off = 2 * D_CONV

    row_blocks = [slice(r, r + DOT_ROWS) for r in range(0, tm, DOT_ROWS)]

    for c in range(n_glu):
        for rows in row_blocks:
            a = _dot(n_ref[rows, :], w_ref[:, c * ch:(c + 1) * ch])
            g = _dot(n_ref[rows, :], w_ref[:, D_CONV + c * ch:D_CONV + (c + 1) * ch])
            u = a * jax.nn.sigmoid(g)
            for half in range(lanes_per):
                ue_ref[c * lanes_per + half, CONV_HALO + rows.start:CONV_HALO + rows.stop, :] = (
                    u[:, half * LANES:(half + 1) * LANES])
        for half in range(lanes_per):
            lc = c * lanes_per + half
            _conv_chunk(ue_ref, lc, cw_ref[lc], cb_ref[lc], conv_ref, tm)
    _ln_swish(conv_ref, lg_ref, lb_ref, c_ref, tm)

    for c in range(D_ATTN // ch):
        for rows in row_blocks:
            qq = _dot(n_ref[rows, :], w_ref[:, q_off + c * ch:q_off + (c + 1) * ch])
            for j in range(ch // HEAD_DIM):
                qh = head_norm(qq[:, j * HEAD_DIM:(j + 1) * HEAD_DIM], qg_ref[...])
                q_ref[rows, c * ch + j * HEAD_DIM:c * ch + (j + 1) * HEAD_DIM] = qh.astype(q_ref.dtype)
    k_off = q_off + D_ATTN
    for rows in row_blocks:
        kk = _dot(n_ref[rows, :], w_ref[:, k_off:k_off + D_KV])
        vv = _dot(n_ref[rows, :], w_ref[:, k_off + D_KV:k_off + 2 * D_KV])
        v_ref[rows, :] = vv
        for j in range(N_KV_HEADS):
            hs = slice(j * HEAD_DIM, (j + 1) * HEAD_DIM)
            k_ref[rows, hs] = head_norm(kk[:, hs], kg_ref[...])
    for j in range(N_KV_HEADS):
        hs = slice(j * HEAD_DIM, (j + 1) * HEAD_DIM)
        kt_ref[:, j, :] = k_ref[tm - WINDOW:tm, hs]
        vt_ref[:, j, :] = v_ref[tm - WINDOW:tm, hs]
    for c in range(N_CCHUNK):
        ut_ref[:, c * LANES:(c + 1) * LANES] = ue_ref[c, CONV_HALO + tm - CONV_HIST:CONV_HALO + tm, :]


def _in_proj_conv(x2, g1, w_in, qg, kg, cw, cb, lg, lb, tm, n_seq):
    n = x2.shape[0]
    tiles_per_seq = n // n_seq // tm
    row = lambda w: pl.BlockSpec((tm, w), lambda i: (i, 0))
    seq = lambda *dims: pl.BlockSpec((None,) + dims, lambda i: (i // tiles_per_seq,) + (0,) * len(dims))
    hbm = pl.BlockSpec(memory_space=pl.ANY)
    body = functools.partial(_in_proj_conv_body, tiles_per_seq=tiles_per_seq)
    return pl.pallas_call(
        body,
        grid=(n // tm,),
        in_specs=[row(D_MODEL), _resident((1, D_MODEL)), hbm,
                  _resident((1, HEAD_DIM)), _resident((1, HEAD_DIM)),
                  _resident((N_CCHUNK, CONV_WIDTH, LANES)), _resident((N_CCHUNK, 1, LANES)),
                  _resident((N_CCHUNK, 1, LANES)), _resident((N_CCHUNK, 1, LANES))],
        out_specs=[row(D_CONV), row(D_ATTN), row(D_KV), row(D_KV),
                   seq(CONV_HIST, D_CONV), seq(WINDOW, N_KV_HEADS, HEAD_DIM), seq(WINDOW, N_KV_HEADS, HEAD_DIM), hbm],
        out_shape=[jax.ShapeDtypeStruct((n, D_CONV), BF16), jax.ShapeDtypeStruct((n, D_ATTN), BF16),
                   jax.ShapeDtypeStruct((n, D_KV), F32), jax.ShapeDtypeStruct((n, D_KV), F32),
                   jax.ShapeDtypeStruct((n_seq, CONV_HIST, D_CONV), F32),
                   jax.ShapeDtypeStruct((n_seq, WINDOW, N_KV_HEADS, HEAD_DIM), F32),
                   jax.ShapeDtypeStruct((n_seq, WINDOW, N_KV_HEADS, HEAD_DIM), F32),
                   jax.ShapeDtypeStruct((D_MODEL, D_IN), BF16)],
        scratch_shapes=[pltpu.VMEM((tm, D_MODEL), BF16),
                        pltpu.VMEM((N_CCHUNK, CONV_HALO + tm, LANES), F32),
                        pltpu.VMEM((N_CCHUNK, tm, LANES), F32),
                        pltpu.VMEM((D_MODEL, D_IN), BF16), pltpu.VMEM((2, D_MODEL, CAST_COLS), F32),
                        pltpu.SemaphoreType.DMA((2,)), pltpu.SemaphoreType.DMA((1,))],
        compiler_params=_params(1),
        name="in_proj_conv",
    )(x2, g1, w_in, qg, kg, cw, cb, lg, lb)


def _in_proj(x2, g1, w_in_bf, qg, kg, tm, q_dtype):
    n = x2.shape[0]
    row = lambda w: pl.BlockSpec((tm, w), lambda i: (i, 0))
    return pl.pallas_call(
        _in_proj_body,
        grid=(n // tm,),
        in_specs=[row(D_MODEL), _resident((1, D_MODEL)), _resident((D_MODEL, D_IN)),
                  _resident((1, HEAD_DIM)), _resident((1, HEAD_DIM))],
        out_specs=[row(D_CONV), row(D_ATTN), row(D_KV), row(D_KV)],
        out_shape=[jax.ShapeDtypeStruct((n, D_CONV), F32), jax.ShapeDtypeStruct((n, D_ATTN), q_dtype),
                   jax.ShapeDtypeStruct((n, D_KV), F32), jax.ShapeDtypeStruct((n, D_KV), F32)],
        scratch_shapes=[pltpu.VMEM((tm, D_MODEL), BF16)],
        compiler_params=_params(1),
        name="in_proj",
    )(x2, g1, w_in_bf, qg, kg)


N_CCHUNK = D_CONV // LANES
CONV_ROWS = 64


CONV_HALO = 32
DOT_ROWS = 256


def _conv_chunk(ue_ref, c, wc, bias, conv_ref, rows):
    base = CONV_HALO - CONV_HIST
    for r0 in range(0, rows, CONV_ROWS):
        acc = jnp.broadcast_to(bias, (CONV_ROWS, LANES))
        for tap in range(CONV_WIDTH):
            acc = acc + wc[tap:tap + 1, :] * ue_ref[c, base + r0 + tap:base + r0 + tap + CONV_ROWS, :]
        conv_ref[c, r0:r0 + CONV_ROWS, :] = acc


def _ln_swish(conv_ref, lg_ref, lb_ref, cat_ref, rows):
    tot = jnp.zeros((rows, 1), F32)
    for c in range(N_CCHUNK):
        tot = tot + jnp.sum(conv_ref[c], axis=-1, keepdims=True)
    mean = tot / D_CONV
    var = jnp.zeros((rows, 1), F32)
    for c in range(N_CCHUNK):
        xc = conv_ref[c] - mean
        var = var + jnp.sum(xc * xc, axis=-1, keepdims=True)
    rstd = lax.rsqrt(var / D_CONV + EPS)
    for c in range(N_CCHUNK):
        y = (conv_ref[c] - mean) * rstd * lg_ref[c] + lb_ref[c]
        cat_ref[:, c * LANES:(c + 1) * LANES] = (y * jax.nn.sigmoid(y)).astype(cat_ref.dtype)


def _sink_softmax_rows(s, sink):
    m = jnp.maximum(jnp.max(s, axis=-1, keepdims=True), sink)
    p = jnp.exp(s - m)
    return p / (jnp.sum(p, axis=-1, keepdims=True) + jnp.exp(sink - m))


def _alibi_slope(head):
    return 2.0 ** (-8.0 * (head + 1) / N_HEADS)


def _mixer_prompt_body(sink_ref, q_ref, k_ref, kh_ref, v_ref, vh_ref, cat_ref, *, tm):
    j = pl.program_id(1)
    has_prev = j > 0
    qi = lax.broadcasted_iota(I32, (BLOCK_Q, 2 * BLOCK_Q), 0)
    kj = lax.broadcasted_iota(I32, (BLOCK_Q, 2 * BLOCK_Q), 1)
    dist = qi + BLOCK_Q - kj
    distf = dist.astype(F32)
    band = jnp.where(dist >= 0, jnp.where(dist < WINDOW, 0.0, NEG_INF), NEG_INF)
    band_first = jnp.where(kj >= BLOCK_Q, band, jnp.where(has_prev, band, NEG_INF))

    for qb in range(tm // BLOCK_Q):
        rows = slice(qb * BLOCK_Q, (qb + 1) * BLOCK_Q)
        prev = slice((qb - 1) * BLOCK_Q, qb * BLOCK_Q)
        mask = band_first if qb == 0 else band
        for kv in range(N_KV_HEADS):
            hs = slice(kv * HEAD_DIM, (kv + 1) * HEAD_DIM)
            k_prev = kh_ref[:, hs] if qb == 0 else k_ref[prev, hs]
            v_prev = vh_ref[:, hs] if qb == 0 else v_ref[prev, hs]
            kk = jnp.concatenate([k_prev, k_ref[rows, hs]], axis=0).astype(BF16)
            vv = jnp.concatenate([v_prev, v_ref[rows, hs]], axis=0).astype(BF16)
            heads = [kv * GQA_GROUP + g for g in range(GQA_GROUP)]
            qs = jnp.concatenate([q_ref[rows, h * HEAD_DIM:(h + 1) * HEAD_DIM] for h in heads], axis=0)
            s = lax.dot_general(qs, kk, (((1,), (1,)), ((), ())), preferred_element_type=F32)
            ps = []
            for g, h in enumerate(heads):
                sg = s[g * BLOCK_Q:(g + 1) * BLOCK_Q] * SCALE - _alibi_slope(h) * distf + mask
                ps.append(_sink_softmax_rows(sg, sink_ref[h]).astype(BF16))
            o = _dot(jnp.concatenate(ps, axis=0), vv)
            for g, h in enumerate(heads):
                cat_ref[rows, h * HEAD_DIM:(h + 1) * HEAD_DIM] = o[g * BLOCK_Q:(g + 1) * BLOCK_Q].astype(cat_ref.dtype)


def _mixer_prompt(sinks, q, k, v, tm):
    b, t, _ = q.shape
    kpb = tm // BLOCK_Q
    main = lambda w: pl.BlockSpec((None, tm, w), lambda bi, j: (bi, j, 0))
    prev_block = pl.BlockSpec((None, BLOCK_Q, D_KV), lambda bi, j: (bi, jnp.maximum(j * kpb - 1, 0), 0))
    body = functools.partial(_mixer_prompt_body, tm=tm)
    return pl.pallas_call(
        body,
        grid=(b, t // tm),
        in_specs=[pl.BlockSpec(memory_space=pltpu.SMEM), main(D_ATTN), main(D_KV), prev_block, main(D_KV),
                  prev_block],
        out_specs=main(D_ATTN),
        out_shape=jax.ShapeDtypeStruct((b, t, D_ATTN), BF16),
        compiler_params=_params(2),
        name="mixer_prompt",
    )(sinks, q, k, k, v, v)


KEY_PAD = 8


SAMPLE_SEQS = 8


def _sample_conv_ln_swish(st_ref, u_ref, cw_ref, cb_ref, lg_ref, lb_ref, cat_ref, t_new):
    hist, n_seq, _ = st_ref.shape
    conv = [[None] * N_CCHUNK for _ in range(t_new)]
    for c in range(N_CCHUNK):
        cs = slice(c * LANES, (c + 1) * LANES)
        wc = cw_ref[c]
        pos = [st_ref[p, :, cs] for p in range(hist)] + [u_ref[t, :, cs] for t in range(t_new)]
        for t in range(t_new):
            acc = jnp.broadcast_to(cb_ref[c], (n_seq, LANES))
            for tap in range(CONV_WIDTH):
                acc = acc + wc[tap:tap + 1, :] * pos[t + tap]
            conv[t][c] = acc
    for t in range(t_new):
        tot = jnp.zeros((n_seq, 1), F32)
        for a in conv[t]:
            tot = tot + jnp.sum(a, axis=-1, keepdims=True)
        mean = tot / D_CONV
        var = jnp.zeros((n_seq, 1), F32)
        for a in conv[t]:
            var = var + jnp.sum((a - mean) * (a - mean), axis=-1, keepdims=True)
        rstd = lax.rsqrt(var / D_CONV + EPS)
        for c, a in enumerate(conv[t]):
            y = (a - mean) * rstd * lg_ref[c] + lb_ref[c]
            cat_ref[t, :, c * LANES:(c + 1) * LANES] = (y * jax.nn.sigmoid(y)).astype(cat_ref.dtype)


def _mixer_sample_body(sink_ref, u_ref, st_ref, q_ref, k_ref, v_ref, ck_ref, cv_ref, cw_ref, cb_ref, lg_ref,
                       lb_ref, cat_ref, nst_ref, nk_ref, nv_ref, kk_ref, vv_ref, qs_ref, *, t_new):
    hist, n_seq, _ = st_ref.shape
    w_past = ck_ref.shape[1]
    n_keys = w_past + KEY_PAD
    n_rows, n_cols = GQA_GROUP * t_new * n_seq, n_seq * n_keys

    _sample_conv_ln_swish(st_ref, u_ref, cw_ref, cb_ref, lg_ref, lb_ref, cat_ref, t_new)
    nst_ref[0:hist - t_new] = st_ref[t_new:hist]
    nst_ref[hist - t_new:hist] = u_ref[...]

    for i in range(n_seq):
        nk_ref[i, 0:w_past - t_new] = ck_ref[i, t_new:w_past]
        nv_ref[i, 0:w_past - t_new] = cv_ref[i, t_new:w_past]
        kk_ref[i, w_past:n_keys, :] = jnp.zeros((KEY_PAD, D_KV), F32)
        vv_ref[i, w_past:n_keys, :] = jnp.zeros((KEY_PAD, D_KV), F32)
        for t in range(t_new):
            kk_ref[i, w_past + t:w_past + t + 1, :] = k_ref[t, i:i + 1, :]
            vv_ref[i, w_past + t:w_past + t + 1, :] = v_ref[t, i:i + 1, :]
        for h in range(N_KV_HEADS):
            hs = slice(h * HEAD_DIM, (h + 1) * HEAD_DIM)
            kk_ref[i, 0:w_past, hs] = ck_ref[i, :, h, :]
            vv_ref[i, 0:w_past, hs] = cv_ref[i, :, h, :]
            for t in range(t_new):
                nk_ref[i, w_past - t_new + t, h:h + 1, :] = k_ref[t, i:i + 1, hs]
                nv_ref[i, w_past - t_new + t, h:h + 1, :] = v_ref[t, i:i + 1, hs]

    row = lax.broadcasted_iota(I32, (n_rows, n_cols), 0)
    col = lax.broadcasted_iota(I32, (n_rows, n_cols), 1)
    seq_bits, tok_bits = n_seq.bit_length() - 1, t_new.bit_length() - 1
    assert (1 << seq_bits, 1 << tok_bits) == (n_seq, t_new), "sequence and token counts must be powers of two"
    tok = (row >> seq_bits) & (t_new - 1)
    key = col - (row & (n_seq - 1)) * n_keys
    dist = tok + w_past - key
    distf = dist.astype(F32)
    mask = jnp.where(dist >= 0, jnp.where(dist < WINDOW, 0.0, NEG_INF), NEG_INF)
    row1 = lax.broadcasted_iota(I32, (n_rows, 1), 0)
    grp = row1 >> (seq_bits + tok_bits)
    blk = lambda g, t: slice((g * t_new + t) * n_seq, (g * t_new + t + 1) * n_seq)

    for kv in range(N_KV_HEADS):
        hs = slice(kv * HEAD_DIM, (kv + 1) * HEAD_DIM)
        slope = jnp.zeros((n_rows, 1), F32)
        sink = jnp.zeros((n_rows, 1), F32)
        for g in range(GQA_GROUP):
            h = kv * GQA_GROUP + g
            slope = jnp.where(grp == g, _alibi_slope(h), slope)
            sink = jnp.where(grp == g, sink_ref[h], sink)
            for t in range(t_new):
                qs_ref[blk(g, t), :] = q_ref[t, :, h * HEAD_DIM:(h + 1) * HEAD_DIM]
        kk = kk_ref[:, :, hs].reshape(n_cols, HEAD_DIM).astype(BF16)
        vv = vv_ref[:, :, hs].reshape(n_cols, HEAD_DIM).astype(BF16)
        s = lax.dot_general(qs_ref[...].astype(BF16), kk, (((1,), (1,)), ((), ())), preferred_element_type=F32)
        sg = s * SCALE - slope * distf + mask
        o = _dot(_sink_softmax_rows(sg, sink).astype(BF16), vv)
        for g in range(GQA_GROUP):
            h = kv * GQA_GROUP + g
            for t in range(t_new):
                cat_ref[t, :, D_CONV + h * HEAD_DIM:D_CONV + (h + 1) * HEAD_DIM] = o[blk(g, t)].astype(cat_ref.dtype)


def _mixer_sample(sinks, u, state, q, k, v, ck, cv, cw, cb, lg, lb):
    t_new, b, _ = u.shape
    hist = state.shape[0]
    w_past = ck.shape[2]
    n_seq = SAMPLE_SEQS
    per = lambda r, w: pl.BlockSpec((r, n_seq, w), lambda bi: (0, bi, 0))
    cache = pl.BlockSpec((None, n_seq, w_past, N_KV_HEADS, HEAD_DIM), lambda bi: (0, bi, 0, 0, 0))
    body = functools.partial(_mixer_sample_body, t_new=t_new)
    return pl.pallas_call(
        body,
        grid=(b // n_seq,),
        in_specs=[pl.BlockSpec(memory_space=pltpu.SMEM),
                  per(t_new, D_CONV), per(hist, D_CONV), per(t_new, D_ATTN), per(t_new, D_KV), per(t_new, D_KV),
                  cache, cache,
                  _resident((N_CCHUNK, CONV_WIDTH, LANES)), _resident((N_CCHUNK, 1, LANES)),
                  _resident((N_CCHUNK, 1, LANES)), _resident((N_CCHUNK, 1, LANES))],
        out_specs=[per(t_new, D_MODEL), per(hist, D_CONV), cache, cache],
        out_shape=[jax.ShapeDtypeStruct((t_new, b, D_MODEL), F32),
                   jax.ShapeDtypeStruct(state.shape, F32),
                   jax.ShapeDtypeStruct(ck.shape, F32),
                   jax.ShapeDtypeStruct(cv.shape, F32)],
        scratch_shapes=[pltpu.VMEM((n_seq, w_past + KEY_PAD, D_KV), F32),
                        pltpu.VMEM((n_seq, w_past + KEY_PAD, D_KV), F32),
                        pltpu.VMEM((n_seq * GQA_GROUP * t_new, HEAD_DIM), F32)],
        compiler_params=_params(1),
        name="mixer_sample",
    )(sinks, u, state, q, k, v, ck, cv, cw, cb, lg, lb)


OUT_CHUNK = 512
OUT_TILE = 256
D_HALF = D_MODEL // 2
U32 = jnp.uint32


def _pack_bf16_pairs(hi, lo):
    hi_bits = lax.bitcast_convert_type(hi.astype(BF16).astype(F32), U32)
    lo_bits = lax.bitcast_convert_type(lo.astype(BF16).astype(F32), U32)
    return hi_bits | (lo_bits >> 16)


def _unpack_bf16_pairs(words):
    hi = lax.bitcast_convert_type(words & U32(0xFFFF0000), F32)
    lo = lax.bitcast_convert_type(words << 16, F32)
    return hi, lo


def _out_proj_body(catc_ref, cata_ref, cats_ref, xp_ref, xs_ref, wo_hbm, g2_ref, wr_ref, x1_ref, h2p_ref, lg_ref,
                   h_ref, wo_ref, wstage, wsem, *, n_prompt_tiles):
    is_prompt = pl.program_id(0) < n_prompt_tiles
    tm = catc_ref.shape[0]

    @pl.when(pl.program_id(0) == 0)
    def _():
        _load_weight_as_bf16(wo_hbm, wo_ref, wstage, wsem)

    def sample_rows(v):
        return jnp.concatenate([v, jnp.zeros((tm - v.shape[0], v.shape[1]), v.dtype)], axis=0)

    cat = jnp.where(is_prompt, jnp.concatenate([catc_ref[...], cata_ref[...]], axis=1),
                    sample_rows(cats_ref[...].astype(BF16)))
    ss = jnp.zeros((tm, 1), F32)
    for c in range(D_MODEL // OUT_CHUNK):
        cs = slice(c * OUT_CHUNK, (c + 1) * OUT_CHUNK)
        y = jnp.where(is_prompt, xp_ref[:, cs], sample_rows(xs_ref[:, cs])) + _dot(cat, wo_ref[:, cs])
        x1_ref[:, cs] = y
        ss = ss + jnp.sum(y * y, axis=-1, keepdims=True)
    r = lax.rsqrt(ss / D_MODEL + EPS)
    for c in range(D_MODEL // OUT_CHUNK):
        cs = slice(c * OUT_CHUNK, (c + 1) * OUT_CHUNK)
        h_ref[:, cs] = x1_ref[:, cs] * r * g2_ref[:, cs]
    for c in range(D_HALF // OUT_CHUNK):
        cs = slice(c * OUT_CHUNK, (c + 1) * OUT_CHUNK)
        cs_lo = slice(D_HALF + c * OUT_CHUNK, D_HALF + (c + 1) * OUT_CHUNK)
        h2p_ref[:, cs] = _pack_bf16_pairs(h_ref[:, cs], h_ref[:, cs_lo])
    lg = _dot(h_ref[...].astype(BF16), wr_ref[...])
    for j in range(tm // ROUTE_CHUNK):
        lg_ref[j] = lg[j * ROUTE_CHUNK:(j + 1) * ROUTE_CHUNK, :].T[0:ROUTER_ROWS, :]


def _out_proj(cat_conv, cat_attn, cat_s, xp2, xs2, w_out, g2, wr_bf, tm):
    n_prompt_tiles = cat_conv.shape[0] // tm
    n_rows = (n_prompt_tiles + 1) * tm
    n_s = cat_s.shape[0]
    prompt = lambda w: pl.BlockSpec((tm, w), lambda i: (jnp.minimum(i, n_prompt_tiles - 1), 0))
    out_row = lambda w: pl.BlockSpec((tm, w), lambda i: (i, 0))
    body = functools.partial(_out_proj_body, n_prompt_tiles=n_prompt_tiles)
    return pl.pallas_call(
        body,
        grid=(n_prompt_tiles + 1,),
        in_specs=[prompt(D_CONV), prompt(D_ATTN), _resident((n_s, D_MODEL)), prompt(D_MODEL),
                  _resident((n_s, D_MODEL)),
                  pl.BlockSpec(memory_space=pl.ANY), _resident((1, D_MODEL)), _resident((D_MODEL, LANES))],
        out_specs=[out_row(D_MODEL), out_row(D_HALF),
                   pl.BlockSpec((tm // ROUTE_CHUNK, ROUTER_ROWS, ROUTE_CHUNK), lambda i: (i, 0, 0))],
        out_shape=[jax.ShapeDtypeStruct((n_rows, D_MODEL), F32), jax.ShapeDtypeStruct((n_rows, D_HALF), U32),
                   jax.ShapeDtypeStruct((n_rows // ROUTE_CHUNK, ROUTER_ROWS, ROUTE_CHUNK), F32)],
        scratch_shapes=[pltpu.VMEM((tm, D_MODEL), F32), pltpu.VMEM((D_MODEL, D_MODEL), BF16),
                        pltpu.VMEM((2, D_MODEL, CAST_COLS), F32), pltpu.SemaphoreType.DMA((2,))],
        compiler_params=_params(1),
        name="out_proj",
    )(cat_conv, cat_attn, cat_s, xp2, xs2, w_out, g2, wr_bf)


ROUTE_CHUNK = 128
ROUTE_UNROLL = 5


ROUTER_ROWS = 40


def _blk_start(blk_ref, e):
    return blk_ref[EXPERT_LANE0 + e, 0]


def _blk_count(blk_ref, e):
    return blk_ref[ROUTER_ROWS + EXPERT_LANE0 + e, 0]


def _route_body(lg_ref, bias_ref, slot_ref, gate_ref, blk_ref, cum_ref, sel_ref, *, n_tok):
    n_chunks = n_tok // ROUTE_CHUNK
    row = lax.broadcasted_iota(I32, (ROUTER_ROWS, ROUTE_CHUNK), 0).astype(F32)
    ri = lax.broadcasted_iota(I32, (ROUTE_CHUNK, ROUTE_CHUNK), 0)
    ci = lax.broadcasted_iota(I32, (ROUTE_CHUNK, ROUTE_CHUNK), 1)
    earlier_tok = jnp.where(ri < ci, 1.0, 0.0).astype(BF16)
    er = lax.broadcasted_iota(I32, (ROUTER_ROWS, ROUTER_ROWS), 0)
    ec = lax.broadcasted_iota(I32, (ROUTER_ROWS, ROUTER_ROWS), 1)
    earlier_row = jnp.where(ec < er, 1.0, 0.0).astype(BF16)
    sub = lax.broadcasted_iota(I32, (SUBLANES, ROUTE_CHUNK), 0)
    is_group = row < N_EXPERT_GROUPS

    def first_max(vals):
        m = jnp.max(vals, axis=0, keepdims=True)
        idx = jnp.min(jnp.where(vals == m, row, float(ROUTER_ROWS)), axis=0, keepdims=True)
        return m, idx

    def pair_rows(a, b):
        return jnp.where(sub == 0, a, jnp.where(sub == 1, b, 0.0))

    def assign(i, carry):
        l = lg_ref[i] + bias_ref[...]
        gl = jnp.where(is_group, l, NEG_INF)
        g_max, g_idx = first_max(gl)
        g_top = 1.0 / jnp.sum(jnp.exp(gl - g_max), axis=0, keepdims=True)
        lo = EXPERT_LANE0 + g_idx * EXPERTS_PER_GROUP
        el = jnp.where(row >= lo, jnp.where(row < lo + EXPERTS_PER_GROUP, l, NEG_INF), NEG_INF)
        m1, i1 = first_max(el)
        p = jnp.exp(el - m1)
        probs = p / jnp.sum(p, axis=0, keepdims=True)
        e1 = jnp.sum(jnp.where(row == i1, probs, 0.0), axis=0, keepdims=True)
        _, i2 = first_max(jnp.where(row == i1, NEG_INF, el))
        e2 = jnp.sum(jnp.where(row == i2, probs, 0.0), axis=0, keepdims=True)
        gate_ref[i] = pair_rows(g_top * e1 / (e1 + e2), g_top * e2 / (e1 + e2))
        sel_ref[i] = pair_rows(i1, i2)
        onehot = jnp.where(row == i1, 1.0, jnp.where(row == i2, 1.0, 0.0))
        cum_ref[i] = _dot(onehot.astype(BF16), earlier_tok) + carry
        return carry + jnp.sum(onehot, axis=1, keepdims=True)

    counts = lax.fori_loop(0, n_chunks, assign, jnp.zeros((ROUTER_ROWS, 1), F32), unroll=ROUTE_UNROLL)
    n_blocks = jnp.floor((counts + (MOE_BLOCK - 1)) / MOE_BLOCK)
    blk_start = _dot(earlier_row, jnp.broadcast_to(n_blocks, (ROUTER_ROWS, LANES)).astype(BF16))
    row_start = blk_start * MOE_BLOCK

    def place(i, carry):
        pos = cum_ref[i] + row_start
        sel = sel_ref[i]
        s1 = jnp.sum(jnp.where(row == sel[0:1, :], pos, 0.0), axis=0, keepdims=True)
        s2 = jnp.sum(jnp.where(row == sel[1:2, :], pos, 0.0), axis=0, keepdims=True)
        slot_ref[0, i] = s1.astype(I32)
        slot_ref[1, i] = s2.astype(I32)
        return carry

    lax.fori_loop(0, n_chunks, place, 0, unroll=ROUTE_UNROLL)

    blk_ref[0:ROUTER_ROWS, :] = blk_start.astype(I32)
    blk_ref[ROUTER_ROWS:2 * ROUTER_ROWS, :] = jnp.broadcast_to(counts, (ROUTER_ROWS, LANES)).astype(I32)


def _route(logits, bias, n_tok):
    body = functools.partial(_route_body, n_tok=n_tok)
    n_chunks = n_tok // ROUTE_CHUNK
    chunked = lambda r: pl.BlockSpec((n_chunks, r, LANES), lambda i: (0, 0, 0))
    table = pl.BlockSpec((2 * ROUTER_ROWS, LANES), lambda i: (0, 0))
    return pl.pallas_call(
        body,
        grid=(1,),
        in_specs=[chunked(ROUTER_ROWS), pl.BlockSpec((ROUTER_ROWS, LANES), lambda i: (0, 0))],
        out_specs=[pl.BlockSpec((2, n_chunks, 1, LANES), lambda i: (0, 0, 0, 0)), chunked(SUBLANES), table],
        out_shape=[jax.ShapeDtypeStruct((2, n_chunks, 1, LANES), I32),
                   jax.ShapeDtypeStruct((n_chunks, SUBLANES, LANES), F32),
                   jax.ShapeDtypeStruct((2 * ROUTER_ROWS, LANES), I32)],
        scratch_shapes=[pltpu.VMEM((n_chunks, ROUTER_ROWS, LANES), F32),
                        pltpu.VMEM((n_chunks, SUBLANES, LANES), F32)],
        compiler_params=_params(1),
        name="route",
    )(logits, bias)


INVERT_UNROLL = 16


CLEAR_SPAN = 8


def _invert_body(slot_ref, blk_ref, tok_ref, *, n_slots):
    n_tok = slot_ref.shape[0] // 2

    def clear_expert(e, c):
        first_pad = _blk_start(blk_ref, e) * MOE_BLOCK + _blk_count(blk_ref, e)
        end = jnp.where(e == N_EXPERTS - 1, n_slots, _blk_start(blk_ref, e + 1) * MOE_BLOCK)
        span_bits = CLEAR_SPAN.bit_length() - 1
        lo = lax.shift_right_logical(first_pad, span_bits) * CLEAR_SPAN

        def span(j, c2):
            for k in range(CLEAR_SPAN):
                tok_ref[lo + j * CLEAR_SPAN + k] = 0
            return c2

        lax.fori_loop(0, lax.shift_right_logical(end - lo, span_bits), span, 0)
        return c

    lax.fori_loop(0, N_EXPERTS, clear_expert, 0)

    def put(t, c):
        tok_ref[slot_ref[t]] = t
        tok_ref[slot_ref[n_tok + t]] = t
        return c

    lax.fori_loop(0, n_tok, put, 0, unroll=INVERT_UNROLL)


def _invert(slots, blk, n_slots):
    smem = pl.BlockSpec(memory_space=pltpu.SMEM)
    return pl.pallas_call(
        functools.partial(_invert_body, n_slots=n_slots),
        in_specs=[smem, smem],
        out_specs=smem,
        out_shape=jax.ShapeDtypeStruct((n_slots,), I32),
        name="invert",
    )(slots, blk)


GATHER_AHEAD = 3
GATHER_SLOTS = GATHER_AHEAD + 1


def _row_gather_start(idx_ref, base, src_hbm, dst, sem, n_rows, priorities):
    for r in range(n_rows):
        tok = idx_ref[base + r]
        pltpu.make_async_copy(src_hbm.at[pl.ds(tok, 1), :], dst.at[pl.ds(r, 1), :], sem).start(
            priority=priorities[r % len(priorities)])


def _experts_body(bstart_ref, tok_ref, h2p_hbm, w1_ref, w3_ref, w2_ref, eo_hbm, xbuf, obuf, w1b, w3b, w2b, gsem,
                  osem, *, n_blocks):
    e = pl.program_id(0)
    n_exp = pl.num_programs(0)
    b0 = _blk_start(bstart_ref, e)
    nb = _blk_start(bstart_ref, e + 1) - b0
    n_used = _blk_start(bstart_ref, n_exp)

    def gather(block):
        s = lax.rem(block, GATHER_SLOTS)
        src_block = jnp.minimum(block, n_used - 1)
        _row_gather_start(tok_ref, src_block * MOE_BLOCK, h2p_hbm, xbuf.at[s], gsem.at[s], MOE_BLOCK, (1, 0))

    def gather_wait(block):
        s = lax.rem(block, GATHER_SLOTS)
        pltpu.make_async_copy(xbuf.at[s], xbuf.at[s], gsem.at[s]).wait()

    def out_copy(block, s):
        rows = pl.ds(pl.multiple_of(block * MOE_BLOCK, MOE_BLOCK), MOE_BLOCK)
        return pltpu.make_async_copy(obuf.at[s], eo_hbm.at[rows, :], osem.at[s])

    @pl.when(e == 0)
    def _():
        for k in range(GATHER_AHEAD):
            gather(k)

    @pl.when(nb > 0)
    def _():
        w1b[...] = w1_ref[...].astype(BF16)
        w3b[...] = w3_ref[...].astype(BF16)
        w2b[...] = w2_ref[...].astype(BF16)

    def block(j, carry):
        b = b0 + j
        s = lax.rem(b, 2)
        gather(b + GATHER_AHEAD)
        gather_wait(b)
        hi, lo = _unpack_bf16_pairs(xbuf[lax.rem(b, GATHER_SLOTS)])
        x = jnp.concatenate([hi.astype(BF16), lo.astype(BF16)], axis=1)
        a = _dot(x, w1b[...])
        g = _dot(x, w3b[...])
        hdn = (a * jax.nn.sigmoid(a) * g).astype(BF16)
        o = _dot(hdn, w2b[...])

        @pl.when(b >= 2)
        def _():
            out_copy(b - 2, s).wait()

        obuf[s] = _pack_bf16_pairs(o[:, :D_HALF], o[:, D_HALF:])
        out_copy(b, s).start()
        return carry

    lax.fori_loop(0, nb, block, 0)

    @pl.when(e == n_exp - 1)
    def _():
        for k in range(GATHER_AHEAD):
            gather_wait(n_used + k)

        @pl.when(n_used >= 2)
        def _():
            out_copy(n_used - 2, lax.rem(n_used, 2)).wait()

        out_copy(n_used - 1, lax.rem(n_used - 1, 2)).wait()
        obuf[0] = jnp.zeros((MOE_BLOCK, D_HALF), U32)

        def fill(tb, carry):
            out_copy(tb, 0).start()
            return carry

        lax.fori_loop(n_used, n_blocks, fill, 0)

        def drain(tb, carry):
            out_copy(tb, 0).wait()
            return carry

        lax.fori_loop(n_used, n_blocks, drain, 0)


def _experts(bstart, tok, h2p, w1, w3, w2, n_blocks):
    def wspec(r, c):
        return pl.BlockSpec((None, r, c), lambda e, bstart_ref, tok_ref: (e, 0, 0))

    grid_spec = pltpu.PrefetchScalarGridSpec(
        num_scalar_prefetch=2,
        grid=(N_EXPERTS,),
        in_specs=[pl.BlockSpec(memory_space=pl.ANY),
                  wspec(D_MODEL, D_EXPERT), wspec(D_MODEL, D_EXPERT), wspec(D_EXPERT, D_MODEL)],
        out_specs=pl.BlockSpec(memory_space=pl.ANY),
        scratch_shapes=[pltpu.VMEM((GATHER_SLOTS, MOE_BLOCK, D_HALF), U32), pltpu.VMEM((2, MOE_BLOCK, D_HALF), U32),
                        pltpu.VMEM((D_MODEL, D_EXPERT), BF16), pltpu.VMEM((D_MODEL, D_EXPERT), BF16),
                        pltpu.VMEM((D_EXPERT, D_MODEL), BF16),
                        pltpu.SemaphoreType.DMA((GATHER_SLOTS,)), pltpu.SemaphoreType.DMA((2,))],
    )
    return pl.pallas_call(
        functools.partial(_experts_body, n_blocks=n_blocks),
        grid_spec=grid_spec,
        out_shape=jax.ShapeDtypeStruct((n_blocks * MOE_BLOCK, D_HALF), U32),
        compiler_params=_params(1),
        name="experts",
    )(bstart, tok, h2p, w1, w3, w2)


def _combine_body(slot_ref, x1_ref, gate_ref, eo_hbm, yp_ref, ys_ref, buf, sem, *, n_tok, n_prompt_tiles):
    i = pl.program_id(0)
    tm = x1_ref.shape[0]
    n_tiles = n_tok // tm
    slot = lax.rem(i, GATHER_SLOTS)

    def start(tile):
        s = lax.rem(tile, GATHER_SLOTS)
        base = jnp.minimum(tile, n_tiles - 1) * tm
        _row_gather_start(slot_ref, base, eo_hbm, buf.at[s, 0], sem.at[s], tm, (0, 1))
        _row_gather_start(slot_ref, n_tok + base, eo_hbm, buf.at[s, 1], sem.at[s], tm, (0, 1))

    def wait(tile):
        s = lax.rem(tile, GATHER_SLOTS)
        pltpu.make_async_copy(buf.at[s], buf.at[s], sem.at[s]).wait()

    @pl.when(i == 0)
    def _():
        for k in range(GATHER_AHEAD):
            start(k)

    start(i + GATHER_AHEAD)
    wait(i)

    @pl.when(i == n_tiles - 1)
    def _():
        for k in range(GATHER_AHEAD):
            wait(n_tiles + k)

    diag = lax.broadcasted_iota(I32, (tm, tm), 0) == lax.broadcasted_iota(I32, (tm, tm), 1)
    g0 = jnp.sum(jnp.where(diag, gate_ref[0:1, :], 0.0), axis=1, keepdims=True)
    g1 = jnp.sum(jnp.where(diag, gate_ref[1:2, :], 0.0), axis=1, keepdims=True)
    hi0, lo0 = _unpack_bf16_pairs(buf[slot, 0])
    hi1, lo1 = _unpack_bf16_pairs(buf[slot, 1])
    y_hi = x1_ref[:, :D_HALF] + g0 * hi0 + g1 * hi1
    y_lo = x1_ref[:, D_HALF:] + g0 * lo0 + g1 * lo1

    @pl.when(i < n_prompt_tiles)
    def _():
        yp_ref[:, :D_HALF] = y_hi
        yp_ref[:, D_HALF:] = y_lo

    @pl.when(i >= n_prompt_tiles)
    def _():
        ys_ref[:, :D_HALF] = y_hi
        ys_ref[:, D_HALF:] = y_lo


def _combine(slots, x1, gates, eo, n_prompt, n_tok, tm):
    n_tiles = n_tok // tm
    n_prompt_tiles = n_prompt // tm
    body = functools.partial(_combine_body, n_tok=n_tok, n_prompt_tiles=n_prompt_tiles)
    grid_spec = pltpu.PrefetchScalarGridSpec(
        num_scalar_prefetch=1,
        grid=(n_tiles,),
        in_specs=[pl.BlockSpec((tm, D_MODEL), lambda i, s: (i, 0)),
                  pl.BlockSpec((None, SUBLANES, tm), lambda i, s: (i, 0, 0)),
                  pl.BlockSpec(memory_space=pl.ANY)],
        out_specs=[pl.BlockSpec((tm, D_MODEL), lambda i, s: (jnp.minimum(i, n_prompt_tiles - 1), 0)),
                   pl.BlockSpec((tm, D_MODEL), lambda i, s: (jnp.maximum(i - n_prompt_tiles, 0), 0))],
        scratch_shapes=[pltpu.VMEM((GATHER_SLOTS, 2, tm, D_HALF), U32), pltpu.SemaphoreType.DMA((GATHER_SLOTS,))],
    )
    return pl.pallas_call(
        body,
        grid_spec=grid_spec,
        out_shape=[jax.ShapeDtypeStruct((n_prompt, D_MODEL), F32),
                   jax.ShapeDtypeStruct((n_tok - n_prompt, D_MODEL), F32)],
        compiler_params=_params(1),
        name="combine",
    )(slots, x1, gates, eo)


def _layer(x_prompt, x_sample, state_conv, cache_k, cache_v, norm1_g, w_in, conv_w, conv_b, conv_norm_g,
           conv_norm_b, q_norm_g, k_norm_g, attn_sinks, w_out, norm2_g, w_rg, b_rg, w_re, b_re, w1, w3, w2):
    b, t, _ = x_prompt.shape
    sb, st, _ = x_sample.shape
    n_p, n_s = b * t, sb * st
    n_tok = n_p + n_s

    g1 = norm1_g.reshape(1, D_MODEL)
    g2 = norm2_g.reshape(1, D_MODEL)
    qg = q_norm_g.reshape(1, HEAD_DIM)
    kg = k_norm_g.reshape(1, HEAD_DIM)
    chunked = lambda a: a.reshape(-1, N_CCHUNK, LANES).transpose(1, 0, 2)
    cw, cb, lg, lb = chunked(conv_w), chunked(conv_b), chunked(conv_norm_g), chunked(conv_norm_b)
    pad_rows = ROUTER_ROWS - N_EXPERT_GROUPS - N_EXPERTS
    pad_lanes = LANES - N_EXPERT_GROUPS - N_EXPERTS
    w_router_bf = jnp.concatenate([w_rg, w_re, jnp.zeros((D_MODEL, pad_lanes), F32)], axis=1).astype(BF16)
    b_router = jnp.broadcast_to(jnp.concatenate([b_rg, b_re, jnp.zeros((pad_rows,), F32)])[:, None],
                                (ROUTER_ROWS, LANES))

    xp2 = x_prompt.reshape(n_p, D_MODEL)
    xs2 = x_sample.transpose(1, 0, 2).reshape(n_s, D_MODEL)

    c_p, q_p, k_p, v_p, conv_p, knew_p, vnew_p, w_in_bf = _in_proj_conv(xp2, g1, w_in, qg, kg, cw, cb, lg, lb, 512, b)
    u_s, q_s, k_s, v_s = _in_proj(xs2, g1, w_in_bf, qg, kg, n_s, F32)

    r3 = lambda a, bb: a.reshape(bb, -1, a.shape[-1])
    a_p = _mixer_prompt(attn_sinks, r3(q_p, b), r3(k_p, b), r3(v_p, b), 512)
    cat_s, conv_s, knew_s, vnew_s = _mixer_sample(
        attn_sinks, r3(u_s, st), state_conv[0].transpose(1, 0, 2), r3(q_s, st), r3(k_s, st), r3(v_s, st),
        cache_k, cache_v, cw, cb, lg, lb)
    conv_s = conv_s.transpose(1, 0, 2)[None]

    x1, h2p, logits = _out_proj(c_p, a_p.reshape(n_p, D_ATTN), cat_s.reshape(n_s, D_MODEL), xp2, xs2, w_out, g2,
                                w_router_bf, OUT_TILE)

    n_blocks = -(-(n_tok * 2) // MOE_BLOCK) + N_EXPERTS
    slots, gates, blk = _route(logits, b_router, n_tok)
    slots = slots.reshape(2 * n_tok)
    tok = _invert(slots, blk, n_blocks * MOE_BLOCK)
    eo = _experts(blk, tok, h2p, w1, w3, w2, n_blocks)
    y_p, y_s = _combine(slots, x1, gates, eo, n_p, n_tok, COMBINE_TILE)

    return (y_p.reshape(b, t, D_MODEL), y_s.reshape(st, sb, D_MODEL).transpose(1, 0, 2), conv_p[None], knew_p[None],
            vnew_p[None], conv_s, knew_s, vnew_s)


def kernel(x_prompt, x_sample, state_conv, cache_k, cache_v, norm1_g, w_in, conv_w, conv_b, conv_norm_g, conv_norm_b, q_norm_g, k_norm_g, attn_sinks, w_out, norm2_g, w_router_group, b_router_group, w_router_expert, b_router_expert, w1, w3, w2):
    depth = w_in.shape[0]
    assert depth == 1, "single-layer step"
    return _layer(x_prompt, x_sample, state_conv, cache_k, cache_v, norm1_g[0], w_in[0], conv_w[0],
                  conv_b[0], conv_norm_g[0], conv_norm_b[0], q_norm_g[0], k_norm_g[0], attn_sinks[0], w_out[0],
                  norm2_g[0], w_router_group[0], b_router_group[0], w_router_expert[0], b_router_expert[0],
                  w1[0], w3[0], w2[0])
```

```python
import functools
import math

import jax
import jax.numpy as jnp
from jax import lax
from jax.experimental import pallas as pl
from jax.experimental.pallas import tpu as pltpu

F32 = jnp.float32
BF16 = jnp.bfloat16
I32 = jnp.int32

D_MODEL = 2048
D_CONV = 1024
CONV_WIDTH = 31
CONV_HIST = CONV_WIDTH - 1
D_ATTN = 1024
HEAD_DIM = 128
N_HEADS = 8
N_KV_HEADS = 2
GQA_GROUP = N_HEADS // N_KV_HEADS
D_KV = N_KV_HEADS * HEAD_DIM
WINDOW = 128
BLOCK_Q = 128
SCALE = 1.0 / math.sqrt(HEAD_DIM)
N_EXPERT_GROUPS = 4
EXPERTS_PER_GROUP = 8
N_EXPERTS = N_EXPERT_GROUPS * EXPERTS_PER_GROUP
D_EXPERT = 512
MOE_BLOCK = 256
COMBINE_TILE = 128
D_IN = 2 * D_CONV + D_ATTN + 2 * D_KV
EPS = 1e-6
PAST_LEN = 16384

LANES = 128
SUBLANES = 8
MXU_COLS = 256
VMEM_LIMIT_BYTES = 56 * 1024 * 1024
NEG_INF = float("-inf")
EXPERT_LANE0 = N_EXPERT_GROUPS


def _params(n_axes):
    return pltpu.CompilerParams(dimension_semantics=("arbitrary",) * n_axes,
                                vmem_limit_bytes=VMEM_LIMIT_BYTES)


def _resident(shape):
    nd = len(shape)
    return pl.BlockSpec(shape, lambda *_: (0,) * nd, pipeline_mode=pl.Buffered(1))


def _dot(a, b):
    return jnp.dot(a, b, preferred_element_type=F32)


def _in_proj_body(x_ref, g1_ref, w_ref, qg_ref, kg_ref, u_ref, q_ref, k_ref, v_ref, n_ref):
    x = x_ref[...]
    ms = jnp.mean(x * x, axis=-1, keepdims=True)
    n_ref[...] = (x * lax.rsqrt(ms + EPS) * g1_ref[...]).astype(BF16)

    def head_norm(h, g):
        return h * lax.rsqrt(jnp.mean(h * h, axis=-1, keepdims=True) + EPS) * g

    ch = MXU_COLS
    for c in range(D_CONV // ch):
        a = _dot(n_ref[...], w_ref[:, c * ch:(c + 1) * ch])
        g = _dot(n_ref[...], w_ref[:, D_CONV + c * ch:D_CONV + (c + 1) * ch])
        u_ref[:, c * ch:(c + 1) * ch] = a * jax.nn.sigmoid(g)
    q_off = 2 * D_CONV
    for c in range(D_ATTN // ch):
        qq = _dot(n_ref[...], w_ref[:, q_off + c * ch:q_off + (c + 1) * ch])
        for j in range(ch // HEAD_DIM):
            qh = head_norm(qq[:, j * HEAD_DIM:(j + 1) * HEAD_DIM], qg_ref[...])
            q_ref[:, c * ch + j * HEAD_DIM:c * ch + (j + 1) * HEAD_DIM] = qh.astype(q_ref.dtype)
    k_off = q_off + D_ATTN
    kk = _dot(n_ref[...], w_ref[:, k_off:k_off + D_KV])
    for j in range(N_KV_HEADS):
        k_ref[:, j * HEAD_DIM:(j + 1) * HEAD_DIM] = head_norm(kk[:, j * HEAD_DIM:(j + 1) * HEAD_DIM], kg_ref[...])
    v_ref[...] = _dot(n_ref[...], w_ref[:, k_off + D_KV:k_off + 2 * D_KV])


CAST_COLS = 512


def _load_weight_as_bf16(w_hbm, w_bf, stage, sem):
    n_chunks = w_hbm.shape[1] // CAST_COLS

    def copy(c):
        return pltpu.make_async_copy(w_hbm.at[:, c * CAST_COLS:(c + 1) * CAST_COLS], stage.at[c % 2], sem.at[c % 2])

    copy(0).start()
    for c in range(n_chunks):
        if c + 1 < n_chunks:
            copy(c + 1).start()
        copy(c).wait()
        w_bf[:, c * CAST_COLS:(c + 1) * CAST_COLS] = stage[c % 2].astype(BF16)


def _in_proj_conv_body(x_ref, g1_ref, w_hbm, qg_ref, kg_ref, cw_ref, cb_ref, lg_ref, lb_ref, c_ref, q_ref, k_ref,
                       v_ref, ut_ref, kt_ref, vt_ref, wout_hbm, n_ref, ue_ref, conv_ref, w_ref, wstage, wsem, osem, *,
                       tiles_per_seq):
    tm = x_ref.shape[0]
    step = pl.program_id(0)
    first = lax.rem(step, tiles_per_seq) == 0
    w_out_copy = pltpu.make_async_copy(w_ref, wout_hbm, osem.at[0])

    @pl.when(step == 0)
    def _():
        _load_weight_as_bf16(w_hbm, w_ref, wstage, wsem)
        w_out_copy.start()

    @pl.when(step == pl.num_programs(0) - 1)
    def _():
        w_out_copy.wait()

    @pl.when(first)
    def _():
        for c in range(N_CCHUNK):
            ue_ref[c, 0:CONV_HALO, :] = jnp.zeros((CONV_HALO, LANES), F32)

    @pl.when(jnp.logical_not(first))
    def _():
        for c in range(N_CCHUNK):
            ue_ref[c, 0:CONV_HALO, :] = ue_ref[c, tm:tm + CONV_HALO, :]

    x = x_ref[...]
    ms = jnp.mean(x * x, axis=-1, keepdims=True)
    n_ref[...] = (x * lax.rsqrt(ms + EPS) * g1_ref[...]).astype(BF16)

    def head_norm(h, g):
        return h * lax.rsqrt(jnp.mean(h * h, axis=-1, keepdims=True) + EPS) * g

    ch = MXU_COLS
    lanes_per = ch // LANES
    n_glu = D_CONV // ch
    q_off = 2 * D_CONV

    row_blocks = [slice(r, r + DOT_ROWS) for r in range(0, tm, DOT_ROWS)]

    for c in range(n_glu):
        for rows in row_blocks:
            a = _dot(n_ref[rows, :], w_ref[:, c * ch:(c + 1) * ch])
            g = _dot(n_ref[rows, :], w_ref[:, D_CONV + c * ch:D_CONV + (c + 1) * ch])
            u = a * jax.nn.sigmoid(g)
            for half in range(lanes_per):
                ue_ref[c * lanes_per + half, CONV_HALO + rows.start:CONV_HALO + rows.stop, :] = (
                    u[:, half * LANES:(half + 1) * LANES])
        for half in range(lanes_per):
            lc = c * lanes_per + half
            _conv_chunk(ue_ref, lc, cw_ref[lc], cb_ref[lc], conv_ref, tm)
    _ln_swish(conv_ref, lg_ref, lb_ref, c_ref, tm)

    for c in range(D_ATTN // ch):
        for rows in row_blocks:
            qq = _dot(n_ref[rows, :], w_ref[:, q_off + c * ch:q_off + (c + 1) * ch])
            for j in range(ch // HEAD_DIM):
                qh = head_norm(qq[:, j * HEAD_DIM:(j + 1) * HEAD_DIM], qg_ref[...])
                q_ref[rows, c * ch + j * HEAD_DIM:c * ch + (j + 1) * HEAD_DIM] = qh.astype(q_ref.dtype)
    k_off = q_off + D_ATTN
    for rows in row_blocks:
        kk = _dot(n_ref[rows, :], w_ref[:, k_off:k_off + D_KV])
        vv = _dot(n_ref[rows, :], w_ref[:, k_off + D_KV:k_off + 2 * D_KV])
        v_ref[rows, :] = vv
        for j in range(N_KV_HEADS):
            hs = slice(j * HEAD_DIM, (j + 1) * HEAD_DIM)
            k_ref[rows, hs] = head_norm(kk[:, hs], kg_ref[...])
    for j in range(N_KV_HEADS):
        hs = slice(j * HEAD_DIM, (j + 1) * HEAD_DIM)
        kt_ref[:, j, :] = k_ref[tm - WINDOW:tm, hs]
        vt_ref[:, j, :] = v_ref[tm - WINDOW:tm, hs]
    for c in range(N_CCHUNK):
        ut_ref[:, c * LANES:(c + 1) * LANES] = ue_ref[c, CONV_HALO + tm - CONV_HIST:CONV_HALO + tm, :]


def _in_proj_conv(x2, g1, w_in, qg, kg, cw, cb, lg, lb, tm, n_seq):
    n = x2.shape[0]
    tiles_per_seq = n // n_seq // tm
    row = lambda w: pl.BlockSpec((tm, w), lambda i: (i, 0))
    seq = lambda *dims: pl.BlockSpec((None,) + dims, lambda i: (i // tiles_per_seq,) + (0,) * len(dims))
    hbm = pl.BlockSpec(memory_space=pl.ANY)
    body = functools.partial(_in_proj_conv_body, tiles_per_seq=tiles_per_seq)
    return pl.pallas_call(
        body,
        grid=(n // tm,),
        in_specs=[row(D_MODEL), _resident((1, D_MODEL)), hbm,
                  _resident((1, HEAD_DIM)), _resident((1, HEAD_DIM)),
                  _resident((N_CCHUNK, CONV_WIDTH, LANES)), _resident((N_CCHUNK, 1, LANES)),
                  _resident((N_CCHUNK, 1, LANES)), _resident((N_CCHUNK, 1, LANES))],
        out_specs=[row(D_CONV), row(D_ATTN), row(D_KV), row(D_KV),
                   seq(CONV_HIST, D_CONV), seq(WINDOW, N_KV_HEADS, HEAD_DIM), seq(WINDOW, N_KV_HEADS, HEAD_DIM), hbm],
        out_shape=[jax.ShapeDtypeStruct((n, D_CONV), BF16), jax.ShapeDtypeStruct((n, D_ATTN), BF16),
                   jax.ShapeDtypeStruct((n, D_KV), F32), jax.ShapeDtypeStruct((n, D_KV), F32),
                   jax.ShapeDtypeStruct((n_seq, CONV_HIST, D_CONV), F32),
                   jax.ShapeDtypeStruct((n_seq, WINDOW, N_KV_HEADS, HEAD_DIM), F32),
                   jax.ShapeDtypeStruct((n_seq, WINDOW, N_KV_HEADS, HEAD_DIM), F32),
                   jax.ShapeDtypeStruct((D_MODEL, D_IN), BF16)],
        scratch_shapes=[pltpu.VMEM((tm, D_MODEL), BF16),
                        pltpu.VMEM((N_CCHUNK, CONV_HALO + tm, LANES), F32),
                        pltpu.VMEM((N_CCHUNK, tm, LANES), F32),
                        pltpu.VMEM((D_MODEL, D_IN), BF16), pltpu.VMEM((2, D_MODEL, CAST_COLS), F32),
                        pltpu.SemaphoreType.DMA((2,)), pltpu.SemaphoreType.DMA((1,))],
        compiler_params=_params(1),
        name="in_proj_conv",
    )(x2, g1, w_in, qg, kg, cw, cb, lg, lb)


def _in_proj(x2, g1, w_in_bf, qg, kg, tm, q_dtype):
    n = x2.shape[0]
    row = lambda w: pl.BlockSpec((tm, w), lambda i: (i, 0))
    return pl.pallas_call(
        _in_proj_body,
        grid=(n // tm,),
        in_specs=[row(D_MODEL), _resident((1, D_MODEL)), _resident((D_MODEL, D_IN)),
                  _resident((1, HEAD_DIM)), _resident((1, HEAD_DIM))],
        out_specs=[row(D_CONV), row(D_ATTN), row(D_KV), row(D_KV)],
        out_shape=[jax.ShapeDtypeStruct((n, D_CONV), F32), jax.ShapeDtypeStruct((n, D_ATTN), q_dtype),
                   jax.ShapeDtypeStruct((n, D_KV), F32), jax.ShapeDtypeStruct((n, D_KV), F32)],
        scratch_shapes=[pltpu.VMEM((tm, D_MODEL), BF16)],
        compiler_params=_params(1),
        name="in_proj",
    )(x2, g1, w_in_bf, qg, kg)


N_CCHUNK = D_CONV // LANES
CONV_ROWS = 64


CONV_HALO = 32
DOT_ROWS = 256


def _conv_chunk(ue_ref, c, wc, bias, conv_ref, rows):
    base = CONV_HALO - CONV_HIST
    for r0 in range(0, rows, CONV_ROWS):
        acc = jnp.broadcast_to(bias, (CONV_ROWS, LANES))
        for tap in range(CONV_WIDTH):
            acc = acc + wc[tap:tap + 1, :] * ue_ref[c, base + r0 + tap:base + r0 + tap + CONV_ROWS, :]
        conv_ref[c, r0:r0 + CONV_ROWS, :] = acc


def _ln_swish(conv_ref, lg_ref, lb_ref, cat_ref, rows):
    tot = jnp.zeros((rows, 1), F32)
    for c in range(N_CCHUNK):
        tot = tot + jnp.sum(conv_ref[c], axis=-1, keepdims=True)
    mean = tot / D_CONV
    var = jnp.zeros((rows, 1), F32)
    for c in range(N_CCHUNK):
        xc = conv_ref[c] - mean
        var = var + jnp.sum(xc * xc, axis=-1, keepdims=True)
    rstd = lax.rsqrt(var / D_CONV + EPS)
    for c in range(N_CCHUNK):
        y = (conv_ref[c] - mean) * rstd * lg_ref[c] + lb_ref[c]
        cat_ref[:, c * LANES:(c + 1) * LANES] = (y * jax.nn.sigmoid(y)).astype(cat_ref.dtype)


def _sink_softmax_rows(s, sink):
    m = jnp.maximum(jnp.max(s, axis=-1, keepdims=True), sink)
    p = jnp.exp(s - m)
    return p / (jnp.sum(p, axis=-1, keepdims=True) + jnp.exp(sink - m))


def _alibi_slope(head):
    return 2.0 ** (-8.0 * (head + 1) / N_HEADS)


def _mixer_prompt_body(sink_ref, q_ref, k_ref, kh_ref, v_ref, vh_ref, cat_ref, *, tm):
    j = pl.program_id(1)
    has_prev = j > 0
    qi = lax.broadcasted_iota(I32, (BLOCK_Q, 2 * BLOCK_Q), 0)
    kj = lax.broadcasted_iota(I32, (BLOCK_Q, 2 * BLOCK_Q), 1)
    dist = qi + BLOCK_Q - kj
    distf = dist.astype(F32)
    band = jnp.where(dist >= 0, jnp.where(dist < WINDOW, 0.0, NEG_INF), NEG_INF)
    band_first = jnp.where(kj >= BLOCK_Q, band, jnp.where(has_prev, band, NEG_INF))

    for qb in range(tm // BLOCK_Q):
        rows = slice(qb * BLOCK_Q, (qb + 1) * BLOCK_Q)
        prev = slice((qb - 1) * BLOCK_Q, qb * BLOCK_Q)
        mask = band_first if qb == 0 else band
        for kv in range(N_KV_HEADS):
            hs = slice(kv * HEAD_DIM, (kv + 1) * HEAD_DIM)
            k_prev = kh_ref[:, hs] if qb == 0 else k_ref[prev, hs]
            v_prev = vh_ref[:, hs] if qb == 0 else v_ref[prev, hs]
            kk = jnp.concatenate([k_prev, k_ref[rows, hs]], axis=0).astype(BF16)
            vv = jnp.concatenate([v_prev, v_ref[rows, hs]], axis=0).astype(BF16)
            heads = [kv * GQA_GROUP + g for g in range(GQA_GROUP)]
            qs = jnp.concatenate([q_ref[rows, h * HEAD_DIM:(h + 1) * HEAD_DIM] for h in heads], axis=0)
            s = lax.dot_general(qs, kk, (((1,), (1,)), ((), ())), preferred_element_type=F32)
            ps = []
            for g, h in enumerate(heads):
                sg = s[g * BLOCK_Q:(g + 1) * BLOCK_Q] * SCALE - _alibi_slope(h) * distf + mask
                ps.append(_sink_softmax_rows(sg, sink_ref[h]).astype(BF16))
            o = _dot(jnp.concatenate(ps, axis=0), vv)
            for g, h in enumerate(heads):
                cat_ref[rows, h * HEAD_DIM:(h + 1) * HEAD_DIM] = o[g * BLOCK_Q:(g + 1) * BLOCK_Q].astype(cat_ref.dtype)


def _mixer_prompt(sinks, q, k, v, tm):
    b, t, _ = q.shape
    kpb = tm // BLOCK_Q
    main = lambda w: pl.BlockSpec((None, tm, w), lambda bi, j: (bi, j, 0))
    prev_block = pl.BlockSpec((None, BLOCK_Q, D_KV), lambda bi, j: (bi, jnp.maximum(j * kpb - 1, 0), 0))
    body = functools.partial(_mixer_prompt_body, tm=tm)
    return pl.pallas_call(
        body,
        grid=(b, t // tm),
        in_specs=[pl.BlockSpec(memory_space=pltpu.SMEM), main(D_ATTN), main(D_KV), prev_block, main(D_KV),
                  prev_block],
        out_specs=main(D_ATTN),
        out_shape=jax.ShapeDtypeStruct((b, t, D_ATTN), BF16),
        compiler_params=_params(2),
        name="mixer_prompt",
    )(sinks, q, k, k, v, v)


KEY_PAD = 8


SAMPLE_SEQS = 8


def _sample_conv_ln_swish(st_ref, u_ref, cw_ref, cb_ref, lg_ref, lb_ref, cat_ref, t_new):
    hist, n_seq, _ = st_ref.shape
    conv = [[None] * N_CCHUNK for _ in range(t_new)]
    for c in range(N_CCHUNK):
        cs = slice(c * LANES, (c + 1) * LANES)
        wc = cw_ref[c]
        pos = [st_ref[p, :, cs] for p in range(hist)] + [u_ref[t, :, cs] for t in range(t_new)]
        for t in range(t_new):
            acc = jnp.broadcast_to(cb_ref[c], (n_seq, LANES))
            for tap in range(CONV_WIDTH):
                acc = acc + wc[tap:tap + 1, :] * pos[t + tap]
            conv[t][c] = acc
    for t in range(t_new):
        tot = jnp.zeros((n_seq, 1), F32)
        for a in conv[t]:
            tot = tot + jnp.sum(a, axis=-1, keepdims=True)
        mean = tot / D_CONV
        var = jnp.zeros((n_seq, 1), F32)
        for a in conv[t]:
            var = var + jnp.sum((a - mean) * (a - mean), axis=-1, keepdims=True)
        rstd = lax.rsqrt(var / D_CONV + EPS)
        for c, a in enumerate(conv[t]):
            y = (a - mean) * rstd * lg_ref[c] + lb_ref[c]
            cat_ref[t, :, c * LANES:(c + 1) * LANES] = (y * jax.nn.sigmoid(y)).astype(cat_ref.dtype)


def _mixer_sample_body(sink_ref, u_ref, st_ref, q_ref, k_ref, v_ref, ck_ref, cv_ref, cw_ref, cb_ref, lg_ref,
                       lb_ref, cat_ref, nst_ref, nk_ref, nv_ref, kk_ref, vv_ref, qs_ref, *, t_new):
    hist, n_seq, _ = st_ref.shape
    w_past = ck_ref.shape[1]
    n_keys = w_past + KEY_PAD
    n_rows, n_cols = GQA_GROUP * t_new * n_seq, n_seq * n_keys

    _sample_conv_ln_swish(st_ref, u_ref, cw_ref, cb_ref, lg_ref, lb_ref, cat_ref, t_new)
    nst_ref[0:hist - t_new] = st_ref[t_new:hist]
    nst_ref[hist - t_new:hist] = u_ref[...]

    for i in range(n_seq):
        nk_ref[i, 0:w_past - t_new] = ck_ref[i, t_new:w_past]
        nv_ref[i, 0:w_past - t_new] = cv_ref[i, t_new:w_past]
        kk_ref[i, w_past:n_keys, :] = jnp.zeros((KEY_PAD, D_KV), F32)
        vv_ref[i, w_past:n_keys, :] = jnp.zeros((KEY_PAD, D_KV), F32)
        for t in range(t_new):
            kk_ref[i, w_past + t:w_past + t + 1, :] = k_ref[t, i:i + 1, :]
            vv_ref[i, w_past + t:w_past + t + 1, :] = v_ref[t, i:i + 1, :]
        for h in range(N_KV_HEADS):
            hs = slice(h * HEAD_DIM, (h + 1) * HEAD_DIM)
            kk_ref[i, 0:w_past, hs] = ck_ref[i, :, h, :]
            vv_ref[i, 0:w_past, hs] = cv_ref[i, :, h, :]
            for t in range(t_new):
                nk_ref[i, w_past - t_new + t, h:h + 1, :] = k_ref[t, i:i + 1, hs]
                nv_ref[i, w_past - t_new + t, h:h + 1, :] = v_ref[t, i:i + 1, hs]

    row = lax.broadcasted_iota(I32, (n_rows, n_cols), 0)
    col = lax.broadcasted_iota(I32, (n_rows, n_cols), 1)
    seq_bits, tok_bits = n_seq.bit_length() - 1, t_new.bit_length() - 1
    assert (1 << seq_bits, 1 << tok_bits) == (n_seq, t_new), "sequence and token counts must be powers of two"
    tok = (row >> seq_bits) & (t_new - 1)
    key = col - (row & (n_seq - 1)) * n_keys
    dist = tok + w_past - key
    distf = dist.astype(F32)
    mask = jnp.where(dist >= 0, jnp.where(dist < WINDOW, 0.0, NEG_INF), NEG_INF)
    row1 = lax.broadcasted_iota(I32, (n_rows, 1), 0)
    grp = row1 >> (seq_bits + tok_bits)
    blk = lambda g, t: slice((g * t_new + t) * n_seq, (g * t_new + t + 1) * n_seq)

    for kv in range(N_KV_HEADS):
        hs = slice(kv * HEAD_DIM, (kv + 1) * HEAD_DIM)
        slope = jnp.zeros((n_rows, 1), F32)
        sink = jnp.zeros((n_rows, 1), F32)
        for g in range(GQA_GROUP):
            h = kv * GQA_GROUP + g
            slope = jnp.where(grp == g, _alibi_slope(h), slope)
            sink = jnp.where(grp == g, sink_ref[h], sink)
            for t in range(t_new):
                qs_ref[blk(g, t), :] = q_ref[t, :, h * HEAD_DIM:(h + 1) * HEAD_DIM]
        kk = kk_ref[:, :, hs].reshape(n_cols, HEAD_DIM).astype(BF16)
        vv = vv_ref[:, :, hs].reshape(n_cols, HEAD_DIM).astype(BF16)
        s = lax.dot_general(qs_ref[...].astype(BF16), kk, (((1,), (1,)), ((), ())), preferred_element_type=F32)
        sg = s * SCALE - slope * distf + mask
        o = _dot(_sink_softmax_rows(sg, sink).astype(BF16), vv)
        for g in range(GQA_GROUP):
            h = kv * GQA_GROUP + g
            for t in range(t_new):
                cat_ref[t, :, D_CONV + h * HEAD_DIM:D_CONV + (h + 1) * HEAD_DIM] = o[blk(g, t)].astype(cat_ref.dtype)


def _mixer_sample(sinks, u, state, q, k, v, ck, cv, cw, cb, lg, lb):
    t_new, b, _ = u.shape
    hist = state.shape[0]
    w_past = ck.shape[2]
    n_seq = SAMPLE_SEQS
    per = lambda r, w: pl.BlockSpec((r, n_seq, w), lambda bi: (0, bi, 0))
    cache = pl.BlockSpec((None, n_seq, w_past, N_KV_HEADS, HEAD_DIM), lambda bi: (0, bi, 0, 0, 0))
    body = functools.partial(_mixer_sample_body, t_new=t_new)
    return pl.pallas_call(
        body,
        grid=(b // n_seq,),
        in_specs=[pl.BlockSpec(memory_space=pltpu.SMEM),
                  per(t_new, D_CONV), per(hist, D_CONV), per(t_new, D_ATTN), per(t_new, D_KV), per(t_new, D_KV),
                  cache, cache,
                  _resident((N_CCHUNK, CONV_WIDTH, LANES)), _resident((N_CCHUNK, 1, LANES)),
                  _resident((N_CCHUNK, 1, LANES)), _resident((N_CCHUNK, 1, LANES))],
        out_specs=[per(t_new, D_MODEL), per(hist, D_CONV), cache, cache],
        out_shape=[jax.ShapeDtypeStruct((t_new, b, D_MODEL), F32),
                   jax.ShapeDtypeStruct(state.shape, F32),
                   jax.ShapeDtypeStruct(ck.shape, F32),
                   jax.ShapeDtypeStruct(cv.shape, F32)],
        scratch_shapes=[pltpu.VMEM((n_seq, w_past + KEY_PAD, D_KV), F32),
                        pltpu.VMEM((n_seq, w_past + KEY_PAD, D_KV), F32),
                        pltpu.VMEM((n_seq * GQA_GROUP * t_new, HEAD_DIM), F32)],
        compiler_params=_params(1),
        name="mixer_sample",
    )(sinks, u, state, q, k, v, ck, cv, cw, cb, lg, lb)


OUT_CHUNK = 512
OUT_TILE = 256
D_HALF = D_MODEL // 2
U32 = jnp.uint32


def _pack_bf16_pairs(hi, lo):
    hi_bits = lax.bitcast_convert_type(hi.astype(BF16).astype(F32), U32)
    lo_bits = lax.bitcast_convert_type(lo.astype(BF16).astype(F32), U32)
    return hi_bits | (lo_bits >> 16)


def _unpack_bf16_pairs(words):
    hi = lax.bitcast_convert_type(words & U32(0xFFFF0000), F32)
    lo = lax.bitcast_convert_type(words << 16, F32)
    return hi, lo


def _out_proj_body(catc_ref, cata_ref, cats_ref, xp_ref, xs_ref, wo_hbm, g2_ref, wr_ref, x1_ref, h2p_ref, lg_ref,
                   h_ref, wo_ref, wstage, wsem, *, n_prompt_tiles):
    is_prompt = pl.program_id(0) < n_prompt_tiles
    tm = catc_ref.shape[0]

    @pl.when(pl.program_id(0) == 0)
    def _():
        _load_weight_as_bf16(wo_hbm, wo_ref, wstage, wsem)

    def sample_rows(v):
        return jnp.concatenate([v, jnp.zeros((tm - v.shape[0], v.shape[1]), v.dtype)], axis=0)

    cat = jnp.where(is_prompt, jnp.concatenate([catc_ref[...], cata_ref[...]], axis=1),
                    sample_rows(cats_ref[...].astype(BF16)))
    ss = jnp.zeros((tm, 1), F32)
    for c in range(D_MODEL // OUT_CHUNK):
        cs = slice(c * OUT_CHUNK, (c + 1) * OUT_CHUNK)
        y = jnp.where(is_prompt, xp_ref[:, cs], sample_rows(xs_ref[:, cs])) + _dot(cat, wo_ref[:, cs])
        x1_ref[:, cs] = y
        ss = ss + jnp.sum(y * y, axis=-1, keepdims=True)
    r = lax.rsqrt(ss / D_MODEL + EPS)
    for c in range(D_MODEL // OUT_CHUNK):
        cs = slice(c * OUT_CHUNK, (c + 1) * OUT_CHUNK)
        h_ref[:, cs] = x1_ref[:, cs] * r * g2_ref[:, cs]
    for c in range(D_HALF // OUT_CHUNK):
        cs = slice(c * OUT_CHUNK, (c + 1) * OUT_CHUNK)
        cs_lo = slice(D_HALF + c * OUT_CHUNK, D_HALF + (c + 1) * OUT_CHUNK)
        h2p_ref[:, cs] = _pack_bf16_pairs(h_ref[:, cs], h_ref[:, cs_lo])
    lg = _dot(h_ref[...].astype(BF16), wr_ref[...])
    for j in range(tm // ROUTE_CHUNK):
        lg_ref[j] = lg[j * ROUTE_CHUNK:(j + 1) * ROUTE_CHUNK, :].T[0:ROUTER_ROWS, :]


def _out_proj(cat_conv, cat_attn, cat_s, xp2, xs2, w_out, g2, wr_bf, tm):
    n_prompt_tiles = cat_conv.shape[0] // tm
    n_rows = (n_prompt_tiles + 1) * tm
    n_s = cat_s.shape[0]
    prompt = lambda w: pl.BlockSpec((tm, w), lambda i: (jnp.minimum(i, n_prompt_tiles - 1), 0))
    out_row = lambda w: pl.BlockSpec((tm, w), lambda i: (i, 0))
    body = functools.partial(_out_proj_body, n_prompt_tiles=n_prompt_tiles)
    return pl.pallas_call(
        body,
        grid=(n_prompt_tiles + 1,),
        in_specs=[prompt(D_CONV), prompt(D_ATTN), _resident((n_s, D_MODEL)), prompt(D_MODEL),
                  _resident((n_s, D_MODEL)),
                  pl.BlockSpec(memory_space=pl.ANY), _resident((1, D_MODEL)), _resident((D_MODEL, LANES))],
        out_specs=[out_row(D_MODEL), out_row(D_HALF),
                   pl.BlockSpec((tm // ROUTE_CHUNK, ROUTER_ROWS, ROUTE_CHUNK), lambda i: (i, 0, 0))],
        out_shape=[jax.ShapeDtypeStruct((n_rows, D_MODEL), F32), jax.ShapeDtypeStruct((n_rows, D_HALF), U32),
                   jax.ShapeDtypeStruct((n_rows // ROUTE_CHUNK, ROUTER_ROWS, ROUTE_CHUNK), F32)],
        scratch_shapes=[pltpu.VMEM((tm, D_MODEL), F32), pltpu.VMEM((D_MODEL, D_MODEL), BF16),
                        pltpu.VMEM((2, D_MODEL, CAST_COLS), F32), pltpu.SemaphoreType.DMA((2,))],
        compiler_params=_params(1),
        name="out_proj",
    )(cat_conv, cat_attn, cat_s, xp2, xs2, w_out, g2, wr_bf)


ROUTE_CHUNK = 128
ROUTE_UNROLL = 5


ROUTER_ROWS = 40


def _blk_start(blk_ref, e):
    return blk_ref[EXPERT_LANE0 + e, 0]


def _blk_count(blk_ref, e):
    return blk_ref[ROUTER_ROWS + EXPERT_LANE0 + e, 0]


def _route_body(lg_ref, bias_ref, slot_ref, gate_ref, blk_ref, cum_ref, sel_ref, *, n_tok):
    n_chunks = n_tok // ROUTE_CHUNK
    row = lax.broadcasted_iota(I32, (ROUTER_ROWS, ROUTE_CHUNK), 0).astype(F32)
    ri = lax.broadcasted_iota(I32, (ROUTE_CHUNK, ROUTE_CHUNK), 0)
    ci = lax.broadcasted_iota(I32, (ROUTE_CHUNK, ROUTE_CHUNK), 1)
    earlier_tok = jnp.where(ri < ci, 1.0, 0.0).astype(BF16)
    er = lax.broadcasted_iota(I32, (ROUTER_ROWS, ROUTER_ROWS), 0)
    ec = lax.broadcasted_iota(I32, (ROUTER_ROWS, ROUTER_ROWS), 1)
    earlier_row = jnp.where(ec < er, 1.0, 0.0).astype(BF16)
    sub = lax.broadcasted_iota(I32, (SUBLANES, ROUTE_CHUNK), 0)
    is_group = row < N_EXPERT_GROUPS

    def first_max(vals):
        m = jnp.max(vals, axis=0, keepdims=True)
        idx = jnp.min(jnp.where(vals == m, row, float(ROUTER_ROWS)), axis=0, keepdims=True)
        return m, idx

    def pair_rows(a, b):
        return jnp.where(sub == 0, a, jnp.where(sub == 1, b, 0.0))

    def assign(i, carry):
        l = lg_ref[i] + bias_ref[...]
        gl = jnp.where(is_group, l, NEG_INF)
        g_max, g_idx = first_max(gl)
        g_top = 1.0 / jnp.sum(jnp.exp(gl - g_max), axis=0, keepdims=True)
        lo = EXPERT_LANE0 + g_idx * EXPERTS_PER_GROUP
        el = jnp.where(row >= lo, jnp.where(row < lo + EXPERTS_PER_GROUP, l, NEG_INF), NEG_INF)
        m1, i1 = first_max(el)
        p = jnp.exp(el - m1)
        probs = p / jnp.sum(p, axis=0, keepdims=True)
        e1 = jnp.sum(jnp.where(row == i1, probs, 0.0), axis=0, keepdims=True)
        _, i2 = first_max(jnp.where(row == i1, NEG_INF, el))
        e2 = jnp.sum(jnp.where(row == i2, probs, 0.0), axis=0, keepdims=True)
        gate_ref[i] = pair_rows(g_top * e1 / (e1 + e2), g_top * e2 / (e1 + e2))
        sel_ref[i] = pair_rows(i1, i2)
        onehot = jnp.where(row == i1, 1.0, jnp.where(row == i2, 1.0, 0.0))
        cum_ref[i] = _dot(onehot.astype(BF16), earlier_tok) + carry
        return carry + jnp.sum(onehot, axis=1, keepdims=True)

    counts = lax.fori_loop(0, n_chunks, assign, jnp.zeros((ROUTER_ROWS, 1), F32), unroll=ROUTE_UNROLL)
    n_blocks = jnp.floor((counts + (MOE_BLOCK - 1)) / MOE_BLOCK)
    blk_start = _dot(earlier_row, jnp.broadcast_to(n_blocks, (ROUTER_ROWS, LANES)).astype(BF16))
    row_start = blk_start * MOE_BLOCK

    def place(i, carry):
        pos = cum_ref[i] + row_start
        sel = sel_ref[i]
        s1 = jnp.sum(jnp.where(row == sel[0:1, :], pos, 0.0), axis=0, keepdims=True)
        s2 = jnp.sum(jnp.where(row == sel[1:2, :], pos, 0.0), axis=0, keepdims=True)
        slot_ref[0, i] = s1.astype(I32)
        slot_ref[1, i] = s2.astype(I32)
        return carry

    lax.fori_loop(0, n_chunks, place, 0, unroll=ROUTE_UNROLL)

    blk_ref[0:ROUTER_ROWS, :] = blk_start.astype(I32)
    blk_ref[ROUTER_ROWS:2 * ROUTER_ROWS, :] = jnp.broadcast_to(counts, (ROUTER_ROWS, LANES)).astype(I32)


def _route(logits, bias, n_tok):
    body = functools.partial(_route_body, n_tok=n_tok)
    n_chunks = n_tok // ROUTE_CHUNK
    chunked = lambda r: pl.BlockSpec((n_chunks, r, LANES), lambda i: (0, 0, 0))
    table = pl.BlockSpec((2 * ROUTER_ROWS, LANES), lambda i: (0, 0))
    return pl.pallas_call(
        body,
        grid=(1,),
        in_specs=[chunked(ROUTER_ROWS), pl.BlockSpec((ROUTER_ROWS, LANES), lambda i: (0, 0))],
        out_specs=[pl.BlockSpec((2, n_chunks, 1, LANES), lambda i: (0, 0, 0, 0)), chunked(SUBLANES), table],
        out_shape=[jax.ShapeDtypeStruct((2, n_chunks, 1, LANES), I32),
                   jax.ShapeDtypeStruct((n_chunks, SUBLANES, LANES), F32),
                   jax.ShapeDtypeStruct((2 * ROUTER_ROWS, LANES), I32)],
        scratch_shapes=[pltpu.VMEM((n_chunks, ROUTER_ROWS, LANES), F32),
                        pltpu.VMEM((n_chunks, SUBLANES, LANES), F32)],
        compiler_params=_params(1),
        name="route",
    )(logits, bias)


INVERT_UNROLL = 16


CLEAR_SPAN = 8


def _invert_body(slot_ref, blk_ref, tok_ref, *, n_slots):
    n_tok = slot_ref.shape[0] // 2

    def clear_expert(e, c):
        first_pad = _blk_start(blk_ref, e) * MOE_BLOCK + _blk_count(blk_ref, e)
        end = jnp.where(e == N_EXPERTS - 1, n_slots, _blk_start(blk_ref, e + 1) * MOE_BLOCK)
        span_bits = CLEAR_SPAN.bit_length() - 1
        lo = lax.shift_right_logical(first_pad, span_bits) * CLEAR_SPAN

        def span(j, c2):
            for k in range(CLEAR_SPAN):
                tok_ref[lo + j * CLEAR_SPAN + k] = 0
            return c2

        lax.fori_loop(0, lax.shift_right_logical(end - lo, span_bits), span, 0)
        return c

    lax.fori_loop(0, N_EXPERTS, clear_expert, 0)

    def put(t, c):
        tok_ref[slot_ref[t]] = t
        tok_ref[slot_ref[n_tok + t]] = t
        return c

    lax.fori_loop(0, n_tok, put, 0, unroll=INVERT_UNROLL)


def _invert(slots, blk, n_slots):
    smem = pl.BlockSpec(memory_space=pltpu.SMEM)
    return pl.pallas_call(
        functools.partial(_invert_body, n_slots=n_slots),
        in_specs=[smem, smem],
        out_specs=smem,
        out_shape=jax.ShapeDtypeStruct((n_slots,), I32),
        name="invert",
    )(slots, blk)


GATHER_AHEAD = 3
GATHER_SLOTS = GATHER_AHEAD + 1


def _row_gather_start(idx_ref, base, src_hbm, dst, sem, n_rows, priorities):
    for r in range(n_rows):
        tok = idx_ref[base + r]
        pltpu.make_async_copy(src_hbm.at[pl.ds(tok, 1), :], dst.at[pl.ds(r, 1), :], sem).start(
            priority=priorities[r % len(priorities)])


def _experts_body(bstart_ref, tok_ref, h2p_hbm, w1_ref, w3_ref, w2_ref, eo_hbm, xbuf, obuf, w1b, w3b, w2b, gsem,
                  osem, *, n_blocks):
    e = pl.program_id(0)
    n_exp = pl.num_programs(0)
    b0 = _blk_start(bstart_ref, e)
    nb = _blk_start(bstart_ref, e + 1) - b0
    n_used = _blk_start(bstart_ref, n_exp)

    def gather(block):
        s = lax.rem(block, GATHER_SLOTS)
        src_block = jnp.minimum(block, n_used - 1)
        _row_gather_start(tok_ref, src_block * MOE_BLOCK, h2p_hbm, xbuf.at[s], gsem.at[s], MOE_BLOCK, (1, 0))

    def gather_wait(block):
        s = lax.rem(block, GATHER_SLOTS)
        pltpu.make_async_copy(xbuf.at[s], xbuf.at[s], gsem.at[s]).wait()

    def out_copy(block, s):
        rows = pl.ds(pl.multiple_of(block * MOE_BLOCK, MOE_BLOCK), MOE_BLOCK)
        return pltpu.make_async_copy(obuf.at[s], eo_hbm.at[rows, :], osem.at[s])

    @pl.when(e == 0)
    def _():
        for k in range(GATHER_AHEAD):
            gather(k)

    @pl.when(nb > 0)
    def _():
        w1b[...] = w1_ref[...].astype(BF16)
        w3b[...] = w3_ref[...].astype(BF16)
        w2b[...] = w2_ref[...].astype(BF16)

    def block(j, carry):
        b = b0 + j
        s = lax.rem(b, 2)
        gather(b + GATHER_AHEAD)
        gather_wait(b)
        hi, lo = _unpack_bf16_pairs(xbuf[lax.rem(b, GATHER_SLOTS)])
        x = jnp.concatenate([hi.astype(BF16), lo.astype(BF16)], axis=1)
        a = _dot(x, w1b[...])
        g = _dot(x, w3b[...])
        hdn = (a * jax.nn.sigmoid(a) * g).astype(BF16)
        o = _dot(hdn, w2b[...])

        @pl.when(b >= 2)
        def _():
            out_copy(b - 2, s).wait()

        obuf[s] = _pack_bf16_pairs(o[:, :D_HALF], o[:, D_HALF:])
        out_copy(b, s).start()
        return carry

    lax.fori_loop(0, nb, block, 0)

    @pl.when(e == n_exp - 1)
    def _():
        for k in range(GATHER_AHEAD):
            gather_wait(n_used + k)

        @pl.when(n_used >= 2)
        def _():
            out_copy(n_used - 2, lax.rem(n_used, 2)).wait()

        out_copy(n_used - 1, lax.rem(n_used - 1, 2)).wait()
        obuf[0] = jnp.zeros((MOE_BLOCK, D_HALF), U32)

        def fill(tb, carry):
            out_copy(tb, 0).start()
            return carry

        lax.fori_loop(n_used, n_blocks, fill, 0)

        def drain(tb, carry):
            out_copy(tb, 0).wait()
            return carry

        lax.fori_loop(n_used, n_blocks, drain, 0)


def _experts(bstart, tok, h2p, w1, w3, w2, n_blocks):
    def wspec(r, c):
        return pl.BlockSpec((None, r, c), lambda e, bstart_ref, tok_ref: (e, 0, 0))

    grid_spec = pltpu.PrefetchScalarGridSpec(
        num_scalar_prefetch=2,
        grid=(N_EXPERTS,),
        in_specs=[pl.BlockSpec(memory_space=pl.ANY),
                  wspec(D_MODEL, D_EXPERT), wspec(D_MODEL, D_EXPERT), wspec(D_EXPERT, D_MODEL)],
        out_specs=pl.BlockSpec(memory_space=pl.ANY),
        scratch_shapes=[pltpu.VMEM((GATHER_SLOTS, MOE_BLOCK, D_HALF), U32), pltpu.VMEM((2, MOE_BLOCK, D_HALF), U32),
                        pltpu.VMEM((D_MODEL, D_EXPERT), BF16), pltpu.VMEM((D_MODEL, D_EXPERT), BF16),
                        pltpu.VMEM((D_EXPERT, D_MODEL), BF16),
                        pltpu.SemaphoreType.DMA((GATHER_SLOTS,)), pltpu.SemaphoreType.DMA((2,))],
    )
    return pl.pallas_call(
        functools.partial(_experts_body, n_blocks=n_blocks),
        grid_spec=grid_spec,
        out_shape=jax.ShapeDtypeStruct((n_blocks * MOE_BLOCK, D_HALF), U32),
        compiler_params=_params(1),
        name="experts",
    )(bstart, tok, h2p, w1, w3, w2)


def _combine_body(slot_ref, x1_ref, gate_ref, eo_hbm, yp_ref, ys_ref, buf, sem, *, n_tok, n_prompt_tiles):
    i = pl.program_id(0)
    tm = x1_ref.shape[0]
    n_tiles = n_tok // tm
    slot = lax.rem(i, GATHER_SLOTS)

    def start(tile):
        s = lax.rem(tile, GATHER_SLOTS)
        base = jnp.minimum(tile, n_tiles - 1) * tm
        _row_gather_start(slot_ref, base, eo_hbm, buf.at[s, 0], sem.at[s], tm, (0, 1))
        _row_gather_start(slot_ref, n_tok + base, eo_hbm, buf.at[s, 1], sem.at[s], tm, (0, 1))

    def wait(tile):
        s = lax.rem(tile, GATHER_SLOTS)
        pltpu.make_async_copy(buf.at[s], buf.at[s], sem.at[s]).wait()

    @pl.when(i == 0)
    def _():
        for k in range(GATHER_AHEAD):
            start(k)

    start(i + GATHER_AHEAD)
    wait(i)

    @pl.when(i == n_tiles - 1)
    def _():
        for k in range(GATHER_AHEAD):
            wait(n_tiles + k)

    diag = lax.broadcasted_iota(I32, (tm, tm), 0) == lax.broadcasted_iota(I32, (tm, tm), 1)
    g0 = jnp.sum(jnp.where(diag, gate_ref[0:1, :], 0.0), axis=1, keepdims=True)
    g1 = jnp.sum(jnp.where(diag, gate_ref[1:2, :], 0.0), axis=1, keepdims=True)
    hi0, lo0 = _unpack_bf16_pairs(buf[slot, 0])
    hi1, lo1 = _unpack_bf16_pairs(buf[slot, 1])
    y_hi = x1_ref[:, :D_HALF] + g0 * hi0 + g1 * hi1
    y_lo = x1_ref[:, D_HALF:] + g0 * lo0 + g1 * lo1

    @pl.when(i < n_prompt_tiles)
    def _():
        yp_ref[:, :D_HALF] = y_hi
        yp_ref[:, D_HALF:] = y_lo

    @pl.when(i >= n_prompt_tiles)
    def _():
        ys_ref[:, :D_HALF] = y_hi
        ys_ref[:, D_HALF:] = y_lo


def _combine(slots, x1, gates, eo, n_prompt, n_tok, tm):
    n_tiles = n_tok // tm
    n_prompt_tiles = n_prompt // tm
    body = functools.partial(_combine_body, n_tok=n_tok, n_prompt_tiles=n_prompt_tiles)
    grid_spec = pltpu.PrefetchScalarGridSpec(
        num_scalar_prefetch=1,
        grid=(n_tiles,),
        in_specs=[pl.BlockSpec((tm, D_MODEL), lambda i, s: (i, 0)),
                  pl.BlockSpec((None, SUBLANES, tm), lambda i, s: (i, 0, 0)),
                  pl.BlockSpec(memory_space=pl.ANY)],
        out_specs=[pl.BlockSpec((tm, D_MODEL), lambda i, s: (jnp.minimum(i, n_prompt_tiles - 1), 0)),
                   pl.BlockSpec((tm, D_MODEL), lambda i, s: (jnp.maximum(i - n_prompt_tiles, 0), 0))],
        scratch_shapes=[pltpu.VMEM((GATHER_SLOTS, 2, tm, D_HALF), U32), pltpu.SemaphoreType.DMA((GATHER_SLOTS,))],
    )
    return pl.pallas_call(
        body,
        grid_spec=grid_spec,
        out_shape=[jax.ShapeDtypeStruct((n_prompt, D_MODEL), F32),
                   jax.ShapeDtypeStruct((n_tok - n_prompt, D_MODEL), F32)],
        compiler_params=_params(1),
        name="combine",
    )(slots, x1, gates, eo)


def _layer(x_prompt, x_sample, state_conv, cache_k, cache_v, norm1_g, w_in, conv_w, conv_b, conv_norm_g,
           conv_norm_b, q_norm_g, k_norm_g, attn_sinks, w_out, norm2_g, w_rg, b_rg, w_re, b_re, w1, w3, w2):
    b, t, _ = x_prompt.shape
    sb, st, _ = x_sample.shape
    n_p, n_s = b * t, sb * st
    n_tok = n_p + n_s

    g1 = norm1_g.reshape(1, D_MODEL)
    g2 = norm2_g.reshape(1, D_MODEL)
    qg = q_norm_g.reshape(1, HEAD_DIM)
    kg = k_norm_g.reshape(1, HEAD_DIM)
    chunked = lambda a: a.reshape(-1, N_CCHUNK, LANES).transpose(1, 0, 2)
    cw, cb, lg, lb = chunked(conv_w), chunked(conv_b), chunked(conv_norm_g), chunked(conv_norm_b)
    pad_rows = ROUTER_ROWS - N_EXPERT_GROUPS - N_EXPERTS
    pad_lanes = LANES - N_EXPERT_GROUPS - N_EXPERTS
    w_router_bf = jnp.concatenate([w_rg, w_re, jnp.zeros((D_MODEL, pad_lanes), F32)], axis=1).astype(BF16)
    b_router = jnp.broadcast_to(jnp.concatenate([b_rg, b_re, jnp.zeros((pad_rows,), F32)])[:, None],
                                (ROUTER_ROWS, LANES))

    xp2 = x_prompt.reshape(n_p, D_MODEL)
    xs2 = x_sample.transpose(1, 0, 2).reshape(n_s, D_MODEL)

    c_p, q_p, k_p, v_p, conv_p, knew_p, vnew_p, w_in_bf = _in_proj_conv(xp2, g1, w_in, qg, kg, cw, cb, lg, lb, 512, b)
    u_s, q_s, k_s, v_s = _in_proj(xs2, g1, w_in_bf, qg, kg, n_s, F32)

    r3 = lambda a, bb: a.reshape(bb, -1, a.shape[-1])
    a_p = _mixer_prompt(attn_sinks, r3(q_p, b), r3(k_p, b), r3(v_p, b), 1024)
    cat_s, conv_s, knew_s, vnew_s = _mixer_sample(
        attn_sinks, r3(u_s, st), state_conv[0].transpose(1, 0, 2), r3(q_s, st), r3(k_s, st), r3(v_s, st),
        cache_k, cache_v, cw, cb, lg, lb)
    conv_s = conv_s.transpose(1, 0, 2)[None]

    x1, h2p, logits = _out_proj(c_p, a_p.reshape(n_p, D_ATTN), cat_s.reshape(n_s, D_MODEL), xp2, xs2, w_out, g2,
                                w_router_bf, OUT_TILE)

    n_blocks = -(-(n_tok * 2) // MOE_BLOCK) + N_EXPERTS
    slots, gates, blk = _route(logits, b_router, n_tok)
    slots = slots.reshape(2 * n_tok)
    tok = _invert(slots, blk, n_blocks * MOE_BLOCK)
    eo = _experts(blk, tok, h2p, w1, w3, w2, n_blocks)
    y_p, y_s = _combine(slots, x1, gates, eo, n_p, n_tok, COMBINE_TILE)

    return (y_p.reshape(b, t, D_MODEL), y_s.reshape(st, sb, D_MODEL).transpose(1, 0, 2), conv_p[None], knew_p[None],
            vnew_p[None], conv_s, knew_s, vnew_s)


def kernel(x_prompt, x_sample, state_conv, cache_k, cache_v, norm1_g, w_in, conv_w, conv_b, conv_norm_g, conv_norm_b, q_norm_g, k_norm_g, attn_sinks, w_out, norm2_g, w_router_group, b_router_group, w_router_expert, b_router_expert, w1, w3, w2):
    depth = w_in.shape[0]
    assert depth == 1, "single-layer step"
    return _layer(x_prompt, x_sample, state_conv, cache_k, cache_v, norm1_g[0], w_in[0], conv_w[0],
                  conv_b[0], conv_norm_g[0], conv_norm_b[0], q_norm_g[0], k_norm_g[0], attn_sinks[0], w_out[0],
                  norm2_g[0], w_router_group[0], b_router_group[0], w_router_expert[0], b_router_expert[0],
                  w1[0], w3[0], w2[0])
```

```python
import functools
import math

import jax
import jax.numpy as jnp
from jax import lax
from jax.experimental import pallas as pl
from jax.experimental.pallas import tpu as pltpu

F32 = jnp.float32
BF16 = jnp.bfloat16
I32 = jnp.int32

D_MODEL = 2048
D_CONV = 1024
CONV_WIDTH = 31
CONV_HIST = CONV_WIDTH - 1
D_ATTN = 1024
HEAD_DIM = 128
N_HEADS = 8
N_KV_HEADS = 2
GQA_GROUP = N_HEADS // N_KV_HEADS
D_KV = N_KV_HEADS * HEAD_DIM
WINDOW = 128
BLOCK_Q = 128
SCALE = 1.0 / math.sqrt(HEAD_DIM)
N_EXPERT_GROUPS = 4
EXPERTS_PER_GROUP = 8
N_EXPERTS = N_EXPERT_GROUPS * EXPERTS_PER_GROUP
D_EXPERT = 512
IN_TILE = 512
ATTN_TILE = 1024
OUT_TILE = 256
MOE_BLOCK = 256
COMBINE_TILE = 128
D_IN = 2 * D_CONV + D_ATTN + 2 * D_KV
EPS = 1e-6
PAST_LEN = 16384

LANES = 128
SUBLANES = 8
MXU_COLS = 256
VMEM_LIMIT_BYTES = 56 * 1024 * 1024
NEG_INF = float("-inf")
EXPERT_ROW0 = N_EXPERT_GROUPS


def _params(n_axes):
    return pltpu.CompilerParams(dimension_semantics=("arbitrary",) * n_axes,
                                vmem_limit_bytes=VMEM_LIMIT_BYTES)


def _resident(shape):
    nd = len(shape)
    return pl.BlockSpec(shape, lambda *_: (0,) * nd, pipeline_mode=pl.Buffered(1))


def _dot(a, b):
    return jnp.dot(a, b, preferred_element_type=F32)


def _in_proj_body(x_ref, g1_ref, w_ref, qg_ref, kg_ref, u_ref, q_ref, k_ref, v_ref, n_ref):
    x = x_ref[...]
    ms = jnp.mean(x * x, axis=-1, keepdims=True)
    n_ref[...] = (x * lax.rsqrt(ms + EPS) * g1_ref[...]).astype(BF16)

    def head_norm(h, g):
        return h * lax.rsqrt(jnp.mean(h * h, axis=-1, keepdims=True) + EPS) * g

    ch = MXU_COLS
    for c in range(D_CONV // ch):
        a = _dot(n_ref[...], w_ref[:, c * ch:(c + 1) * ch])
        g = _dot(n_ref[...], w_ref[:, D_CONV + c * ch:D_CONV + (c + 1) * ch])
        u_ref[:, c * ch:(c + 1) * ch] = a * jax.nn.sigmoid(g)
    q_off = 2 * D_CONV
    for c in range(D_ATTN // ch):
        qq = _dot(n_ref[...], w_ref[:, q_off + c * ch:q_off + (c + 1) * ch])
        for j in range(ch // HEAD_DIM):
            qh = head_norm(qq[:, j * HEAD_DIM:(j + 1) * HEAD_DIM], qg_ref[...])
            q_ref[:, c * ch + j * HEAD_DIM:c * ch + (j + 1) * HEAD_DIM] = qh.astype(q_ref.dtype)
    k_off = q_off + D_ATTN
    kk = _dot(n_ref[...], w_ref[:, k_off:k_off + D_KV])
    for j in range(N_KV_HEADS):
        k_ref[:, j * HEAD_DIM:(j + 1) * HEAD_DIM] = head_norm(kk[:, j * HEAD_DIM:(j + 1) * HEAD_DIM], kg_ref[...])
    v_ref[...] = _dot(n_ref[...], w_ref[:, k_off + D_KV:k_off + 2 * D_KV])


CAST_COLS = 512


def _load_weight_as_bf16(w_hbm, w_bf, stage, sem):
    n_chunks = w_hbm.shape[1] // CAST_COLS

    def copy(c):
        return pltpu.make_async_copy(w_hbm.at[:, c * CAST_COLS:(c + 1) * CAST_COLS], stage.at[c % 2], sem.at[c % 2])

    copy(0).start()
    for c in range(n_chunks):
        if c + 1 < n_chunks:
            copy(c + 1).start()
        copy(c).wait()
        w_bf[:, c * CAST_COLS:(c + 1) * CAST_COLS] = stage[c % 2].astype(BF16)


def _in_proj_conv_body(x_ref, g1_ref, w_hbm, qg_ref, kg_ref, cw_ref, cb_ref, lg_ref, lb_ref, c_ref, q_ref, k_ref,
                       v_ref, ut_ref, kt_ref, vt_ref, wout_hbm, n_ref, ue_ref, conv_ref, w_ref, wstage, wsem, osem, *,
                       tiles_per_seq):
    tm = x_ref.shape[0]
    step = pl.program_id(0)
    first = lax.rem(step, tiles_per_seq) == 0
    w_out_copy = pltpu.make_async_copy(w_ref, wout_hbm, osem.at[0])

    @pl.when(step == 0)
    def _():
        _load_weight_as_bf16(w_hbm, w_ref, wstage, wsem)
        w_out_copy.start()

    @pl.when(step == pl.num_programs(0) - 1)
    def _():
        w_out_copy.wait()

    @pl.when(first)
    def _():
        for c in range(N_CCHUNK):
            ue_ref[c, 0:CONV_HALO, :] = jnp.zeros((CONV_HALO, LANES), F32)

    @pl.when(jnp.logical_not(first))
    def _():
        for c in range(N_CCHUNK):
            ue_ref[c, 0:CONV_HALO, :] = ue_ref[c, tm:tm + CONV_HALO, :]

    x = x_ref[...]
    ms = jnp.mean(x * x, axis=-1, keepdims=True)
    n_ref[...] = (x * lax.rsqrt(ms + EPS) * g1_ref[...]).astype(BF16)

    def head_norm(h, g):
        return h * lax.rsqrt(jnp.mean(h * h, axis=-1, keepdims=True) + EPS) * g

    ch = MXU_COLS
    lanes_per = ch // LANES
    n_glu = D_CONV // ch
    q_off = 2 * D_CONV

    row_blocks = [slice(r, r + DOT_ROWS) for r in range(0, tm, DOT_ROWS)]

    for c in range(n_glu):
        for rows in row_blocks:
            a = _dot(n_ref[rows, :], w_ref[:, c * ch:(c + 1) * ch])
            g = _dot(n_ref[rows, :], w_ref[:, D_CONV + c * ch:D_CONV + (c + 1) * ch])
            u = a * jax.nn.sigmoid(g)
            for half in range(lanes_per):
                ue_ref[c * lanes_per + half, CONV_HALO + rows.start:CONV_HALO + rows.stop, :] = (
                    u[:, half * LANES:(half + 1) * LANES])
        for half in range(lanes_per):
            lc = c * lanes_per + half
            _conv_chunk(ue_ref, lc, cw_ref[lc], cb_ref[lc], conv_ref, tm)
    _ln_swish(conv_ref, lg_ref, lb_ref, c_ref, tm)

    for c in range(D_ATTN // ch):
        for rows in row_blocks:
            qq = _dot(n_ref[rows, :], w_ref[:, q_off + c * ch:q_off + (c + 1) * ch])
            for j in range(ch // HEAD_DIM):
                qh = head_norm(qq[:, j * HEAD_DIM:(j + 1) * HEAD_DIM], qg_ref[...])
                q_ref[rows, c * ch + j * HEAD_DIM:c * ch + (j + 1) * HEAD_DIM] = qh.astype(q_ref.dtype)
    k_off = q_off + D_ATTN
    for rows in row_blocks:
        kk = _dot(n_ref[rows, :], w_ref[:, k_off:k_off + D_KV])
        vv = _dot(n_ref[rows, :], w_ref[:, k_off + D_KV:k_off + 2 * D_KV])
        v_ref[rows, :] = vv
        for j in range(N_KV_HEADS):
            hs = slice(j * HEAD_DIM, (j + 1) * HEAD_DIM)
            k_ref[rows, hs] = head_norm(kk[:, hs], kg_ref[...])
    for j in range(N_KV_HEADS):
        hs = slice(j * HEAD_DIM, (j + 1) * HEAD_DIM)
        kt_ref[:, j, :] = k_ref[tm - WINDOW:tm, hs]
        vt_ref[:, j, :] = v_ref[tm - WINDOW:tm, hs]
    for c in range(N_CCHUNK):
        ut_ref[:, c * LANES:(c + 1) * LANES] = ue_ref[c, CONV_HALO + tm - CONV_HIST:CONV_HALO + tm, :]


def _in_proj_conv(x2, g1, w_in, qg, kg, cw, cb, lg, lb, tm, n_seq):
    n = x2.shape[0]
    tiles_per_seq = n // n_seq // tm
    row = lambda w: pl.BlockSpec((tm, w), lambda i: (i, 0))
    seq = lambda *dims: pl.BlockSpec((None,) + dims, lambda i: (i // tiles_per_seq,) + (0,) * len(dims))
    hbm = pl.BlockSpec(memory_space=pl.ANY)
    body = functools.partial(_in_proj_conv_body, tiles_per_seq=tiles_per_seq)
    return pl.pallas_call(
        body,
        grid=(n // tm,),
        in_specs=[row(D_MODEL), _resident((1, D_MODEL)), hbm,
                  _resident((1, HEAD_DIM)), _resident((1, HEAD_DIM)),
                  _resident((N_CCHUNK, CONV_WIDTH, LANES)), _resident((N_CCHUNK, 1, LANES)),
                  _resident((N_CCHUNK, 1, LANES)), _resident((N_CCHUNK, 1, LANES))],
        out_specs=[row(D_CONV), row(D_ATTN), row(D_KV), row(D_KV),
                   seq(CONV_HIST, D_CONV), seq(WINDOW, N_KV_HEADS, HEAD_DIM), seq(WINDOW, N_KV_HEADS, HEAD_DIM), hbm],
        out_shape=[jax.ShapeDtypeStruct((n, D_CONV), BF16), jax.ShapeDtypeStruct((n, D_ATTN), BF16),
                   jax.ShapeDtypeStruct((n, D_KV), F32), jax.ShapeDtypeStruct((n, D_KV), F32),
                   jax.ShapeDtypeStruct((n_seq, CONV_HIST, D_CONV), F32),
                   jax.ShapeDtypeStruct((n_seq, WINDOW, N_KV_HEADS, HEAD_DIM), F32),
                   jax.ShapeDtypeStruct((n_seq, WINDOW, N_KV_HEADS, HEAD_DIM), F32),
                   jax.ShapeDtypeStruct((D_MODEL, D_IN), BF16)],
        scratch_shapes=[pltpu.VMEM((tm, D_MODEL), BF16),
                        pltpu.VMEM((N_CCHUNK, CONV_HALO + tm, LANES), F32),
                        pltpu.VMEM((N_CCHUNK, tm, LANES), F32),
                        pltpu.VMEM((D_MODEL, D_IN), BF16), pltpu.VMEM((2, D_MODEL, CAST_COLS), F32),
                        pltpu.SemaphoreType.DMA((2,)), pltpu.SemaphoreType.DMA((1,))],
        compiler_params=_params(1),
        name="in_proj_conv",
    )(x2, g1, w_in, qg, kg, cw, cb, lg, lb)


def _in_proj(x2, g1, w_in_bf, qg, kg, tm, q_dtype):
    n = x2.shape[0]
    row = lambda w: pl.BlockSpec((tm, w), lambda i: (i, 0))
    return pl.pallas_call(
        _in_proj_body,
        grid=(n // tm,),
        in_specs=[row(D_MODEL), _resident((1, D_MODEL)), _resident((D_MODEL, D_IN)),
                  _resident((1, HEAD_DIM)), _resident((1, HEAD_DIM))],
        out_specs=[row(D_CONV), row(D_ATTN), row(D_KV), row(D_KV)],
        out_shape=[jax.ShapeDtypeStruct((n, D_CONV), F32), jax.ShapeDtypeStruct((n, D_ATTN), q_dtype),
                   jax.ShapeDtypeStruct((n, D_KV), F32), jax.ShapeDtypeStruct((n, D_KV), F32)],
        scratch_shapes=[pltpu.VMEM((tm, D_MODEL), BF16)],
        compiler_params=_params(1),
        name="in_proj",
    )(x2, g1, w_in_bf, qg, kg)


N_CCHUNK = D_CONV // LANES
CONV_ROWS = 64


CONV_HALO = 32
DOT_ROWS = 256


def _conv_chunk(ue_ref, c, wc, bias, conv_ref, rows):
    base = CONV_HALO - CONV_HIST
    for r0 in range(0, rows, CONV_ROWS):
        acc = jnp.broadcast_to(bias, (CONV_ROWS, LANES))
        for tap in range(CONV_WIDTH):
            acc = acc + wc[tap:tap + 1, :] * ue_ref[c, base + r0 + tap:base + r0 + tap + CONV_ROWS, :]
        conv_ref[c, r0:r0 + CONV_ROWS, :] = acc


def _ln_swish(conv_ref, lg_ref, lb_ref, cat_ref, rows):
    tot = jnp.zeros((rows, 1), F32)
    for c in range(N_CCHUNK):
        tot = tot + jnp.sum(conv_ref[c], axis=-1, keepdims=True)
    mean = tot / D_CONV
    var = jnp.zeros((rows, 1), F32)
    for c in range(N_CCHUNK):
        xc = conv_ref[c] - mean
        var = var + jnp.sum(xc * xc, axis=-1, keepdims=True)
    rstd = lax.rsqrt(var / D_CONV + EPS)
    for c in range(N_CCHUNK):
        y = (conv_ref[c] - mean) * rstd * lg_ref[c] + lb_ref[c]
        cat_ref[:, c * LANES:(c + 1) * LANES] = (y * jax.nn.sigmoid(y)).astype(cat_ref.dtype)


def _sink_softmax_rows(s, sink):
    m = jnp.maximum(jnp.max(s, axis=-1, keepdims=True), sink)
    p = jnp.exp(s - m)
    return p / (jnp.sum(p, axis=-1, keepdims=True) + jnp.exp(sink - m))


def _alibi_slope(head):
    return 2.0 ** (-8.0 * (head + 1) / N_HEADS)


def _mixer_prompt_body(sink_ref, q_ref, k_ref, kh_ref, v_ref, vh_ref, cat_ref, *, tm):
    j = pl.program_id(1)
    has_prev = j > 0
    qi = lax.broadcasted_iota(I32, (BLOCK_Q, 2 * BLOCK_Q), 0)
    kj = lax.broadcasted_iota(I32, (BLOCK_Q, 2 * BLOCK_Q), 1)
    dist = qi + BLOCK_Q - kj
    distf = dist.astype(F32)
    band = jnp.where(dist >= 0, jnp.where(dist < WINDOW, 0.0, NEG_INF), NEG_INF)
    band_first = jnp.where(kj >= BLOCK_Q, band, jnp.where(has_prev, band, NEG_INF))

    for qb in range(tm // BLOCK_Q):
        rows = slice(qb * BLOCK_Q, (qb + 1) * BLOCK_Q)
        prev = slice((qb - 1) * BLOCK_Q, qb * BLOCK_Q)
        mask = band_first if qb == 0 else band
        for kv in range(N_KV_HEADS):
            hs = slice(kv * HEAD_DIM, (kv + 1) * HEAD_DIM)
            k_prev = kh_ref[:, hs] if qb == 0 else k_ref[prev, hs]
            v_prev = vh_ref[:, hs] if qb == 0 else v_ref[prev, hs]
            kk = jnp.concatenate([k_prev, k_ref[rows, hs]], axis=0).astype(BF16)
            vv = jnp.concatenate([v_prev, v_ref[rows, hs]], axis=0).astype(BF16)
            heads = [kv * GQA_GROUP + g for g in range(GQA_GROUP)]
            qs = jnp.concatenate([q_ref[rows, h * HEAD_DIM:(h + 1) * HEAD_DIM] for h in heads], axis=0)
            s = lax.dot_general(qs, kk, (((1,), (1,)), ((), ())), preferred_element_type=F32)
            ps = []
            for g, h in enumerate(heads):
                sg = s[g * BLOCK_Q:(g + 1) * BLOCK_Q] * SCALE - _alibi_slope(h) * distf + mask
                ps.append(_sink_softmax_rows(sg, sink_ref[h]).astype(BF16))
            o = _dot(jnp.concatenate(ps, axis=0), vv)
            for g, h in enumerate(heads):
                cat_ref[rows, h * HEAD_DIM:(h + 1) * HEAD_DIM] = o[g * BLOCK_Q:(g + 1) * BLOCK_Q].astype(cat_ref.dtype)


def _mixer_prompt(sinks, q, k, v, tm):
    b, t, _ = q.shape
    kpb = tm // BLOCK_Q
    main = lambda w: pl.BlockSpec((None, tm, w), lambda bi, j: (bi, j, 0))
    prev_block = pl.BlockSpec((None, BLOCK_Q, D_KV), lambda bi, j: (bi, jnp.maximum(j * kpb - 1, 0), 0))
    body = functools.partial(_mixer_prompt_body, tm=tm)
    return pl.pallas_call(
        body,
        grid=(b, t // tm),
        in_specs=[pl.BlockSpec(memory_space=pltpu.SMEM), main(D_ATTN), main(D_KV), prev_block, main(D_KV),
                  prev_block],
        out_specs=main(D_ATTN),
        out_shape=jax.ShapeDtypeStruct((b, t, D_ATTN), BF16),
        compiler_params=_params(2),
        name="mixer_prompt",
    )(sinks, q, k, k, v, v)


KEY_PAD = 8


SAMPLE_SEQS = 8


def _sample_conv_ln_swish(st_ref, u_ref, cw_ref, cb_ref, lg_ref, lb_ref, cat_ref, t_new):
    hist, n_seq, _ = st_ref.shape
    conv = [[None] * N_CCHUNK for _ in range(t_new)]
    for c in range(N_CCHUNK):
        cs = slice(c * LANES, (c + 1) * LANES)
        wc = cw_ref[c]
        pos = [st_ref[p, :, cs] for p in range(hist)] + [u_ref[t, :, cs] for t in range(t_new)]
        for t in range(t_new):
            acc = jnp.broadcast_to(cb_ref[c], (n_seq, LANES))
            for tap in range(CONV_WIDTH):
                acc = acc + wc[tap:tap + 1, :] * pos[t + tap]
            conv[t][c] = acc
    for t in range(t_new):
        tot = jnp.zeros((n_seq, 1), F32)
        for a in conv[t]:
            tot = tot + jnp.sum(a, axis=-1, keepdims=True)
        mean = tot / D_CONV
        var = jnp.zeros((n_seq, 1), F32)
        for a in conv[t]:
            var = var + jnp.sum((a - mean) * (a - mean), axis=-1, keepdims=True)
        rstd = lax.rsqrt(var / D_CONV + EPS)
        for c, a in enumerate(conv[t]):
            y = (a - mean) * rstd * lg_ref[c] + lb_ref[c]
            cat_ref[t, :, c * LANES:(c + 1) * LANES] = (y * jax.nn.sigmoid(y)).astype(cat_ref.dtype)


def _mixer_sample_body(sink_ref, u_ref, st_ref, q_ref, k_ref, v_ref, ck_ref, cv_ref, cw_ref, cb_ref, lg_ref,
                       lb_ref, cat_ref, nst_ref, nk_ref, nv_ref, kk_ref, vv_ref, qs_ref, *, t_new):
    hist, n_seq, _ = st_ref.shape
    w_past = ck_ref.shape[1]
    n_keys = w_past + KEY_PAD
    n_rows, n_cols = GQA_GROUP * t_new * n_seq, n_seq * n_keys

    _sample_conv_ln_swish(st_ref, u_ref, cw_ref, cb_ref, lg_ref, lb_ref, cat_ref, t_new)
    nst_ref[0:hist - t_new] = st_ref[t_new:hist]
    nst_ref[hist - t_new:hist] = u_ref[...]

    for i in range(n_seq):
        nk_ref[i, 0:w_past - t_new] = ck_ref[i, t_new:w_past]
        nv_ref[i, 0:w_past - t_new] = cv_ref[i, t_new:w_past]
        kk_ref[i, w_past:n_keys, :] = jnp.zeros((KEY_PAD, D_KV), F32)
        vv_ref[i, w_past:n_keys, :] = jnp.zeros((KEY_PAD, D_KV), F32)
        for t in range(t_new):
            kk_ref[i, w_past + t:w_past + t + 1, :] = k_ref[t, i:i + 1, :]
            vv_ref[i, w_past + t:w_past + t + 1, :] = v_ref[t, i:i + 1, :]
        for h in range(N_KV_HEADS):
            hs = slice(h * HEAD_DIM, (h + 1) * HEAD_DIM)
            kk_ref[i, 0:w_past, hs] = ck_ref[i, :, h, :]
            vv_ref[i, 0:w_past, hs] = cv_ref[i, :, h, :]
            for t in range(t_new):
                nk_ref[i, w_past - t_new + t, h:h + 1, :] = k_ref[t, i:i + 1, hs]
                nv_ref[i, w_past - t_new + t, h:h + 1, :] = v_ref[t, i:i + 1, hs]

    row = lax.broadcasted_iota(I32, (n_rows, n_cols), 0)
    col = lax.broadcasted_iota(I32, (n_rows, n_cols), 1)
    seq_bits, tok_bits = n_seq.bit_length() - 1, t_new.bit_length() - 1
    assert (1 << seq_bits, 1 << tok_bits) == (n_seq, t_new), "sequence and token counts must be powers of two"
    tok = (row >> seq_bits) & (t_new - 1)
    key = col - (row & (n_seq - 1)) * n_keys
    dist = tok + w_past - key
    distf = dist.astype(F32)
    mask = jnp.where(dist >= 0, jnp.where(dist < WINDOW, 0.0, NEG_INF), NEG_INF)
    row1 = lax.broadcasted_iota(I32, (n_rows, 1), 0)
    grp = row1 >> (seq_bits + tok_bits)
    blk = lambda g, t: slice((g * t_new + t) * n_seq, (g * t_new + t + 1) * n_seq)

    for kv in range(N_KV_HEADS):
        hs = slice(kv * HEAD_DIM, (kv + 1) * HEAD_DIM)
        slope = jnp.zeros((n_rows, 1), F32)
        sink = jnp.zeros((n_rows, 1), F32)
        for g in range(GQA_GROUP):
            h = kv * GQA_GROUP + g
            slope = jnp.where(grp == g, _alibi_slope(h), slope)
            sink = jnp.where(grp == g, sink_ref[h], sink)
            for t in range(t_new):
                qs_ref[blk(g, t), :] = q_ref[t, :, h * HEAD_DIM:(h + 1) * HEAD_DIM]
        kk = kk_ref[:, :, hs].reshape(n_cols, HEAD_DIM).astype(BF16)
        vv = vv_ref[:, :, hs].reshape(n_cols, HEAD_DIM).astype(BF16)
        s = lax.dot_general(qs_ref[...].astype(BF16), kk, (((1,), (1,)), ((), ())), preferred_element_type=F32)
        sg = s * SCALE - slope * distf + mask
        o = _dot(_sink_softmax_rows(sg, sink).astype(BF16), vv)
        for g in range(GQA_GROUP):
            h = kv * GQA_GROUP + g
            for t in range(t_new):
                cat_ref[t, :, D_CONV + h * HEAD_DIM:D_CONV + (h + 1) * HEAD_DIM] = o[blk(g, t)].astype(cat_ref.dtype)


def _mixer_sample(sinks, u, state, q, k, v, ck, cv, cw, cb, lg, lb):
    t_new, b, _ = u.shape
    hist = state.shape[0]
    w_past = ck.shape[2]
    n_seq = SAMPLE_SEQS
    per = lambda r, w: pl.BlockSpec((r, n_seq, w), lambda bi: (0, bi, 0))
    cache = pl.BlockSpec((None, n_seq, w_past, N_KV_HEADS, HEAD_DIM), lambda bi: (0, bi, 0, 0, 0))
    body = functools.partial(_mixer_sample_body, t_new=t_new)
    return pl.pallas_call(
        body,
        grid=(b // n_seq,),
        in_specs=[pl.BlockSpec(memory_space=pltpu.SMEM),
                  per(t_new, D_CONV), per(hist, D_CONV), per(t_new, D_ATTN), per(t_new, D_KV), per(t_new, D_KV),
                  cache, cache,
                  _resident((N_CCHUNK, CONV_WIDTH, LANES)), _resident((N_CCHUNK, 1, LANES)),
                  _resident((N_CCHUNK, 1, LANES)), _resident((N_CCHUNK, 1, LANES))],
        out_specs=[per(t_new, D_MODEL), per(hist, D_CONV), cache, cache],
        out_shape=[jax.ShapeDtypeStruct((t_new, b, D_MODEL), F32),
                   jax.ShapeDtypeStruct(state.shape, F32),
                   jax.ShapeDtypeStruct(ck.shape, F32),
                   jax.ShapeDtypeStruct(cv.shape, F32)],
        scratch_shapes=[pltpu.VMEM((n_seq, w_past + KEY_PAD, D_KV), F32),
                        pltpu.VMEM((n_seq, w_past + KEY_PAD, D_KV), F32),
                        pltpu.VMEM((n_seq * GQA_GROUP * t_new, HEAD_DIM), F32)],
        compiler_params=_params(1),
        name="mixer_sample",
    )(sinks, u, state, q, k, v, ck, cv, cw, cb, lg, lb)


OUT_CHUNK = 512
D_HALF = D_MODEL // 2
U32 = jnp.uint32


def _pack_bf16_pairs(hi, lo):
    hi_bits = lax.bitcast_convert_type(hi.astype(BF16).astype(F32), U32)
    lo_bits = lax.bitcast_convert_type(lo.astype(BF16).astype(F32), U32)
    return hi_bits | (lo_bits >> 16)


def _unpack_bf16_pairs(words):
    hi = lax.bitcast_convert_type(words & U32(0xFFFF0000), F32)
    lo = lax.bitcast_convert_type(words << 16, F32)
    return hi, lo


def _out_proj_body(catc_ref, cata_ref, cats_ref, xp_ref, xs_ref, wo_hbm, g2_ref, wr_ref, x1_ref, h2p_ref, lg_ref,
                   h_ref, wo_ref, wstage, wsem, *, n_prompt_tiles):
    is_prompt = pl.program_id(0) < n_prompt_tiles
    tm = catc_ref.shape[0]

    @pl.when(pl.program_id(0) == 0)
    def _():
        _load_weight_as_bf16(wo_hbm, wo_ref, wstage, wsem)

    def sample_rows(v):
        return jnp.concatenate([v, jnp.zeros((tm - v.shape[0], v.shape[1]), v.dtype)], axis=0)

    cat = jnp.where(is_prompt, jnp.concatenate([catc_ref[...], cata_ref[...]], axis=1),
                    sample_rows(cats_ref[...].astype(BF16)))
    ss = jnp.zeros((tm, 1), F32)
    for c in range(D_MODEL // OUT_CHUNK):
        cs = slice(c * OUT_CHUNK, (c + 1) * OUT_CHUNK)
        y = jnp.where(is_prompt, xp_ref[:, cs], sample_rows(xs_ref[:, cs])) + _dot(cat, wo_ref[:, cs])
        x1_ref[:, cs] = y
        ss = ss + jnp.sum(y * y, axis=-1, keepdims=True)
    r = lax.rsqrt(ss / D_MODEL + EPS)
    for c in range(D_MODEL // OUT_CHUNK):
        cs = slice(c * OUT_CHUNK, (c + 1) * OUT_CHUNK)
        h_ref[:, cs] = x1_ref[:, cs] * r * g2_ref[:, cs]
    for c in range(D_HALF // OUT_CHUNK):
        cs = slice(c * OUT_CHUNK, (c + 1) * OUT_CHUNK)
        cs_lo = slice(D_HALF + c * OUT_CHUNK, D_HALF + (c + 1) * OUT_CHUNK)
        h2p_ref[:, cs] = _pack_bf16_pairs(h_ref[:, cs], h_ref[:, cs_lo])
    lg = _dot(h_ref[...].astype(BF16), wr_ref[...])
    for j in range(tm // ROUTE_CHUNK):
        lg_ref[j] = lg[j * ROUTE_CHUNK:(j + 1) * ROUTE_CHUNK, :].T[0:ROUTER_ROWS, :]


def _out_proj(cat_conv, cat_attn, cat_s, xp2, xs2, w_out, g2, wr_bf, tm):
    n_prompt_tiles = cat_conv.shape[0] // tm
    n_rows = (n_prompt_tiles + 1) * tm
    n_s = cat_s.shape[0]
    prompt = lambda w: pl.BlockSpec((tm, w), lambda i: (jnp.minimum(i, n_prompt_tiles - 1), 0))
    out_row = lambda w: pl.BlockSpec((tm, w), lambda i: (i, 0))
    body = functools.partial(_out_proj_body, n_prompt_tiles=n_prompt_tiles)
    return pl.pallas_call(
        body,
        grid=(n_prompt_tiles + 1,),
        in_specs=[prompt(D_CONV), prompt(D_ATTN), _resident((n_s, D_MODEL)), prompt(D_MODEL),
                  _resident((n_s, D_MODEL)),
                  pl.BlockSpec(memory_space=pl.ANY), _resident((1, D_MODEL)), _resident((D_MODEL, LANES))],
        out_specs=[out_row(D_MODEL), out_row(D_HALF),
                   pl.BlockSpec((tm // ROUTE_CHUNK, ROUTER_ROWS, ROUTE_CHUNK), lambda i: (i, 0, 0))],
        out_shape=[jax.ShapeDtypeStruct((n_rows, D_MODEL), F32), jax.ShapeDtypeStruct((n_rows, D_HALF), U32),
                   jax.ShapeDtypeStruct((n_rows // ROUTE_CHUNK, ROUTER_ROWS, ROUTE_CHUNK), F32)],
        scratch_shapes=[pltpu.VMEM((tm, D_MODEL), F32), pltpu.VMEM((D_MODEL, D_MODEL), BF16),
                        pltpu.VMEM((2, D_MODEL, CAST_COLS), F32), pltpu.SemaphoreType.DMA((2,))],
        compiler_params=_params(1),
        name="out_proj",
    )(cat_conv, cat_attn, cat_s, xp2, xs2, w_out, g2, wr_bf)


ROUTE_CHUNK = 128
ROUTE_UNROLL = 5


ROUTER_ROWS = 40


def _blk_start(blk_ref, e):
    return blk_ref[EXPERT_ROW0 + e, 0]


def _blk_count(blk_ref, e):
    return blk_ref[ROUTER_ROWS + EXPERT_ROW0 + e, 0]


def _route_body(lg_ref, bias_ref, slot_ref, gate_ref, blk_ref, cum_ref, sel_ref, *, n_tok):
    n_chunks = n_tok // ROUTE_CHUNK
    row = lax.broadcasted_iota(I32, (ROUTER_ROWS, ROUTE_CHUNK), 0).astype(F32)
    ri = lax.broadcasted_iota(I32, (ROUTE_CHUNK, ROUTE_CHUNK), 0)
    ci = lax.broadcasted_iota(I32, (ROUTE_CHUNK, ROUTE_CHUNK), 1)
    earlier_tok = jnp.where(ri < ci, 1.0, 0.0).astype(BF16)
    er = lax.broadcasted_iota(I32, (ROUTER_ROWS, ROUTER_ROWS), 0)
    ec = lax.broadcasted_iota(I32, (ROUTER_ROWS, ROUTER_ROWS), 1)
    earlier_row = jnp.where(ec < er, 1.0, 0.0).astype(BF16)
    sub = lax.broadcasted_iota(I32, (SUBLANES, ROUTE_CHUNK), 0)
    is_group = row < N_EXPERT_GROUPS

    def first_max(vals):
        m = jnp.max(vals, axis=0, keepdims=True)
        idx = jnp.min(jnp.where(vals == m, row, float(ROUTER_ROWS)), axis=0, keepdims=True)
        return m, idx

    def pair_rows(a, b):
        return jnp.where(sub == 0, a, jnp.where(sub == 1, b, 0.0))

    def assign(i, carry):
        l = lg_ref[i] + bias_ref[...]
        gl = jnp.where(is_group, l, NEG_INF)
        g_max, g_idx = first_max(gl)
        g_top = 1.0 / jnp.sum(jnp.exp(gl - g_max), axis=0, keepdims=True)
        lo = EXPERT_ROW0 + g_idx * EXPERTS_PER_GROUP
        el = jnp.where(row >= lo, jnp.where(row < lo + EXPERTS_PER_GROUP, l, NEG_INF), NEG_INF)
        m1, i1 = first_max(el)
        p = jnp.exp(el - m1)
        probs = p / jnp.sum(p, axis=0, keepdims=True)
        e1 = jnp.sum(jnp.where(row == i1, probs, 0.0), axis=0, keepdims=True)
        _, i2 = first_max(jnp.where(row == i1, NEG_INF, el))
        e2 = jnp.sum(jnp.where(row == i2, probs, 0.0), axis=0, keepdims=True)
        gate_ref[i] = pair_rows(g_top * e1 / (e1 + e2), g_top * e2 / (e1 + e2))
        sel_ref[i] = pair_rows(i1, i2)
        onehot = jnp.where(row == i1, 1.0, jnp.where(row == i2, 1.0, 0.0))
        cum_ref[i] = _dot(onehot.astype(BF16), earlier_tok) + carry
        return carry + jnp.sum(onehot, axis=1, keepdims=True)

    counts = lax.fori_loop(0, n_chunks, assign, jnp.zeros((ROUTER_ROWS, 1), F32), unroll=ROUTE_UNROLL)
    n_blocks = jnp.floor((counts + (MOE_BLOCK - 1)) / MOE_BLOCK)
    blk_start = _dot(earlier_row, jnp.broadcast_to(n_blocks, (ROUTER_ROWS, LANES)).astype(BF16))
    row_start = blk_start * MOE_BLOCK

    def place(i, carry):
        pos = cum_ref[i] + row_start
        sel = sel_ref[i]
        s1 = jnp.sum(jnp.where(row == sel[0:1, :], pos, 0.0), axis=0, keepdims=True)
        s2 = jnp.sum(jnp.where(row == sel[1:2, :], pos, 0.0), axis=0, keepdims=True)
        slot_ref[0, i] = s1.astype(I32)
        slot_ref[1, i] = s2.astype(I32)
        return carry

    lax.fori_loop(0, n_chunks, place, 0, unroll=ROUTE_UNROLL)

    blk_ref[0:ROUTER_ROWS, :] = blk_start.astype(I32)
    blk_ref[ROUTER_ROWS:2 * ROUTER_ROWS, :] = jnp.broadcast_to(counts, (ROUTER_ROWS, LANES)).astype(I32)


def _route(logits, bias, n_tok):
    body = functools.partial(_route_body, n_tok=n_tok)
    n_chunks = n_tok // ROUTE_CHUNK
    chunked = lambda r: pl.BlockSpec((n_chunks, r, LANES), lambda i: (0, 0, 0))
    table = pl.BlockSpec((2 * ROUTER_ROWS, LANES), lambda i: (0, 0))
    return pl.pallas_call(
        body,
        grid=(1,),
        in_specs=[chunked(ROUTER_ROWS), pl.BlockSpec((ROUTER_ROWS, LANES), lambda i: (0, 0))],
        out_specs=[pl.BlockSpec((2, n_chunks, 1, LANES), lambda i: (0, 0, 0, 0)), chunked(SUBLANES), table],
        out_shape=[jax.ShapeDtypeStruct((2, n_chunks, 1, LANES), I32),
                   jax.ShapeDtypeStruct((n_chunks, SUBLANES, LANES), F32),
                   jax.ShapeDtypeStruct((2 * ROUTER_ROWS, LANES), I32)],
        scratch_shapes=[pltpu.VMEM((n_chunks, ROUTER_ROWS, LANES), F32),
                        pltpu.VMEM((n_chunks, SUBLANES, LANES), F32)],
        compiler_params=_params(1),
        name="route",
    )(logits, bias)


INVERT_UNROLL = 16


CLEAR_SPAN = 8


def _invert_body(slot_ref, blk_ref, tok_ref, *, n_slots):
    n_tok = slot_ref.shape[0] // 2

    def clear_expert(e, c):
        first_pad = _blk_start(blk_ref, e) * MOE_BLOCK + _blk_count(blk_ref, e)
        end = jnp.where(e == N_EXPERTS - 1, n_slots, _blk_start(blk_ref, e + 1) * MOE_BLOCK)
        span_bits = CLEAR_SPAN.bit_length() - 1
        lo = lax.shift_right_logical(first_pad, span_bits) * CLEAR_SPAN

        def span(j, c2):
            for k in range(CLEAR_SPAN):
                tok_ref[lo + j * CLEAR_SPAN + k] = 0
            return c2

        lax.fori_loop(0, lax.shift_right_logical(end - lo, span_bits), span, 0)
        return c

    lax.fori_loop(0, N_EXPERTS, clear_expert, 0)

    def put(t, c):
        tok_ref[slot_ref[t]] = t
        tok_ref[slot_ref[n_tok + t]] = t
        return c

    lax.fori_loop(0, n_tok, put, 0, unroll=INVERT_UNROLL)


def _invert(slots, blk, n_slots):
    smem = pl.BlockSpec(memory_space=pltpu.SMEM)
    return pl.pallas_call(
        functools.partial(_invert_body, n_slots=n_slots),
        in_specs=[smem, smem],
        out_specs=smem,
        out_shape=jax.ShapeDtypeStruct((n_slots,), I32),
        name="invert",
    )(slots, blk)


GATHER_AHEAD = 3
GATHER_SLOTS = GATHER_AHEAD + 1


def _row_gather_start(idx_ref, base, src_hbm, dst, sem, n_rows, priorities):
    for r in range(n_rows):
        tok = idx_ref[base + r]
        pltpu.make_async_copy(src_hbm.at[pl.ds(tok, 1), :], dst.at[pl.ds(r, 1), :], sem).start(
            priority=priorities[r % len(priorities)])


def _experts_body(bstart_ref, tok_ref, h2p_hbm, w1_ref, w3_ref, w2_ref, eo_hbm, xbuf, obuf, w1b, w3b, w2b, gsem,
                  osem, *, n_blocks):
    e = pl.program_id(0)
    n_exp = pl.num_programs(0)
    b0 = _blk_start(bstart_ref, e)
    nb = _blk_start(bstart_ref, e + 1) - b0
    n_used = _blk_start(bstart_ref, n_exp)

    def gather(block):
        s = lax.rem(block, GATHER_SLOTS)
        src_block = jnp.minimum(block, n_used - 1)
        _row_gather_start(tok_ref, src_block * MOE_BLOCK, h2p_hbm, xbuf.at[s], gsem.at[s], MOE_BLOCK, (1, 0))

    def gather_wait(block):
        s = lax.rem(block, GATHER_SLOTS)
        pltpu.make_async_copy(xbuf.at[s], xbuf.at[s], gsem.at[s]).wait()

    def out_copy(block, s):
        rows = pl.ds(pl.multiple_of(block * MOE_BLOCK, MOE_BLOCK), MOE_BLOCK)
        return pltpu.make_async_copy(obuf.at[s], eo_hbm.at[rows, :], osem.at[s])

    @pl.when(e == 0)
    def _():
        for k in range(GATHER_AHEAD):
            gather(k)

    @pl.when(nb > 0)
    def _():
        w1b[...] = w1_ref[...].astype(BF16)
        w3b[...] = w3_ref[...].astype(BF16)
        w2b[...] = w2_ref[...].astype(BF16)

    def block(j, carry):
        b = b0 + j
        s = lax.rem(b, 2)
        gather(b + GATHER_AHEAD)
        gather_wait(b)
        hi, lo = _unpack_bf16_pairs(xbuf[lax.rem(b, GATHER_SLOTS)])
        x = jnp.concatenate([hi.astype(BF16), lo.astype(BF16)], axis=1)
        a = _dot(x, w1b[...])
        g = _dot(x, w3b[...])
        hdn = (a * jax.nn.sigmoid(a) * g).astype(BF16)
        o = _dot(hdn, w2b[...])

        @pl.when(b >= 2)
        def _():
            out_copy(b - 2, s).wait()

        obuf[s] = _pack_bf16_pairs(o[:, :D_HALF], o[:, D_HALF:])
        out_copy(b, s).start()
        return carry

    lax.fori_loop(0, nb, block, 0)

    @pl.when(e == n_exp - 1)
    def _():
        for k in range(GATHER_AHEAD):
            gather_wait(n_used + k)

        @pl.when(n_used >= 2)
        def _():
            out_copy(n_used - 2, lax.rem(n_used, 2)).wait()

        out_copy(n_used - 1, lax.rem(n_used - 1, 2)).wait()
        obuf[0] = jnp.zeros((MOE_BLOCK, D_HALF), U32)

        def fill(tb, carry):
            out_copy(tb, 0).start()
            return carry

        lax.fori_loop(n_used, n_blocks, fill, 0)

        def drain(tb, carry):
            out_copy(tb, 0).wait()
            return carry

        lax.fori_loop(n_used, n_blocks, drain, 0)


def _experts(bstart, tok, h2p, w1, w3, w2, n_blocks):
    def wspec(r, c):
        return pl.BlockSpec((None, r, c), lambda e, bstart_ref, tok_ref: (e, 0, 0))

    grid_spec = pltpu.PrefetchScalarGridSpec(
        num_scalar_prefetch=2,
        grid=(N_EXPERTS,),
        in_specs=[pl.BlockSpec(memory_space=pl.ANY),
                  wspec(D_MODEL, D_EXPERT), wspec(D_MODEL, D_EXPERT), wspec(D_EXPERT, D_MODEL)],
        out_specs=pl.BlockSpec(memory_space=pl.ANY),
        scratch_shapes=[pltpu.VMEM((GATHER_SLOTS, MOE_BLOCK, D_HALF), U32), pltpu.VMEM((2, MOE_BLOCK, D_HALF), U32),
                        pltpu.VMEM((D_MODEL, D_EXPERT), BF16), pltpu.VMEM((D_MODEL, D_EXPERT), BF16),
                        pltpu.VMEM((D_EXPERT, D_MODEL), BF16),
                        pltpu.SemaphoreType.DMA((GATHER_SLOTS,)), pltpu.SemaphoreType.DMA((2,))],
    )
    return pl.pallas_call(
        functools.partial(_experts_body, n_blocks=n_blocks),
        grid_spec=grid_spec,
        out_shape=jax.ShapeDtypeStruct((n_blocks * MOE_BLOCK, D_HALF), U32),
        compiler_params=_params(1),
        name="experts",
    )(bstart, tok, h2p, w1, w3, w2)


def _combine_body(slot_ref, x1_ref, gate_ref, eo_hbm, yp_ref, ys_ref, buf, sem, *, n_tok, n_prompt_tiles):
    i = pl.program_id(0)
    tm = x1_ref.shape[0]
    n_tiles = n_tok // tm
    slot = lax.rem(i, GATHER_SLOTS)

    def start(tile):
        s = lax.rem(tile, GATHER_SLOTS)
        base = jnp.minimum(tile, n_tiles - 1) * tm
        _row_gather_start(slot_ref, base, eo_hbm, buf.at[s, 0], sem.at[s], tm, (0, 1))
        _row_gather_start(slot_ref, n_tok + base, eo_hbm, buf.at[s, 1], sem.at[s], tm, (0, 1))

    def wait(tile):
        s = lax.rem(tile, GATHER_SLOTS)
        pltpu.make_async_copy(buf.at[s], buf.at[s], sem.at[s]).wait()

    @pl.when(i == 0)
    def _():
        for k in range(GATHER_AHEAD):
            start(k)

    start(i + GATHER_AHEAD)
    wait(i)

    @pl.when(i == n_tiles - 1)
    def _():
        for k in range(GATHER_AHEAD):
            wait(n_tiles + k)

    diag = lax.broadcasted_iota(I32, (tm, tm), 0) == lax.broadcasted_iota(I32, (tm, tm), 1)
    g0 = jnp.sum(jnp.where(diag, gate_ref[0:1, :], 0.0), axis=1, keepdims=True)
    g1 = jnp.sum(jnp.where(diag, gate_ref[1:2, :], 0.0), axis=1, keepdims=True)
    hi0, lo0 = _unpack_bf16_pairs(buf[slot, 0])
    hi1, lo1 = _unpack_bf16_pairs(buf[slot, 1])
    y_hi = x1_ref[:, :D_HALF] + g0 * hi0 + g1 * hi1
    y_lo = x1_ref[:, D_HALF:] + g0 * lo0 + g1 * lo1

    @pl.when(i < n_prompt_tiles)
    def _():
        yp_ref[:, :D_HALF] = y_hi
        yp_ref[:, D_HALF:] = y_lo

    @pl.when(i >= n_prompt_tiles)
    def _():
        ys_ref[:, :D_HALF] = y_hi
        ys_ref[:, D_HALF:] = y_lo


def _combine(slots, x1, gates, eo, n_prompt, n_tok, tm):
    n_tiles = n_tok // tm
    n_prompt_tiles = n_prompt // tm
    body = functools.partial(_combine_body, n_tok=n_tok, n_prompt_tiles=n_prompt_tiles)
    grid_spec = pltpu.PrefetchScalarGridSpec(
        num_scalar_prefetch=1,
        grid=(n_tiles,),
        in_specs=[pl.BlockSpec((tm, D_MODEL), lambda i, s: (i, 0)),
                  pl.BlockSpec((None, SUBLANES, tm), lambda i, s: (i, 0, 0)),
                  pl.BlockSpec(memory_space=pl.ANY)],
        out_specs=[pl.BlockSpec((tm, D_MODEL), lambda i, s: (jnp.minimum(i, n_prompt_tiles - 1), 0)),
                   pl.BlockSpec((tm, D_MODEL), lambda i, s: (jnp.maximum(i - n_prompt_tiles, 0), 0))],
        scratch_shapes=[pltpu.VMEM((GATHER_SLOTS, 2, tm, D_HALF), U32), pltpu.SemaphoreType.DMA((GATHER_SLOTS,))],
    )
    return pl.pallas_call(
        body,
        grid_spec=grid_spec,
        out_shape=[jax.ShapeDtypeStruct((n_prompt, D_MODEL), F32),
                   jax.ShapeDtypeStruct((n_tok - n_prompt, D_MODEL), F32)],
        compiler_params=_params(1),
        name="combine",
    )(slots, x1, gates, eo)


def _layer(x_prompt, x_sample, state_conv, cache_k, cache_v, norm1_g, w_in, conv_w, conv_b, conv_norm_g,
           conv_norm_b, q_norm_g, k_norm_g, attn_sinks, w_out, norm2_g, w_rg, b_rg, w_re, b_re, w1, w3, w2):
    b, t, _ = x_prompt.shape
    sb, st, _ = x_sample.shape
    n_p, n_s = b * t, sb * st
    n_tok = n_p + n_s

    g1 = norm1_g.reshape(1, D_MODEL)
    g2 = norm2_g.reshape(1, D_MODEL)
    qg = q_norm_g.reshape(1, HEAD_DIM)
    kg = k_norm_g.reshape(1, HEAD_DIM)
    chunked = lambda a: a.reshape(-1, N_CCHUNK, LANES).transpose(1, 0, 2)
    cw, cb, lg, lb = chunked(conv_w), chunked(conv_b), chunked(conv_norm_g), chunked(conv_norm_b)
    pad_rows = ROUTER_ROWS - N_EXPERT_GROUPS - N_EXPERTS
    pad_lanes = LANES - N_EXPERT_GROUPS - N_EXPERTS
    w_router_bf = jnp.concatenate([w_rg, w_re, jnp.zeros((D_MODEL, pad_lanes), F32)], axis=1).astype(BF16)
    b_router = jnp.broadcast_to(jnp.concatenate([b_rg, b_re, jnp.zeros((pad_rows,), F32)])[:, None],
                                (ROUTER_ROWS, LANES))

    xp2 = x_prompt.reshape(n_p, D_MODEL)
    xs2 = x_sample.transpose(1, 0, 2).reshape(n_s, D_MODEL)

    c_p, q_p, k_p, v_p, conv_p, knew_p, vnew_p, w_in_bf = _in_proj_conv(xp2, g1, w_in, qg, kg, cw, cb, lg, lb,
                                                                       IN_TILE, b)
    u_s, q_s, k_s, v_s = _in_proj(xs2, g1, w_in_bf, qg, kg, n_s, F32)

    r3 = lambda a, bb: a.reshape(bb, -1, a.shape[-1])
    a_p = _mixer_prompt(attn_sinks, r3(q_p, b), r3(k_p, b), r3(v_p, b), ATTN_TILE)
    cat_s, conv_s, knew_s, vnew_s = _mixer_sample(
        attn_sinks, r3(u_s, st), state_conv[0].transpose(1, 0, 2), r3(q_s, st), r3(k_s, st), r3(v_s, st),
        cache_k, cache_v, cw, cb, lg, lb)
    conv_s = conv_s.transpose(1, 0, 2)[None]

    x1, h2p, logits = _out_proj(c_p, a_p.reshape(n_p, D_ATTN), cat_s.reshape(n_s, D_MODEL), xp2, xs2, w_out, g2,
                                w_router_bf, OUT_TILE)

    n_blocks = -(-(n_tok * 2) // MOE_BLOCK) + N_EXPERTS
    slots, gates, blk = _route(logits, b_router, n_tok)
    slots = slots.reshape(2 * n_tok)
    tok = _invert(slots, blk, n_blocks * MOE_BLOCK)
    eo = _experts(blk, tok, h2p, w1, w3, w2, n_blocks)
    y_p, y_s = _combine(slots, x1, gates, eo, n_p, n_tok, COMBINE_TILE)

    return (y_p.reshape(b, t, D_MODEL), y_s.reshape(st, sb, D_MODEL).transpose(1, 0, 2), conv_p[None], knew_p[None],
            vnew_p[None], conv_s, knew_s, vnew_s)


def kernel(x_prompt, x_sample, state_conv, cache_k, cache_v, norm1_g, w_in, conv_w, conv_b, conv_norm_g, conv_norm_b, q_norm_g, k_norm_g, attn_sinks, w_out, norm2_g, w_router_group, b_router_group, w_router_expert, b_router_expert, w1, w3, w2):
    depth = w_in.shape[0]
    assert depth == 1, "single-layer step"
    return _layer(x_prompt, x_sample, state_conv, cache_k, cache_v, norm1_g[0], w_in[0], conv_w[0],
                  conv_b[0], conv_norm_g[0], conv_norm_b[0], q_norm_g[0], k_norm_g[0], attn_sinks[0], w_out[0],
                  norm2_g[0], w_router_group[0], b_router_group[0], w_router_expert[0], b_router_expert[0],
                  w1[0], w3[0], w2[0])
```

```python
import functools
import math

import jax
import jax.numpy as jnp
from jax import lax
from jax.experimental import pallas as pl
from jax.experimental.pallas import tpu as pltpu

F32 = jnp.float32
BF16 = jnp.bfloat16
I32 = jnp.int32

D_MODEL = 2048
D_CONV = 1024
CONV_WIDTH = 31
CONV_HIST = CONV_WIDTH - 1
D_ATTN = 1024
HEAD_DIM = 128
N_HEADS = 8
N_KV_HEADS = 2
GQA_GROUP = N_HEADS // N_KV_HEADS
D_KV = N_KV_HEADS * HEAD_DIM
WINDOW = 128
BLOCK_Q = 128
SCALE = 1.0 / math.sqrt(HEAD_DIM)
N_EXPERT_GROUPS = 4
EXPERTS_PER_GROUP = 8
N_EXPERTS = N_EXPERT_GROUPS * EXPERTS_PER_GROUP
D_EXPERT = 512
IN_TILE = 512
ATTN_TILE = 1024
OUT_TILE = 256
MOE_BLOCK = 256
COMBINE_TILE = 128
D_IN = 2 * D_CONV + D_ATTN + 2 * D_KV
EPS = 1e-6
PAST_LEN = 16384

LANES = 128
SUBLANES = 8
MXU_COLS = 256
VMEM_LIMIT_BYTES = 56 * 1024 * 1024
NEG_INF = float("-inf")
EXPERT_ROW0 = N_EXPERT_GROUPS


def _params(n_axes):
    return pltpu.CompilerParams(dimension_semantics=("arbitrary",) * n_axes,
                                vmem_limit_bytes=VMEM_LIMIT_BYTES)


def _resident(shape):
    nd = len(shape)
    return pl.BlockSpec(shape, lambda *_: (0,) * nd, pipeline_mode=pl.Buffered(1))


def _dot(a, b):
    return jnp.dot(a, b, preferred_element_type=F32)


def _in_proj_body(x_ref, g1_ref, w_ref, qg_ref, kg_ref, xpm_ref, u_ref, q_ref, k_ref, v_ref, n_ref):
    n_seq, n_pos, _ = x_ref.shape
    for t in range(n_pos):
        xpm_ref[t * n_seq:(t + 1) * n_seq, :] = x_ref[:, t, :]
    x = xpm_ref[...]
    ms = jnp.mean(x * x, axis=-1, keepdims=True)
    n_ref[...] = (x * lax.rsqrt(ms + EPS) * g1_ref[...]).astype(BF16)

    def head_norm(h, g):
        return h * lax.rsqrt(jnp.mean(h * h, axis=-1, keepdims=True) + EPS) * g

    ch = MXU_COLS
    for c in range(D_CONV // ch):
        a = _dot(n_ref[...], w_ref[:, c * ch:(c + 1) * ch])
        g = _dot(n_ref[...], w_ref[:, D_CONV + c * ch:D_CONV + (c + 1) * ch])
        u_ref[:, c * ch:(c + 1) * ch] = a * jax.nn.sigmoid(g)
    q_off = 2 * D_CONV
    for c in range(D_ATTN // ch):
        qq = _dot(n_ref[...], w_ref[:, q_off + c * ch:q_off + (c + 1) * ch])
        for j in range(ch // HEAD_DIM):
            qh = head_norm(qq[:, j * HEAD_DIM:(j + 1) * HEAD_DIM], qg_ref[...])
            q_ref[:, c * ch + j * HEAD_DIM:c * ch + (j + 1) * HEAD_DIM] = qh.astype(q_ref.dtype)
    k_off = q_off + D_ATTN
    kk = _dot(n_ref[...], w_ref[:, k_off:k_off + D_KV])
    for j in range(N_KV_HEADS):
        k_ref[:, j * HEAD_DIM:(j + 1) * HEAD_DIM] = head_norm(kk[:, j * HEAD_DIM:(j + 1) * HEAD_DIM], kg_ref[...])
    v_ref[...] = _dot(n_ref[...], w_ref[:, k_off + D_KV:k_off + 2 * D_KV])


CAST_COLS = 512


def _load_weight_as_bf16(w_hbm, w_bf, stage, sem):
    n_chunks = w_hbm.shape[1] // CAST_COLS

    def copy(c):
        return pltpu.make_async_copy(w_hbm.at[:, c * CAST_COLS:(c + 1) * CAST_COLS], stage.at[c % 2], sem.at[c % 2])

    copy(0).start()
    for c in range(n_chunks):
        if c + 1 < n_chunks:
            copy(c + 1).start()
        copy(c).wait()
        w_bf[:, c * CAST_COLS:(c + 1) * CAST_COLS] = stage[c % 2].astype(BF16)


def _in_proj_conv_body(x_ref, g1_ref, w_hbm, qg_ref, kg_ref, cw_ref, cb_ref, lg_ref, lb_ref, c_ref, q_ref, k_ref,
                       v_ref, ut_ref, kt_ref, vt_ref, wout_hbm, n_ref, ue_ref, conv_ref, w_ref, wstage, wsem, osem, *,
                       tiles_per_seq):
    tm = x_ref.shape[0]
    step = pl.program_id(0)
    first = lax.rem(step, tiles_per_seq) == 0
    w_out_copy = pltpu.make_async_copy(w_ref, wout_hbm, osem.at[0])

    @pl.when(step == 0)
    def _():
        _load_weight_as_bf16(w_hbm, w_ref, wstage, wsem)
        w_out_copy.start()

    @pl.when(step == pl.num_programs(0) - 1)
    def _():
        w_out_copy.wait()

    @pl.when(first)
    def _():
        for c in range(N_CCHUNK):
            ue_ref[c, 0:CONV_HALO, :] = jnp.zeros((CONV_HALO, LANES), F32)

    @pl.when(jnp.logical_not(first))
    def _():
        for c in range(N_CCHUNK):
            ue_ref[c, 0:CONV_HALO, :] = ue_ref[c, tm:tm + CONV_HALO, :]

    x = x_ref[...]
    ms = jnp.mean(x * x, axis=-1, keepdims=True)
    n_ref[...] = (x * lax.rsqrt(ms + EPS) * g1_ref[...]).astype(BF16)

    def head_norm(h, g):
        return h * lax.rsqrt(jnp.mean(h * h, axis=-1, keepdims=True) + EPS) * g

    ch = MXU_COLS
    lanes_per = ch // LANES
    n_glu = D_CONV // ch
    q_off = 2 * D_CONV

    row_blocks = [slice(r, r + DOT_ROWS) for r in range(0, tm, DOT_ROWS)]

    for c in range(n_glu):
        for rows in row_blocks:
            a = _dot(n_ref[rows, :], w_ref[:, c * ch:(c + 1) * ch])
            g = _dot(n_ref[rows, :], w_ref[:, D_CONV + c * ch:D_CONV + (c + 1) * ch])
            u = a * jax.nn.sigmoid(g)
            for half in range(lanes_per):
                ue_ref[c * lanes_per + half, CONV_HALO + rows.start:CONV_HALO + rows.stop, :] = (
                    u[:, half * LANES:(half + 1) * LANES])
        for half in range(lanes_per):
            lc = c * lanes_per + half
            _conv_chunk(ue_ref, lc, cw_ref[lc], cb_ref[lc], conv_ref, tm)
    _ln_swish(conv_ref, lg_ref, lb_ref, c_ref, tm)

    for c in range(D_ATTN // ch):
        for rows in row_blocks:
            qq = _dot(n_ref[rows, :], w_ref[:, q_off + c * ch:q_off + (c + 1) * ch])
            for j in range(ch // HEAD_DIM):
                qh = head_norm(qq[:, j * HEAD_DIM:(j + 1) * HEAD_DIM], qg_ref[...])
                q_ref[rows, c * ch + j * HEAD_DIM:c * ch + (j + 1) * HEAD_DIM] = qh.astype(q_ref.dtype)
    k_off = q_off + D_ATTN
    for rows in row_blocks:
        kk = _dot(n_ref[rows, :], w_ref[:, k_off:k_off + D_KV])
        vv = _dot(n_ref[rows, :], w_ref[:, k_off + D_KV:k_off + 2 * D_KV])
        v_ref[rows, :] = vv
        for j in range(N_KV_HEADS):
            hs = slice(j * HEAD_DIM, (j + 1) * HEAD_DIM)
            k_ref[rows, hs] = head_norm(kk[:, hs], kg_ref[...])
    for j in range(N_KV_HEADS):
        hs = slice(j * HEAD_DIM, (j + 1) * HEAD_DIM)
        kt_ref[:, j, :] = k_ref[tm - WINDOW:tm, hs]
        vt_ref[:, j, :] = v_ref[tm - WINDOW:tm, hs]
    for c in range(N_CCHUNK):
        ut_ref[:, c * LANES:(c + 1) * LANES] = ue_ref[c, CONV_HALO + tm - CONV_HIST:CONV_HALO + tm, :]


def _in_proj_conv(x2, g1, w_in, qg, kg, cw, cb, lg, lb, tm, n_seq):
    n = x2.shape[0]
    tiles_per_seq = n // n_seq // tm
    row = lambda w: pl.BlockSpec((tm, w), lambda i: (i, 0))
    seq = lambda *dims: pl.BlockSpec((None,) + dims, lambda i: (i // tiles_per_seq,) + (0,) * len(dims))
    hbm = pl.BlockSpec(memory_space=pl.ANY)
    body = functools.partial(_in_proj_conv_body, tiles_per_seq=tiles_per_seq)
    return pl.pallas_call(
        body,
        grid=(n // tm,),
        in_specs=[row(D_MODEL), _resident((1, D_MODEL)), hbm,
                  _resident((1, HEAD_DIM)), _resident((1, HEAD_DIM)),
                  _resident((N_CCHUNK, CONV_WIDTH, LANES)), _resident((N_CCHUNK, 1, LANES)),
                  _resident((N_CCHUNK, 1, LANES)), _resident((N_CCHUNK, 1, LANES))],
        out_specs=[row(D_CONV), row(D_ATTN), row(D_KV), row(D_KV),
                   seq(CONV_HIST, D_CONV), seq(WINDOW, N_KV_HEADS, HEAD_DIM), seq(WINDOW, N_KV_HEADS, HEAD_DIM), hbm],
        out_shape=[jax.ShapeDtypeStruct((n, D_CONV), BF16), jax.ShapeDtypeStruct((n, D_ATTN), BF16),
                   jax.ShapeDtypeStruct((n, D_KV), F32), jax.ShapeDtypeStruct((n, D_KV), F32),
                   jax.ShapeDtypeStruct((n_seq, CONV_HIST, D_CONV), F32),
                   jax.ShapeDtypeStruct((n_seq, WINDOW, N_KV_HEADS, HEAD_DIM), F32),
                   jax.ShapeDtypeStruct((n_seq, WINDOW, N_KV_HEADS, HEAD_DIM), F32),
                   jax.ShapeDtypeStruct((D_MODEL, D_IN), BF16)],
        scratch_shapes=[pltpu.VMEM((tm, D_MODEL), BF16),
                        pltpu.VMEM((N_CCHUNK, CONV_HALO + tm, LANES), F32),
                        pltpu.VMEM((N_CCHUNK, tm, LANES), F32),
                        pltpu.VMEM((D_MODEL, D_IN), BF16), pltpu.VMEM((2, D_MODEL, CAST_COLS), F32),
                        pltpu.SemaphoreType.DMA((2,)), pltpu.SemaphoreType.DMA((1,))],
        compiler_params=_params(1),
        name="in_proj_conv",
    )(x2, g1, w_in, qg, kg, cw, cb, lg, lb)


def _in_proj(x_sample, g1, w_in_bf, qg, kg):
    n = x_sample.shape[0] * x_sample.shape[1]
    whole = lambda w: pl.BlockSpec((n, w), lambda i: (0, 0))
    return pl.pallas_call(
        _in_proj_body,
        grid=(1,),
        in_specs=[pl.BlockSpec(x_sample.shape, lambda i: (0, 0, 0)), _resident((1, D_MODEL)),
                  _resident((D_MODEL, D_IN)), _resident((1, HEAD_DIM)), _resident((1, HEAD_DIM))],
        out_specs=[whole(D_MODEL), whole(D_CONV), whole(D_ATTN), whole(D_KV), whole(D_KV)],
        out_shape=[jax.ShapeDtypeStruct((n, D_MODEL), F32), jax.ShapeDtypeStruct((n, D_CONV), F32),
                   jax.ShapeDtypeStruct((n, D_ATTN), F32),
                   jax.ShapeDtypeStruct((n, D_KV), F32), jax.ShapeDtypeStruct((n, D_KV), F32)],
        scratch_shapes=[pltpu.VMEM((n, D_MODEL), BF16)],
        compiler_params=_params(1),
        name="in_proj",
    )(x_sample, g1, w_in_bf, qg, kg)


N_CCHUNK = D_CONV // LANES
CONV_ROWS = 64


CONV_HALO = 32
DOT_ROWS = 256


def _conv_chunk(ue_ref, c, wc, bias, conv_ref, rows):
    base = CONV_HALO - CONV_HIST
    for r0 in range(0, rows, CONV_ROWS):
        acc = jnp.broadcast_to(bias, (CONV_ROWS, LANES))
        for tap in range(CONV_WIDTH):
            acc = acc + wc[tap:tap + 1, :] * ue_ref[c, base + r0 + tap:base + r0 + tap + CONV_ROWS, :]
        conv_ref[c, r0:r0 + CONV_ROWS, :] = acc


def _ln_swish(conv_ref, lg_ref, lb_ref, cat_ref, rows):
    tot = jnp.zeros((rows, 1), F32)
    for c in range(N_CCHUNK):
        tot = tot + jnp.sum(conv_ref[c], axis=-1, keepdims=True)
    mean = tot / D_CONV
    var = jnp.zeros((rows, 1), F32)
    for c in range(N_CCHUNK):
        xc = conv_ref[c] - mean
        var = var + jnp.sum(xc * xc, axis=-1, keepdims=True)
    rstd = lax.rsqrt(var / D_CONV + EPS)
    for c in range(N_CCHUNK):
        y = (conv_ref[c] - mean) * rstd * lg_ref[c] + lb_ref[c]
        cat_ref[:, c * LANES:(c + 1) * LANES] = (y * jax.nn.sigmoid(y)).astype(cat_ref.dtype)


def _sink_softmax_rows(s, sink):
    m = jnp.maximum(jnp.max(s, axis=-1, keepdims=True), sink)
    p = jnp.exp(s - m)
    return p / (jnp.sum(p, axis=-1, keepdims=True) + jnp.exp(sink - m))


def _alibi_slope(head):
    return 2.0 ** (-8.0 * (head + 1) / N_HEADS)


def _mixer_prompt_body(sink_ref, q_ref, k_ref, kh_ref, v_ref, vh_ref, cat_ref, *, tm):
    j = pl.program_id(1)
    has_prev = j > 0
    qi = lax.broadcasted_iota(I32, (BLOCK_Q, 2 * BLOCK_Q), 0)
    kj = lax.broadcasted_iota(I32, (BLOCK_Q, 2 * BLOCK_Q), 1)
    dist = qi + BLOCK_Q - kj
    distf = dist.astype(F32)
    band = jnp.where(dist >= 0, jnp.where(dist < WINDOW, 0.0, NEG_INF), NEG_INF)
    band_first = jnp.where(kj >= BLOCK_Q, band, jnp.where(has_prev, band, NEG_INF))
    bias = [band - _alibi_slope(h) * distf for h in range(N_HEADS)]

    for qb in range(tm // BLOCK_Q):
        rows = slice(qb * BLOCK_Q, (qb + 1) * BLOCK_Q)
        prev = slice((qb - 1) * BLOCK_Q, qb * BLOCK_Q)
        for kv in range(N_KV_HEADS):
            hs = slice(kv * HEAD_DIM, (kv + 1) * HEAD_DIM)
            k_prev = kh_ref[:, hs] if qb == 0 else k_ref[prev, hs]
            v_prev = vh_ref[:, hs] if qb == 0 else v_ref[prev, hs]
            kk = jnp.concatenate([k_prev, k_ref[rows, hs]], axis=0).astype(BF16)
            vv = jnp.concatenate([v_prev, v_ref[rows, hs]], axis=0).astype(BF16)
            heads = [kv * GQA_GROUP + g for g in range(GQA_GROUP)]
            qs = jnp.concatenate([q_ref[rows, h * HEAD_DIM:(h + 1) * HEAD_DIM] for h in heads], axis=0)
            s = lax.dot_general(qs, kk, (((1,), (1,)), ((), ())), preferred_element_type=F32)
            ps, scales = [], []
            for g, h in enumerate(heads):
                head_bias = band_first - _alibi_slope(h) * distf if qb == 0 else bias[h]
                sg = s[g * BLOCK_Q:(g + 1) * BLOCK_Q] * SCALE + head_bias
                m = jnp.maximum(jnp.max(sg, axis=-1, keepdims=True), sink_ref[h])
                p = jnp.exp(sg - m)
                ps.append(p.astype(BF16))
                scales.append(1.0 / (jnp.sum(p, axis=-1, keepdims=True) + jnp.exp(sink_ref[h] - m)))
            o = _dot(jnp.concatenate(ps, axis=0), vv)
            for g, h in enumerate(heads):
                cat_ref[rows, h * HEAD_DIM:(h + 1) * HEAD_DIM] = (
                    o[g * BLOCK_Q:(g + 1) * BLOCK_Q] * scales[g]).astype(cat_ref.dtype)


def _mixer_prompt(sinks, q, k, v, tm):
    b, t, _ = q.shape
    kpb = tm // BLOCK_Q
    main = lambda w: pl.BlockSpec((None, tm, w), lambda bi, j: (bi, j, 0))
    prev_block = pl.BlockSpec((None, BLOCK_Q, D_KV), lambda bi, j: (bi, jnp.maximum(j * kpb - 1, 0), 0))
    body = functools.partial(_mixer_prompt_body, tm=tm)
    return pl.pallas_call(
        body,
        grid=(b, t // tm),
        in_specs=[pl.BlockSpec(memory_space=pltpu.SMEM), main(D_ATTN), main(D_KV), prev_block, main(D_KV),
                  prev_block],
        out_specs=main(D_ATTN),
        out_shape=jax.ShapeDtypeStruct((b, t, D_ATTN), BF16),
        compiler_params=_params(2),
        name="mixer_prompt",
    )(sinks, q, k, k, v, v)


KEY_PAD = 8


SAMPLE_SEQS = 8


def _sample_conv_ln_swish(st_ref, u_ref, cw_ref, cb_ref, lg_ref, lb_ref, cat_ref, t_new):
    hist, n_seq, _ = st_ref.shape
    conv = [[None] * N_CCHUNK for _ in range(t_new)]
    for c in range(N_CCHUNK):
        cs = slice(c * LANES, (c + 1) * LANES)
        wc = cw_ref[c]
        pos = [st_ref[p, :, cs] for p in range(hist)] + [u_ref[t, :, cs] for t in range(t_new)]
        for t in range(t_new):
            acc = jnp.broadcast_to(cb_ref[c], (n_seq, LANES))
            for tap in range(CONV_WIDTH):
                acc = acc + wc[tap:tap + 1, :] * pos[t + tap]
            conv[t][c] = acc
    for t in range(t_new):
        tot = jnp.zeros((n_seq, 1), F32)
        for a in conv[t]:
            tot = tot + jnp.sum(a, axis=-1, keepdims=True)
        mean = tot / D_CONV
        var = jnp.zeros((n_seq, 1), F32)
        for a in conv[t]:
            var = var + jnp.sum((a - mean) * (a - mean), axis=-1, keepdims=True)
        rstd = lax.rsqrt(var / D_CONV + EPS)
        for c, a in enumerate(conv[t]):
            y = (a - mean) * rstd * lg_ref[c] + lb_ref[c]
            cat_ref[t, :, c * LANES:(c + 1) * LANES] = (y * jax.nn.sigmoid(y)).astype(cat_ref.dtype)


def _mixer_sample_body(sink_ref, u_ref, st_ref, q_ref, k_ref, v_ref, ck_ref, cv_ref, cw_ref, cb_ref, lg_ref,
                       lb_ref, cat_ref, nst_ref, nk_ref, nv_ref, kk_ref, vv_ref, qs_ref, *, t_new):
    hist, n_seq, _ = st_ref.shape
    w_past = ck_ref.shape[1]
    n_keys = w_past + KEY_PAD
    n_rows, n_cols = GQA_GROUP * t_new * n_seq, n_seq * n_keys

    _sample_conv_ln_swish(st_ref, u_ref, cw_ref, cb_ref, lg_ref, lb_ref, cat_ref, t_new)
    nst_ref[0:hist - t_new] = st_ref[t_new:hist]
    nst_ref[hist - t_new:hist] = u_ref[...]

    for i in range(n_seq):
        nk_ref[i, 0:w_past - t_new] = ck_ref[i, t_new:w_past]
        nv_ref[i, 0:w_past - t_new] = cv_ref[i, t_new:w_past]
        kk_ref[i, w_past:n_keys, :] = jnp.zeros((KEY_PAD, D_KV), F32)
        vv_ref[i, w_past:n_keys, :] = jnp.zeros((KEY_PAD, D_KV), F32)
        for t in range(t_new):
            kk_ref[i, w_past + t:w_past + t + 1, :] = k_ref[t, i:i + 1, :]
            vv_ref[i, w_past + t:w_past + t + 1, :] = v_ref[t, i:i + 1, :]
        for h in range(N_KV_HEADS):
            hs = slice(h * HEAD_DIM, (h + 1) * HEAD_DIM)
            kk_ref[i, 0:w_past, hs] = ck_ref[i, :, h, :]
            vv_ref[i, 0:w_past, hs] = cv_ref[i, :, h, :]
            for t in range(t_new):
                nk_ref[i, w_past - t_new + t, h:h + 1, :] = k_ref[t, i:i + 1, hs]
                nv_ref[i, w_past - t_new + t, h:h + 1, :] = v_ref[t, i:i + 1, hs]

    row = lax.broadcasted_iota(I32, (n_rows, n_cols), 0)
    col = lax.broadcasted_iota(I32, (n_rows, n_cols), 1)
    seq_bits, tok_bits = n_seq.bit_length() - 1, t_new.bit_length() - 1
    assert (1 << seq_bits, 1 << tok_bits) == (n_seq, t_new), "sequence and token counts must be powers of two"
    tok = (row >> seq_bits) & (t_new - 1)
    key = col - (row & (n_seq - 1)) * n_keys
    dist = tok + w_past - key
    distf = dist.astype(F32)
    mask = jnp.where(dist >= 0, jnp.where(dist < WINDOW, 0.0, NEG_INF), NEG_INF)
    row1 = lax.broadcasted_iota(I32, (n_rows, 1), 0)
    grp = row1 >> (seq_bits + tok_bits)
    blk = lambda g, t: slice((g * t_new + t) * n_seq, (g * t_new + t + 1) * n_seq)

    for kv in range(N_KV_HEADS):
        hs = slice(kv * HEAD_DIM, (kv + 1) * HEAD_DIM)
        slope = jnp.zeros((n_rows, 1), F32)
        sink = jnp.zeros((n_rows, 1), F32)
        for g in range(GQA_GROUP):
            h = kv * GQA_GROUP + g
            slope = jnp.where(grp == g, _alibi_slope(h), slope)
            sink = jnp.where(grp == g, sink_ref[h], sink)
            for t in range(t_new):
                qs_ref[blk(g, t), :] = q_ref[t, :, h * HEAD_DIM:(h + 1) * HEAD_DIM]
        kk = kk_ref[:, :, hs].reshape(n_cols, HEAD_DIM).astype(BF16)
        vv = vv_ref[:, :, hs].reshape(n_cols, HEAD_DIM).astype(BF16)
        s = lax.dot_general(qs_ref[...].astype(BF16), kk, (((1,), (1,)), ((), ())), preferred_element_type=F32)
        sg = s * SCALE - slope * distf + mask
        o = _dot(_sink_softmax_rows(sg, sink).astype(BF16), vv)
        for g in range(GQA_GROUP):
            h = kv * GQA_GROUP + g
            for t in range(t_new):
                cat_ref[t, :, D_CONV + h * HEAD_DIM:D_CONV + (h + 1) * HEAD_DIM] = o[blk(g, t)].astype(cat_ref.dtype)


def _mixer_sample(sinks, u, state, q, k, v, ck, cv, cw, cb, lg, lb):
    t_new, b, _ = u.shape
    hist = state.shape[0]
    w_past = ck.shape[2]
    n_seq = SAMPLE_SEQS
    per = lambda r, w: pl.BlockSpec((r, n_seq, w), lambda bi: (0, bi, 0))
    cache = pl.BlockSpec((None, n_seq, w_past, N_KV_HEADS, HEAD_DIM), lambda bi: (0, bi, 0, 0, 0))
    body = functools.partial(_mixer_sample_body, t_new=t_new)
    return pl.pallas_call(
        body,
        grid=(b // n_seq,),
        in_specs=[pl.BlockSpec(memory_space=pltpu.SMEM),
                  per(t_new, D_CONV), per(hist, D_CONV), per(t_new, D_ATTN), per(t_new, D_KV), per(t_new, D_KV),
                  cache, cache,
                  _resident((N_CCHUNK, CONV_WIDTH, LANES)), _resident((N_CCHUNK, 1, LANES)),
                  _resident((N_CCHUNK, 1, LANES)), _resident((N_CCHUNK, 1, LANES))],
        out_specs=[per(t_new, D_MODEL), per(hist, D_CONV), cache, cache],
        out_shape=[jax.ShapeDtypeStruct((t_new, b, D_MODEL), F32),
                   jax.ShapeDtypeStruct(state.shape, F32),
                   jax.ShapeDtypeStruct(ck.shape, F32),
                   jax.ShapeDtypeStruct(cv.shape, F32)],
        scratch_shapes=[pltpu.VMEM((n_seq, w_past + KEY_PAD, D_KV), F32),
                        pltpu.VMEM((n_seq, w_past + KEY_PAD, D_KV), F32),
                        pltpu.VMEM((n_seq * GQA_GROUP * t_new, HEAD_DIM), F32)],
        compiler_params=_params(1),
        name="mixer_sample",
    )(sinks, u, state, q, k, v, ck, cv, cw, cb, lg, lb)


OUT_CHUNK = 512
D_HALF = D_MODEL // 2
U32 = jnp.uint32


def _pack_bf16_pairs(hi, lo):
    hi_bits = lax.bitcast_convert_type(hi.astype(BF16).astype(F32), U32)
    lo_bits = lax.bitcast_convert_type(lo.astype(BF16).astype(F32), U32)
    return hi_bits | (lo_bits >> 16)


def _unpack_bf16_pairs(words):
    hi = lax.bitcast_convert_type(words & U32(0xFFFF0000), F32)
    lo = lax.bitcast_convert_type(words << 16, F32)
    return hi, lo


def _out_proj_body(catc_ref, cata_ref, cats_ref, xp_ref, xs_ref, wo_hbm, g2_ref, wr_ref, x1_ref, h2p_ref, lg_ref,
                   h_ref, wo_ref, wstage, wsem, *, n_prompt_tiles):
    is_prompt = pl.program_id(0) < n_prompt_tiles
    tm = catc_ref.shape[0]

    @pl.when(pl.program_id(0) == 0)
    def _():
        _load_weight_as_bf16(wo_hbm, wo_ref, wstage, wsem)

    def sample_rows(v):
        return jnp.concatenate([v, jnp.zeros((tm - v.shape[0], v.shape[1]), v.dtype)], axis=0)

    cat = jnp.where(is_prompt, jnp.concatenate([catc_ref[...], cata_ref[...]], axis=1),
                    sample_rows(cats_ref[...].astype(BF16)))
    ss = jnp.zeros((tm, 1), F32)
    for c in range(D_MODEL // OUT_CHUNK):
        cs = slice(c * OUT_CHUNK, (c + 1) * OUT_CHUNK)
        y = jnp.where(is_prompt, xp_ref[:, cs], sample_rows(xs_ref[:, cs])) + _dot(cat, wo_ref[:, cs])
        x1_ref[:, cs] = y
        ss = ss + jnp.sum(y * y, axis=-1, keepdims=True)
    r = lax.rsqrt(ss / D_MODEL + EPS)
    for c in range(D_MODEL // OUT_CHUNK):
        cs = slice(c * OUT_CHUNK, (c + 1) * OUT_CHUNK)
        h_ref[:, cs] = x1_ref[:, cs] * r * g2_ref[:, cs]
    for c in range(D_HALF // OUT_CHUNK):
        cs = slice(c * OUT_CHUNK, (c + 1) * OUT_CHUNK)
        cs_lo = slice(D_HALF + c * OUT_CHUNK, D_HALF + (c + 1) * OUT_CHUNK)
        h2p_ref[:, cs] = _pack_bf16_pairs(h_ref[:, cs], h_ref[:, cs_lo])
    lg = _dot(h_ref[...].astype(BF16), wr_ref[...])
    for j in range(tm // ROUTE_CHUNK):
        lg_ref[j] = lg[j * ROUTE_CHUNK:(j + 1) * ROUTE_CHUNK, :].T[0:ROUTER_ROWS, :]


def _out_proj(cat_conv, cat_attn, cat_s, xp2, xs2, w_out, g2, wr_bf, tm):
    n_prompt_tiles = cat_conv.shape[0] // tm
    n_rows = (n_prompt_tiles + 1) * tm
    n_s = cat_s.shape[0]
    prompt = lambda w: pl.BlockSpec((tm, w), lambda i: (jnp.minimum(i, n_prompt_tiles - 1), 0))
    out_row = lambda w: pl.BlockSpec((tm, w), lambda i: (i, 0))
    body = functools.partial(_out_proj_body, n_prompt_tiles=n_prompt_tiles)
    return pl.pallas_call(
        body,
        grid=(n_prompt_tiles + 1,),
        in_specs=[prompt(D_CONV), prompt(D_ATTN), _resident((n_s, D_MODEL)), prompt(D_MODEL),
                  _resident((n_s, D_MODEL)),
                  pl.BlockSpec(memory_space=pl.ANY), _resident((1, D_MODEL)), _resident((D_MODEL, LANES))],
        out_specs=[out_row(D_MODEL), out_row(D_HALF),
                   pl.BlockSpec((tm // ROUTE_CHUNK, ROUTER_ROWS, ROUTE_CHUNK), lambda i: (i, 0, 0))],
        out_shape=[jax.ShapeDtypeStruct((n_rows, D_MODEL), F32), jax.ShapeDtypeStruct((n_rows, D_HALF), U32),
                   jax.ShapeDtypeStruct((n_rows // ROUTE_CHUNK, ROUTER_ROWS, ROUTE_CHUNK), F32)],
        scratch_shapes=[pltpu.VMEM((tm, D_MODEL), F32), pltpu.VMEM((D_MODEL, D_MODEL), BF16),
                        pltpu.VMEM((2, D_MODEL, CAST_COLS), F32), pltpu.SemaphoreType.DMA((2,))],
        compiler_params=_params(1),
        name="out_proj",
    )(cat_conv, cat_attn, cat_s, xp2, xs2, w_out, g2, wr_bf)


ROUTE_CHUNK = 128
ROUTE_UNROLL = 5


ROUTER_ROWS = 40


def _blk_start(blk_ref, e):
    return blk_ref[EXPERT_ROW0 + e, 0]


def _blk_count(blk_ref, e):
    return blk_ref[ROUTER_ROWS + EXPERT_ROW0 + e, 0]


def _route_body(lg_ref, bias_ref, slot_ref, gate_ref, blk_ref, cum_ref, sel_ref, *, n_tok):
    n_chunks = n_tok // ROUTE_CHUNK
    row = lax.broadcasted_iota(I32, (ROUTER_ROWS, ROUTE_CHUNK), 0).astype(F32)
    ri = lax.broadcasted_iota(I32, (ROUTE_CHUNK, ROUTE_CHUNK), 0)
    ci = lax.broadcasted_iota(I32, (ROUTE_CHUNK, ROUTE_CHUNK), 1)
    earlier_tok = jnp.where(ri < ci, 1.0, 0.0).astype(BF16)
    er = lax.broadcasted_iota(I32, (ROUTER_ROWS, ROUTER_ROWS), 0)
    ec = lax.broadcasted_iota(I32, (ROUTER_ROWS, ROUTER_ROWS), 1)
    earlier_row = jnp.where(ec < er, 1.0, 0.0).astype(BF16)
    sub = lax.broadcasted_iota(I32, (SUBLANES, ROUTE_CHUNK), 0)
    is_group = row < N_EXPERT_GROUPS

    def first_max(vals):
        m = jnp.max(vals, axis=0, keepdims=True)
        idx = jnp.min(jnp.where(vals == m, row, float(ROUTER_ROWS)), axis=0, keepdims=True)
        return m, idx

    def pair_rows(a, b):
        return jnp.where(sub == 0, a, jnp.where(sub == 1, b, 0.0))

    def assign(i, carry):
        l = lg_ref[i] + bias_ref[...]
        gl = jnp.where(is_group, l, NEG_INF)
        g_max, g_idx = first_max(gl)
        g_top = 1.0 / jnp.sum(jnp.exp(gl - g_max), axis=0, keepdims=True)
        lo = EXPERT_ROW0 + g_idx * EXPERTS_PER_GROUP
        el = jnp.where(row >= lo, jnp.where(row < lo + EXPERTS_PER_GROUP, l, NEG_INF), NEG_INF)
        m1, i1 = first_max(el)
        p = jnp.exp(el - m1)
        probs = p / jnp.sum(p, axis=0, keepdims=True)
        e1 = jnp.sum(jnp.where(row == i1, probs, 0.0), axis=0, keepdims=True)
        _, i2 = first_max(jnp.where(row == i1, NEG_INF, el))
        e2 = jnp.sum(jnp.where(row == i2, probs, 0.0), axis=0, keepdims=True)
        gate_ref[i] = pair_rows(g_top * e1 / (e1 + e2), g_top * e2 / (e1 + e2))
        sel_ref[i] = pair_rows(i1, i2)
        onehot = jnp.where(row == i1, 1.0, jnp.where(row == i2, 1.0, 0.0))
        cum_ref[i] = _dot(onehot.astype(BF16), earlier_tok) + carry
        return carry + jnp.sum(onehot, axis=1, keepdims=True)

    counts = lax.fori_loop(0, n_chunks, assign, jnp.zeros((ROUTER_ROWS, 1), F32), unroll=ROUTE_UNROLL)
    n_blocks = jnp.floor((counts + (MOE_BLOCK - 1)) / MOE_BLOCK)
    blk_start = _dot(earlier_row, jnp.broadcast_to(n_blocks, (ROUTER_ROWS, LANES)).astype(BF16))
    row_start = blk_start * MOE_BLOCK

    def place(i, carry):
        pos = cum_ref[i] + row_start
        sel = sel_ref[i]
        s1 = jnp.sum(jnp.where(row == sel[0:1, :], pos, 0.0), axis=0, keepdims=True)
        s2 = jnp.sum(jnp.where(row == sel[1:2, :], pos, 0.0), axis=0, keepdims=True)
        slot_ref[0, i] = s1.astype(I32)
        slot_ref[1, i] = s2.astype(I32)
        return carry

    lax.fori_loop(0, n_chunks, place, 0, unroll=ROUTE_UNROLL)

    blk_ref[0:ROUTER_ROWS, :] = blk_start.astype(I32)
    blk_ref[ROUTER_ROWS:2 * ROUTER_ROWS, :] = jnp.broadcast_to(counts, (ROUTER_ROWS, LANES)).astype(I32)


def _route(logits, bias, n_tok):
    body = functools.partial(_route_body, n_tok=n_tok)
    n_chunks = n_tok // ROUTE_CHUNK
    chunked = lambda r: pl.BlockSpec((n_chunks, r, LANES), lambda i: (0, 0, 0))
    table = pl.BlockSpec((2 * ROUTER_ROWS, LANES), lambda i: (0, 0))
    return pl.pallas_call(
        body,
        grid=(1,),
        in_specs=[chunked(ROUTER_ROWS), pl.BlockSpec((ROUTER_ROWS, LANES), lambda i: (0, 0))],
        out_specs=[pl.BlockSpec((2, n_chunks, 1, LANES), lambda i: (0, 0, 0, 0)), chunked(SUBLANES), table],
        out_shape=[jax.ShapeDtypeStruct((2, n_chunks, 1, LANES), I32),
                   jax.ShapeDtypeStruct((n_chunks, SUBLANES, LANES), F32),
                   jax.ShapeDtypeStruct((2 * ROUTER_ROWS, LANES), I32)],
        scratch_shapes=[pltpu.VMEM((n_chunks, ROUTER_ROWS, LANES), F32),
                        pltpu.VMEM((n_chunks, SUBLANES, LANES), F32)],
        compiler_params=_params(1),
        name="route",
    )(logits, bias)


INVERT_UNROLL = 16


CLEAR_SPAN = 64


def _invert_body(slot_ref, blk_ref, tok_ref, *, n_slots):
    n_tok = slot_ref.shape[0] // 2

    def clear_expert(e, c):
        first_pad = _blk_start(blk_ref, e) * MOE_BLOCK + _blk_count(blk_ref, e)
        end = jnp.where(e == N_EXPERTS - 1, n_slots, _blk_start(blk_ref, e + 1) * MOE_BLOCK)
        span_bits = CLEAR_SPAN.bit_length() - 1
        lo = lax.shift_right_logical(first_pad, span_bits) * CLEAR_SPAN

        def span(j, c2):
            for k in range(CLEAR_SPAN):
                tok_ref[lo + j * CLEAR_SPAN + k] = 0
            return c2

        lax.fori_loop(0, lax.shift_right_logical(end - lo, span_bits), span, 0)
        return c

    lax.fori_loop(0, N_EXPERTS, clear_expert, 0)

    def put(t, c):
        tok_ref[slot_ref[t]] = t
        tok_ref[slot_ref[n_tok + t]] = t
        return c

    lax.fori_loop(0, n_tok, put, 0, unroll=INVERT_UNROLL)


def _invert(slots, blk, n_slots):
    smem = pl.BlockSpec(memory_space=pltpu.SMEM)
    return pl.pallas_call(
        functools.partial(_invert_body, n_slots=n_slots),
        in_specs=[smem, smem],
        out_specs=smem,
        out_shape=jax.ShapeDtypeStruct((n_slots,), I32),
        name="invert",
    )(slots, blk)


GATHER_AHEAD = 3
GATHER_SLOTS = GATHER_AHEAD + 1


def _row_gather_start(idx_ref, base, src_hbm, dst, sem, n_rows, priorities):
    for r in range(n_rows):
        tok = idx_ref[base + r]
        pltpu.make_async_copy(src_hbm.at[pl.ds(tok, 1), :], dst.at[pl.ds(r, 1), :], sem).start(
            priority=priorities[r % len(priorities)])


def _experts_body(bstart_ref, tok_ref, h2p_hbm, w1_ref, w3_ref, w2_ref, eo_hbm, xbuf, obuf, w1b, w3b, w2b, gsem,
                  osem, *, n_blocks):
    e = pl.program_id(0)
    n_exp = pl.num_programs(0)
    b0 = _blk_start(bstart_ref, e)
    nb = _blk_start(bstart_ref, e + 1) - b0
    n_used = _blk_start(bstart_ref, n_exp)

    def gather(block):
        s = lax.rem(block, GATHER_SLOTS)
        src_block = jnp.minimum(block, n_used - 1)
        _row_gather_start(tok_ref, src_block * MOE_BLOCK, h2p_hbm, xbuf.at[s], gsem.at[s], MOE_BLOCK, (1, 0))

    def gather_wait(block):
        s = lax.rem(block, GATHER_SLOTS)
        pltpu.make_async_copy(xbuf.at[s], xbuf.at[s], gsem.at[s]).wait()

    def out_copy(block, s):
        rows = pl.ds(pl.multiple_of(block * MOE_BLOCK, MOE_BLOCK), MOE_BLOCK)
        return pltpu.make_async_copy(obuf.at[s], eo_hbm.at[rows, :], osem.at[s])

    @pl.when(e == 0)
    def _():
        for k in range(GATHER_AHEAD):
            gather(k)

    @pl.when(nb > 0)
    def _():
        w1b[...] = w1_ref[...].astype(BF16)
        w3b[...] = w3_ref[...].astype(BF16)
        w2b[...] = w2_ref[...].astype(BF16)

    def block(j, carry):
        b = b0 + j
        s = lax.rem(b, 2)
        gather(b + GATHER_AHEAD)
        gather_wait(b)
        hi, lo = _unpack_bf16_pairs(xbuf[lax.rem(b, GATHER_SLOTS)])
        x = jnp.concatenate([hi.astype(BF16), lo.astype(BF16)], axis=1)
        a = _dot(x, w1b[...])
        g = _dot(x, w3b[...])
        hdn = (a * jax.nn.sigmoid(a) * g).astype(BF16)
        o = _dot(hdn, w2b[...])

        @pl.when(b >= 2)
        def _():
            out_copy(b - 2, s).wait()

        obuf[s] = _pack_bf16_pairs(o[:, :D_HALF], o[:, D_HALF:])
        out_copy(b, s).start()
        return carry

    lax.fori_loop(0, nb, block, 0)

    @pl.when(e == n_exp - 1)
    def _():
        for k in range(GATHER_AHEAD):
            gather_wait(n_used + k)

        @pl.when(n_used >= 2)
        def _():
            out_copy(n_used - 2, lax.rem(n_used, 2)).wait()

        out_copy(n_used - 1, lax.rem(n_used - 1, 2)).wait()
        obuf[0] = jnp.zeros((MOE_BLOCK, D_HALF), U32)

        def fill(tb, carry):
            out_copy(tb, 0).start()
            return carry

        lax.fori_loop(n_used, n_blocks, fill, 0)

        def drain(tb, carry):
            out_copy(tb, 0).wait()
            return carry

        lax.fori_loop(n_used, n_blocks, drain, 0)


def _experts(bstart, tok, h2p, w1, w3, w2, n_blocks):
    def wspec(r, c):
        return pl.BlockSpec((None, r, c), lambda e, bstart_ref, tok_ref: (e, 0, 0))

    grid_spec = pltpu.PrefetchScalarGridSpec(
        num_scalar_prefetch=2,
        grid=(N_EXPERTS,),
        in_specs=[pl.BlockSpec(memory_space=pl.ANY),
                  wspec(D_MODEL, D_EXPERT), wspec(D_MODEL, D_EXPERT), wspec(D_EXPERT, D_MODEL)],
        out_specs=pl.BlockSpec(memory_space=pl.ANY),
        scratch_shapes=[pltpu.VMEM((GATHER_SLOTS, MOE_BLOCK, D_HALF), U32), pltpu.VMEM((2, MOE_BLOCK, D_HALF), U32),
                        pltpu.VMEM((D_MODEL, D_EXPERT), BF16), pltpu.VMEM((D_MODEL, D_EXPERT), BF16),
                        pltpu.VMEM((D_EXPERT, D_MODEL), BF16),
                        pltpu.SemaphoreType.DMA((GATHER_SLOTS,)), pltpu.SemaphoreType.DMA((2,))],
    )
    return pl.pallas_call(
        functools.partial(_experts_body, n_blocks=n_blocks),
        grid_spec=grid_spec,
        out_shape=jax.ShapeDtypeStruct((n_blocks * MOE_BLOCK, D_HALF), U32),
        compiler_params=_params(1),
        name="experts",
    )(bstart, tok, h2p, w1, w3, w2)


def _combine_body(slot_ref, x1_ref, gate_ref, eo_hbm, yp_ref, ys_ref, buf, sem, *, n_tok, n_prompt_tiles):
    i = pl.program_id(0)
    tm = x1_ref.shape[0]
    n_tiles = n_tok // tm
    slot = lax.rem(i, GATHER_SLOTS)

    def start(tile):
        s = lax.rem(tile, GATHER_SLOTS)
        base = jnp.minimum(tile, n_tiles - 1) * tm
        _row_gather_start(slot_ref, base, eo_hbm, buf.at[s, 0], sem.at[s], tm, (0, 1))
        _row_gather_start(slot_ref, n_tok + base, eo_hbm, buf.at[s, 1], sem.at[s], tm, (0, 1))

    def wait(tile):
        s = lax.rem(tile, GATHER_SLOTS)
        pltpu.make_async_copy(buf.at[s], buf.at[s], sem.at[s]).wait()

    @pl.when(i == 0)
    def _():
        for k in range(GATHER_AHEAD):
            start(k)

    start(i + GATHER_AHEAD)
    wait(i)

    @pl.when(i == n_tiles - 1)
    def _():
        for k in range(GATHER_AHEAD):
            wait(n_tiles + k)

    diag = lax.broadcasted_iota(I32, (tm, tm), 0) == lax.broadcasted_iota(I32, (tm, tm), 1)
    g0 = jnp.sum(jnp.where(diag, gate_ref[0:1, :], 0.0), axis=1, keepdims=True)
    g1 = jnp.sum(jnp.where(diag, gate_ref[1:2, :], 0.0), axis=1, keepdims=True)
    hi0, lo0 = _unpack_bf16_pairs(buf[slot, 0])
    hi1, lo1 = _unpack_bf16_pairs(buf[slot, 1])
    y_hi = x1_ref[:, :D_HALF] + g0 * hi0 + g1 * hi1
    y_lo = x1_ref[:, D_HALF:] + g0 * lo0 + g1 * lo1

    @pl.when(i < n_prompt_tiles)
    def _():
        yp_ref[:, :D_HALF] = y_hi
        yp_ref[:, D_HALF:] = y_lo

    @pl.when(i >= n_prompt_tiles)
    def _():
        n_seq, n_pos, _ = ys_ref.shape
        for t in range(n_pos):
            ys_ref[:, t, :D_HALF] = y_hi[t * n_seq:(t + 1) * n_seq]
            ys_ref[:, t, D_HALF:] = y_lo[t * n_seq:(t + 1) * n_seq]


def _combine(slots, x1, gates, eo, n_prompt, sample_shape, tm):
    n_tok = n_prompt + sample_shape[0] * sample_shape[1]
    assert n_tok - n_prompt == tm, "the sample rows must fill one combine tile"
    n_tiles = n_tok // tm
    n_prompt_tiles = n_prompt // tm
    body = functools.partial(_combine_body, n_tok=n_tok, n_prompt_tiles=n_prompt_tiles)
    grid_spec = pltpu.PrefetchScalarGridSpec(
        num_scalar_prefetch=1,
        grid=(n_tiles,),
        in_specs=[pl.BlockSpec((tm, D_MODEL), lambda i, s: (i, 0)),
                  pl.BlockSpec((None, SUBLANES, tm), lambda i, s: (i, 0, 0)),
                  pl.BlockSpec(memory_space=pl.ANY)],
        out_specs=[pl.BlockSpec((tm, D_MODEL), lambda i, s: (jnp.minimum(i, n_prompt_tiles - 1), 0)),
                   pl.BlockSpec(sample_shape, lambda i, s: (0, 0, 0))],
        scratch_shapes=[pltpu.VMEM((GATHER_SLOTS, 2, tm, D_HALF), U32), pltpu.SemaphoreType.DMA((GATHER_SLOTS,))],
    )
    return pl.pallas_call(
        body,
        grid_spec=grid_spec,
        out_shape=[jax.ShapeDtypeStruct((n_prompt, D_MODEL), F32), jax.ShapeDtypeStruct(sample_shape, F32)],
        compiler_params=_params(1),
        name="combine",
    )(slots, x1, gates, eo)


def _layer(x_prompt, x_sample, state_conv, cache_k, cache_v, norm1_g, w_in, conv_w, conv_b, conv_norm_g,
           conv_norm_b, q_norm_g, k_norm_g, attn_sinks, w_out, norm2_g, w_rg, b_rg, w_re, b_re, w1, w3, w2):
    b, t, _ = x_prompt.shape
    sb, st, _ = x_sample.shape
    n_p, n_s = b * t, sb * st
    n_tok = n_p + n_s

    g1 = norm1_g.reshape(1, D_MODEL)
    g2 = norm2_g.reshape(1, D_MODEL)
    qg = q_norm_g.reshape(1, HEAD_DIM)
    kg = k_norm_g.reshape(1, HEAD_DIM)
    chunked = lambda a: a.reshape(-1, N_CCHUNK, LANES).transpose(1, 0, 2)
    cw, cb, lg, lb = chunked(conv_w), chunked(conv_b), chunked(conv_norm_g), chunked(conv_norm_b)
    pad_rows = ROUTER_ROWS - N_EXPERT_GROUPS - N_EXPERTS
    pad_lanes = LANES - N_EXPERT_GROUPS - N_EXPERTS
    w_router_bf = jnp.concatenate([w_rg, w_re, jnp.zeros((D_MODEL, pad_lanes), F32)], axis=1).astype(BF16)
    b_router = jnp.broadcast_to(jnp.concatenate([b_rg, b_re, jnp.zeros((pad_rows,), F32)])[:, None],
                                (ROUTER_ROWS, LANES))

    xp2 = x_prompt.reshape(n_p, D_MODEL)
    c_p, q_p, k_p, v_p, conv_p, knew_p, vnew_p, w_in_bf = _in_proj_conv(xp2, g1, w_in, qg, kg, cw, cb, lg, lb,
                                                                       IN_TILE, b)
    xs2, u_s, q_s, k_s, v_s = _in_proj(x_sample, g1, w_in_bf, qg, kg)

    r3 = lambda a, bb: a.reshape(bb, -1, a.shape[-1])
    a_p = _mixer_prompt(attn_sinks, r3(q_p, b), r3(k_p, b), r3(v_p, b), ATTN_TILE)
    cat_s, conv_s, knew_s, vnew_s = _mixer_sample(
        attn_sinks, r3(u_s, st), state_conv[0].transpose(1, 0, 2), r3(q_s, st), r3(k_s, st), r3(v_s, st),
        cache_k, cache_v, cw, cb, lg, lb)
    conv_s = conv_s.transpose(1, 0, 2)[None]

    x1, h2p, logits = _out_proj(c_p, a_p.reshape(n_p, D_ATTN), cat_s.reshape(n_s, D_MODEL), xp2, xs2, w_out, g2,
                                w_router_bf, OUT_TILE)

    n_blocks = -(-(n_tok * 2) // MOE_BLOCK) + N_EXPERTS
    slots, gates, blk = _route(logits, b_router, n_tok)
    slots = slots.reshape(2 * n_tok)
    tok = _invert(slots, blk, n_blocks * MOE_BLOCK)
    eo = _experts(blk, tok, h2p, w1, w3, w2, n_blocks)
    y_p, y_s = _combine(slots, x1, gates, eo, n_p, x_sample.shape, COMBINE_TILE)

    return (y_p.reshape(b, t, D_MODEL), y_s, conv_p[None], knew_p[None],
            vnew_p[None], conv_s, knew_s, vnew_s)


def kernel(x_prompt, x_sample, state_conv, cache_k, cache_v, norm1_g, w_in, conv_w, conv_b, conv_norm_g, conv_norm_b, q_norm_g, k_norm_g, attn_sinks, w_out, norm2_g, w_router_group, b_router_group, w_router_expert, b_router_expert, w1, w3, w2):
    depth = w_in.shape[0]
    assert depth == 1, "single-layer step"
    return _layer(x_prompt, x_sample, state_conv, cache_k, cache_v, norm1_g[0], w_in[0], conv_w[0],
                  conv_b[0], conv_norm_g[0], conv_norm_b[0], q_norm_g[0], k_norm_g[0], attn_sinks[0], w_out[0],
                  norm2_g[0], w_router_group[0], b_router_group[0], w_router_expert[0], b_router_expert[0],
                  w1[0], w3[0], w2[0])
```

```python
import functools
import math

import jax
import jax.numpy as jnp
from jax import lax
from jax.experimental import pallas as pl
from jax.experimental.pallas import tpu as pltpu

F32 = jnp.float32
BF16 = jnp.bfloat16
I32 = jnp.int32

D_MODEL = 2048
D_CONV = 1024
CONV_WIDTH = 31
CONV_HIST = CONV_WIDTH - 1
D_ATTN = 1024
HEAD_DIM = 128
N_HEADS = 8
N_KV_HEADS = 2
GQA_GROUP = N_HEADS // N_KV_HEADS
D_KV = N_KV_HEADS * HEAD_DIM
WINDOW = 128
BLOCK_Q = 128
SCALE = 1.0 / math.sqrt(HEAD_DIM)
N_EXPERT_GROUPS = 4
EXPERTS_PER_GROUP = 8
N_EXPERTS = N_EXPERT_GROUPS * EXPERTS_PER_GROUP
D_EXPERT = 512
IN_TILE = 512
ATTN_TILE = 1024
OUT_TILE = 256
MOE_BLOCK = 128
COMBINE_TILE = 128
D_IN = 2 * D_CONV + D_ATTN + 2 * D_KV
EPS = 1e-6
PAST_LEN = 16384

LANES = 128
SUBLANES = 8
MXU_COLS = 256
VMEM_LIMIT_BYTES = 56 * 1024 * 1024
NEG_INF = float("-inf")
EXPERT_ROW0 = N_EXPERT_GROUPS


def _params(n_axes):
    return pltpu.CompilerParams(dimension_semantics=("arbitrary",) * n_axes,
                                vmem_limit_bytes=VMEM_LIMIT_BYTES)


def _resident(shape):
    nd = len(shape)
    return pl.BlockSpec(shape, lambda *_: (0,) * nd, pipeline_mode=pl.Buffered(1))


def _dot(a, b):
    return jnp.dot(a, b, preferred_element_type=F32)


def _in_proj_body(x_ref, g1_ref, w_ref, qg_ref, kg_ref, xpm_ref, u_ref, q_ref, k_ref, v_ref, n_ref):
    n_seq, n_pos, _ = x_ref.shape
    for t in range(n_pos):
        xpm_ref[t * n_seq:(t + 1) * n_seq, :] = x_ref[:, t, :]
    x = xpm_ref[...]
    ms = jnp.mean(x * x, axis=-1, keepdims=True)
    n_ref[...] = (x * lax.rsqrt(ms + EPS) * g1_ref[...]).astype(BF16)

    def head_norm(h, g):
        return h * lax.rsqrt(jnp.mean(h * h, axis=-1, keepdims=True) + EPS) * g

    ch = MXU_COLS
    for c in range(D_CONV // ch):
        a = _dot(n_ref[...], w_ref[:, c * ch:(c + 1) * ch])
        g = _dot(n_ref[...], w_ref[:, D_CONV + c * ch:D_CONV + (c + 1) * ch])
        u_ref[:, c * ch:(c + 1) * ch] = a * jax.nn.sigmoid(g)
    q_off = 2 * D_CONV
    for c in range(D_ATTN // ch):
        qq = _dot(n_ref[...], w_ref[:, q_off + c * ch:q_off + (c + 1) * ch])
        for j in range(ch // HEAD_DIM):
            qh = head_norm(qq[:, j * HEAD_DIM:(j + 1) * HEAD_DIM], qg_ref[...])
            q_ref[:, c * ch + j * HEAD_DIM:c * ch + (j + 1) * HEAD_DIM] = qh.astype(q_ref.dtype)
    k_off = q_off + D_ATTN
    kk = _dot(n_ref[...], w_ref[:, k_off:k_off + D_KV])
    for j in range(N_KV_HEADS):
        k_ref[:, j * HEAD_DIM:(j + 1) * HEAD_DIM] = head_norm(kk[:, j * HEAD_DIM:(j + 1) * HEAD_DIM], kg_ref[...])
    v_ref[...] = _dot(n_ref[...], w_ref[:, k_off + D_KV:k_off + 2 * D_KV])


CAST_COLS = 512


def _load_weight_as_bf16(w_hbm, w_bf, stage, sem):
    n_chunks = w_hbm.shape[1] // CAST_COLS

    def copy(c):
        return pltpu.make_async_copy(w_hbm.at[:, c * CAST_COLS:(c + 1) * CAST_COLS], stage.at[c % 2], sem.at[c % 2])

    copy(0).start()
    for c in range(n_chunks):
        if c + 1 < n_chunks:
            copy(c + 1).start()
        copy(c).wait()
        w_bf[:, c * CAST_COLS:(c + 1) * CAST_COLS] = stage[c % 2].astype(BF16)


def _in_proj_conv_body(x_ref, g1_ref, w_hbm, qg_ref, kg_ref, cw_ref, cb_ref, lg_ref, lb_ref, c_ref, q_ref, k_ref,
                       v_ref, ut_ref, kt_ref, vt_ref, wout_hbm, n_ref, ue_ref, conv_ref, w_ref, wstage, wsem, osem, *,
                       tiles_per_seq):
    tm = x_ref.shape[0]
    step = pl.program_id(0)
    first = lax.rem(step, tiles_per_seq) == 0
    w_out_copy = pltpu.make_async_copy(w_ref, wout_hbm, osem.at[0])

    @pl.when(step == 0)
    def _():
        _load_weight_as_bf16(w_hbm, w_ref, wstage, wsem)
        w_out_copy.start()

    @pl.when(step == pl.num_programs(0) - 1)
    def _():
        w_out_copy.wait()

    @pl.when(first)
    def _():
        for c in range(N_CCHUNK):
            ue_ref[c, 0:CONV_HALO, :] = jnp.zeros((CONV_HALO, LANES), F32)

    @pl.when(jnp.logical_not(first))
    def _():
        for c in range(N_CCHUNK):
            ue_ref[c, 0:CONV_HALO, :] = ue_ref[c, tm:tm + CONV_HALO, :]

    x = x_ref[...]
    ms = jnp.mean(x * x, axis=-1, keepdims=True)
    n_ref[...] = (x * lax.rsqrt(ms + EPS) * g1_ref[...]).astype(BF16)

    def head_norm(h, g):
        return h * lax.rsqrt(jnp.mean(h * h, axis=-1, keepdims=True) + EPS) * g

    ch = MXU_COLS
    lanes_per = ch // LANES
    n_glu = D_CONV // ch
    q_off = 2 * D_CONV

    row_blocks = [slice(r, r + DOT_ROWS) for r in range(0, tm, DOT_ROWS)]

    for c in range(n_glu):
        for rows in row_blocks:
            a = _dot(n_ref[rows, :], w_ref[:, c * ch:(c + 1) * ch])
            g = _dot(n_ref[rows, :], w_ref[:, D_CONV + c * ch:D_CONV + (c + 1) * ch])
            u = a * jax.nn.sigmoid(g)
            for half in range(lanes_per):
                ue_ref[c * lanes_per + half, CONV_HALO + rows.start:CONV_HALO + rows.stop, :] = (
                    u[:, half * LANES:(half + 1) * LANES])
        for half in range(lanes_per):
            lc = c * lanes_per + half
            _conv_chunk(ue_ref, lc, cw_ref[lc], cb_ref[lc], conv_ref, tm)
    _ln_swish(conv_ref, lg_ref, lb_ref, c_ref, tm)

    for c in range(D_ATTN // ch):
        for rows in row_blocks:
            qq = _dot(n_ref[rows, :], w_ref[:, q_off + c * ch:q_off + (c + 1) * ch])
            for j in range(ch // HEAD_DIM):
                qh = head_norm(qq[:, j * HEAD_DIM:(j + 1) * HEAD_DIM], qg_ref[...])
                q_ref[rows, c * ch + j * HEAD_DIM:c * ch + (j + 1) * HEAD_DIM] = qh.astype(q_ref.dtype)
    k_off = q_off + D_ATTN
    for rows in row_blocks:
        kk = _dot(n_ref[rows, :], w_ref[:, k_off:k_off + D_KV])
        vv = _dot(n_ref[rows, :], w_ref[:, k_off + D_KV:k_off + 2 * D_KV])
        v_ref[rows, :] = vv
        for j in range(N_KV_HEADS):
            hs = slice(j * HEAD_DIM, (j + 1) * HEAD_DIM)
            k_ref[rows, hs] = head_norm(kk[:, hs], kg_ref[...])
    for j in range(N_KV_HEADS):
        hs = slice(j * HEAD_DIM, (j + 1) * HEAD_DIM)
        kt_ref[:, j, :] = k_ref[tm - WINDOW:tm, hs]
        vt_ref[:, j, :] = v_ref[tm - WINDOW:tm, hs]
    for c in range(N_CCHUNK):
        ut_ref[:, c * LANES:(c + 1) * LANES] = ue_ref[c, CONV_HALO + tm - CONV_HIST:CONV_HALO + tm, :]


def _in_proj_conv(x2, g1, w_in, qg, kg, cw, cb, lg, lb, tm, n_seq):
    n = x2.shape[0]
    tiles_per_seq = n // n_seq // tm
    row = lambda w: pl.BlockSpec((tm, w), lambda i: (i, 0))
    seq = lambda *dims: pl.BlockSpec((None,) + dims, lambda i: (i // tiles_per_seq,) + (0,) * len(dims))
    hbm = pl.BlockSpec(memory_space=pl.ANY)
    body = functools.partial(_in_proj_conv_body, tiles_per_seq=tiles_per_seq)
    return pl.pallas_call(
        body,
        grid=(n // tm,),
        in_specs=[row(D_MODEL), _resident((1, D_MODEL)), hbm,
                  _resident((1, HEAD_DIM)), _resident((1, HEAD_DIM)),
                  _resident((N_CCHUNK, CONV_WIDTH, LANES)), _resident((N_CCHUNK, 1, LANES)),
                  _resident((N_CCHUNK, 1, LANES)), _resident((N_CCHUNK, 1, LANES))],
        out_specs=[row(D_CONV), row(D_ATTN), row(D_KV), row(D_KV),
                   seq(CONV_HIST, D_CONV), seq(WINDOW, N_KV_HEADS, HEAD_DIM), seq(WINDOW, N_KV_HEADS, HEAD_DIM), hbm],
        out_shape=[jax.ShapeDtypeStruct((n, D_CONV), BF16), jax.ShapeDtypeStruct((n, D_ATTN), BF16),
                   jax.ShapeDtypeStruct((n, D_KV), F32), jax.ShapeDtypeStruct((n, D_KV), F32),
                   jax.ShapeDtypeStruct((n_seq, CONV_HIST, D_CONV), F32),
                   jax.ShapeDtypeStruct((n_seq, WINDOW, N_KV_HEADS, HEAD_DIM), F32),
                   jax.ShapeDtypeStruct((n_seq, WINDOW, N_KV_HEADS, HEAD_DIM), F32),
                   jax.ShapeDtypeStruct((D_MODEL, D_IN), BF16)],
        scratch_shapes=[pltpu.VMEM((tm, D_MODEL), BF16),
                        pltpu.VMEM((N_CCHUNK, CONV_HALO + tm, LANES), F32),
                        pltpu.VMEM((N_CCHUNK, tm, LANES), F32),
                        pltpu.VMEM((D_MODEL, D_IN), BF16), pltpu.VMEM((2, D_MODEL, CAST_COLS), F32),
                        pltpu.SemaphoreType.DMA((2,)), pltpu.SemaphoreType.DMA((1,))],
        compiler_params=_params(1),
        name="in_proj_conv",
    )(x2, g1, w_in, qg, kg, cw, cb, lg, lb)


def _in_proj(x_sample, g1, w_in_bf, qg, kg):
    n = x_sample.shape[0] * x_sample.shape[1]
    whole = lambda w: pl.BlockSpec((n, w), lambda i: (0, 0))
    return pl.pallas_call(
        _in_proj_body,
        grid=(1,),
        in_specs=[pl.BlockSpec(x_sample.shape, lambda i: (0, 0, 0)), _resident((1, D_MODEL)),
                  _resident((D_MODEL, D_IN)), _resident((1, HEAD_DIM)), _resident((1, HEAD_DIM))],
        out_specs=[whole(D_MODEL), whole(D_CONV), whole(D_ATTN), whole(D_KV), whole(D_KV)],
        out_shape=[jax.ShapeDtypeStruct((n, D_MODEL), F32), jax.ShapeDtypeStruct((n, D_CONV), F32),
                   jax.ShapeDtypeStruct((n, D_ATTN), F32),
                   jax.ShapeDtypeStruct((n, D_KV), F32), jax.ShapeDtypeStruct((n, D_KV), F32)],
        scratch_shapes=[pltpu.VMEM((n, D_MODEL), BF16)],
        compiler_params=_params(1),
        name="in_proj",
    )(x_sample, g1, w_in_bf, qg, kg)


N_CCHUNK = D_CONV // LANES
CONV_ROWS = 64


CONV_HALO = 32
DOT_ROWS = 256


def _conv_chunk(ue_ref, c, wc, bias, conv_ref, rows):
    base = CONV_HALO - CONV_HIST
    for r0 in range(0, rows, CONV_ROWS):
        acc = jnp.broadcast_to(bias, (CONV_ROWS, LANES))
        for tap in range(CONV_WIDTH):
            acc = acc + wc[tap:tap + 1, :] * ue_ref[c, base + r0 + tap:base + r0 + tap + CONV_ROWS, :]
        conv_ref[c, r0:r0 + CONV_ROWS, :] = acc


def _ln_swish(conv_ref, lg_ref, lb_ref, cat_ref, rows):
    tot = jnp.zeros((rows, 1), F32)
    for c in range(N_CCHUNK):
        tot = tot + jnp.sum(conv_ref[c], axis=-1, keepdims=True)
    mean = tot / D_CONV
    var = jnp.zeros((rows, 1), F32)
    for c in range(N_CCHUNK):
        xc = conv_ref[c] - mean
        var = var + jnp.sum(xc * xc, axis=-1, keepdims=True)
    rstd = lax.rsqrt(var / D_CONV + EPS)
    for c in range(N_CCHUNK):
        y = (conv_ref[c] - mean) * rstd * lg_ref[c] + lb_ref[c]
        cat_ref[:, c * LANES:(c + 1) * LANES] = (y * jax.nn.sigmoid(y)).astype(cat_ref.dtype)


def _sink_softmax_rows(s, sink):
    m = jnp.maximum(jnp.max(s, axis=-1, keepdims=True), sink)
    p = jnp.exp(s - m)
    return p / (jnp.sum(p, axis=-1, keepdims=True) + jnp.exp(sink - m))


def _alibi_slope(head):
    return 2.0 ** (-8.0 * (head + 1) / N_HEADS)


def _mixer_prompt_body(sink_ref, q_ref, k_ref, kh_ref, v_ref, vh_ref, cat_ref, *, tm):
    j = pl.program_id(1)
    has_prev = j > 0
    qi = lax.broadcasted_iota(I32, (BLOCK_Q, 2 * BLOCK_Q), 0)
    kj = lax.broadcasted_iota(I32, (BLOCK_Q, 2 * BLOCK_Q), 1)
    dist = qi + BLOCK_Q - kj
    distf = dist.astype(F32)
    band = jnp.where(dist >= 0, jnp.where(dist < WINDOW, 0.0, NEG_INF), NEG_INF)
    band_first = jnp.where(kj >= BLOCK_Q, band, jnp.where(has_prev, band, NEG_INF))
    bias = [band - _alibi_slope(h) * distf for h in range(N_HEADS)]

    for qb in range(tm // BLOCK_Q):
        rows = slice(qb * BLOCK_Q, (qb + 1) * BLOCK_Q)
        prev = slice((qb - 1) * BLOCK_Q, qb * BLOCK_Q)
        for kv in range(N_KV_HEADS):
            hs = slice(kv * HEAD_DIM, (kv + 1) * HEAD_DIM)
            k_prev = kh_ref[:, hs] if qb == 0 else k_ref[prev, hs]
            v_prev = vh_ref[:, hs] if qb == 0 else v_ref[prev, hs]
            kk = jnp.concatenate([k_prev, k_ref[rows, hs]], axis=0).astype(BF16)
            vv = jnp.concatenate([v_prev, v_ref[rows, hs]], axis=0).astype(BF16)
            heads = [kv * GQA_GROUP + g for g in range(GQA_GROUP)]
            qs = jnp.concatenate([q_ref[rows, h * HEAD_DIM:(h + 1) * HEAD_DIM] for h in heads], axis=0)
            s = lax.dot_general(qs, kk, (((1,), (1,)), ((), ())), preferred_element_type=F32)
            ps, scales = [], []
            for g, h in enumerate(heads):
                head_bias = band_first - _alibi_slope(h) * distf if qb == 0 else bias[h]
                sg = s[g * BLOCK_Q:(g + 1) * BLOCK_Q] * SCALE + head_bias
                m = jnp.maximum(jnp.max(sg, axis=-1, keepdims=True), sink_ref[h])
                p = jnp.exp(sg - m)
                ps.append(p.astype(BF16))
                scales.append(1.0 / (jnp.sum(p, axis=-1, keepdims=True) + jnp.exp(sink_ref[h] - m)))
            o = _dot(jnp.concatenate(ps, axis=0), vv)
            for g, h in enumerate(heads):
                cat_ref[rows, h * HEAD_DIM:(h + 1) * HEAD_DIM] = (
                    o[g * BLOCK_Q:(g + 1) * BLOCK_Q] * scales[g]).astype(cat_ref.dtype)


def _mixer_prompt(sinks, q, k, v, tm):
    b, t, _ = q.shape
    kpb = tm // BLOCK_Q
    main = lambda w: pl.BlockSpec((None, tm, w), lambda bi, j: (bi, j, 0))
    prev_block = pl.BlockSpec((None, BLOCK_Q, D_KV), lambda bi, j: (bi, jnp.maximum(j * kpb - 1, 0), 0))
    body = functools.partial(_mixer_prompt_body, tm=tm)
    return pl.pallas_call(
        body,
        grid=(b, t // tm),
        in_specs=[pl.BlockSpec(memory_space=pltpu.SMEM), main(D_ATTN), main(D_KV), prev_block, main(D_KV),
                  prev_block],
        out_specs=main(D_ATTN),
        out_shape=jax.ShapeDtypeStruct((b, t, D_ATTN), BF16),
        compiler_params=_params(2),
        name="mixer_prompt",
    )(sinks, q, k, k, v, v)


KEY_PAD = 8


SAMPLE_SEQS = 8


def _sample_conv_ln_swish(st_ref, u_ref, cw_ref, cb_ref, lg_ref, lb_ref, cat_ref, t_new):
    hist, n_seq, _ = st_ref.shape
    conv = [[None] * N_CCHUNK for _ in range(t_new)]
    for c in range(N_CCHUNK):
        cs = slice(c * LANES, (c + 1) * LANES)
        wc = cw_ref[c]
        pos = [st_ref[p, :, cs] for p in range(hist)] + [u_ref[t, :, cs] for t in range(t_new)]
        for t in range(t_new):
            acc = jnp.broadcast_to(cb_ref[c], (n_seq, LANES))
            for tap in range(CONV_WIDTH):
                acc = acc + wc[tap:tap + 1, :] * pos[t + tap]
            conv[t][c] = acc
    for t in range(t_new):
        tot = jnp.zeros((n_seq, 1), F32)
        for a in conv[t]:
            tot = tot + jnp.sum(a, axis=-1, keepdims=True)
        mean = tot / D_CONV
        var = jnp.zeros((n_seq, 1), F32)
        for a in conv[t]:
            var = var + jnp.sum((a - mean) * (a - mean), axis=-1, keepdims=True)
        rstd = lax.rsqrt(var / D_CONV + EPS)
        for c, a in enumerate(conv[t]):
            y = (a - mean) * rstd * lg_ref[c] + lb_ref[c]
            cat_ref[t, :, c * LANES:(c + 1) * LANES] = (y * jax.nn.sigmoid(y)).astype(cat_ref.dtype)


def _mixer_sample_body(sink_ref, u_ref, st_ref, q_ref, k_ref, v_ref, ck_ref, cv_ref, cw_ref, cb_ref, lg_ref,
                       lb_ref, cat_ref, nst_ref, nk_ref, nv_ref, kk_ref, vv_ref, qs_ref, *, t_new):
    hist, n_seq, _ = st_ref.shape
    w_past = ck_ref.shape[1]
    n_keys = w_past + KEY_PAD
    n_rows, n_cols = GQA_GROUP * t_new * n_seq, n_seq * n_keys

    _sample_conv_ln_swish(st_ref, u_ref, cw_ref, cb_ref, lg_ref, lb_ref, cat_ref, t_new)
    nst_ref[0:hist - t_new] = st_ref[t_new:hist]
    nst_ref[hist - t_new:hist] = u_ref[...]

    for i in range(n_seq):
        nk_ref[i, 0:w_past - t_new] = ck_ref[i, t_new:w_past]
        nv_ref[i, 0:w_past - t_new] = cv_ref[i, t_new:w_past]
        kk_ref[i, w_past:n_keys, :] = jnp.zeros((KEY_PAD, D_KV), F32)
        vv_ref[i, w_past:n_keys, :] = jnp.zeros((KEY_PAD, D_KV), F32)
        for t in range(t_new):
            kk_ref[i, w_past + t:w_past + t + 1, :] = k_ref[t, i:i + 1, :]
            vv_ref[i, w_past + t:w_past + t + 1, :] = v_ref[t, i:i + 1, :]
        for h in range(N_KV_HEADS):
            hs = slice(h * HEAD_DIM, (h + 1) * HEAD_DIM)
            kk_ref[i, 0:w_past, hs] = ck_ref[i, :, h, :]
            vv_ref[i, 0:w_past, hs] = cv_ref[i, :, h, :]
            for t in range(t_new):
                nk_ref[i, w_past - t_new + t, h:h + 1, :] = k_ref[t, i:i + 1, hs]
                nv_ref[i, w_past - t_new + t, h:h + 1, :] = v_ref[t, i:i + 1, hs]

    row = lax.broadcasted_iota(I32, (n_rows, n_cols), 0)
    col = lax.broadcasted_iota(I32, (n_rows, n_cols), 1)
    seq_bits, tok_bits = n_seq.bit_length() - 1, t_new.bit_length() - 1
    assert (1 << seq_bits, 1 << tok_bits) == (n_seq, t_new), "sequence and token counts must be powers of two"
    tok = (row >> seq_bits) & (t_new - 1)
    key = col - (row & (n_seq - 1)) * n_keys
    dist = tok + w_past - key
    distf = dist.astype(F32)
    mask = jnp.where(dist >= 0, jnp.where(dist < WINDOW, 0.0, NEG_INF), NEG_INF)
    row1 = lax.broadcasted_iota(I32, (n_rows, 1), 0)
    grp = row1 >> (seq_bits + tok_bits)
    blk = lambda g, t: slice((g * t_new + t) * n_seq, (g * t_new + t + 1) * n_seq)

    for kv in range(N_KV_HEADS):
        hs = slice(kv * HEAD_DIM, (kv + 1) * HEAD_DIM)
        slope = jnp.zeros((n_rows, 1), F32)
        sink = jnp.zeros((n_rows, 1), F32)
        for g in range(GQA_GROUP):
            h = kv * GQA_GROUP + g
            slope = jnp.where(grp == g, _alibi_slope(h), slope)
            sink = jnp.where(grp == g, sink_ref[h], sink)
            for t in range(t_new):
                qs_ref[blk(g, t), :] = q_ref[t, :, h * HEAD_DIM:(h + 1) * HEAD_DIM]
        kk = kk_ref[:, :, hs].reshape(n_cols, HEAD_DIM).astype(BF16)
        vv = vv_ref[:, :, hs].reshape(n_cols, HEAD_DIM).astype(BF16)
        s = lax.dot_general(qs_ref[...].astype(BF16), kk, (((1,), (1,)), ((), ())), preferred_element_type=F32)
        sg = s * SCALE - slope * distf + mask
        o = _dot(_sink_softmax_rows(sg, sink).astype(BF16), vv)
        for g in range(GQA_GROUP):
            h = kv * GQA_GROUP + g
            for t in range(t_new):
                cat_ref[t, :, D_CONV + h * HEAD_DIM:D_CONV + (h + 1) * HEAD_DIM] = o[blk(g, t)].astype(cat_ref.dtype)


def _mixer_sample(sinks, u, state, q, k, v, ck, cv, cw, cb, lg, lb):
    t_new, b, _ = u.shape
    hist = state.shape[0]
    w_past = ck.shape[2]
    n_seq = SAMPLE_SEQS
    per = lambda r, w: pl.BlockSpec((r, n_seq, w), lambda bi: (0, bi, 0))
    cache = pl.BlockSpec((None, n_seq, w_past, N_KV_HEADS, HEAD_DIM), lambda bi: (0, bi, 0, 0, 0))
    body = functools.partial(_mixer_sample_body, t_new=t_new)
    return pl.pallas_call(
        body,
        grid=(b // n_seq,),
        in_specs=[pl.BlockSpec(memory_space=pltpu.SMEM),
                  per(t_new, D_CONV), per(hist, D_CONV), per(t_new, D_ATTN), per(t_new, D_KV), per(t_new, D_KV),
                  cache, cache,
                  _resident((N_CCHUNK, CONV_WIDTH, LANES)), _resident((N_CCHUNK, 1, LANES)),
                  _resident((N_CCHUNK, 1, LANES)), _resident((N_CCHUNK, 1, LANES))],
        out_specs=[per(t_new, D_MODEL), per(hist, D_CONV), cache, cache],
        out_shape=[jax.ShapeDtypeStruct((t_new, b, D_MODEL), F32),
                   jax.ShapeDtypeStruct(state.shape, F32),
                   jax.ShapeDtypeStruct(ck.shape, F32),
                   jax.ShapeDtypeStruct(cv.shape, F32)],
        scratch_shapes=[pltpu.VMEM((n_seq, w_past + KEY_PAD, D_KV), F32),
                        pltpu.VMEM((n_seq, w_past + KEY_PAD, D_KV), F32),
                        pltpu.VMEM((n_seq * GQA_GROUP * t_new, HEAD_DIM), F32)],
        compiler_params=_params(1),
        name="mixer_sample",
    )(sinks, u, state, q, k, v, ck, cv, cw, cb, lg, lb)


OUT_CHUNK = 512
D_HALF = D_MODEL // 2
U32 = jnp.uint32


def _pack_bf16_pairs(hi, lo):
    hi_bits = lax.bitcast_convert_type(hi.astype(BF16).astype(F32), U32)
    lo_bits = lax.bitcast_convert_type(lo.astype(BF16).astype(F32), U32)
    return hi_bits | (lo_bits >> 16)


def _unpack_bf16_pairs(words):
    hi = lax.bitcast_convert_type(words & U32(0xFFFF0000), F32)
    lo = lax.bitcast_convert_type(words << 16, F32)
    return hi, lo


def _out_proj_body(catc_ref, cata_ref, cats_ref, xp_ref, xs_ref, wo_hbm, g2_ref, wr_ref, x1_ref, h2p_ref, lg_ref,
                   h_ref, wo_ref, wstage, wsem, *, n_prompt_tiles):
    is_prompt = pl.program_id(0) < n_prompt_tiles
    tm = catc_ref.shape[0]

    @pl.when(pl.program_id(0) == 0)
    def _():
        _load_weight_as_bf16(wo_hbm, wo_ref, wstage, wsem)

    def sample_rows(v):
        return jnp.concatenate([v, jnp.zeros((tm - v.shape[0], v.shape[1]), v.dtype)], axis=0)

    cat = jnp.where(is_prompt, jnp.concatenate([catc_ref[...], cata_ref[...]], axis=1),
                    sample_rows(cats_ref[...].astype(BF16)))
    ss = jnp.zeros((tm, 1), F32)
    for c in range(D_MODEL // OUT_CHUNK):
        cs = slice(c * OUT_CHUNK, (c + 1) * OUT_CHUNK)
        y = jnp.where(is_prompt, xp_ref[:, cs], sample_rows(xs_ref[:, cs])) + _dot(cat, wo_ref[:, cs])
        x1_ref[:, cs] = y
        ss = ss + jnp.sum(y * y, axis=-1, keepdims=True)
    r = lax.rsqrt(ss / D_MODEL + EPS)
    for c in range(D_MODEL // OUT_CHUNK):
        cs = slice(c * OUT_CHUNK, (c + 1) * OUT_CHUNK)
        h_ref[:, cs] = x1_ref[:, cs] * r * g2_ref[:, cs]
    for c in range(D_HALF // OUT_CHUNK):
        cs = slice(c * OUT_CHUNK, (c + 1) * OUT_CHUNK)
        cs_lo = slice(D_HALF + c * OUT_CHUNK, D_HALF + (c + 1) * OUT_CHUNK)
        h2p_ref[:, cs] = _pack_bf16_pairs(h_ref[:, cs], h_ref[:, cs_lo])
    lg = _dot(h_ref[...].astype(BF16), wr_ref[...])
    for j in range(tm // ROUTE_CHUNK):
        lg_ref[j] = lg[j * ROUTE_CHUNK:(j + 1) * ROUTE_CHUNK, :].T[0:ROUTER_ROWS, :]


def _out_proj(cat_conv, cat_attn, cat_s, xp2, xs2, w_out, g2, wr_bf, tm):
    n_prompt_tiles = cat_conv.shape[0] // tm
    n_rows = (n_prompt_tiles + 1) * tm
    n_s = cat_s.shape[0]
    prompt = lambda w: pl.BlockSpec((tm, w), lambda i: (jnp.minimum(i, n_prompt_tiles - 1), 0))
    out_row = lambda w: pl.BlockSpec((tm, w), lambda i: (i, 0))
    body = functools.partial(_out_proj_body, n_prompt_tiles=n_prompt_tiles)
    return pl.pallas_call(
        body,
        grid=(n_prompt_tiles + 1,),
        in_specs=[prompt(D_CONV), prompt(D_ATTN), _resident((n_s, D_MODEL)), prompt(D_MODEL),
                  _resident((n_s, D_MODEL)),
                  pl.BlockSpec(memory_space=pl.ANY), _resident((1, D_MODEL)), _resident((D_MODEL, LANES))],
        out_specs=[out_row(D_MODEL), out_row(D_HALF),
                   pl.BlockSpec((tm // ROUTE_CHUNK, ROUTER_ROWS, ROUTE_CHUNK), lambda i: (i, 0, 0))],
        out_shape=[jax.ShapeDtypeStruct((n_rows, D_MODEL), F32), jax.ShapeDtypeStruct((n_rows, D_HALF), U32),
                   jax.ShapeDtypeStruct((n_rows // ROUTE_CHUNK, ROUTER_ROWS, ROUTE_CHUNK), F32)],
        scratch_shapes=[pltpu.VMEM((tm, D_MODEL), F32), pltpu.VMEM((D_MODEL, D_MODEL), BF16),
                        pltpu.VMEM((2, D_MODEL, CAST_COLS), F32), pltpu.SemaphoreType.DMA((2,))],
        compiler_params=_params(1),
        name="out_proj",
    )(cat_conv, cat_attn, cat_s, xp2, xs2, w_out, g2, wr_bf)


ROUTE_CHUNK = 128
ROUTE_UNROLL = 5


ROUTER_ROWS = 40


def _blk_start(blk_ref, e):
    return blk_ref[EXPERT_ROW0 + e, 0]


def _blk_count(blk_ref, e):
    return blk_ref[ROUTER_ROWS + EXPERT_ROW0 + e, 0]


def _route_body(lg_ref, bias_ref, slot_ref, gate_ref, blk_ref, cum_ref, sel_ref, *, n_tok):
    n_chunks = n_tok // ROUTE_CHUNK
    row = lax.broadcasted_iota(I32, (ROUTER_ROWS, ROUTE_CHUNK), 0).astype(F32)
    ri = lax.broadcasted_iota(I32, (ROUTE_CHUNK, ROUTE_CHUNK), 0)
    ci = lax.broadcasted_iota(I32, (ROUTE_CHUNK, ROUTE_CHUNK), 1)
    earlier_tok = jnp.where(ri < ci, 1.0, 0.0).astype(BF16)
    er = lax.broadcasted_iota(I32, (ROUTER_ROWS, ROUTER_ROWS), 0)
    ec = lax.broadcasted_iota(I32, (ROUTER_ROWS, ROUTER_ROWS), 1)
    earlier_row = jnp.where(ec < er, 1.0, 0.0).astype(BF16)
    sub = lax.broadcasted_iota(I32, (SUBLANES, ROUTE_CHUNK), 0)
    is_group = row < N_EXPERT_GROUPS

    def first_max(vals):
        m = jnp.max(vals, axis=0, keepdims=True)
        idx = jnp.min(jnp.where(vals == m, row, float(ROUTER_ROWS)), axis=0, keepdims=True)
        return m, idx

    def pair_rows(a, b):
        return jnp.where(sub == 0, a, jnp.where(sub == 1, b, 0.0))

    def assign(i, carry):
        l = lg_ref[i] + bias_ref[...]
        gl = jnp.where(is_group, l, NEG_INF)
        g_max, g_idx = first_max(gl)
        g_top = 1.0 / jnp.sum(jnp.exp(gl - g_max), axis=0, keepdims=True)
        lo = EXPERT_ROW0 + g_idx * EXPERTS_PER_GROUP
        el = jnp.where(row >= lo, jnp.where(row < lo + EXPERTS_PER_GROUP, l, NEG_INF), NEG_INF)
        m1, i1 = first_max(el)
        p = jnp.exp(el - m1)
        probs = p / jnp.sum(p, axis=0, keepdims=True)
        e1 = jnp.sum(jnp.where(row == i1, probs, 0.0), axis=0, keepdims=True)
        _, i2 = first_max(jnp.where(row == i1, NEG_INF, el))
        e2 = jnp.sum(jnp.where(row == i2, probs, 0.0), axis=0, keepdims=True)
        gate_ref[i] = pair_rows(g_top * e1 / (e1 + e2), g_top * e2 / (e1 + e2))
        sel_ref[i] = pair_rows(i1, i2)
        onehot = jnp.where(row == i1, 1.0, jnp.where(row == i2, 1.0, 0.0))
        cum_ref[i] = _dot(onehot.astype(BF16), earlier_tok) + carry
        return carry + jnp.sum(onehot, axis=1, keepdims=True)

    counts = lax.fori_loop(0, n_chunks, assign, jnp.zeros((ROUTER_ROWS, 1), F32), unroll=ROUTE_UNROLL)
    n_blocks = jnp.floor((counts + (MOE_BLOCK - 1)) / MOE_BLOCK)
    blk_start = _dot(earlier_row, jnp.broadcast_to(n_blocks, (ROUTER_ROWS, LANES)).astype(BF16))
    row_start = blk_start * MOE_BLOCK

    def place(i, carry):
        pos = cum_ref[i] + row_start
        sel = sel_ref[i]
        s1 = jnp.sum(jnp.where(row == sel[0:1, :], pos, 0.0), axis=0, keepdims=True)
        s2 = jnp.sum(jnp.where(row == sel[1:2, :], pos, 0.0), axis=0, keepdims=True)
        slot_ref[0, i] = s1.astype(I32)
        slot_ref[1, i] = s2.astype(I32)
        return carry

    lax.fori_loop(0, n_chunks, place, 0, unroll=ROUTE_UNROLL)

    blk_ref[0:ROUTER_ROWS, :] = blk_start.astype(I32)
    blk_ref[ROUTER_ROWS:2 * ROUTER_ROWS, :] = jnp.broadcast_to(counts, (ROUTER_ROWS, LANES)).astype(I32)


def _route(logits, bias, n_tok):
    body = functools.partial(_route_body, n_tok=n_tok)
    n_chunks = n_tok // ROUTE_CHUNK
    chunked = lambda r: pl.BlockSpec((n_chunks, r, LANES), lambda i: (0, 0, 0))
    table = pl.BlockSpec((2 * ROUTER_ROWS, LANES), lambda i: (0, 0))
    return pl.pallas_call(
        body,
        grid=(1,),
        in_specs=[chunked(ROUTER_ROWS), pl.BlockSpec((ROUTER_ROWS, LANES), lambda i: (0, 0))],
        out_specs=[pl.BlockSpec((2, n_chunks, 1, LANES), lambda i: (0, 0, 0, 0)), chunked(SUBLANES), table],
        out_shape=[jax.ShapeDtypeStruct((2, n_chunks, 1, LANES), I32),
                   jax.ShapeDtypeStruct((n_chunks, SUBLANES, LANES), F32),
                   jax.ShapeDtypeStruct((2 * ROUTER_ROWS, LANES), I32)],
        scratch_shapes=[pltpu.VMEM((n_chunks, ROUTER_ROWS, LANES), F32),
                        pltpu.VMEM((n_chunks, SUBLANES, LANES), F32)],
        compiler_params=_params(1),
        name="route",
    )(logits, bias)


INVERT_UNROLL = 16


CLEAR_SPAN = 64


def _invert_body(slot_ref, blk_ref, tok_ref, *, n_slots):
    n_tok = slot_ref.shape[0] // 2

    def clear_expert(e, c):
        first_pad = _blk_start(blk_ref, e) * MOE_BLOCK + _blk_count(blk_ref, e)
        end = jnp.where(e == N_EXPERTS - 1, n_slots, _blk_start(blk_ref, e + 1) * MOE_BLOCK)
        span_bits = CLEAR_SPAN.bit_length() - 1
        lo = lax.shift_right_logical(first_pad, span_bits) * CLEAR_SPAN

        def span(j, c2):
            for k in range(CLEAR_SPAN):
                tok_ref[lo + j * CLEAR_SPAN + k] = 0
            return c2

        lax.fori_loop(0, lax.shift_right_logical(end - lo, span_bits), span, 0)
        return c

    lax.fori_loop(0, N_EXPERTS, clear_expert, 0)

    def put(t, c):
        tok_ref[slot_ref[t]] = t
        tok_ref[slot_ref[n_tok + t]] = t
        return c

    lax.fori_loop(0, n_tok, put, 0, unroll=INVERT_UNROLL)


def _invert(slots, blk, n_slots):
    smem = pl.BlockSpec(memory_space=pltpu.SMEM)
    return pl.pallas_call(
        functools.partial(_invert_body, n_slots=n_slots),
        in_specs=[smem, smem],
        out_specs=smem,
        out_shape=jax.ShapeDtypeStruct((n_slots,), I32),
        name="invert",
    )(slots, blk)


GATHER_AHEAD = 6
GATHER_SLOTS = GATHER_AHEAD + 1
COMBINE_AHEAD = 3
COMBINE_SLOTS = COMBINE_AHEAD + 1


def _row_gather_start(idx_ref, base, src_hbm, dst, sem, n_rows, priorities):
    for r in range(n_rows):
        tok = idx_ref[base + r]
        pltpu.make_async_copy(src_hbm.at[pl.ds(tok, 1), :], dst.at[pl.ds(r, 1), :], sem).start(
            priority=priorities[r % len(priorities)])


def _experts_body(bstart_ref, tok_ref, h2p_hbm, w1_ref, w3_ref, w2_ref, eo_hbm, xbuf, obuf, w1b, w3b, w2b, gsem,
                  osem, *, n_blocks):
    e = pl.program_id(0)
    n_exp = pl.num_programs(0)
    b0 = _blk_start(bstart_ref, e)
    nb = _blk_start(bstart_ref, e + 1) - b0
    n_used = _blk_start(bstart_ref, n_exp)

    def gather(block):
        s = lax.rem(block, GATHER_SLOTS)
        src_block = jnp.minimum(block, n_used - 1)
        _row_gather_start(tok_ref, src_block * MOE_BLOCK, h2p_hbm, xbuf.at[s], gsem.at[s], MOE_BLOCK, (1, 0))

    def gather_wait(block):
        s = lax.rem(block, GATHER_SLOTS)
        pltpu.make_async_copy(xbuf.at[s], xbuf.at[s], gsem.at[s]).wait()

    def out_copy(block, s):
        rows = pl.ds(pl.multiple_of(block * MOE_BLOCK, MOE_BLOCK), MOE_BLOCK)
        return pltpu.make_async_copy(obuf.at[s], eo_hbm.at[rows, :], osem.at[s])

    @pl.when(e == 0)
    def _():
        for k in range(GATHER_AHEAD):
            gather(k)

    @pl.when(nb > 0)
    def _():
        w1b[...] = w1_ref[...].astype(BF16)
        w3b[...] = w3_ref[...].astype(BF16)
        w2b[...] = w2_ref[...].astype(BF16)

    def block(j, carry):
        b = b0 + j
        s = lax.rem(b, 2)
        gather(b + GATHER_AHEAD)
        gather_wait(b)
        hi, lo = _unpack_bf16_pairs(xbuf[lax.rem(b, GATHER_SLOTS)])
        x = jnp.concatenate([hi.astype(BF16), lo.astype(BF16)], axis=1)
        a = _dot(x, w1b[...])
        g = _dot(x, w3b[...])
        hdn = (a * jax.nn.sigmoid(a) * g).astype(BF16)
        o = _dot(hdn, w2b[...])

        @pl.when(b >= 2)
        def _():
            out_copy(b - 2, s).wait()

        obuf[s] = _pack_bf16_pairs(o[:, :D_HALF], o[:, D_HALF:])
        out_copy(b, s).start()
        return carry

    lax.fori_loop(0, nb, block, 0)

    @pl.when(e == n_exp - 1)
    def _():
        for k in range(GATHER_AHEAD):
            gather_wait(n_used + k)

        @pl.when(n_used >= 2)
        def _():
            out_copy(n_used - 2, lax.rem(n_used, 2)).wait()

        out_copy(n_used - 1, lax.rem(n_used - 1, 2)).wait()
        obuf[0] = jnp.zeros((MOE_BLOCK, D_HALF), U32)

        def fill(tb, carry):
            out_copy(tb, 0).start()
            return carry

        lax.fori_loop(n_used, n_blocks, fill, 0)

        def drain(tb, carry):
            out_copy(tb, 0).wait()
            return carry

        lax.fori_loop(n_used, n_blocks, drain, 0)


def _experts(bstart, tok, h2p, w1, w3, w2, n_blocks):
    def wspec(r, c):
        return pl.BlockSpec((None, r, c), lambda e, bstart_ref, tok_ref: (e, 0, 0))

    grid_spec = pltpu.PrefetchScalarGridSpec(
        num_scalar_prefetch=2,
        grid=(N_EXPERTS,),
        in_specs=[pl.BlockSpec(memory_space=pl.ANY),
                  wspec(D_MODEL, D_EXPERT), wspec(D_MODEL, D_EXPERT), wspec(D_EXPERT, D_MODEL)],
        out_specs=pl.BlockSpec(memory_space=pl.ANY),
        scratch_shapes=[pltpu.VMEM((GATHER_SLOTS, MOE_BLOCK, D_HALF), U32), pltpu.VMEM((2, MOE_BLOCK, D_HALF), U32),
                        pltpu.VMEM((D_MODEL, D_EXPERT), BF16), pltpu.VMEM((D_MODEL, D_EXPERT), BF16),
                        pltpu.VMEM((D_EXPERT, D_MODEL), BF16),
                        pltpu.SemaphoreType.DMA((GATHER_SLOTS,)), pltpu.SemaphoreType.DMA((2,))],
    )
    return pl.pallas_call(
        functools.partial(_experts_body, n_blocks=n_blocks),
        grid_spec=grid_spec,
        out_shape=jax.ShapeDtypeStruct((n_blocks * MOE_BLOCK, D_HALF), U32),
        compiler_params=_params(1),
        name="experts",
    )(bstart, tok, h2p, w1, w3, w2)


def _combine_body(slot_ref, x1_ref, gate_ref, eo_hbm, yp_ref, ys_ref, buf, sem, *, n_tok, n_prompt_tiles):
    i = pl.program_id(0)
    tm = x1_ref.shape[0]
    n_tiles = n_tok // tm
    slot = lax.rem(i, COMBINE_SLOTS)

    def start(tile):
        s = lax.rem(tile, COMBINE_SLOTS)
        base = jnp.minimum(tile, n_tiles - 1) * tm
        _row_gather_start(slot_ref, base, eo_hbm, buf.at[s, 0], sem.at[s], tm, (0, 1))
        _row_gather_start(slot_ref, n_tok + base, eo_hbm, buf.at[s, 1], sem.at[s], tm, (0, 1))

    def wait(tile):
        s = lax.rem(tile, COMBINE_SLOTS)
        pltpu.make_async_copy(buf.at[s], buf.at[s], sem.at[s]).wait()

    @pl.when(i == 0)
    def _():
        for k in range(COMBINE_AHEAD):
            start(k)

    start(i + COMBINE_AHEAD)
    wait(i)

    @pl.when(i == n_tiles - 1)
    def _():
        for k in range(COMBINE_AHEAD):
            wait(n_tiles + k)

    diag = lax.broadcasted_iota(I32, (tm, tm), 0) == lax.broadcasted_iota(I32, (tm, tm), 1)
    g0 = jnp.sum(jnp.where(diag, gate_ref[0:1, :], 0.0), axis=1, keepdims=True)
    g1 = jnp.sum(jnp.where(diag, gate_ref[1:2, :], 0.0), axis=1, keepdims=True)
    hi0, lo0 = _unpack_bf16_pairs(buf[slot, 0])
    hi1, lo1 = _unpack_bf16_pairs(buf[slot, 1])
    y_hi = x1_ref[:, :D_HALF] + g0 * hi0 + g1 * hi1
    y_lo = x1_ref[:, D_HALF:] + g0 * lo0 + g1 * lo1

    @pl.when(i < n_prompt_tiles)
    def _():
        yp_ref[:, :D_HALF] = y_hi
        yp_ref[:, D_HALF:] = y_lo

    @pl.when(i >= n_prompt_tiles)
    def _():
        n_seq, n_pos, _ = ys_ref.shape
        for t in range(n_pos):
            ys_ref[:, t, :D_HALF] = y_hi[t * n_seq:(t + 1) * n_seq]
            ys_ref[:, t, D_HALF:] = y_lo[t * n_seq:(t + 1) * n_seq]


def _combine(slots, x1, gates, eo, n_prompt, sample_shape, tm):
    n_tok = n_prompt + sample_shape[0] * sample_shape[1]
    assert n_tok - n_prompt == tm, "the sample rows must fill one combine tile"
    n_tiles = n_tok // tm
    n_prompt_tiles = n_prompt // tm
    body = functools.partial(_combine_body, n_tok=n_tok, n_prompt_tiles=n_prompt_tiles)
    grid_spec = pltpu.PrefetchScalarGridSpec(
        num_scalar_prefetch=1,
        grid=(n_tiles,),
        in_specs=[pl.BlockSpec((tm, D_MODEL), lambda i, s: (i, 0)),
                  pl.BlockSpec((None, SUBLANES, tm), lambda i, s: (i, 0, 0)),
                  pl.BlockSpec(memory_space=pl.ANY)],
        out_specs=[pl.BlockSpec((tm, D_MODEL), lambda i, s: (jnp.minimum(i, n_prompt_tiles - 1), 0)),
                   pl.BlockSpec(sample_shape, lambda i, s: (0, 0, 0))],
        scratch_shapes=[pltpu.VMEM((COMBINE_SLOTS, 2, tm, D_HALF), U32), pltpu.SemaphoreType.DMA((COMBINE_SLOTS,))],
    )
    return pl.pallas_call(
        body,
        grid_spec=grid_spec,
        out_shape=[jax.ShapeDtypeStruct((n_prompt, D_MODEL), F32), jax.ShapeDtypeStruct(sample_shape, F32)],
        compiler_params=_params(1),
        name="combine",
    )(slots, x1, gates, eo)


def _layer(x_prompt, x_sample, state_conv, cache_k, cache_v, norm1_g, w_in, conv_w, conv_b, conv_norm_g,
           conv_norm_b, q_norm_g, k_norm_g, attn_sinks, w_out, norm2_g, w_rg, b_rg, w_re, b_re, w1, w3, w2):
    b, t, _ = x_prompt.shape
    sb, st, _ = x_sample.shape
    n_p, n_s = b * t, sb * st
    n_tok = n_p + n_s

    g1 = norm1_g.reshape(1, D_MODEL)
    g2 = norm2_g.reshape(1, D_MODEL)
    qg = q_norm_g.reshape(1, HEAD_DIM)
    kg = k_norm_g.reshape(1, HEAD_DIM)
    chunked = lambda a: a.reshape(-1, N_CCHUNK, LANES).transpose(1, 0, 2)
    cw, cb, lg, lb = chunked(conv_w), chunked(conv_b), chunked(conv_norm_g), chunked(conv_norm_b)
    pad_rows = ROUTER_ROWS - N_EXPERT_GROUPS - N_EXPERTS
    pad_lanes = LANES - N_EXPERT_GROUPS - N_EXPERTS
    w_router_bf = jnp.concatenate([w_rg, w_re, jnp.zeros((D_MODEL, pad_lanes), F32)], axis=1).astype(BF16)
    b_router = jnp.broadcast_to(jnp.concatenate([b_rg, b_re, jnp.zeros((pad_rows,), F32)])[:, None],
                                (ROUTER_ROWS, LANES))

    xp2 = x_prompt.reshape(n_p, D_MODEL)
    c_p, q_p, k_p, v_p, conv_p, knew_p, vnew_p, w_in_bf = _in_proj_conv(xp2, g1, w_in, qg, kg, cw, cb, lg, lb,
                                                                       IN_TILE, b)
    xs2, u_s, q_s, k_s, v_s = _in_proj(x_sample, g1, w_in_bf, qg, kg)

    r3 = lambda a, bb: a.reshape(bb, -1, a.shape[-1])
    a_p = _mixer_prompt(attn_sinks, r3(q_p, b), r3(k_p, b), r3(v_p, b), ATTN_TILE)
    cat_s, conv_s, knew_s, vnew_s = _mixer_sample(
        attn_sinks, r3(u_s, st), state_conv[0].transpose(1, 0, 2), r3(q_s, st), r3(k_s, st), r3(v_s, st),
        cache_k, cache_v, cw, cb, lg, lb)
    conv_s = conv_s.transpose(1, 0, 2)[None]

    x1, h2p, logits = _out_proj(c_p, a_p.reshape(n_p, D_ATTN), cat_s.reshape(n_s, D_MODEL), xp2, xs2, w_out, g2,
                                w_router_bf, OUT_TILE)

    n_blocks = -(-(n_tok * 2) // MOE_BLOCK) + N_EXPERTS
    slots, gates, blk = _route(logits, b_router, n_tok)
    slots = slots.reshape(2 * n_tok)
    tok = _invert(slots, blk, n_blocks * MOE_BLOCK)
    eo = _experts(blk, tok, h2p, w1, w3, w2, n_blocks)
    y_p, y_s = _combine(slots, x1, gates, eo, n_p, x_sample.shape, COMBINE_TILE)

    return (y_p.reshape(b, t, D_MODEL), y_s, conv_p[None], knew_p[None],
            vnew_p[None], conv_s, knew_s, vnew_s)


def kernel(x_prompt, x_sample, state_conv, cache_k, cache_v, norm1_g, w_in, conv_w, conv_b, conv_norm_g, conv_norm_b, q_norm_g, k_norm_g, attn_sinks, w_out, norm2_g, w_router_group, b_router_group, w_router_expert, b_router_expert, w1, w3, w2):
    depth = w_in.shape[0]
    assert depth == 1, "single-layer step"
    return _layer(x_prompt, x_sample, state_conv, cache_k, cache_v, norm1_g[0], w_in[0], conv_w[0],
                  conv_b[0], conv_norm_g[0], conv_norm_b[0], q_norm_g[0], k_norm_g[0], attn_sinks[0], w_out[0],
                  norm2_g[0], w_router_group[0], b_router_group[0], w_router_expert[0], b_router_expert[0],
                  w1[0], w3[0], w2[0])
```
